```python
import math
import jax, jax.numpy as jnp
from jax import lax
import numpy as np

D_MODEL = 1024
BATCH = 4
SEQ = 8192
DEPTH = 4

A_HEADS = 8
A_HEAD_DIM = 64
A_WIDTH = A_HEADS * A_HEAD_DIM
A_KV_RANK = 128
IDX_HEADS = 8
IDX_DIM = 64
TOPK_MAX = 256
Q_BLOCK = 128
REL_BUCKETS = 32
REL_MAX_DIST = 128
B_HEADS = 4
B_HEAD_DIM = 128
B_WIDTH = B_HEADS * B_HEAD_DIM
CONV_WIDTH = 4
CHUNK = 64
C_WIDTH = 512
C_GROUP = 16
C_GROUPS = C_WIDTH // C_GROUP
C_STATE = 64
DT_MIN = 0.001
DT_MAX = 0.1
N_EXPERTS = 64
TOP_K = 8
D_EXPERT = 256
D_SHARED = 256
ROUTED_SCALE = 2.5
MOE_BLOCK = 256
SPLITS = (A_WIDTH, A_KV_RANK, IDX_HEADS * IDX_DIM, IDX_DIM, IDX_HEADS, 3 * B_WIDTH, B_WIDTH, B_HEADS, B_HEADS, C_WIDTH, 3 * D_MODEL)
D_IN = sum(SPLITS)
DEEPNORM_ALPHA = (2 * DEPTH) ** 0.25
DEEPNORM_BETA = (8 * DEPTH) ** -0.25
LN_EPS = 1e-5
RMS_EPS = 1e-6
NEG_INF = -1e30

kernel_name = 'hybrid_dsa_deltanet_s5_moe_deepnorm'


def layer_norm(x, g, b):
    xf = x.astype(jnp.float32)
    mu = jnp.mean(xf, -1, keepdims=True)
    var = jnp.mean(jnp.square(xf - mu), -1, keepdims=True)
    return ((xf - mu) * lax.rsqrt(var + LN_EPS) * g + b).astype(x.dtype)


def rms_norm(x, g):
    xf = x.astype(jnp.float32)
    return (xf * lax.rsqrt(jnp.mean(xf * xf, -1, keepdims=True) + RMS_EPS) * g).astype(x.dtype)


def l2_normalize(x):
    return x * lax.rsqrt(jnp.sum(x * x, -1, keepdims=True) + RMS_EPS)


def rel_bucket(dist):
    n_exact = REL_BUCKETS // 2
    d = jnp.maximum(dist, 1).astype(jnp.float32)
    large = n_exact + (jnp.log(d / n_exact) / math.log(REL_MAX_DIST / n_exact) * (REL_BUCKETS - n_exact)).astype(jnp.int32)
    return jnp.where(dist < n_exact, dist, jnp.minimum(large, REL_BUCKETS - 1))


def dsa_attention(q, c_kv, q_idx, k_idx, w_idx, kv_norm_g, w_kv_up, rel_bias):
    bn, s = q.shape[0], q.shape[1]
    k_sel = min(TOPK_MAX, s // 4)
    kv = jnp.einsum('bsr,rd->bsd', rms_norm(c_kv, kv_norm_g), w_kv_up).reshape(bn, s, 2, A_HEADS, A_HEAD_DIM)
    k, v = kv[:, :, 0], kv[:, :, 1]
    nb = s // Q_BLOCK
    key_pos = jnp.arange(s, dtype=jnp.int32)

    def to_blocks(t):
        return jnp.moveaxis(t.reshape(bn, nb, Q_BLOCK, *t.shape[2:]), 1, 0)

    def one_block(args):
        qb, qib, wb, tpos = args
        s_idx = jnp.einsum('bqhd,bsd->bqhs', qib, k_idx) * IDX_DIM ** -0.5
        score = jnp.einsum('bqh,bqhs->bqs', wb, jax.nn.relu(s_idx)).astype(jnp.float32)
        visible = key_pos[None, :] <= tpos[:, None]
        score = jnp.where(visible[None], score, -jnp.inf)
        _, sel = lax.top_k(score, k_sel)
        k_g = jax.vmap(lambda kb, ib: kb[ib])(k, sel)
        v_g = jax.vmap(lambda vb, ib: vb[ib])(v, sel)
        dist = tpos[None, :, None] - sel
        bias = jnp.moveaxis(rel_bias[rel_bucket(jnp.maximum(dist, 0))], 3, 1)
        logits = jnp.einsum('bqhd,bqkhd->bhqk', qb, k_g).astype(jnp.float32) * A_HEAD_DIM ** -0.5 + bias
        logits = jnp.where((dist >= 0)[:, None], logits, NEG_INF)
        p = jax.nn.softmax(logits, axis=-1).astype(v.dtype)
        return jnp.einsum('bhqk,bqkhd->bqhd', p, v_g)

    out = lax.map(one_block, (to_blocks(q), to_blocks(q_idx), to_blocks(w_idx), key_pos.reshape(nb, Q_BLOCK)))
    return jnp.moveaxis(out, 0, 1).reshape(bn, s, A_WIDTH)


def causal_depthwise_conv(x, w):
    s = x.shape[1]
    xp = jnp.pad(x, ((0, 0), (CONV_WIDTH - 1, 0), (0, 0)))
    return sum(xp[:, j:j + s] * w[j] for j in range(CONV_WIDTH))


def chunk_gated_delta_rule(q, k, v, g, beta):
    bn, s, h, dk = q.shape
    dv = v.shape[-1]
    n = s // CHUNK

    def chunks(t):
        return jnp.moveaxis(t.reshape(bn, n, CHUNK, h, -1), 3, 1)

    q, k, v = chunks(q), chunks(k), chunks(v)
    gc = jnp.cumsum(chunks(g[..., None])[..., 0], axis=-1)
    beta = chunks(beta[..., None])[..., 0]
    causal = jnp.tril(jnp.ones((CHUNK, CHUNK), bool))
    strict = jnp.tril(jnp.ones((CHUNK, CHUNK), bool), -1)
    decay = jnp.exp(jnp.where(causal, gc[..., :, None] - gc[..., None, :], -jnp.inf))
    k_beta = k * beta[..., None]
    m = jnp.where(strict, jnp.einsum('bhnid,bhnjd->bhnij', k_beta, k) * decay, 0.0)
    eye = jnp.eye(CHUNK, dtype=jnp.float32)
    t_inv = lax.linalg.triangular_solve(eye + m, jnp.broadcast_to(eye, m.shape), left_side=True, lower=True)
    u = t_inv @ (v * beta[..., None])
    w = t_inv @ (k_beta * jnp.exp(gc)[..., None])
    qk = jnp.einsum('bhnid,bhnjd->bhnij', q, k) * decay
    q_dec = q * jnp.exp(gc)[..., None]
    k_dec = k * jnp.exp(gc[..., -1:] - gc)[..., None]
    g_last = jnp.exp(gc[..., -1])

    def step(state, inp):
        u_n, w_n, qd_n, kd_n, qk_n, gl_n = inp
        v_new = u_n - w_n @ state
        o_n = qd_n @ state + qk_n @ v_new
        state = state * gl_n[..., None, None] + jnp.swapaxes(kd_n, -1, -2) @ v_new
        return state, o_n

    xs = tuple(jnp.moveaxis(t, 2, 0) for t in (u, w, q_dec, k_dec, qk, g_last))
    _, o = lax.scan(step, jnp.zeros((bn, h, dk, dv), jnp.float32), xs)
    o = jnp.moveaxis(jnp.moveaxis(o, 0, 2), 1, 3)
    return o.reshape(bn, s, h, dv)


def gated_deltanet(qkv, z, a, b, conv_w, a_log, dt_bias, out_norm_g):
    bn, s, _ = qkv.shape
    dtype = qkv.dtype
    qkv = jax.nn.silu(causal_depthwise_conv(qkv, conv_w)).astype(jnp.float32)
    q, k, v = [t.reshape(bn, s, B_HEADS, B_HEAD_DIM) for t in jnp.split(qkv, 3, axis=-1)]
    q = l2_normalize(q) * B_HEAD_DIM ** -0.5
    k = l2_normalize(k)
    beta = jax.nn.sigmoid(b.astype(jnp.float32))
    g = -jnp.exp(a_log.astype(jnp.float32)) * jax.nn.softplus(a.astype(jnp.float32) + dt_bias)
    o = chunk_gated_delta_rule(q, k, v, g, beta)
    o = rms_norm(o, out_norm_g) * jax.nn.silu(z.astype(jnp.float32)).reshape(bn, s, B_HEADS, B_HEAD_DIM)
    return o.reshape(bn, s, B_WIDTH).astype(dtype)


def complex_affine_combine(e1, e2):
    a1r, a1i, b1r, b1i = e1
    a2r, a2i, b2r, b2i = e2
    return (a2r * a1r - a2i * a1i, a2r * a1i + a2i * a1r,
            a2r * b1r - a2i * b1i + b2r, a2r * b1i + a2i * b1r + b2i)


def s5_branch(u, lam_re, lam_im, log_dt, b_re, b_im, c_re, c_im, d_skip, glu_w, glu_b):
    bn, s, _ = u.shape
    uf = u.astype(jnp.float32).reshape(bn, s, C_GROUPS, C_GROUP)
    dt = jnp.exp(log_dt.astype(jnp.float32))[:, None]
    lr, li = lam_re.astype(jnp.float32), lam_im.astype(jnp.float32)
    mag = jnp.exp(lr * dt)
    a_re, a_im = mag * jnp.cos(li * dt), mag * jnp.sin(li * dt)
    den = lr * lr + li * li
    f_re = ((a_re - 1.0) * lr + a_im * li) / den
    f_im = (a_im * lr - (a_re - 1.0) * li) / den
    bb_re = f_re[..., None] * b_re - f_im[..., None] * b_im
    bb_im = f_re[..., None] * b_im + f_im[..., None] * b_re
    bu_re = jnp.einsum('bsgi,gpi->bsgp', uf, bb_re)
    bu_im = jnp.einsum('bsgi,gpi->bsgp', uf, bb_im)
    elems = (jnp.broadcast_to(a_re, bu_re.shape), jnp.broadcast_to(a_im, bu_re.shape), bu_re, bu_im)
    _, _, x_re, x_im = lax.associative_scan(complex_affine_combine, elems, axis=1)
    y = jnp.einsum('bsgp,gip->bsgi', x_re, c_re) - jnp.einsum('bsgp,gip->bsgi', x_im, c_im) + d_skip * uf
    y = jax.nn.gelu(y.reshape(bn, s, C_WIDTH))
    out = y * jax.nn.sigmoid(y @ glu_w + glu_b)
    return out.astype(u.dtype)


def hybrid_mixer(x, rel_bias, w_in, a_kv_norm, a_kv_up, b_conv, b_a_log, b_dt_bias, b_out_norm,
                 c_lambda_re, c_lambda_im, c_log_dt, c_b_re, c_b_im, c_c_re, c_c_im, c_d, c_glu_w, c_glu_b,
                 w_br_a, w_br_b, w_br_c, w_o):
    bn, s, _ = x.shape
    h = jnp.einsum('bsd,de->bse', x, w_in)
    offsets = np.cumsum(SPLITS)[:-1].tolist()
    a_q, a_ckv, idx_q, idx_k, idx_w, b_qkv, b_z, b_a, b_b, c_u, gate_logits = jnp.split(h, offsets, axis=-1)
    ya = dsa_attention(a_q.reshape(bn, s, A_HEADS, A_HEAD_DIM), a_ckv,
                       idx_q.reshape(bn, s, IDX_HEADS, IDX_DIM), idx_k, idx_w * IDX_HEADS ** -0.5,
                       a_kv_norm, a_kv_up, rel_bias)
    yb = gated_deltanet(b_qkv, b_z, b_a, b_b, b_conv, b_a_log, b_dt_bias, b_out_norm)
    yc = s5_branch(c_u, c_lambda_re, c_lambda_im, c_log_dt, c_b_re, c_b_im, c_c_re, c_c_im, c_d, c_glu_w, c_glu_b)
    ga, gb, gc = jnp.split(jax.nn.sigmoid(gate_logits.astype(jnp.float32)).astype(x.dtype), 3, axis=-1)
    merged = ga * (ya @ w_br_a) + gb * (yb @ w_br_b) + gc * (yc @ w_br_c)
    return merged @ w_o


def moe_ffn(x, router_w, router_bias, exp_w_gu, exp_w_down, sh_w_gu, sh_w_down):
    bn, s, d = x.shape
    xt = x.reshape(bn * s, d)
    n_tok = xt.shape[0]
    scores = jax.nn.sigmoid(jnp.einsum('td,de->te', xt, router_w).astype(jnp.float32))
    _, top_idx = lax.top_k(scores + router_bias.astype(jnp.float32), TOP_K)
    top_s = jnp.take_along_axis(scores, top_idx, axis=-1)
    gates = top_s / jnp.sum(top_s, -1, keepdims=True) * ROUTED_SCALE
    e_flat = top_idx.reshape(-1)
    n_asg = e_flat.shape[0]
    order = jnp.argsort(e_flat)
    e_sorted = e_flat[order]
    counts = jax.ops.segment_sum(jnp.ones_like(e_flat), e_flat, num_segments=N_EXPERTS)
    starts = jnp.cumsum(counts) - counts
    padded = (counts + MOE_BLOCK - 1) // MOE_BLOCK * MOE_BLOCK
    pad_ends = jnp.cumsum(padded)
    dest = pad_ends[e_sorted] - padded[e_sorted] + jnp.arange(n_asg, dtype=jnp.int32) - starts[e_sorted]
    n_blk = n_asg // MOE_BLOCK + N_EXPERTS + 1
    n_pad = n_blk * MOE_BLOCK
    buf_tok = jnp.full((n_pad,), n_tok, jnp.int32).at[dest].set((order // TOP_K).astype(jnp.int32))
    buf_gate = jnp.zeros((n_pad,), jnp.float32).at[dest].set(gates.reshape(-1)[order])
    blk_expert = jnp.minimum(jnp.searchsorted(pad_ends, jnp.arange(n_blk, dtype=jnp.int32) * MOE_BLOCK, side='right'), N_EXPERTS - 1)
    x_pad = jnp.concatenate([xt, jnp.zeros((1, d), xt.dtype)], axis=0)

    def expert_block(acc, blk):
        tok, gate, e = blk
        hg, hu = jnp.split(x_pad[tok] @ exp_w_gu[e], 2, axis=-1)
        y = (jax.nn.silu(hg) * hu) @ exp_w_down[e]
        return acc.at[tok].add((y * gate[:, None]).astype(acc.dtype)), None

    routed, _ = lax.scan(expert_block, jnp.zeros_like(x_pad),
                         (buf_tok.reshape(n_blk, MOE_BLOCK), buf_gate.reshape(n_blk, MOE_BLOCK), blk_expert))
    sg, su = jnp.split(xt @ sh_w_gu, 2, axis=-1)
    shared = (jax.nn.silu(sg) * su) @ sh_w_down
    return (routed[:n_tok] + shared).reshape(bn, s, d)


def setup_inputs(seed: int = 0) -> dict:
    key = jax.random.key(seed)
    ks = jax.random.split(key, 40)
    L, G, P, GH = DEPTH, C_GROUPS, C_STATE, C_GROUP

    def nrm(k, shape, scale):
        return jax.random.normal(k, shape, jnp.float32) * scale

    dt_b = jnp.exp(jax.random.uniform(ks[7], (L, B_HEADS), minval=math.log(DT_MIN), maxval=math.log(DT_MAX)))
    return {
        'x': nrm(ks[0], (BATCH, SEQ, D_MODEL), 1.0),
        'rel_bias': nrm(ks[1], (REL_BUCKETS, A_HEADS), 0.5),
        'w_in': nrm(ks[2], (L, D_MODEL, D_IN), D_MODEL ** -0.5),
        'a_kv_norm': 1.0 + nrm(ks[3], (L, A_KV_RANK), 0.02),
        'a_kv_up': nrm(ks[4], (L, A_KV_RANK, 2 * A_WIDTH), A_KV_RANK ** -0.5),
        'b_conv': nrm(ks[5], (L, CONV_WIDTH, 3 * B_WIDTH), CONV_WIDTH ** -0.5),
        'b_a_log': jnp.log(jax.random.uniform(ks[6], (L, B_HEADS), minval=1.0, maxval=16.0)),
        'b_dt_bias': dt_b + jnp.log(-jnp.expm1(-dt_b)),
        'b_out_norm': 1.0 + nrm(ks[8], (L, B_HEAD_DIM), 0.02),
        'c_lambda_re': -0.5 + nrm(ks[9], (L, G, P), 0.01),
        'c_lambda_im': math.pi * jnp.arange(P, dtype=jnp.float32) + nrm(ks[10], (L, G, P), 0.01),
        'c_log_dt': jax.random.uniform(ks[11], (L, G), minval=math.log(DT_MIN), maxval=math.log(DT_MAX)),
        'c_b_re': nrm(ks[12], (L, G, P, GH), (2 * GH) ** -0.5),
        'c_b_im': nrm(ks[13], (L, G, P, GH), (2 * GH) ** -0.5),
        'c_c_re': nrm(ks[14], (L, G, GH, P), P ** -0.5),
        'c_c_im': nrm(ks[15], (L, G, GH, P), P ** -0.5),
        'c_d': nrm(ks[16], (L, G, GH), 1.0),
        'c_glu_w': nrm(ks[17], (L, C_WIDTH, C_WIDTH), C_WIDTH ** -0.5),
        'c_glu_b': nrm(ks[18], (L, C_WIDTH), 0.02),
        'w_br_a': nrm(ks[19], (L, A_WIDTH, D_MODEL), A_WIDTH ** -0.5),
        'w_br_b': nrm(ks[20], (L, B_WIDTH, D_MODEL), B_WIDTH ** -0.5),
        'w_br_c': nrm(ks[21], (L, C_WIDTH, D_MODEL), C_WIDTH ** -0.5),
        'w_o': nrm(ks[22], (L, D_MODEL, D_MODEL), D_MODEL ** -0.5 * DEEPNORM_BETA),
        'ln1_g': 1.0 + nrm(ks[23], (L, D_MODEL), 0.02),
        'ln1_b': nrm(ks[24], (L, D_MODEL), 0.02),
        'router_w': nrm(ks[25], (L, D_MODEL, N_EXPERTS), D_MODEL ** -0.5),
        'router_bias': nrm(ks[26], (L, N_EXPERTS), 0.01),
        'exp_w_gu': nrm(ks[27], (L, N_EXPERTS, D_MODEL, 2 * D_EXPERT), D_MODEL ** -0.5),
        'exp_w_down': nrm(ks[28], (L, N_EXPERTS, D_EXPERT, D_MODEL), D_EXPERT ** -0.5 * DEEPNORM_BETA),
        'sh_w_gu': nrm(ks[29], (L, D_MODEL, 2 * D_SHARED), D_MODEL ** -0.5),
        'sh_w_down': nrm(ks[30], (L, D_SHARED, D_MODEL), D_SHARED ** -0.5 * DEEPNORM_BETA),
        'ln2_g': 1.0 + nrm(ks[31], (L, D_MODEL), 0.02),
        'ln2_b': nrm(ks[32], (L, D_MODEL), 0.02),
    }


def reference(x, rel_bias, w_in, a_kv_norm, a_kv_up, b_conv, b_a_log, b_dt_bias, b_out_norm,
              c_lambda_re, c_lambda_im, c_log_dt, c_b_re, c_b_im, c_c_re, c_c_im, c_d, c_glu_w, c_glu_b,
              w_br_a, w_br_b, w_br_c, w_o, ln1_g, ln1_b,
              router_w, router_bias, exp_w_gu, exp_w_down, sh_w_gu, sh_w_down, ln2_g, ln2_b):
    for i in range(DEPTH):
        mix = hybrid_mixer(x, rel_bias, w_in[i], a_kv_norm[i], a_kv_up[i], b_conv[i], b_a_log[i], b_dt_bias[i],
                           b_out_norm[i], c_lambda_re[i], c_lambda_im[i], c_log_dt[i], c_b_re[i], c_b_im[i],
                           c_c_re[i], c_c_im[i], c_d[i], c_glu_w[i], c_glu_b[i], w_br_a[i], w_br_b[i], w_br_c[i], w_o[i])
        x = layer_norm(DEEPNORM_ALPHA * x + mix, ln1_g[i], ln1_b[i])
        ffn = moe_ffn(x, router_w[i], router_bias[i], exp_w_gu[i], exp_w_down[i], sh_w_gu[i], sh_w_down[i])
        x = layer_norm(DEEPNORM_ALPHA * x + ffn, ln2_g[i], ln2_b[i])
    return x
```

```python
import functools
import math

import jax
import jax.numpy as jnp
import numpy as np
from jax import lax
from jax.experimental import pallas as pl
from jax.experimental.pallas import tpu as pltpu

F32 = jnp.float32
BF16 = jnp.bfloat16

D_MODEL = 1024
DEPTH = 4
A_HEADS = 8
A_HEAD_DIM = 64
A_WIDTH = A_HEADS * A_HEAD_DIM
A_KV_RANK = 128
IDX_HEADS = 8
IDX_DIM = 64
TOPK_MAX = 256
Q_BLOCK = 128
REL_BUCKETS = 32
REL_MAX_DIST = 128
B_HEADS = 4
B_HEAD_DIM = 128
B_WIDTH = B_HEADS * B_HEAD_DIM
CONV_WIDTH = 4
CHUNK = 64
C_WIDTH = 512
C_GROUP = 16
C_GROUPS = C_WIDTH // C_GROUP
C_STATE = 64
N_EXPERTS = 64
TOP_K = 8
D_EXPERT = 256
D_SHARED = 256
ROUTED_SCALE = 2.5
MOE_BLOCK = 256
SPLITS = (A_WIDTH, A_KV_RANK, IDX_HEADS * IDX_DIM, IDX_DIM, IDX_HEADS, 3 * B_WIDTH, B_WIDTH, B_HEADS, B_HEADS, C_WIDTH, 3 * D_MODEL)
DEEPNORM_ALPHA = (2 * DEPTH) ** 0.25
LN_EPS = 1e-5
RMS_EPS = 1e-6
NEG_INF = -1e30

VMEM_LIMIT_BYTES = 48 * 1024 * 1024
LANE = 128


def _pick_tile(n, candidates):
    for c in candidates:
        if n % c == 0:
            return c
    return n


def _mm_kernel(a_ref, b_ref, o_ref):
    o_ref[...] = jnp.dot(a_ref[...].astype(BF16), b_ref[...], preferred_element_type=F32).astype(o_ref.dtype)


def matmul(a, b, out_dtype=F32):
    m, k = a.shape
    _, n = b.shape
    tm = _pick_tile(m, (1024, 512, 256, 128, 8))
    tn = _pick_tile(n, (1024, 768, 512, 384, 256, 128))
    return pl.pallas_call(
        _mm_kernel,
        grid=(m // tm, n // tn),
        in_specs=[pl.BlockSpec((tm, k), lambda i, j: (i, 0)), pl.BlockSpec((k, tn), lambda i, j: (0, j))],
        out_specs=pl.BlockSpec((tm, tn), lambda i, j: (i, j)),
        out_shape=jax.ShapeDtypeStruct((m, n), out_dtype),
        compiler_params=pltpu.CompilerParams(dimension_semantics=("arbitrary", "arbitrary"), vmem_limit_bytes=VMEM_LIMIT_BYTES),
        name="matmul",
    )(a, b.astype(BF16))


def _res_ln_kernel(x_ref, y_ref, g_ref, b_ref, o_ref):
    z = DEEPNORM_ALPHA * x_ref[...] + y_ref[...]
    mu = jnp.mean(z, -1, keepdims=True)
    zc = z - mu
    var = jnp.mean(zc * zc, -1, keepdims=True)
    o_ref[...] = zc * lax.rsqrt(var + LN_EPS) * g_ref[...] + b_ref[...]


def residual_layer_norm(x, y, g, b):
    t, d = x.shape
    tm = _pick_tile(t, (512, 256, 128, 8))
    row = pl.BlockSpec((tm, d), lambda i: (i, 0))
    vec = pl.BlockSpec((1, d), lambda i: (0, 0))
    return pl.pallas_call(
        _res_ln_kernel,
        grid=(t // tm,),
        in_specs=[row, row, vec, vec],
        out_specs=row,
        out_shape=jax.ShapeDtypeStruct((t, d), F32),
        compiler_params=pltpu.CompilerParams(dimension_semantics=("arbitrary",), vmem_limit_bytes=VMEM_LIMIT_BYTES),
        name="residual_layer_norm",
    )(x, y, g.reshape(1, d), b.reshape(1, d))


def rms_norm(x, g):
    return x * lax.rsqrt(jnp.mean(x * x, -1, keepdims=True) + RMS_EPS) * g


def l2_normalize(x):
    return x * lax.rsqrt(jnp.sum(x * x, -1, keepdims=True) + RMS_EPS)


def rel_bucket(dist):
    n_exact = REL_BUCKETS // 2
    d = jnp.maximum(dist, 1).astype(F32)
    large = n_exact + (jnp.log(d / n_exact) / math.log(REL_MAX_DIST / n_exact) * (REL_BUCKETS - n_exact)).astype(jnp.int32)
    return jnp.where(dist < n_exact, dist, jnp.minimum(large, REL_BUCKETS - 1))


def dsa_attention(q, c_kv, q_idx, k_idx, w_idx, kv_norm_g, w_kv_up, rel_bias):
    bn, s = q.shape[0], q.shape[1]
    k_sel = min(TOPK_MAX, s // 4)
    kv = matmul(rms_norm(c_kv, kv_norm_g).reshape(bn * s, A_KV_RANK), w_kv_up).reshape(bn, s, 2, A_HEADS, A_HEAD_DIM)
    k, v = kv[:, :, 0], kv[:, :, 1]
    nb = s // Q_BLOCK
    key_pos = jnp.arange(s, dtype=jnp.int32)

    def to_blocks(t):
        return jnp.moveaxis(t.reshape(bn, nb, Q_BLOCK, *t.shape[2:]), 1, 0)

    def one_block(args):
        qb, qib, wb, tpos = args
        s_idx = jnp.einsum('bqhd,bsd->bqhs', qib, k_idx) * IDX_DIM ** -0.5
        score = jnp.einsum('bqh,bqhs->bqs', wb, jax.nn.relu(s_idx)).astype(F32)
        visible = key_pos[None, :] <= tpos[:, None]
        score = jnp.where(visible[None], score, -jnp.inf)
        _, sel = lax.top_k(score, k_sel)
        k_g = jax.vmap(lambda kb, ib: kb[ib])(k, sel)
        v_g = jax.vmap(lambda vb, ib: vb[ib])(v, sel)
        dist = tpos[None, :, None] - sel
        bias = jnp.moveaxis(rel_bias[rel_bucket(jnp.maximum(dist, 0))], 3, 1)
        logits = jnp.einsum('bqhd,bqkhd->bhqk', qb, k_g).astype(F32) * A_HEAD_DIM ** -0.5 + bias
        logits = jnp.where((dist >= 0)[:, None], logits, NEG_INF)
        p = jax.nn.softmax(logits, axis=-1).astype(v.dtype)
        return jnp.einsum('bhqk,bqkhd->bqhd', p, v_g)

    out = lax.map(one_block, (to_blocks(q), to_blocks(q_idx), to_blocks(w_idx), key_pos.reshape(nb, Q_BLOCK)))
    return jnp.moveaxis(out, 0, 1).reshape(bn, s, A_WIDTH)


def causal_depthwise_conv(x, w):
    s = x.shape[1]
    xp = jnp.pad(x, ((0, 0), (CONV_WIDTH - 1, 0), (0, 0)))
    return sum(xp[:, j:j + s] * w[j] for j in range(CONV_WIDTH))


def chunk_gated_delta_rule(q, k, v, g, beta):
    bn, s, h, dk = q.shape
    dv = v.shape[-1]
    n = s // CHUNK

    def chunks(t):
        return jnp.moveaxis(t.reshape(bn, n, CHUNK, h, -1), 3, 1)

    q, k, v = chunks(q), chunks(k), chunks(v)
    gc = jnp.cumsum(chunks(g[..., None])[..., 0], axis=-1)
    beta = chunks(beta[..., None])[..., 0]
    causal = jnp.tril(jnp.ones((CHUNK, CHUNK), bool))
    strict = jnp.tril(jnp.ones((CHUNK, CHUNK), bool), -1)
    decay = jnp.exp(jnp.where(causal, gc[..., :, None] - gc[..., None, :], -jnp.inf))
    k_beta = k * beta[..., None]
    m = jnp.where(strict, jnp.einsum('bhnid,bhnjd->bhnij', k_beta, k) * decay, 0.0)
    eye = jnp.eye(CHUNK, dtype=F32)
    t_inv = lax.linalg.triangular_solve(eye + m, jnp.broadcast_to(eye, m.shape), left_side=True, lower=True)
    u = t_inv @ (v * beta[..., None])
    w = t_inv @ (k_beta * jnp.exp(gc)[..., None])
    qk = jnp.einsum('bhnid,bhnjd->bhnij', q, k) * decay
    q_dec = q * jnp.exp(gc)[..., None]
    k_dec = k * jnp.exp(gc[..., -1:] - gc)[..., None]
    g_last = jnp.exp(gc[..., -1])

    def step(state, inp):
        u_n, w_n, qd_n, kd_n, qk_n, gl_n = inp
        v_new = u_n - w_n @ state
        o_n = qd_n @ state + qk_n @ v_new
        state = state * gl_n[..., None, None] + jnp.swapaxes(kd_n, -1, -2) @ v_new
        return state, o_n

    xs = tuple(jnp.moveaxis(t, 2, 0) for t in (u, w, q_dec, k_dec, qk, g_last))
    _, o = lax.scan(step, jnp.zeros((bn, h, dk, dv), F32), xs)
    o = jnp.moveaxis(jnp.moveaxis(o, 0, 2), 1, 3)
    return o.reshape(bn, s, h, dv)


def gated_deltanet(qkv, z, a, b, conv_w, a_log, dt_bias, out_norm_g):
    bn, s, _ = qkv.shape
    qkv = jax.nn.silu(causal_depthwise_conv(qkv, conv_w))
    q, k, v = [t.reshape(bn, s, B_HEADS, B_HEAD_DIM) for t in jnp.split(qkv, 3, axis=-1)]
    q = l2_normalize(q) * B_HEAD_DIM ** -0.5
    k = l2_normalize(k)
    beta = jax.nn.sigmoid(b)
    g = -jnp.exp(a_log) * jax.nn.softplus(a + dt_bias)
    o = chunk_gated_delta_rule(q, k, v, g, beta)
    o = rms_norm(o, out_norm_g) * jax.nn.silu(z).reshape(bn, s, B_HEADS, B_HEAD_DIM)
    return o.reshape(bn, s, B_WIDTH)


def complex_affine_combine(e1, e2):
    a1r, a1i, b1r, b1i = e1
    a2r, a2i, b2r, b2i = e2
    return (a2r * a1r - a2i * a1i, a2r * a1i + a2i * a1r,
            a2r * b1r - a2i * b1i + b2r, a2r * b1i + a2i * b1r + b2i)


def s5_branch(u, lam_re, lam_im, log_dt, b_re, b_im, c_re, c_im, d_skip, glu_w, glu_b):
    bn, s, _ = u.shape
    uf = u.reshape(bn, s, C_GROUPS, C_GROUP)
    dt = jnp.exp(log_dt)[:, None]
    lr, li = lam_re, lam_im
    mag = jnp.exp(lr * dt)
    a_re, a_im = mag * jnp.cos(li * dt), mag * jnp.sin(li * dt)
    den = lr * lr + li * li
    f_re = ((a_re - 1.0) * lr + a_im * li) / den
    f_im = (a_im * lr - (a_re - 1.0) * li) / den
    bb_re = f_re[..., None] * b_re - f_im[..., None] * b_im
    bb_im = f_re[..., None] * b_im + f_im[..., None] * b_re
    bu_re = jnp.einsum('bsgi,gpi->bsgp', uf, bb_re)
    bu_im = jnp.einsum('bsgi,gpi->bsgp', uf, bb_im)
    elems = (jnp.broadcast_to(a_re, bu_re.shape), jnp.broadcast_to(a_im, bu_re.shape), bu_re, bu_im)
    _, _, x_re, x_im = lax.associative_scan(complex_affine_combine, elems, axis=1)
    y = jnp.einsum('bsgp,gip->bsgi', x_re, c_re) - jnp.einsum('bsgp,gip->bsgi', x_im, c_im) + d_skip * uf
    y = jax.nn.gelu(y.reshape(bn, s, C_WIDTH))
    gate = matmul(y.reshape(bn * s, C_WIDTH), glu_w).reshape(bn, s, C_WIDTH)
    return y * jax.nn.sigmoid(gate + glu_b)


def hybrid_mixer(x, rel_bias, w_in, a_kv_norm, a_kv_up, b_conv, b_a_log, b_dt_bias, b_out_norm,
                 c_lambda_re, c_lambda_im, c_log_dt, c_b_re, c_b_im, c_c_re, c_c_im, c_d, c_glu_w, c_glu_b,
                 w_br_a, w_br_b, w_br_c, w_o):
    bn, s, d = x.shape
    t = bn * s
    d_in = w_in.shape[1]
    pad = -d_in % LANE
    h = matmul(x.reshape(t, d), jnp.pad(w_in, ((0, 0), (0, pad))))[:, :d_in].reshape(bn, s, d_in)
    offsets = np.cumsum(SPLITS)[:-1].tolist()
    a_q, a_ckv, idx_q, idx_k, idx_w, b_qkv, b_z, b_a, b_b, c_u, gate_logits = jnp.split(h, offsets, axis=-1)
    ya = dsa_attention(a_q.reshape(bn, s, A_HEADS, A_HEAD_DIM), a_ckv,
                       idx_q.reshape(bn, s, IDX_HEADS, IDX_DIM), idx_k, idx_w * IDX_HEADS ** -0.5,
                       a_kv_norm, a_kv_up, rel_bias)
    yb = gated_deltanet(b_qkv, b_z, b_a, b_b, b_conv, b_a_log, b_dt_bias, b_out_norm)
    yc = s5_branch(c_u, c_lambda_re, c_lambda_im, c_log_dt, c_b_re, c_b_im, c_c_re, c_c_im, c_d, c_glu_w, c_glu_b)
    ga, gb, gc = jnp.split(jax.nn.sigmoid(gate_logits).reshape(t, 3 * d), 3, axis=-1)
    merged = (ga * matmul(ya.reshape(t, A_WIDTH), w_br_a) + gb * matmul(yb.reshape(t, B_WIDTH), w_br_b)
              + gc * matmul(yc.reshape(t, C_WIDTH), w_br_c))
    return matmul(merged, w_o)


def moe_ffn(xt, router_w, router_bias, exp_w_gu, exp_w_down, sh_w_gu, sh_w_down):
    n_tok, d = xt.shape
    scores = jax.nn.sigmoid(jnp.einsum('td,de->te', xt, router_w).astype(F32))
    _, top_idx = lax.top_k(scores + router_bias, TOP_K)
    top_s = jnp.take_along_axis(scores, top_idx, axis=-1)
    gates = top_s / jnp.sum(top_s, -1, keepdims=True) * ROUTED_SCALE
    e_flat = top_idx.reshape(-1)
    n_asg = e_flat.shape[0]
    order = jnp.argsort(e_flat)
    e_sorted = e_flat[order]
    counts = jax.ops.segment_sum(jnp.ones_like(e_flat), e_flat, num_segments=N_EXPERTS)
    starts = jnp.cumsum(counts) - counts
    padded = (counts + MOE_BLOCK - 1) // MOE_BLOCK * MOE_BLOCK
    pad_ends = jnp.cumsum(padded)
    dest = pad_ends[e_sorted] - padded[e_sorted] + jnp.arange(n_asg, dtype=jnp.int32) - starts[e_sorted]
    n_blk = n_asg // MOE_BLOCK + N_EXPERTS + 1
    n_pad = n_blk * MOE_BLOCK
    buf_tok = jnp.full((n_pad,), n_tok, jnp.int32).at[dest].set((order // TOP_K).astype(jnp.int32))
    buf_gate = jnp.zeros((n_pad,), F32).at[dest].set(gates.reshape(-1)[order])
    blk_expert = jnp.minimum(jnp.searchsorted(pad_ends, jnp.arange(n_blk, dtype=jnp.int32) * MOE_BLOCK, side='right'), N_EXPERTS - 1)
    x_pad = jnp.concatenate([xt, jnp.zeros((1, d), xt.dtype)], axis=0)

    def expert_block(acc, blk):
        tok, gate, e = blk
        hg, hu = jnp.split(x_pad[tok] @ exp_w_gu[e], 2, axis=-1)
        y = (jax.nn.silu(hg) * hu) @ exp_w_down[e]
        return acc.at[tok].add((y * gate[:, None]).astype(acc.dtype)), None

    routed, _ = lax.scan(expert_block, jnp.zeros_like(x_pad),
                         (buf_tok.reshape(n_blk, MOE_BLOCK), buf_gate.reshape(n_blk, MOE_BLOCK), blk_expert))
    sg, su = jnp.split(matmul(xt, sh_w_gu), 2, axis=-1)
    shared = matmul(jax.nn.silu(sg) * su, sh_w_down)
    return routed[:n_tok] + shared


def kernel(x, rel_bias, w_in, a_kv_norm, a_kv_up, b_conv, b_a_log, b_dt_bias, b_out_norm, c_lambda_re, c_lambda_im, c_log_dt, c_b_re, c_b_im, c_c_re, c_c_im, c_d, c_glu_w, c_glu_b, w_br_a, w_br_b, w_br_c, w_o, ln1_g, ln1_b, router_w, router_bias, exp_w_gu, exp_w_down, sh_w_gu, sh_w_down, ln2_g, ln2_b):
    bn, s, d = x.shape
    t = bn * s
    for i in range(w_in.shape[0]):
        mix = hybrid_mixer(x, rel_bias, w_in[i], a_kv_norm[i], a_kv_up[i], b_conv[i], b_a_log[i], b_dt_bias[i],
                           b_out_norm[i], c_lambda_re[i], c_lambda_im[i], c_log_dt[i], c_b_re[i], c_b_im[i],
                           c_c_re[i], c_c_im[i], c_d[i], c_glu_w[i], c_glu_b[i], w_br_a[i], w_br_b[i], w_br_c[i], w_o[i])
        xt = residual_layer_norm(x.reshape(t, d), mix, ln1_g[i], ln1_b[i])
        ffn = moe_ffn(xt, router_w[i], router_bias[i], exp_w_gu[i], exp_w_down[i], sh_w_gu[i], sh_w_down[i])
        x = residual_layer_norm(xt, ffn, ln2_g[i], ln2_b[i]).reshape(bn, s, d)
    return x
```

```python
import functools
import math

import jax
import jax.numpy as jnp
import numpy as np
from jax import lax
from jax.experimental import pallas as pl
from jax.experimental.pallas import tpu as pltpu

F32 = jnp.float32
BF16 = jnp.bfloat16

D_MODEL = 1024
DEPTH = 4
A_HEADS = 8
A_HEAD_DIM = 64
A_WIDTH = A_HEADS * A_HEAD_DIM
A_KV_RANK = 128
IDX_HEADS = 8
IDX_DIM = 64
TOPK_MAX = 256
Q_BLOCK = 128
REL_BUCKETS = 32
REL_MAX_DIST = 128
B_HEADS = 4
B_HEAD_DIM = 128
B_WIDTH = B_HEADS * B_HEAD_DIM
CONV_WIDTH = 4
CHUNK = 64
C_WIDTH = 512
C_GROUP = 16
C_GROUPS = C_WIDTH // C_GROUP
C_STATE = 64
N_EXPERTS = 64
TOP_K = 8
D_EXPERT = 256
D_SHARED = 256
ROUTED_SCALE = 2.5
MOE_BLOCK = 256
SPLITS = (A_WIDTH, A_KV_RANK, IDX_HEADS * IDX_DIM, IDX_DIM, IDX_HEADS, 3 * B_WIDTH, B_WIDTH, B_HEADS, B_HEADS, C_WIDTH, 3 * D_MODEL)
DEEPNORM_ALPHA = (2 * DEPTH) ** 0.25
LN_EPS = 1e-5
RMS_EPS = 1e-6
NEG_INF = -1e30

VMEM_LIMIT_BYTES = 48 * 1024 * 1024
LANE = 128


def _pick_tile(n, candidates):
    for c in candidates:
        if n % c == 0:
            return c
    return n


def _mm_kernel(a_ref, b_ref, o_ref):
    o_ref[...] = jnp.dot(a_ref[...].astype(BF16), b_ref[...], preferred_element_type=F32).astype(o_ref.dtype)


def matmul(a, b, out_dtype=F32):
    m, k = a.shape
    _, n = b.shape
    tm = _pick_tile(m, (1024, 512, 256, 128, 8))
    tn = _pick_tile(n, (1024, 768, 512, 384, 256, 128))
    return pl.pallas_call(
        _mm_kernel,
        grid=(m // tm, n // tn),
        in_specs=[pl.BlockSpec((tm, k), lambda i, j: (i, 0)), pl.BlockSpec((k, tn), lambda i, j: (0, j))],
        out_specs=pl.BlockSpec((tm, tn), lambda i, j: (i, j)),
        out_shape=jax.ShapeDtypeStruct((m, n), out_dtype),
        compiler_params=pltpu.CompilerParams(dimension_semantics=("arbitrary", "arbitrary"), vmem_limit_bytes=VMEM_LIMIT_BYTES),
        name="matmul",
    )(a, b.astype(BF16))


def _res_ln_kernel(x_ref, y_ref, g_ref, b_ref, o_ref):
    z = DEEPNORM_ALPHA * x_ref[...] + y_ref[...]
    mu = jnp.mean(z, -1, keepdims=True)
    zc = z - mu
    var = jnp.mean(zc * zc, -1, keepdims=True)
    o_ref[...] = zc * lax.rsqrt(var + LN_EPS) * g_ref[...] + b_ref[...]


def residual_layer_norm(x, y, g, b):
    t, d = x.shape
    tm = _pick_tile(t, (512, 256, 128, 8))
    row = pl.BlockSpec((tm, d), lambda i: (i, 0))
    vec = pl.BlockSpec((1, d), lambda i: (0, 0))
    return pl.pallas_call(
        _res_ln_kernel,
        grid=(t // tm,),
        in_specs=[row, row, vec, vec],
        out_specs=row,
        out_shape=jax.ShapeDtypeStruct((t, d), F32),
        compiler_params=pltpu.CompilerParams(dimension_semantics=("arbitrary",), vmem_limit_bytes=VMEM_LIMIT_BYTES),
        name="residual_layer_norm",
    )(x, y, g.reshape(1, d), b.reshape(1, d))


def rms_norm(x, g):
    return x * lax.rsqrt(jnp.mean(x * x, -1, keepdims=True) + RMS_EPS) * g


def l2_normalize(x):
    return x * lax.rsqrt(jnp.sum(x * x, -1, keepdims=True) + RMS_EPS)


def rel_bucket(dist):
    n_exact = REL_BUCKETS // 2
    d = jnp.maximum(dist, 1).astype(F32)
    large = n_exact + (jnp.log(d / n_exact) / math.log(REL_MAX_DIST / n_exact) * (REL_BUCKETS - n_exact)).astype(jnp.int32)
    return jnp.where(dist < n_exact, dist, jnp.minimum(large, REL_BUCKETS - 1))


KEY_BLOCK = 128
DSA_CHUNK = 512
BLOCKS_PER_CHUNK = DSA_CHUNK // KEY_BLOCK
HEAD_PAIRS = A_HEADS // 2
PAIR_WIDTH = 2 * A_HEAD_DIM
INT_MIN = -2 ** 31


def _kv_kernel(c_ref, g_ref, wk_ref, wvt_ref, k_ref, vt_ref):
    c = c_ref[...]
    cn = (c * lax.rsqrt(jnp.mean(c * c, -1, keepdims=True) + RMS_EPS) * g_ref[...]).astype(BF16)
    k_ref[...] = jnp.dot(cn, wk_ref[...], preferred_element_type=F32).astype(BF16)
    vt = lax.dot_general(wvt_ref[...], cn, (((1,), (1,)), ((), ())), preferred_element_type=F32).astype(BF16)
    for i in range(vt_ref.shape[1]):
        vt_ref[0, i] = vt[:, i * KEY_BLOCK:(i + 1) * KEY_BLOCK]


def kv_project(c_kv, g, w_kv_up, bn, s):
    ts = 512
    w = w_kv_up.astype(BF16)
    return pl.pallas_call(
        _kv_kernel,
        grid=(bn, s // ts),
        in_specs=[pl.BlockSpec((ts, A_KV_RANK), lambda b, i: (b * (s // ts) + i, 0)),
                  pl.BlockSpec((1, A_KV_RANK), lambda b, i: (0, 0)),
                  pl.BlockSpec((A_KV_RANK, A_WIDTH), lambda b, i: (0, 0)),
                  pl.BlockSpec((A_WIDTH, A_KV_RANK), lambda b, i: (0, 0))],
        out_specs=[pl.BlockSpec((None, ts, A_WIDTH), lambda b, i: (b, i, 0)),
                   pl.BlockSpec((1, ts // KEY_BLOCK, A_WIDTH, KEY_BLOCK), lambda b, i: (b, i, 0, 0))],
        out_shape=[jax.ShapeDtypeStruct((bn, s, A_WIDTH), BF16),
                   jax.ShapeDtypeStruct((bn, s // KEY_BLOCK, A_WIDTH, KEY_BLOCK), BF16)],
        compiler_params=pltpu.CompilerParams(dimension_semantics=("arbitrary", "arbitrary"), vmem_limit_bytes=VMEM_LIMIT_BYTES),
        name="kv_project",
    )(c_kv, g.reshape(1, A_KV_RANK), w[:, :A_WIDTH], w[:, A_WIDTH:].T)


def _mm_nt_kernel(wt_ref, x_ref, o_ref):
    o_ref[0] = lax.dot_general(wt_ref[...], x_ref[...].astype(BF16), (((1,), (1,)), ((), ())),
                               preferred_element_type=F32).astype(o_ref.dtype)


def matmul_nt(wt, x, bn, s, out_dtype=BF16):
    n, k = wt.shape
    ts = 512
    return pl.pallas_call(
        _mm_nt_kernel,
        grid=(bn, s // ts),
        in_specs=[pl.BlockSpec((n, k), lambda b, i: (0, 0)),
                  pl.BlockSpec((ts, k), lambda b, i: (b * (s // ts) + i, 0))],
        out_specs=pl.BlockSpec((1, n, ts), lambda b, i: (b, 0, i)),
        out_shape=jax.ShapeDtypeStruct((bn, n, s), out_dtype),
        compiler_params=pltpu.CompilerParams(dimension_semantics=("arbitrary", "arbitrary"), vmem_limit_bytes=VMEM_LIMIT_BYTES),
        name="matmul_nt",
    )(wt.astype(BF16), x)


def _sortable_key(score):
    bits = pltpu.bitcast(score, jnp.int32)
    return bits ^ (lax.shift_right_arithmetic(bits, 31) & jnp.int32(0x7FFFFFFF))


def _dsa_kernel(qt_ref, iqt_ref, wt_ref, kidx_ref, k_ref, vt_ref, bias_ref, o_ref,
                skey_ref, widx_ref, wq_ref, acc_ref, m_ref, l_ref, *, k_sel):
    j = pl.program_id(1)
    n_chunks = (j + BLOCKS_PER_CHUNK) // BLOCKS_PER_CHUNK
    q_pos = j * Q_BLOCK + lax.broadcasted_iota(jnp.int32, (1, Q_BLOCK), 1)

    for h in range(IDX_HEADS):
        widx_ref[:, h * Q_BLOCK:(h + 1) * Q_BLOCK] = iqt_ref[0, h * IDX_DIM:(h + 1) * IDX_DIM, :]
    wq_ref[...] = jnp.zeros_like(wq_ref)
    for p in range(HEAD_PAIRS):
        wq_ref[p, 0:A_HEAD_DIM, 0:Q_BLOCK] = qt_ref[0, (2 * p) * A_HEAD_DIM:(2 * p + 1) * A_HEAD_DIM, :]
        wq_ref[p, A_HEAD_DIM:PAIR_WIDTH, Q_BLOCK:2 * Q_BLOCK] = qt_ref[0, (2 * p + 1) * A_HEAD_DIM:(2 * p + 2) * A_HEAD_DIM, :]

    w_rows = [wt_ref[0, h:h + 1, :] for h in range(IDX_HEADS)]

    def idx_chunk(c, carry):
        start = pl.multiple_of(c * DSA_CHUNK, DSA_CHUNK)
        kc = kidx_ref[0, pl.ds(start, DSA_CHUNK), :]
        score = jnp.zeros((DSA_CHUNK, Q_BLOCK), F32)
        for p in range(IDX_HEADS // 2):
            z = jnp.dot(kc, widx_ref[:, p * 2 * Q_BLOCK:(p + 1) * 2 * Q_BLOCK], preferred_element_type=F32)
            score = score + jnp.maximum(z[:, :Q_BLOCK], 0.0) * w_rows[2 * p]
            score = score + jnp.maximum(z[:, Q_BLOCK:], 0.0) * w_rows[2 * p + 1]
        pos = start + lax.broadcasted_iota(jnp.int32, (DSA_CHUNK, Q_BLOCK), 0)
        skey_ref[pl.ds(start, DSA_CHUNK), :] = jnp.where(pos <= q_pos, _sortable_key(score), INT_MIN)
        return carry

    lax.fori_loop(0, n_chunks, idx_chunk, 0)

    def count_ge(cand):
        def body(c, cnt):
            start = pl.multiple_of(c * DSA_CHUNK, DSA_CHUNK)
            hit = (skey_ref[pl.ds(start, DSA_CHUNK), :] >= cand).astype(jnp.int32)
            return cnt + jnp.sum(hit.reshape(DSA_CHUNK // 8, 8, Q_BLOCK), axis=0)
        cnt = lax.fori_loop(0, n_chunks, body, jnp.zeros((8, Q_BLOCK), jnp.int32))
        return jnp.sum(cnt, axis=0, keepdims=True)

    def bit_step(i, thr_biased):
        cand_biased = thr_biased | lax.shift_left(jnp.int32(1), 31 - i)
        return jnp.where(count_ge(cand_biased ^ INT_MIN) >= k_sel, cand_biased, thr_biased)

    thr = lax.fori_loop(0, 32, bit_step, jnp.zeros((1, Q_BLOCK), jnp.int32)) ^ INT_MIN

    n_gt = count_ge(thr + 1)
    n_ge = count_ge(thr)
    tied = (n_ge > k_sel) & (thr > INT_MIN)

    @pl.when(jnp.max(tied.astype(jnp.int32)) > 0)
    def _():
        need = k_sel - n_gt

        def count_tied_upto(limit):
            def body(c, cnt):
                start = pl.multiple_of(c * DSA_CHUNK, DSA_CHUNK)
                pos = start + lax.broadcasted_iota(jnp.int32, (DSA_CHUNK, Q_BLOCK), 0)
                hit = jnp.where(skey_ref[pl.ds(start, DSA_CHUNK), :] == thr, (pos <= limit).astype(jnp.int32), 0)
                return cnt + jnp.sum(hit.reshape(DSA_CHUNK // 8, 8, Q_BLOCK), axis=0)
            cnt = lax.fori_loop(0, n_chunks, body, jnp.zeros((8, Q_BLOCK), jnp.int32))
            return jnp.sum(cnt, axis=0, keepdims=True)

        def pos_step(i, lim):
            cand = lim - lax.shift_left(jnp.int32(1), 30 - i)
            return jnp.where(count_tied_upto(cand) >= need, cand, lim)

        limit = lax.fori_loop(0, 31, pos_step, jnp.full((1, Q_BLOCK), 2 ** 31 - 1, jnp.int32))

        def demote(c, carry):
            start = pl.multiple_of(c * DSA_CHUNK, DSA_CHUNK)
            pos = start + lax.broadcasted_iota(jnp.int32, (DSA_CHUNK, Q_BLOCK), 0)
            sk = skey_ref[pl.ds(start, DSA_CHUNK), :]
            skey_ref[pl.ds(start, DSA_CHUNK), :] = jnp.where(tied & (sk == thr) & (pos > limit), thr - 1, sk)
            return carry

        lax.fori_loop(0, n_chunks, demote, 0)

    m_ref[...] = jnp.full_like(m_ref, NEG_INF)
    l_ref[...] = jnp.zeros_like(l_ref)
    acc_ref[...] = jnp.zeros_like(acc_ref)

    def attend(s_list, vt_of, n_blocks):
        for p in range(HEAD_PAIRS):
            s = s_list[p]
            m_old = m_ref[p]
            m_new = jnp.maximum(m_old, jnp.max(s, axis=0, keepdims=True))
            alpha = jnp.exp(m_old - m_new)
            pexp = jnp.exp(s - m_new)
            l_ref[p] = l_ref[p] * alpha + jnp.sum(pexp, axis=0, keepdims=True)
            pb = pexp.astype(BF16)
            pv = jnp.dot(vt_of(p, 0), pb[0:KEY_BLOCK], preferred_element_type=F32)
            for i in range(1, n_blocks):
                pv = pv + jnp.dot(vt_of(p, i), pb[i * KEY_BLOCK:(i + 1) * KEY_BLOCK], preferred_element_type=F32)
            acc_ref[p] = acc_ref[p] * alpha + pv
            m_ref[p] = m_new

    far_end = (j - 1) * KEY_BLOCK
    n_far = (jnp.maximum(j - 1, 0) + BLOCKS_PER_CHUNK - 1) // BLOCKS_PER_CHUNK

    def far_chunk(c, carry):
        start = pl.multiple_of(c * DSA_CHUNK, DSA_CHUNK)
        pos = start + lax.broadcasted_iota(jnp.int32, (DSA_CHUNK, Q_BLOCK), 0)
        sel = (skey_ref[pl.ds(start, DSA_CHUNK), :] >= thr) & (pos < far_end)
        madd = jnp.where(sel, 0.0, NEG_INF)
        madd2 = jnp.concatenate([madd, madd], axis=1)
        kc = k_ref[0, pl.ds(start, DSA_CHUNK), :]
        s_list = [jnp.dot(kc[:, p * PAIR_WIDTH:(p + 1) * PAIR_WIDTH], wq_ref[p], preferred_element_type=F32) + madd2
                  for p in range(HEAD_PAIRS)]
        blk0 = c * BLOCKS_PER_CHUNK
        attend(s_list, lambda p, i: vt_ref[0, blk0 + i, p * PAIR_WIDTH:(p + 1) * PAIR_WIDTH, :], BLOCKS_PER_CHUNK)
        return carry

    lax.fori_loop(0, n_far, far_chunk, 0)

    def near_block(kb, delta):
        start = pl.multiple_of(kb * KEY_BLOCK, KEY_BLOCK)
        pos = start + lax.broadcasted_iota(jnp.int32, (KEY_BLOCK, Q_BLOCK), 0)
        sel = (skey_ref[pl.ds(start, KEY_BLOCK), :] >= thr) & (pos <= q_pos)
        madd = jnp.where(sel, 0.0, NEG_INF)
        madd2 = jnp.concatenate([madd, madd], axis=1)
        kc = k_ref[0, pl.ds(start, KEY_BLOCK), :]
        s_list = [jnp.dot(kc[:, p * PAIR_WIDTH:(p + 1) * PAIR_WIDTH], wq_ref[p], preferred_element_type=F32)
                  + bias_ref[delta, p] + madd2 for p in range(HEAD_PAIRS)]
        attend(s_list, lambda p, i: vt_ref[0, kb, p * PAIR_WIDTH:(p + 1) * PAIR_WIDTH, :], 1)

    @pl.when(j > 0)
    def _():
        near_block(j - 1, 1)

    near_block(j, 0)

    outs = []
    for p in range(HEAD_PAIRS):
        o = acc_ref[p] / l_ref[p]
        outs.append(o[0:A_HEAD_DIM, 0:Q_BLOCK])
        outs.append(o[A_HEAD_DIM:PAIR_WIDTH, Q_BLOCK:2 * Q_BLOCK])
    o_ref[0] = jnp.concatenate(outs, axis=0).T.astype(o_ref.dtype)


def _rel_bias_tiles(rel_bias):
    tab = rel_bias[rel_bucket(jnp.arange(2 * Q_BLOCK, dtype=jnp.int32))] - rel_bias[REL_BUCKETS - 1]
    key = jnp.arange(KEY_BLOCK)[:, None]
    qry = jnp.arange(Q_BLOCK)[None, :]
    tiles = []
    for delta in (0, 1):
        dist = jnp.maximum(delta * Q_BLOCK + qry - key, 0)
        t = jnp.moveaxis(tab[dist], 2, 0)
        t = t.reshape(HEAD_PAIRS, 2, KEY_BLOCK, Q_BLOCK)
        tiles.append(jnp.concatenate([t[:, 0], t[:, 1]], axis=-1))
    return jnp.stack(tiles)


def dsa_attention(qiq_t, wt, k_idx, k, vt, bias_tiles, bn, s):
    assert A_WIDTH == IDX_HEADS * IDX_DIM
    k_sel = min(TOPK_MAX, s // 4)
    nb = s // Q_BLOCK
    s_pad = s + DSA_CHUNK
    return pl.pallas_call(
        functools.partial(_dsa_kernel, k_sel=k_sel),
        grid=(bn, nb),
        in_specs=[pl.BlockSpec((1, A_WIDTH, Q_BLOCK), lambda b, j: (b, 0, j)),
                  pl.BlockSpec((1, IDX_HEADS * IDX_DIM, Q_BLOCK), lambda b, j: (b, 1, j)),
                  pl.BlockSpec((1, IDX_HEADS, Q_BLOCK), lambda b, j: (b, 0, j)),
                  pl.BlockSpec((1, s, IDX_DIM), lambda b, j: (b, 0, 0)),
                  pl.BlockSpec((1, s, A_WIDTH), lambda b, j: (b, 0, 0)),
                  pl.BlockSpec((1, s // KEY_BLOCK, A_WIDTH, KEY_BLOCK), lambda b, j: (b, 0, 0, 0)),
                  pl.BlockSpec((2, HEAD_PAIRS, KEY_BLOCK, 2 * Q_BLOCK), lambda b, j: (0, 0, 0, 0))],
        out_specs=pl.BlockSpec((1, Q_BLOCK, A_WIDTH), lambda b, j: (b, j, 0)),
        out_shape=jax.ShapeDtypeStruct((bn, s, A_WIDTH), BF16),
        scratch_shapes=[pltpu.VMEM((s_pad, Q_BLOCK), jnp.int32),
                        pltpu.VMEM((IDX_DIM, IDX_HEADS * Q_BLOCK), BF16),
                        pltpu.VMEM((HEAD_PAIRS, PAIR_WIDTH, 2 * Q_BLOCK), BF16),
                        pltpu.VMEM((HEAD_PAIRS, PAIR_WIDTH, 2 * Q_BLOCK), F32),
                        pltpu.VMEM((HEAD_PAIRS, 1, 2 * Q_BLOCK), F32),
                        pltpu.VMEM((HEAD_PAIRS, 1, 2 * Q_BLOCK), F32)],
        compiler_params=pltpu.CompilerParams(dimension_semantics=("arbitrary", "arbitrary"), vmem_limit_bytes=56 * 1024 * 1024),
        name="dsa_attention",
    )(qiq_t, qiq_t, wt, k_idx, k, vt, bias_tiles)


def causal_depthwise_conv(x, w):
    s = x.shape[1]
    xp = jnp.pad(x, ((0, 0), (CONV_WIDTH - 1, 0), (0, 0)))
    return sum(xp[:, j:j + s] * w[j] for j in range(CONV_WIDTH))


def chunk_gated_delta_rule(q, k, v, g, beta):
    bn, s, h, dk = q.shape
    dv = v.shape[-1]
    n = s // CHUNK

    def chunks(t):
        return jnp.moveaxis(t.reshape(bn, n, CHUNK, h, -1), 3, 1)

    q, k, v = chunks(q), chunks(k), chunks(v)
    gc = jnp.cumsum(chunks(g[..., None])[..., 0], axis=-1)
    beta = chunks(beta[..., None])[..., 0]
    causal = jnp.tril(jnp.ones((CHUNK, CHUNK), bool))
    strict = jnp.tril(jnp.ones((CHUNK, CHUNK), bool), -1)
    decay = jnp.exp(jnp.where(causal, gc[..., :, None] - gc[..., None, :], -jnp.inf))
    k_beta = k * beta[..., None]
    m = jnp.where(strict, jnp.einsum('bhnid,bhnjd->bhnij', k_beta, k) * decay, 0.0)
    eye = jnp.eye(CHUNK, dtype=F32)
    t_inv = lax.linalg.triangular_solve(eye + m, jnp.broadcast_to(eye, m.shape), left_side=True, lower=True)
    u = t_inv @ (v * beta[..., None])
    w = t_inv @ (k_beta * jnp.exp(gc)[..., None])
    qk = jnp.einsum('bhnid,bhnjd->bhnij', q, k) * decay
    q_dec = q * jnp.exp(gc)[..., None]
    k_dec = k * jnp.exp(gc[..., -1:] - gc)[..., None]
    g_last = jnp.exp(gc[..., -1])

    def step(state, inp):
        u_n, w_n, qd_n, kd_n, qk_n, gl_n = inp
        v_new = u_n - w_n @ state
        o_n = qd_n @ state + qk_n @ v_new
        state = state * gl_n[..., None, None] + jnp.swapaxes(kd_n, -1, -2) @ v_new
        return state, o_n

    xs = tuple(jnp.moveaxis(t, 2, 0) for t in (u, w, q_dec, k_dec, qk, g_last))
    _, o = lax.scan(step, jnp.zeros((bn, h, dk, dv), F32), xs)
    o = jnp.moveaxis(jnp.moveaxis(o, 0, 2), 1, 3)
    return o.reshape(bn, s, h, dv)


def gated_deltanet(qkv, z, a, b, conv_w, a_log, dt_bias, out_norm_g):
    bn, s, _ = qkv.shape
    qkv = jax.nn.silu(causal_depthwise_conv(qkv, conv_w))
    q, k, v = [t.reshape(bn, s, B_HEADS, B_HEAD_DIM) for t in jnp.split(qkv, 3, axis=-1)]
    q = l2_normalize(q) * B_HEAD_DIM ** -0.5
    k = l2_normalize(k)
    beta = jax.nn.sigmoid(b)
    g = -jnp.exp(a_log) * jax.nn.softplus(a + dt_bias)
    o = chunk_gated_delta_rule(q, k, v, g, beta)
    o = rms_norm(o, out_norm_g) * jax.nn.silu(z).reshape(bn, s, B_HEADS, B_HEAD_DIM)
    return o.reshape(bn, s, B_WIDTH)


def complex_affine_combine(e1, e2):
    a1r, a1i, b1r, b1i = e1
    a2r, a2i, b2r, b2i = e2
    return (a2r * a1r - a2i * a1i, a2r * a1i + a2i * a1r,
            a2r * b1r - a2i * b1i + b2r, a2r * b1i + a2i * b1r + b2i)


def s5_branch(u, lam_re, lam_im, log_dt, b_re, b_im, c_re, c_im, d_skip, glu_w, glu_b):
    bn, s, _ = u.shape
    uf = u.reshape(bn, s, C_GROUPS, C_GROUP)
    dt = jnp.exp(log_dt)[:, None]
    lr, li = lam_re, lam_im
    mag = jnp.exp(lr * dt)
    a_re, a_im = mag * jnp.cos(li * dt), mag * jnp.sin(li * dt)
    den = lr * lr + li * li
    f_re = ((a_re - 1.0) * lr + a_im * li) / den
    f_im = (a_im * lr - (a_re - 1.0) * li) / den
    bb_re = f_re[..., None] * b_re - f_im[..., None] * b_im
    bb_im = f_re[..., None] * b_im + f_im[..., None] * b_re
    bu_re = jnp.einsum('bsgi,gpi->bsgp', uf, bb_re)
    bu_im = jnp.einsum('bsgi,gpi->bsgp', uf, bb_im)
    elems = (jnp.broadcast_to(a_re, bu_re.shape), jnp.broadcast_to(a_im, bu_re.shape), bu_re, bu_im)
    _, _, x_re, x_im = lax.associative_scan(complex_affine_combine, elems, axis=1)
    y = jnp.einsum('bsgp,gip->bsgi', x_re, c_re) - jnp.einsum('bsgp,gip->bsgi', x_im, c_im) + d_skip * uf
    y = jax.nn.gelu(y.reshape(bn, s, C_WIDTH))
    gate = matmul(y.reshape(bn * s, C_WIDTH), glu_w).reshape(bn, s, C_WIDTH)
    return y * jax.nn.sigmoid(gate + glu_b)


def hybrid_mixer(x, rel_bias, w_in, a_kv_norm, a_kv_up, b_conv, b_a_log, b_dt_bias, b_out_norm,
                 c_lambda_re, c_lambda_im, c_log_dt, c_b_re, c_b_im, c_c_re, c_c_im, c_d, c_glu_w, c_glu_b,
                 w_br_a, w_br_b, w_br_c, w_o):
    bn, s, d = x.shape
    t = bn * s
    xt = x.reshape(t, d)
    offsets = [0] + np.cumsum(SPLITS).tolist()
    w_cols = [w_in[:, offsets[i]:offsets[i + 1]] for i in range(len(SPLITS))]
    (w_aq, w_ckv, w_iq, w_ik, w_iw, w_bqkv, w_bz, w_ba, w_bb, w_cu, w_gate) = w_cols
    small = jnp.concatenate([w_ik, w_iw, w_ba, w_bb], axis=1)
    small = jnp.pad(small, ((0, 0), (0, LANE - small.shape[1])))
    h = matmul(xt, jnp.concatenate([w_ckv, small, w_bqkv, w_bz, w_cu, w_gate], axis=1))
    o = np.cumsum([0, A_KV_RANK, LANE, 3 * B_WIDTH, B_WIDTH, C_WIDTH, 3 * D_MODEL]).tolist()
    a_ckv, h_small, b_qkv, b_z, c_u, gate_logits = [h[:, o[i]:o[i + 1]] for i in range(6)]
    idx_k = h_small[:, :IDX_DIM].astype(BF16).reshape(bn, s, IDX_DIM)
    idx_w = h_small[:, IDX_DIM:IDX_DIM + IDX_HEADS]
    b_a = h_small[:, IDX_DIM + IDX_HEADS:IDX_DIM + IDX_HEADS + B_HEADS].reshape(bn, s, B_HEADS)
    b_b = h_small[:, IDX_DIM + IDX_HEADS + B_HEADS:IDX_DIM + IDX_HEADS + 2 * B_HEADS].reshape(bn, s, B_HEADS)
    b_qkv, b_z, c_u = (v.reshape(bn, s, -1) for v in (b_qkv, b_z, c_u))
    qiq_t = matmul_nt(jnp.concatenate([w_aq * A_HEAD_DIM ** -0.5, w_iq], axis=1).T, xt, bn, s)
    wt = jnp.swapaxes((idx_w * (IDX_HEADS ** -0.5 * IDX_DIM ** -0.5)).reshape(bn, s, IDX_HEADS), 1, 2)
    k, vt = kv_project(h, a_kv_norm, a_kv_up, bn, s)
    ya = dsa_attention(qiq_t, wt, idx_k, k, vt, _rel_bias_tiles(rel_bias), bn, s)
    yb = gated_deltanet(b_qkv, b_z, b_a, b_b, b_conv, b_a_log, b_dt_bias, b_out_norm)
    yc = s5_branch(c_u, c_lambda_re, c_lambda_im, c_log_dt, c_b_re, c_b_im, c_c_re, c_c_im, c_d, c_glu_w, c_glu_b)
    ga, gb, gc = jnp.split(jax.nn.sigmoid(gate_logits).reshape(t, 3 * d), 3, axis=-1)
    merged = (ga * matmul(ya.reshape(t, A_WIDTH), w_br_a) + gb * matmul(yb.reshape(t, B_WIDTH), w_br_b)
              + gc * matmul(yc.reshape(t, C_WIDTH), w_br_c))
    return matmul(merged, w_o)


def moe_ffn(xt, router_w, router_bias, exp_w_gu, exp_w_down, sh_w_gu, sh_w_down):
    n_tok, d = xt.shape
    scores = jax.nn.sigmoid(jnp.einsum('td,de->te', xt, router_w).astype(F32))
    _, top_idx = lax.top_k(scores + router_bias, TOP_K)
    top_s = jnp.take_along_axis(scores, top_idx, axis=-1)
    gates = top_s / jnp.sum(top_s, -1, keepdims=True) * ROUTED_SCALE
    e_flat = top_idx.reshape(-1)
    n_asg = e_flat.shape[0]
    order = jnp.argsort(e_flat)
    e_sorted = e_flat[order]
    counts = jax.ops.segment_sum(jnp.ones_like(e_flat), e_flat, num_segments=N_EXPERTS)
    starts = jnp.cumsum(counts) - counts
    padded = (counts + MOE_BLOCK - 1) // MOE_BLOCK * MOE_BLOCK
    pad_ends = jnp.cumsum(padded)
    dest = pad_ends[e_sorted] - padded[e_sorted] + jnp.arange(n_asg, dtype=jnp.int32) - starts[e_sorted]
    n_blk = n_asg // MOE_BLOCK + N_EXPERTS + 1
    n_pad = n_blk * MOE_BLOCK
    buf_tok = jnp.full((n_pad,), n_tok, jnp.int32).at[dest].set((order // TOP_K).astype(jnp.int32))
    buf_gate = jnp.zeros((n_pad,), F32).at[dest].set(gates.reshape(-1)[order])
    blk_expert = jnp.minimum(jnp.searchsorted(pad_ends, jnp.arange(n_blk, dtype=jnp.int32) * MOE_BLOCK, side='right'), N_EXPERTS - 1)
    x_pad = jnp.concatenate([xt, jnp.zeros((1, d), xt.dtype)], axis=0)

    def expert_block(acc, blk):
        tok, gate, e = blk
        hg, hu = jnp.split(x_pad[tok] @ exp_w_gu[e], 2, axis=-1)
        y = (jax.nn.silu(hg) * hu) @ exp_w_down[e]
        return acc.at[tok].add((y * gate[:, None]).astype(acc.dtype)), None

    routed, _ = lax.scan(expert_block, jnp.zeros_like(x_pad),
                         (buf_tok.reshape(n_blk, MOE_BLOCK), buf_gate.reshape(n_blk, MOE_BLOCK), blk_expert))
    sg, su = jnp.split(matmul(xt, sh_w_gu), 2, axis=-1)
    shared = matmul(jax.nn.silu(sg) * su, sh_w_down)
    return routed[:n_tok] + shared


def kernel(x, rel_bias, w_in, a_kv_norm, a_kv_up, b_conv, b_a_log, b_dt_bias, b_out_norm, c_lambda_re, c_lambda_im, c_log_dt, c_b_re, c_b_im, c_c_re, c_c_im, c_d, c_glu_w, c_glu_b, w_br_a, w_br_b, w_br_c, w_o, ln1_g, ln1_b, router_w, router_bias, exp_w_gu, exp_w_down, sh_w_gu, sh_w_down, ln2_g, ln2_b):
    bn, s, d = x.shape
    t = bn * s
    for i in range(w_in.shape[0]):
        mix = hybrid_mixer(x, rel_bias, w_in[i], a_kv_norm[i], a_kv_up[i], b_conv[i], b_a_log[i], b_dt_bias[i],
                           b_out_norm[i], c_lambda_re[i], c_lambda_im[i], c_log_dt[i], c_b_re[i], c_b_im[i],
                           c_c_re[i], c_c_im[i], c_d[i], c_glu_w[i], c_glu_b[i], w_br_a[i], w_br_b[i], w_br_c[i], w_o[i])
        xt = residual_layer_norm(x.reshape(t, d), mix, ln1_g[i], ln1_b[i])
        ffn = moe_ffn(xt, router_w[i], router_bias[i], exp_w_gu[i], exp_w_down[i], sh_w_gu[i], sh_w_down[i])
        x = residual_layer_norm(xt, ffn, ln2_g[i], ln2_b[i]).reshape(bn, s, d)
    return x
```

```python
import functools
import math

import jax
import jax.numpy as jnp
import numpy as np
from jax import lax
from jax.experimental import pallas as pl
from jax.experimental.pallas import tpu as pltpu

F32 = jnp.float32
BF16 = jnp.bfloat16

D_MODEL = 1024
DEPTH = 4
A_HEADS = 8
A_HEAD_DIM = 64
A_WIDTH = A_HEADS * A_HEAD_DIM
A_KV_RANK = 128
IDX_HEADS = 8
IDX_DIM = 64
TOPK_MAX = 256
Q_BLOCK = 128
REL_BUCKETS = 32
REL_MAX_DIST = 128
B_HEADS = 4
B_HEAD_DIM = 128
B_WIDTH = B_HEADS * B_HEAD_DIM
CONV_WIDTH = 4
CHUNK = 64
C_WIDTH = 512
C_GROUP = 16
C_GROUPS = C_WIDTH // C_GROUP
C_STATE = 64
N_EXPERTS = 64
TOP_K = 8
D_EXPERT = 256
D_SHARED = 256
ROUTED_SCALE = 2.5
MOE_BLOCK = 256
SPLITS = (A_WIDTH, A_KV_RANK, IDX_HEADS * IDX_DIM, IDX_DIM, IDX_HEADS, 3 * B_WIDTH, B_WIDTH, B_HEADS, B_HEADS, C_WIDTH, 3 * D_MODEL)
DEEPNORM_ALPHA = (2 * DEPTH) ** 0.25
LN_EPS = 1e-5
RMS_EPS = 1e-6
NEG_INF = -1e30

VMEM_LIMIT_BYTES = 48 * 1024 * 1024
LANE = 128


def _pick_tile(n, candidates):
    for c in candidates:
        if n % c == 0:
            return c
    return n


def _mm_kernel(a_ref, b_ref, o_ref):
    o_ref[...] = jnp.dot(a_ref[...].astype(BF16), b_ref[...], preferred_element_type=F32).astype(o_ref.dtype)


def matmul(a, b, out_dtype=F32):
    m, k = a.shape
    _, n = b.shape
    tm = _pick_tile(m, (1024, 512, 256, 128, 8))
    tn = _pick_tile(n, (1024, 768, 512, 384, 256, 128))
    return pl.pallas_call(
        _mm_kernel,
        grid=(m // tm, n // tn),
        in_specs=[pl.BlockSpec((tm, k), lambda i, j: (i, 0)), pl.BlockSpec((k, tn), lambda i, j: (0, j))],
        out_specs=pl.BlockSpec((tm, tn), lambda i, j: (i, j)),
        out_shape=jax.ShapeDtypeStruct((m, n), out_dtype),
        compiler_params=pltpu.CompilerParams(dimension_semantics=("arbitrary", "arbitrary"), vmem_limit_bytes=VMEM_LIMIT_BYTES),
        name="matmul",
    )(a, b.astype(BF16))


def _res_ln_kernel(x_ref, y_ref, g_ref, b_ref, o_ref):
    z = DEEPNORM_ALPHA * x_ref[...] + y_ref[...]
    mu = jnp.mean(z, -1, keepdims=True)
    zc = z - mu
    var = jnp.mean(zc * zc, -1, keepdims=True)
    o_ref[...] = zc * lax.rsqrt(var + LN_EPS) * g_ref[...] + b_ref[...]


def residual_layer_norm(x, y, g, b):
    t, d = x.shape
    tm = _pick_tile(t, (512, 256, 128, 8))
    row = pl.BlockSpec((tm, d), lambda i: (i, 0))
    vec = pl.BlockSpec((1, d), lambda i: (0, 0))
    return pl.pallas_call(
        _res_ln_kernel,
        grid=(t // tm,),
        in_specs=[row, row, vec, vec],
        out_specs=row,
        out_shape=jax.ShapeDtypeStruct((t, d), F32),
        compiler_params=pltpu.CompilerParams(dimension_semantics=("arbitrary",), vmem_limit_bytes=VMEM_LIMIT_BYTES),
        name="residual_layer_norm",
    )(x, y, g.reshape(1, d), b.reshape(1, d))


def rms_norm(x, g):
    return x * lax.rsqrt(jnp.mean(x * x, -1, keepdims=True) + RMS_EPS) * g


def l2_normalize(x):
    return x * lax.rsqrt(jnp.sum(x * x, -1, keepdims=True) + RMS_EPS)


def rel_bucket(dist):
    n_exact = REL_BUCKETS // 2
    d = jnp.maximum(dist, 1).astype(F32)
    large = n_exact + (jnp.log(d / n_exact) / math.log(REL_MAX_DIST / n_exact) * (REL_BUCKETS - n_exact)).astype(jnp.int32)
    return jnp.where(dist < n_exact, dist, jnp.minimum(large, REL_BUCKETS - 1))


KEY_BLOCK = 128
DSA_CHUNK = 512
BLOCKS_PER_CHUNK = DSA_CHUNK // KEY_BLOCK
HEAD_PAIRS = A_HEADS // 2
PAIR_WIDTH = 2 * A_HEAD_DIM
INT_MIN = -2 ** 31


def _kv_kernel(c_ref, g_ref, wk_ref, wvt_ref, k_ref, vt_ref):
    c = c_ref[...]
    cn = (c * lax.rsqrt(jnp.mean(c * c, -1, keepdims=True) + RMS_EPS) * g_ref[...]).astype(BF16)
    k_ref[...] = jnp.dot(cn, wk_ref[...], preferred_element_type=F32).astype(BF16)
    vt = lax.dot_general(wvt_ref[...], cn, (((1,), (1,)), ((), ())), preferred_element_type=F32).astype(BF16)
    for i in range(vt_ref.shape[1]):
        vt_ref[0, i] = vt[:, i * KEY_BLOCK:(i + 1) * KEY_BLOCK]


def kv_project(c_kv, g, w_kv_up, bn, s):
    ts = 512
    w = w_kv_up.astype(BF16)
    return pl.pallas_call(
        _kv_kernel,
        grid=(bn, s // ts),
        in_specs=[pl.BlockSpec((ts, A_KV_RANK), lambda b, i: (b * (s // ts) + i, 0)),
                  pl.BlockSpec((1, A_KV_RANK), lambda b, i: (0, 0)),
                  pl.BlockSpec((A_KV_RANK, A_WIDTH), lambda b, i: (0, 0)),
                  pl.BlockSpec((A_WIDTH, A_KV_RANK), lambda b, i: (0, 0))],
        out_specs=[pl.BlockSpec((None, ts, A_WIDTH), lambda b, i: (b, i, 0)),
                   pl.BlockSpec((1, ts // KEY_BLOCK, A_WIDTH, KEY_BLOCK), lambda b, i: (b, i, 0, 0))],
        out_shape=[jax.ShapeDtypeStruct((bn, s, A_WIDTH), BF16),
                   jax.ShapeDtypeStruct((bn, s // KEY_BLOCK, A_WIDTH, KEY_BLOCK), BF16)],
        compiler_params=pltpu.CompilerParams(dimension_semantics=("arbitrary", "arbitrary"), vmem_limit_bytes=VMEM_LIMIT_BYTES),
        name="kv_project",
    )(c_kv, g.reshape(1, A_KV_RANK), w[:, :A_WIDTH], w[:, A_WIDTH:].T)


def _mm_nt_kernel(wt_ref, x_ref, o_ref):
    o_ref[0] = lax.dot_general(wt_ref[...], x_ref[...].astype(BF16), (((1,), (1,)), ((), ())),
                               preferred_element_type=F32).astype(o_ref.dtype)


def matmul_nt(wt, x, bn, s, out_dtype=BF16):
    n, k = wt.shape
    ts = 512
    return pl.pallas_call(
        _mm_nt_kernel,
        grid=(bn, s // ts),
        in_specs=[pl.BlockSpec((n, k), lambda b, i: (0, 0)),
                  pl.BlockSpec((ts, k), lambda b, i: (b * (s // ts) + i, 0))],
        out_specs=pl.BlockSpec((1, n, ts), lambda b, i: (b, 0, i)),
        out_shape=jax.ShapeDtypeStruct((bn, n, s), out_dtype),
        compiler_params=pltpu.CompilerParams(dimension_semantics=("arbitrary", "arbitrary"), vmem_limit_bytes=VMEM_LIMIT_BYTES),
        name="matmul_nt",
    )(wt.astype(BF16), x)


def _sortable_key(score):
    bits = pltpu.bitcast(score, jnp.int32)
    return bits ^ (lax.shift_right_arithmetic(bits, 31) & jnp.int32(0x7FFFFFFF))


def _dsa_kernel(qt_ref, iqt_ref, wt_ref, kidx_ref, k_ref, vt_ref, bias_ref, o_ref,
                skey_ref, widx_ref, wq_ref, acc_ref, m_ref, l_ref, *, k_sel):
    j = pl.program_id(1)
    n_chunks = (j + BLOCKS_PER_CHUNK) // BLOCKS_PER_CHUNK
    q_pos = j * Q_BLOCK + lax.broadcasted_iota(jnp.int32, (1, Q_BLOCK), 1)

    for h in range(IDX_HEADS):
        widx_ref[:, h * Q_BLOCK:(h + 1) * Q_BLOCK] = iqt_ref[0, h * IDX_DIM:(h + 1) * IDX_DIM, :]
    wq_ref[...] = jnp.zeros_like(wq_ref)
    for p in range(HEAD_PAIRS):
        wq_ref[p, 0:A_HEAD_DIM, 0:Q_BLOCK] = qt_ref[0, (2 * p) * A_HEAD_DIM:(2 * p + 1) * A_HEAD_DIM, :]
        wq_ref[p, A_HEAD_DIM:PAIR_WIDTH, Q_BLOCK:2 * Q_BLOCK] = qt_ref[0, (2 * p + 1) * A_HEAD_DIM:(2 * p + 2) * A_HEAD_DIM, :]

    w_rows = [wt_ref[0, h:h + 1, :] for h in range(IDX_HEADS)]

    def idx_chunk(c, carry):
        start = pl.multiple_of(c * DSA_CHUNK, DSA_CHUNK)
        kc = kidx_ref[0, pl.ds(start, DSA_CHUNK), :]
        score = jnp.zeros((DSA_CHUNK, Q_BLOCK), F32)
        for p in range(IDX_HEADS // 2):
            z = jnp.dot(kc, widx_ref[:, p * 2 * Q_BLOCK:(p + 1) * 2 * Q_BLOCK], preferred_element_type=F32)
            score = score + jnp.maximum(z[:, :Q_BLOCK], 0.0) * w_rows[2 * p]
            score = score + jnp.maximum(z[:, Q_BLOCK:], 0.0) * w_rows[2 * p + 1]
        pos = start + lax.broadcasted_iota(jnp.int32, (DSA_CHUNK, Q_BLOCK), 0)
        skey_ref[pl.ds(start, DSA_CHUNK), :] = jnp.where(pos <= q_pos, _sortable_key(score), INT_MIN)
        return carry

    lax.fori_loop(0, n_chunks, idx_chunk, 0)

    def count_ge(cand):
        def body(c, cnt):
            start = pl.multiple_of(c * DSA_CHUNK, DSA_CHUNK)
            hit = (skey_ref[pl.ds(start, DSA_CHUNK), :] >= cand).astype(jnp.int32)
            return cnt + jnp.sum(hit.reshape(DSA_CHUNK // 8, 8, Q_BLOCK), axis=0)
        cnt = lax.fori_loop(0, n_chunks, body, jnp.zeros((8, Q_BLOCK), jnp.int32))
        return jnp.sum(cnt, axis=0, keepdims=True)

    def bit_step(i, thr_biased):
        cand_biased = thr_biased | lax.shift_left(jnp.int32(1), 31 - i)
        return jnp.where(count_ge(cand_biased ^ INT_MIN) >= k_sel, cand_biased, thr_biased)

    thr = lax.fori_loop(0, 32, bit_step, jnp.zeros((1, Q_BLOCK), jnp.int32)) ^ INT_MIN

    n_gt = count_ge(thr + 1)
    n_ge = count_ge(thr)
    tied = (n_ge > k_sel) & (thr > INT_MIN)

    @pl.when(jnp.max(tied.astype(jnp.int32)) > 0)
    def _():
        need = k_sel - n_gt

        def count_tied_upto(limit):
            def body(c, cnt):
                start = pl.multiple_of(c * DSA_CHUNK, DSA_CHUNK)
                pos = start + lax.broadcasted_iota(jnp.int32, (DSA_CHUNK, Q_BLOCK), 0)
                hit = jnp.where(skey_ref[pl.ds(start, DSA_CHUNK), :] == thr, (pos <= limit).astype(jnp.int32), 0)
                return cnt + jnp.sum(hit.reshape(DSA_CHUNK // 8, 8, Q_BLOCK), axis=0)
            cnt = lax.fori_loop(0, n_chunks, body, jnp.zeros((8, Q_BLOCK), jnp.int32))
            return jnp.sum(cnt, axis=0, keepdims=True)

        def pos_step(i, lim):
            cand = lim - lax.shift_left(jnp.int32(1), 30 - i)
            return jnp.where(count_tied_upto(cand) >= need, cand, lim)

        limit = lax.fori_loop(0, 31, pos_step, jnp.full((1, Q_BLOCK), 2 ** 31 - 1, jnp.int32))

        def demote(c, carry):
            start = pl.multiple_of(c * DSA_CHUNK, DSA_CHUNK)
            pos = start + lax.broadcasted_iota(jnp.int32, (DSA_CHUNK, Q_BLOCK), 0)
            sk = skey_ref[pl.ds(start, DSA_CHUNK), :]
            skey_ref[pl.ds(start, DSA_CHUNK), :] = jnp.where(tied & (sk == thr) & (pos > limit), thr - 1, sk)
            return carry

        lax.fori_loop(0, n_chunks, demote, 0)

    m_ref[...] = jnp.full_like(m_ref, NEG_INF)
    l_ref[...] = jnp.zeros_like(l_ref)
    acc_ref[...] = jnp.zeros_like(acc_ref)

    def attend(s_list, vt_of, n_blocks):
        for p in range(HEAD_PAIRS):
            s = s_list[p]
            m_old = m_ref[p]
            m_new = jnp.maximum(m_old, jnp.max(s, axis=0, keepdims=True))
            alpha = jnp.exp(m_old - m_new)
            pexp = jnp.exp(s - m_new)
            l_ref[p] = l_ref[p] * alpha + jnp.sum(pexp, axis=0, keepdims=True)
            pb = pexp.astype(BF16)
            pv = jnp.dot(vt_of(p, 0), pb[0:KEY_BLOCK], preferred_element_type=F32)
            for i in range(1, n_blocks):
                pv = pv + jnp.dot(vt_of(p, i), pb[i * KEY_BLOCK:(i + 1) * KEY_BLOCK], preferred_element_type=F32)
            acc_ref[p] = acc_ref[p] * alpha + pv
            m_ref[p] = m_new

    far_end = (j - 1) * KEY_BLOCK
    n_far = (jnp.maximum(j - 1, 0) + BLOCKS_PER_CHUNK - 1) // BLOCKS_PER_CHUNK

    def far_chunk(c, carry):
        start = pl.multiple_of(c * DSA_CHUNK, DSA_CHUNK)
        pos = start + lax.broadcasted_iota(jnp.int32, (DSA_CHUNK, Q_BLOCK), 0)
        sel = (skey_ref[pl.ds(start, DSA_CHUNK), :] >= thr) & (pos < far_end)
        madd = jnp.where(sel, 0.0, NEG_INF)
        madd2 = jnp.concatenate([madd, madd], axis=1)
        kc = k_ref[0, pl.ds(start, DSA_CHUNK), :]
        s_list = [jnp.dot(kc[:, p * PAIR_WIDTH:(p + 1) * PAIR_WIDTH], wq_ref[p], preferred_element_type=F32) + madd2
                  for p in range(HEAD_PAIRS)]
        blk0 = c * BLOCKS_PER_CHUNK
        attend(s_list, lambda p, i: vt_ref[0, blk0 + i, p * PAIR_WIDTH:(p + 1) * PAIR_WIDTH, :], BLOCKS_PER_CHUNK)
        return carry

    lax.fori_loop(0, n_far, far_chunk, 0)

    def near_block(kb, delta):
        start = pl.multiple_of(kb * KEY_BLOCK, KEY_BLOCK)
        pos = start + lax.broadcasted_iota(jnp.int32, (KEY_BLOCK, Q_BLOCK), 0)
        sel = (skey_ref[pl.ds(start, KEY_BLOCK), :] >= thr) & (pos <= q_pos)
        madd = jnp.where(sel, 0.0, NEG_INF)
        madd2 = jnp.concatenate([madd, madd], axis=1)
        kc = k_ref[0, pl.ds(start, KEY_BLOCK), :]
        s_list = [jnp.dot(kc[:, p * PAIR_WIDTH:(p + 1) * PAIR_WIDTH], wq_ref[p], preferred_element_type=F32)
                  + bias_ref[delta, p] + madd2 for p in range(HEAD_PAIRS)]
        attend(s_list, lambda p, i: vt_ref[0, kb, p * PAIR_WIDTH:(p + 1) * PAIR_WIDTH, :], 1)

    @pl.when(j > 0)
    def _():
        near_block(j - 1, 1)

    near_block(j, 0)

    outs = []
    for p in range(HEAD_PAIRS):
        o = acc_ref[p] / l_ref[p]
        outs.append(o[0:A_HEAD_DIM, 0:Q_BLOCK])
        outs.append(o[A_HEAD_DIM:PAIR_WIDTH, Q_BLOCK:2 * Q_BLOCK])
    o_ref[0] = jnp.concatenate(outs, axis=0).T.astype(o_ref.dtype)


def _rel_bias_tiles(rel_bias):
    tab = rel_bias[rel_bucket(jnp.arange(2 * Q_BLOCK, dtype=jnp.int32))] - rel_bias[REL_BUCKETS - 1]
    key = jnp.arange(KEY_BLOCK)[:, None]
    qry = jnp.arange(Q_BLOCK)[None, :]
    tiles = []
    for delta in (0, 1):
        dist = jnp.maximum(delta * Q_BLOCK + qry - key, 0)
        t = jnp.moveaxis(tab[dist], 2, 0)
        t = t.reshape(HEAD_PAIRS, 2, KEY_BLOCK, Q_BLOCK)
        tiles.append(jnp.concatenate([t[:, 0], t[:, 1]], axis=-1))
    return jnp.stack(tiles)


def dsa_attention(qiq_t, wt, k_idx, k, vt, bias_tiles, bn, s):
    assert A_WIDTH == IDX_HEADS * IDX_DIM
    k_sel = min(TOPK_MAX, s // 4)
    nb = s // Q_BLOCK
    s_pad = s + DSA_CHUNK
    return pl.pallas_call(
        functools.partial(_dsa_kernel, k_sel=k_sel),
        grid=(bn, nb),
        in_specs=[pl.BlockSpec((1, A_WIDTH, Q_BLOCK), lambda b, j: (b, 0, j)),
                  pl.BlockSpec((1, IDX_HEADS * IDX_DIM, Q_BLOCK), lambda b, j: (b, 1, j)),
                  pl.BlockSpec((1, IDX_HEADS, Q_BLOCK), lambda b, j: (b, 0, j)),
                  pl.BlockSpec((1, s, IDX_DIM), lambda b, j: (b, 0, 0)),
                  pl.BlockSpec((1, s, A_WIDTH), lambda b, j: (b, 0, 0)),
                  pl.BlockSpec((1, s // KEY_BLOCK, A_WIDTH, KEY_BLOCK), lambda b, j: (b, 0, 0, 0)),
                  pl.BlockSpec((2, HEAD_PAIRS, KEY_BLOCK, 2 * Q_BLOCK), lambda b, j: (0, 0, 0, 0))],
        out_specs=pl.BlockSpec((1, Q_BLOCK, A_WIDTH), lambda b, j: (b, j, 0)),
        out_shape=jax.ShapeDtypeStruct((bn, s, A_WIDTH), BF16),
        scratch_shapes=[pltpu.VMEM((s_pad, Q_BLOCK), jnp.int32),
                        pltpu.VMEM((IDX_DIM, IDX_HEADS * Q_BLOCK), BF16),
                        pltpu.VMEM((HEAD_PAIRS, PAIR_WIDTH, 2 * Q_BLOCK), BF16),
                        pltpu.VMEM((HEAD_PAIRS, PAIR_WIDTH, 2 * Q_BLOCK), F32),
                        pltpu.VMEM((HEAD_PAIRS, 1, 2 * Q_BLOCK), F32),
                        pltpu.VMEM((HEAD_PAIRS, 1, 2 * Q_BLOCK), F32)],
        compiler_params=pltpu.CompilerParams(dimension_semantics=("arbitrary", "arbitrary"), vmem_limit_bytes=56 * 1024 * 1024),
        name="dsa_attention",
    )(qiq_t, qiq_t, wt, k_idx, k, vt, bias_tiles)


def causal_depthwise_conv(x, w):
    s = x.shape[1]
    xp = jnp.pad(x, ((0, 0), (CONV_WIDTH - 1, 0), (0, 0)))
    return sum(xp[:, j:j + s] * w[j] for j in range(CONV_WIDTH))


def chunk_gated_delta_rule(q, k, v, g, beta):
    bn, s, h, dk = q.shape
    dv = v.shape[-1]
    n = s // CHUNK

    def chunks(t):
        return jnp.moveaxis(t.reshape(bn, n, CHUNK, h, -1), 3, 1)

    q, k, v = chunks(q), chunks(k), chunks(v)
    gc = jnp.cumsum(chunks(g[..., None])[..., 0], axis=-1)
    beta = chunks(beta[..., None])[..., 0]
    causal = jnp.tril(jnp.ones((CHUNK, CHUNK), bool))
    strict = jnp.tril(jnp.ones((CHUNK, CHUNK), bool), -1)
    decay = jnp.exp(jnp.where(causal, gc[..., :, None] - gc[..., None, :], -jnp.inf))
    k_beta = k * beta[..., None]
    m = jnp.where(strict, jnp.einsum('bhnid,bhnjd->bhnij', k_beta, k) * decay, 0.0)
    eye = jnp.eye(CHUNK, dtype=F32)
    t_inv = lax.linalg.triangular_solve(eye + m, jnp.broadcast_to(eye, m.shape), left_side=True, lower=True)
    u = t_inv @ (v * beta[..., None])
    w = t_inv @ (k_beta * jnp.exp(gc)[..., None])
    qk = jnp.einsum('bhnid,bhnjd->bhnij', q, k) * decay
    q_dec = q * jnp.exp(gc)[..., None]
    k_dec = k * jnp.exp(gc[..., -1:] - gc)[..., None]
    g_last = jnp.exp(gc[..., -1])

    def step(state, inp):
        u_n, w_n, qd_n, kd_n, qk_n, gl_n = inp
        v_new = u_n - w_n @ state
        o_n = qd_n @ state + qk_n @ v_new
        state = state * gl_n[..., None, None] + jnp.swapaxes(kd_n, -1, -2) @ v_new
        return state, o_n

    xs = tuple(jnp.moveaxis(t, 2, 0) for t in (u, w, q_dec, k_dec, qk, g_last))
    _, o = lax.scan(step, jnp.zeros((bn, h, dk, dv), F32), xs)
    o = jnp.moveaxis(jnp.moveaxis(o, 0, 2), 1, 3)
    return o.reshape(bn, s, h, dv)


def gated_deltanet(qkv, z, a, b, conv_w, a_log, dt_bias, out_norm_g):
    bn, s, _ = qkv.shape
    qkv = jax.nn.silu(causal_depthwise_conv(qkv, conv_w))
    q, k, v = [t.reshape(bn, s, B_HEADS, B_HEAD_DIM) for t in jnp.split(qkv, 3, axis=-1)]
    q = l2_normalize(q) * B_HEAD_DIM ** -0.5
    k = l2_normalize(k)
    beta = jax.nn.sigmoid(b)
    g = -jnp.exp(a_log) * jax.nn.softplus(a + dt_bias)
    o = chunk_gated_delta_rule(q, k, v, g, beta)
    o = rms_norm(o, out_norm_g) * jax.nn.silu(z).reshape(bn, s, B_HEADS, B_HEAD_DIM)
    return o.reshape(bn, s, B_WIDTH)


S5_CHUNK = 32
S5_ROW = S5_CHUNK * C_GROUP
STATE2 = 2 * C_STATE


def _s5_tables(lam_re, lam_im, log_dt, b_re, b_im, c_re, c_im, d_skip):
    hp = lax.Precision.HIGHEST
    ell = S5_CHUNK
    dt = jnp.exp(log_dt)[:, None]
    lr, li = lam_re, lam_im
    mag = jnp.exp(lr * dt)
    a_re, a_im = mag * jnp.cos(li * dt), mag * jnp.sin(li * dt)
    den = lr * lr + li * li
    f_re = ((a_re - 1.0) * lr + a_im * li) / den
    f_im = (a_im * lr - (a_re - 1.0) * li) / den
    bb_re = f_re[..., None] * b_re - f_im[..., None] * b_im
    bb_im = f_re[..., None] * b_im + f_im[..., None] * b_re
    tau = jnp.arange(ell + 1, dtype=F32)[:, None, None]
    pmag, ang = jnp.exp(tau * (lr * dt)), tau * (li * dt)
    p_re, p_im = pmag * jnp.cos(ang), pmag * jnp.sin(ang)
    ab_re = p_re[:ell, ..., None] * bb_re - p_im[:ell, ..., None] * bb_im
    ab_im = p_re[:ell, ..., None] * bb_im + p_im[:ell, ..., None] * bb_re
    kern = (jnp.einsum('gip,tgpj->tgij', c_re, ab_re, precision=hp)
            - jnp.einsum('gip,tgpj->tgij', c_im, ab_im, precision=hp))
    lag = jnp.arange(ell)[None, :] - jnp.arange(ell)[:, None]
    toep = jnp.where((lag >= 0)[:, :, None, None, None], kern[jnp.maximum(lag, 0)], 0.0)
    toep = jnp.transpose(toep, (2, 0, 4, 1, 3)).reshape(C_GROUPS, S5_ROW, S5_ROW)
    w_re = jnp.transpose(ab_re[::-1], (1, 0, 3, 2)).reshape(C_GROUPS, S5_ROW, C_STATE)
    w_im = jnp.transpose(ab_im[::-1], (1, 0, 3, 2)).reshape(C_GROUPS, S5_ROW, C_STATE)
    tw = jnp.concatenate([toep, w_re, w_im, w_im, w_re], axis=-1).astype(BF16)
    pt_re = jnp.swapaxes(p_re[1:], 0, 1)[:, :, None, :]
    pt_im = jnp.swapaxes(p_im[1:], 0, 1)[:, :, None, :]
    ca_re = c_re[:, None] * pt_re - c_im[:, None] * pt_im
    ca_im = c_re[:, None] * pt_im + c_im[:, None] * pt_re
    v_re = jnp.transpose(ca_re, (0, 3, 1, 2)).reshape(C_GROUPS, C_STATE, S5_ROW)
    v_im = jnp.transpose(-ca_im, (0, 3, 1, 2)).reshape(C_GROUPS, C_STATE, S5_ROW)
    v = jnp.concatenate([v_re, v_im], axis=1).astype(BF16)
    al_re, al_im = p_re[ell], p_im[ell]
    coef = jnp.stack([jnp.concatenate([al_re, al_re], -1), jnp.concatenate([-al_im, al_im], -1),
                      jnp.concatenate([al_im, -al_im], -1)], axis=1)
    d_flat = jnp.tile(d_skip, (1, ell)).reshape(C_GROUPS, 1, S5_ROW)
    return tw, v, coef, d_flat


def _s5_kernel(u_ref, tw_ref, v_ref, coef_ref, d_ref, o_ref, sin_ref, xin_ref, *, bn):
    rows = u_ref.shape[1]
    n_chunks = rows // bn
    u = u_ref[0]
    tw = jnp.dot(u.astype(BF16), tw_ref[0], preferred_element_type=F32)
    o_ref[0] = tw[:, :S5_ROW] + u * d_ref[0]
    sin_ref[...] = tw[:, S5_ROW:]
    a1, a2, a2s = coef_ref[0, 0:1, :], coef_ref[0, 1:2, :], coef_ref[0, 2:3, :]

    def step(c, carry):
        new = []
        for b in range(bn):
            x, xs = carry[2 * b], carry[2 * b + 1]
            row = b * n_chunks + c
            xin_ref[pl.ds(row, 1), :] = x
            s_row = sin_ref[pl.ds(row, 1), :]
            new.append(a1 * x + a2 * xs + s_row[:, :STATE2])
            new.append(a1 * xs + a2s * x + s_row[:, STATE2:])
        return tuple(new)

    zero = jnp.zeros((1, STATE2), F32)
    lax.fori_loop(0, n_chunks, step, (zero,) * (2 * bn))
    y = o_ref[0] + jnp.dot(xin_ref[...].astype(BF16), v_ref[0], preferred_element_type=F32)
    o_ref[0] = jax.nn.gelu(y)


def _glu_kernel(y_ref, w_ref, b_ref, o_ref):
    y = y_ref[...]
    gate = jnp.dot(y.astype(BF16), w_ref[...], preferred_element_type=F32) + b_ref[...]
    o_ref[...] = (y * jax.nn.sigmoid(gate)).astype(o_ref.dtype)


def s5_branch(u, lam_re, lam_im, log_dt, b_re, b_im, c_re, c_im, d_skip, glu_w, glu_b):
    bn, s, _ = u.shape
    n_chunks = s // S5_CHUNK
    rows = bn * n_chunks
    tw, v, coef, d_flat = _s5_tables(lam_re, lam_im, log_dt, b_re, b_im, c_re, c_im, d_skip)
    uf = jnp.transpose(u.reshape(bn, n_chunks, S5_CHUNK, C_GROUPS, C_GROUP), (3, 0, 1, 2, 4)).reshape(C_GROUPS, rows, S5_ROW)
    per_group = lambda shape: pl.BlockSpec((1,) + shape, lambda g: (g, 0, 0))
    y = pl.pallas_call(
        functools.partial(_s5_kernel, bn=bn),
        grid=(C_GROUPS,),
        in_specs=[per_group((rows, S5_ROW)), per_group((S5_ROW, S5_ROW + 2 * STATE2)), per_group((STATE2, S5_ROW)),
                  per_group((3, STATE2)), per_group((1, S5_ROW))],
        out_specs=per_group((rows, S5_ROW)),
        out_shape=jax.ShapeDtypeStruct((C_GROUPS, rows, S5_ROW), F32),
        scratch_shapes=[pltpu.VMEM((rows, 2 * STATE2), F32), pltpu.VMEM((rows, STATE2), F32)],
        compiler_params=pltpu.CompilerParams(dimension_semantics=("arbitrary",), vmem_limit_bytes=VMEM_LIMIT_BYTES),
        name="s5_scan",
    )(uf, tw, v, coef, d_flat)
    y = jnp.transpose(y.reshape(C_GROUPS, bn, n_chunks, S5_CHUNK, C_GROUP), (1, 2, 3, 0, 4)).reshape(bn * s, C_WIDTH)
    t = bn * s
    tm = _pick_tile(t, (1024, 512, 256, 128, 8))
    return pl.pallas_call(
        _glu_kernel,
        grid=(t // tm,),
        in_specs=[pl.BlockSpec((tm, C_WIDTH), lambda i: (i, 0)), pl.BlockSpec((C_WIDTH, C_WIDTH), lambda i: (0, 0)),
                  pl.BlockSpec((1, C_WIDTH), lambda i: (0, 0))],
        out_specs=pl.BlockSpec((tm, C_WIDTH), lambda i: (i, 0)),
        out_shape=jax.ShapeDtypeStruct((t, C_WIDTH), BF16),
        compiler_params=pltpu.CompilerParams(dimension_semantics=("arbitrary",), vmem_limit_bytes=VMEM_LIMIT_BYTES),
        name="s5_glu",
    )(y, glu_w.astype(BF16), glu_b.reshape(1, C_WIDTH))


def hybrid_mixer(x, rel_bias, w_in, a_kv_norm, a_kv_up, b_conv, b_a_log, b_dt_bias, b_out_norm,
                 c_lambda_re, c_lambda_im, c_log_dt, c_b_re, c_b_im, c_c_re, c_c_im, c_d, c_glu_w, c_glu_b,
                 w_br_a, w_br_b, w_br_c, w_o):
    bn, s, d = x.shape
    t = bn * s
    xt = x.reshape(t, d)
    offsets = [0] + np.cumsum(SPLITS).tolist()
    w_cols = [w_in[:, offsets[i]:offsets[i + 1]] for i in range(len(SPLITS))]
    (w_aq, w_ckv, w_iq, w_ik, w_iw, w_bqkv, w_bz, w_ba, w_bb, w_cu, w_gate) = w_cols
    small = jnp.concatenate([w_ik, w_iw, w_ba, w_bb], axis=1)
    small = jnp.pad(small, ((0, 0), (0, LANE - small.shape[1])))
    h = matmul(xt, jnp.concatenate([w_ckv, small, w_bqkv, w_bz, w_cu, w_gate], axis=1))
    o = np.cumsum([0, A_KV_RANK, LANE, 3 * B_WIDTH, B_WIDTH, C_WIDTH, 3 * D_MODEL]).tolist()
    a_ckv, h_small, b_qkv, b_z, c_u, gate_logits = [h[:, o[i]:o[i + 1]] for i in range(6)]
    idx_k = h_small[:, :IDX_DIM].astype(BF16).reshape(bn, s, IDX_DIM)
    idx_w = h_small[:, IDX_DIM:IDX_DIM + IDX_HEADS]
    b_a = h_small[:, IDX_DIM + IDX_HEADS:IDX_DIM + IDX_HEADS + B_HEADS].reshape(bn, s, B_HEADS)
    b_b = h_small[:, IDX_DIM + IDX_HEADS + B_HEADS:IDX_DIM + IDX_HEADS + 2 * B_HEADS].reshape(bn, s, B_HEADS)
    b_qkv, b_z, c_u = (v.reshape(bn, s, -1) for v in (b_qkv, b_z, c_u))
    qiq_t = matmul_nt(jnp.concatenate([w_aq * A_HEAD_DIM ** -0.5, w_iq], axis=1).T, xt, bn, s)
    wt = jnp.swapaxes((idx_w * (IDX_HEADS ** -0.5 * IDX_DIM ** -0.5)).reshape(bn, s, IDX_HEADS), 1, 2)
    k, vt = kv_project(h, a_kv_norm, a_kv_up, bn, s)
    ya = dsa_attention(qiq_t, wt, idx_k, k, vt, _rel_bias_tiles(rel_bias), bn, s)
    yb = gated_deltanet(b_qkv, b_z, b_a, b_b, b_conv, b_a_log, b_dt_bias, b_out_norm)
    yc = s5_branch(c_u, c_lambda_re, c_lambda_im, c_log_dt, c_b_re, c_b_im, c_c_re, c_c_im, c_d, c_glu_w, c_glu_b)
    ga, gb, gc = jnp.split(jax.nn.sigmoid(gate_logits).reshape(t, 3 * d), 3, axis=-1)
    merged = (ga * matmul(ya.reshape(t, A_WIDTH), w_br_a) + gb * matmul(yb.reshape(t, B_WIDTH), w_br_b)
              + gc * matmul(yc.reshape(t, C_WIDTH), w_br_c))
    return matmul(merged, w_o)


ROUTER_TILE = 512


def _router_kernel(x_ref, rwt_ref, bias_ref, tri_ref, idx_ref, gate_ref, pos_ref, cnt_ref, carry_ref):
    @pl.when(pl.program_id(0) == 0)
    def _():
        carry_ref[...] = jnp.zeros_like(carry_ref)

    logits = lax.dot_general(rwt_ref[...], x_ref[...].astype(BF16), (((1,), (1,)), ((), ())), preferred_element_type=F32)
    scores = jax.nn.sigmoid(logits)
    remaining = scores + bias_ref[...]
    expert = lax.broadcasted_iota(jnp.int32, scores.shape, 0)
    picks = []
    for _ in range(TOP_K):
        best = jnp.max(remaining, axis=0, keepdims=True)
        first = jnp.min(jnp.where(remaining == best, expert, N_EXPERTS), axis=0, keepdims=True)
        pick = expert == first
        picks.append((first, pick))
        remaining = jnp.where(pick, -jnp.inf, remaining)
    chosen = picks[0][1]
    for _, pick in picks[1:]:
        chosen = chosen | pick
    total = jnp.sum(jnp.where(chosen, scores, 0.0), axis=0, keepdims=True)
    prefix = jnp.dot(jnp.where(chosen, 1.0, 0.0).astype(BF16), tri_ref[...], preferred_element_type=F32)
    rank = carry_ref[...] + prefix.astype(jnp.int32) - 1
    carry_ref[...] = carry_ref[...] + prefix[:, -1:].astype(jnp.int32)
    cnt_ref[...] = carry_ref[...]
    for k, (first, pick) in enumerate(picks):
        idx_ref[k:k + 1, :] = first
        gate_ref[k:k + 1, :] = jnp.sum(jnp.where(pick, scores, 0.0), axis=0, keepdims=True) / total * ROUTED_SCALE
        pos_ref[k:k + 1, :] = jnp.sum(jnp.where(pick, rank, 0), axis=0, keepdims=True)


def moe_router(xt, router_w, router_bias):
    t, d = xt.shape
    tile = ROUTER_TILE
    tri = (jnp.arange(tile)[:, None] <= jnp.arange(tile)[None, :]).astype(BF16)
    row = pl.BlockSpec((TOP_K, tile), lambda i: (0, i))
    return pl.pallas_call(
        _router_kernel,
        grid=(t // tile,),
        in_specs=[pl.BlockSpec((tile, d), lambda i: (i, 0)), pl.BlockSpec((N_EXPERTS, d), lambda i: (0, 0)),
                  pl.BlockSpec((N_EXPERTS, 1), lambda i: (0, 0)), pl.BlockSpec((tile, tile), lambda i: (0, 0))],
        out_specs=[row, row, row, pl.BlockSpec((N_EXPERTS, 1), lambda i: (0, 0))],
        out_shape=[jax.ShapeDtypeStruct((TOP_K, t), jnp.int32), jax.ShapeDtypeStruct((TOP_K, t), F32),
                   jax.ShapeDtypeStruct((TOP_K, t), jnp.int32), jax.ShapeDtypeStruct((N_EXPERTS, 1), jnp.int32)],
        scratch_shapes=[pltpu.VMEM((N_EXPERTS, 1), jnp.int32)],
        compiler_params=pltpu.CompilerParams(dimension_semantics=("arbitrary",), vmem_limit_bytes=VMEM_LIMIT_BYTES),
        name="moe_router",
    )(xt, router_w.T.astype(BF16), router_bias.reshape(N_EXPERTS, 1), tri)


def _expert_kernel(blk_expert_ref, blk_rows_ref, x_ref, wgu_ref, wdn_ref, o_ref):
    n_valid = blk_rows_ref[pl.program_id(0)]

    @pl.when(n_valid > 0)
    def _():
        h = jnp.dot(x_ref[...], wgu_ref[0], preferred_element_type=F32)
        hg, hu = h[:, :D_EXPERT], h[:, D_EXPERT:]
        act = (hg * jax.nn.sigmoid(hg) * hu).astype(BF16)
        y = jnp.dot(act, wdn_ref[0], preferred_element_type=F32)
        row = lax.broadcasted_iota(jnp.int32, y.shape, 0)
        o_ref[...] = jnp.where(row < n_valid, y, 0.0)

    @pl.when(n_valid <= 0)
    def _():
        o_ref[...] = jnp.zeros_like(o_ref)


def expert_ffn(xs, blk_expert, blk_rows, w_gu, w_dn):
    n_pad, d = xs.shape
    n_blk = n_pad // MOE_BLOCK
    return pl.pallas_call(
        _expert_kernel,
        grid_spec=pltpu.PrefetchScalarGridSpec(
            num_scalar_prefetch=2,
            grid=(n_blk,),
            in_specs=[pl.BlockSpec((MOE_BLOCK, d), lambda b, be, br: (b, 0)),
                      pl.BlockSpec((1, d, 2 * D_EXPERT), lambda b, be, br: (be[b], 0, 0)),
                      pl.BlockSpec((1, D_EXPERT, d), lambda b, be, br: (be[b], 0, 0))],
            out_specs=pl.BlockSpec((MOE_BLOCK, d), lambda b, be, br: (b, 0))),
        out_shape=jax.ShapeDtypeStruct((n_pad, d), F32),
        compiler_params=pltpu.CompilerParams(dimension_semantics=("arbitrary",), vmem_limit_bytes=VMEM_LIMIT_BYTES),
        name="expert_ffn",
    )(blk_expert, blk_rows, xs, w_gu, w_dn)


def moe_ffn(xt, router_w, router_bias, exp_w_gu, exp_w_down, sh_w_gu, sh_w_down):
    n_tok, d = xt.shape
    idx, gate, pos, counts = moe_router(xt, router_w, router_bias)
    counts = counts[:, 0]
    padded = (counts + MOE_BLOCK - 1) // MOE_BLOCK * MOE_BLOCK
    pad_ends = jnp.cumsum(padded)
    pad_start = pad_ends - padded
    n_blk = n_tok * TOP_K // MOE_BLOCK + N_EXPERTS
    n_pad = n_blk * MOE_BLOCK
    dest = pad_start[idx] + pos
    blk_lo = jnp.arange(n_blk, dtype=jnp.int32) * MOE_BLOCK
    blk_expert = jnp.minimum(jnp.searchsorted(pad_ends, blk_lo, side='right'), N_EXPERTS - 1).astype(jnp.int32)
    blk_rows = jnp.clip(pad_start[blk_expert] + counts[blk_expert] - blk_lo, 0, MOE_BLOCK).astype(jnp.int32)
    tok_of_row = jnp.zeros((n_pad,), jnp.int32).at[dest.reshape(-1)].set(jnp.tile(jnp.arange(n_tok, dtype=jnp.int32), TOP_K))
    xs = xt.astype(BF16)[tok_of_row]
    ys = expert_ffn(xs, blk_expert, blk_rows, exp_w_gu.astype(BF16), exp_w_down.astype(BF16))
    routed = jnp.sum(ys[dest] * gate[:, :, None], axis=0)
    sg, su = jnp.split(matmul(xt, sh_w_gu), 2, axis=-1)
    shared = matmul(jax.nn.silu(sg) * su, sh_w_down)
    return routed + shared


def kernel(x, rel_bias, w_in, a_kv_norm, a_kv_up, b_conv, b_a_log, b_dt_bias, b_out_norm, c_lambda_re, c_lambda_im, c_log_dt, c_b_re, c_b_im, c_c_re, c_c_im, c_d, c_glu_w, c_glu_b, w_br_a, w_br_b, w_br_c, w_o, ln1_g, ln1_b, router_w, router_bias, exp_w_gu, exp_w_down, sh_w_gu, sh_w_down, ln2_g, ln2_b):
    bn, s, d = x.shape
    t = bn * s
    for i in range(w_in.shape[0]):
        mix = hybrid_mixer(x, rel_bias, w_in[i], a_kv_norm[i], a_kv_up[i], b_conv[i], b_a_log[i], b_dt_bias[i],
                           b_out_norm[i], c_lambda_re[i], c_lambda_im[i], c_log_dt[i], c_b_re[i], c_b_im[i],
                           c_c_re[i], c_c_im[i], c_d[i], c_glu_w[i], c_glu_b[i], w_br_a[i], w_br_b[i], w_br_c[i], w_o[i])
        xt = residual_layer_norm(x.reshape(t, d), mix, ln1_g[i], ln1_b[i])
        ffn = moe_ffn(xt, router_w[i], router_bias[i], exp_w_gu[i], exp_w_down[i], sh_w_gu[i], sh_w_down[i])
        x = residual_layer_norm(xt, ffn, ln2_g[i], ln2_b[i]).reshape(bn, s, d)
    return x
```

```python
import functools
import math

import jax
import jax.numpy as jnp
import numpy as np
from jax import lax
from jax.experimental import pallas as pl
from jax.experimental.pallas import tpu as pltpu

F32 = jnp.float32
BF16 = jnp.bfloat16

D_MODEL = 1024
DEPTH = 4
A_HEADS = 8
A_HEAD_DIM = 64
A_WIDTH = A_HEADS * A_HEAD_DIM
A_KV_RANK = 128
IDX_HEADS = 8
IDX_DIM = 64
TOPK_MAX = 256
Q_BLOCK = 128
REL_BUCKETS = 32
REL_MAX_DIST = 128
B_HEADS = 4
B_HEAD_DIM = 128
B_WIDTH = B_HEADS * B_HEAD_DIM
CONV_WIDTH = 4
CHUNK = 64
C_WIDTH = 512
C_GROUP = 16
C_GROUPS = C_WIDTH // C_GROUP
C_STATE = 64
N_EXPERTS = 64
TOP_K = 8
D_EXPERT = 256
D_SHARED = 256
ROUTED_SCALE = 2.5
MOE_BLOCK = 256
SPLITS = (A_WIDTH, A_KV_RANK, IDX_HEADS * IDX_DIM, IDX_DIM, IDX_HEADS, 3 * B_WIDTH, B_WIDTH, B_HEADS, B_HEADS, C_WIDTH, 3 * D_MODEL)
DEEPNORM_ALPHA = (2 * DEPTH) ** 0.25
LN_EPS = 1e-5
RMS_EPS = 1e-6
NEG_INF = -1e30

VMEM_LIMIT_BYTES = 48 * 1024 * 1024
LANE = 128


def _pick_tile(n, candidates):
    for c in candidates:
        if n % c == 0:
            return c
    return n


def _mm_kernel(a_ref, b_ref, o_ref):
    o_ref[...] = jnp.dot(a_ref[...].astype(BF16), b_ref[...], preferred_element_type=F32).astype(o_ref.dtype)


def matmul(a, b, out_dtype=F32):
    m, k = a.shape
    _, n = b.shape
    tm = _pick_tile(m, (1024, 512, 256, 128, 8))
    tn = _pick_tile(n, (1024, 768, 512, 384, 256, 128))
    return pl.pallas_call(
        _mm_kernel,
        grid=(m // tm, n // tn),
        in_specs=[pl.BlockSpec((tm, k), lambda i, j: (i, 0)), pl.BlockSpec((k, tn), lambda i, j: (0, j))],
        out_specs=pl.BlockSpec((tm, tn), lambda i, j: (i, j)),
        out_shape=jax.ShapeDtypeStruct((m, n), out_dtype),
        compiler_params=pltpu.CompilerParams(dimension_semantics=("arbitrary", "arbitrary"), vmem_limit_bytes=VMEM_LIMIT_BYTES),
        name="matmul",
    )(a, b.astype(BF16))


def _res_ln_kernel(x_ref, y_ref, g_ref, b_ref, o_ref):
    z = DEEPNORM_ALPHA * x_ref[...] + y_ref[...]
    mu = jnp.mean(z, -1, keepdims=True)
    zc = z - mu
    var = jnp.mean(zc * zc, -1, keepdims=True)
    o_ref[...] = zc * lax.rsqrt(var + LN_EPS) * g_ref[...] + b_ref[...]


def residual_layer_norm(x, y, g, b):
    t, d = x.shape
    tm = _pick_tile(t, (512, 256, 128, 8))
    row = pl.BlockSpec((tm, d), lambda i: (i, 0))
    vec = pl.BlockSpec((1, d), lambda i: (0, 0))
    return pl.pallas_call(
        _res_ln_kernel,
        grid=(t // tm,),
        in_specs=[row, row, vec, vec],
        out_specs=row,
        out_shape=jax.ShapeDtypeStruct((t, d), F32),
        compiler_params=pltpu.CompilerParams(dimension_semantics=("arbitrary",), vmem_limit_bytes=VMEM_LIMIT_BYTES),
        name="residual_layer_norm",
    )(x, y, g.reshape(1, d), b.reshape(1, d))


def rms_norm(x, g):
    return x * lax.rsqrt(jnp.mean(x * x, -1, keepdims=True) + RMS_EPS) * g


def l2_normalize(x):
    return x * lax.rsqrt(jnp.sum(x * x, -1, keepdims=True) + RMS_EPS)


def rel_bucket(dist):
    n_exact = REL_BUCKETS // 2
    d = jnp.maximum(dist, 1).astype(F32)
    large = n_exact + (jnp.log(d / n_exact) / math.log(REL_MAX_DIST / n_exact) * (REL_BUCKETS - n_exact)).astype(jnp.int32)
    return jnp.where(dist < n_exact, dist, jnp.minimum(large, REL_BUCKETS - 1))


KEY_BLOCK = 128
DSA_CHUNK = 512
BLOCKS_PER_CHUNK = DSA_CHUNK // KEY_BLOCK
HEAD_PAIRS = A_HEADS // 2
PAIR_WIDTH = 2 * A_HEAD_DIM
INT_MIN = -2 ** 31


def _kv_kernel(c_ref, g_ref, wk_ref, wvt_ref, k_ref, vt_ref):
    c = c_ref[...]
    cn = (c * lax.rsqrt(jnp.mean(c * c, -1, keepdims=True) + RMS_EPS) * g_ref[...]).astype(BF16)
    k_ref[...] = jnp.dot(cn, wk_ref[...], preferred_element_type=F32).astype(BF16)
    vt = lax.dot_general(wvt_ref[...], cn, (((1,), (1,)), ((), ())), preferred_element_type=F32).astype(BF16)
    for i in range(vt_ref.shape[1]):
        vt_ref[0, i] = vt[:, i * KEY_BLOCK:(i + 1) * KEY_BLOCK]


def kv_project(c_kv, col, g, w_kv_up, bn, s):
    ts = 512
    w = w_kv_up.astype(BF16)
    return pl.pallas_call(
        _kv_kernel,
        grid=(bn, s // ts),
        in_specs=[pl.BlockSpec((ts, A_KV_RANK), lambda b, i: (b * (s // ts) + i, col)),
                  pl.BlockSpec((1, A_KV_RANK), lambda b, i: (0, 0)),
                  pl.BlockSpec((A_KV_RANK, A_WIDTH), lambda b, i: (0, 0)),
                  pl.BlockSpec((A_WIDTH, A_KV_RANK), lambda b, i: (0, 0))],
        out_specs=[pl.BlockSpec((None, ts, A_WIDTH), lambda b, i: (b, i, 0)),
                   pl.BlockSpec((1, ts // KEY_BLOCK, A_WIDTH, KEY_BLOCK), lambda b, i: (b, i, 0, 0))],
        out_shape=[jax.ShapeDtypeStruct((bn, s, A_WIDTH), BF16),
                   jax.ShapeDtypeStruct((bn, s // KEY_BLOCK, A_WIDTH, KEY_BLOCK), BF16)],
        compiler_params=pltpu.CompilerParams(dimension_semantics=("arbitrary", "arbitrary"), vmem_limit_bytes=VMEM_LIMIT_BYTES),
        name="kv_project",
    )(c_kv, g.reshape(1, A_KV_RANK), w[:, :A_WIDTH], w[:, A_WIDTH:].T)


def _mm_nt_kernel(wt_ref, x_ref, o_ref):
    o_ref[0] = lax.dot_general(wt_ref[...], x_ref[...].astype(BF16), (((1,), (1,)), ((), ())),
                               preferred_element_type=F32).astype(o_ref.dtype)


def matmul_nt(wt, x, bn, s, out_dtype=BF16):
    n, k = wt.shape
    ts = 512
    return pl.pallas_call(
        _mm_nt_kernel,
        grid=(bn, s // ts),
        in_specs=[pl.BlockSpec((n, k), lambda b, i: (0, 0)),
                  pl.BlockSpec((ts, k), lambda b, i: (b * (s // ts) + i, 0))],
        out_specs=pl.BlockSpec((1, n, ts), lambda b, i: (b, 0, i)),
        out_shape=jax.ShapeDtypeStruct((bn, n, s), out_dtype),
        compiler_params=pltpu.CompilerParams(dimension_semantics=("arbitrary", "arbitrary"), vmem_limit_bytes=VMEM_LIMIT_BYTES),
        name="matmul_nt",
    )(wt.astype(BF16), x)


def _sortable_key(score):
    bits = pltpu.bitcast(score, jnp.int32)
    return bits ^ (lax.shift_right_arithmetic(bits, 31) & jnp.int32(0x7FFFFFFF))


def _dsa_kernel(qt_ref, iqt_ref, wt_ref, kidx_ref, k_ref, vt_ref, bias_ref, o_ref,
                skey_ref, widx_ref, wq_ref, acc_ref, m_ref, l_ref, *, k_sel):
    j = pl.program_id(1)
    n_chunks = (j + BLOCKS_PER_CHUNK) // BLOCKS_PER_CHUNK
    q_pos = j * Q_BLOCK + lax.broadcasted_iota(jnp.int32, (1, Q_BLOCK), 1)

    for h in range(IDX_HEADS):
        widx_ref[:, h * Q_BLOCK:(h + 1) * Q_BLOCK] = iqt_ref[0, h * IDX_DIM:(h + 1) * IDX_DIM, :]
    wq_ref[...] = jnp.zeros_like(wq_ref)
    for p in range(HEAD_PAIRS):
        wq_ref[p, 0:A_HEAD_DIM, 0:Q_BLOCK] = qt_ref[0, (2 * p) * A_HEAD_DIM:(2 * p + 1) * A_HEAD_DIM, :]
        wq_ref[p, A_HEAD_DIM:PAIR_WIDTH, Q_BLOCK:2 * Q_BLOCK] = qt_ref[0, (2 * p + 1) * A_HEAD_DIM:(2 * p + 2) * A_HEAD_DIM, :]

    w_rows = [wt_ref[0, h:h + 1, :] * (IDX_HEADS ** -0.5 * IDX_DIM ** -0.5) for h in range(IDX_HEADS)]

    def idx_chunk(c, carry):
        start = pl.multiple_of(c * DSA_CHUNK, DSA_CHUNK)
        kc = kidx_ref[0, pl.ds(start, DSA_CHUNK), :]
        score = jnp.zeros((DSA_CHUNK, Q_BLOCK), F32)
        for p in range(IDX_HEADS // 2):
            z = jnp.dot(kc, widx_ref[:, p * 2 * Q_BLOCK:(p + 1) * 2 * Q_BLOCK], preferred_element_type=F32)
            score = score + jnp.maximum(z[:, :Q_BLOCK], 0.0) * w_rows[2 * p]
            score = score + jnp.maximum(z[:, Q_BLOCK:], 0.0) * w_rows[2 * p + 1]
        pos = start + lax.broadcasted_iota(jnp.int32, (DSA_CHUNK, Q_BLOCK), 0)
        skey_ref[pl.ds(start, DSA_CHUNK), :] = jnp.where(pos <= q_pos, _sortable_key(score), INT_MIN)
        return carry

    lax.fori_loop(0, n_chunks, idx_chunk, 0)

    def count_ge(cand):
        def body(c, cnt):
            start = pl.multiple_of(c * DSA_CHUNK, DSA_CHUNK)
            hit = (skey_ref[pl.ds(start, DSA_CHUNK), :] >= cand).astype(jnp.int32)
            return cnt + jnp.sum(hit.reshape(DSA_CHUNK // 8, 8, Q_BLOCK), axis=0)
        cnt = lax.fori_loop(0, n_chunks, body, jnp.zeros((8, Q_BLOCK), jnp.int32))
        return jnp.sum(cnt, axis=0, keepdims=True)

    def bit_step(i, thr_biased):
        cand_biased = thr_biased | lax.shift_left(jnp.int32(1), 31 - i)
        return jnp.where(count_ge(cand_biased ^ INT_MIN) >= k_sel, cand_biased, thr_biased)

    thr = lax.fori_loop(0, 32, bit_step, jnp.zeros((1, Q_BLOCK), jnp.int32)) ^ INT_MIN

    n_gt = count_ge(thr + 1)
    n_ge = count_ge(thr)
    tied = (n_ge > k_sel) & (thr > INT_MIN)

    @pl.when(jnp.max(tied.astype(jnp.int32)) > 0)
    def _():
        need = k_sel - n_gt

        def count_tied_upto(limit):
            def body(c, cnt):
                start = pl.multiple_of(c * DSA_CHUNK, DSA_CHUNK)
                pos = start + lax.broadcasted_iota(jnp.int32, (DSA_CHUNK, Q_BLOCK), 0)
                hit = jnp.where(skey_ref[pl.ds(start, DSA_CHUNK), :] == thr, (pos <= limit).astype(jnp.int32), 0)
                return cnt + jnp.sum(hit.reshape(DSA_CHUNK // 8, 8, Q_BLOCK), axis=0)
            cnt = lax.fori_loop(0, n_chunks, body, jnp.zeros((8, Q_BLOCK), jnp.int32))
            return jnp.sum(cnt, axis=0, keepdims=True)

        def pos_step(i, lim):
            cand = lim - lax.shift_left(jnp.int32(1), 30 - i)
            return jnp.where(count_tied_upto(cand) >= need, cand, lim)

        limit = lax.fori_loop(0, 31, pos_step, jnp.full((1, Q_BLOCK), 2 ** 31 - 1, jnp.int32))

        def demote(c, carry):
            start = pl.multiple_of(c * DSA_CHUNK, DSA_CHUNK)
            pos = start + lax.broadcasted_iota(jnp.int32, (DSA_CHUNK, Q_BLOCK), 0)
            sk = skey_ref[pl.ds(start, DSA_CHUNK), :]
            skey_ref[pl.ds(start, DSA_CHUNK), :] = jnp.where(tied & (sk == thr) & (pos > limit), thr - 1, sk)
            return carry

        lax.fori_loop(0, n_chunks, demote, 0)

    m_ref[...] = jnp.full_like(m_ref, NEG_INF)
    l_ref[...] = jnp.zeros_like(l_ref)
    acc_ref[...] = jnp.zeros_like(acc_ref)

    def attend(s_list, vt_of, n_blocks):
        for p in range(HEAD_PAIRS):
            s = s_list[p]
            m_old = m_ref[p]
            m_new = jnp.maximum(m_old, jnp.max(s, axis=0, keepdims=True))
            alpha = jnp.exp(m_old - m_new)
            pexp = jnp.exp(s - m_new)
            l_ref[p] = l_ref[p] * alpha + jnp.sum(pexp, axis=0, keepdims=True)
            pb = pexp.astype(BF16)
            pv = jnp.dot(vt_of(p, 0), pb[0:KEY_BLOCK], preferred_element_type=F32)
            for i in range(1, n_blocks):
                pv = pv + jnp.dot(vt_of(p, i), pb[i * KEY_BLOCK:(i + 1) * KEY_BLOCK], preferred_element_type=F32)
            acc_ref[p] = acc_ref[p] * alpha + pv
            m_ref[p] = m_new

    far_end = (j - 1) * KEY_BLOCK
    n_far = (jnp.maximum(j - 1, 0) + BLOCKS_PER_CHUNK - 1) // BLOCKS_PER_CHUNK

    def far_chunk(c, carry):
        start = pl.multiple_of(c * DSA_CHUNK, DSA_CHUNK)
        pos = start + lax.broadcasted_iota(jnp.int32, (DSA_CHUNK, Q_BLOCK), 0)
        sel = (skey_ref[pl.ds(start, DSA_CHUNK), :] >= thr) & (pos < far_end)
        madd = jnp.where(sel, 0.0, NEG_INF)
        madd2 = jnp.concatenate([madd, madd], axis=1)
        kc = k_ref[0, pl.ds(start, DSA_CHUNK), :]
        s_list = [jnp.dot(kc[:, p * PAIR_WIDTH:(p + 1) * PAIR_WIDTH], wq_ref[p], preferred_element_type=F32) + madd2
                  for p in range(HEAD_PAIRS)]
        blk0 = c * BLOCKS_PER_CHUNK
        attend(s_list, lambda p, i: vt_ref[0, blk0 + i, p * PAIR_WIDTH:(p + 1) * PAIR_WIDTH, :], BLOCKS_PER_CHUNK)
        return carry

    lax.fori_loop(0, n_far, far_chunk, 0)

    def near_block(kb, delta):
        start = pl.multiple_of(kb * KEY_BLOCK, KEY_BLOCK)
        pos = start + lax.broadcasted_iota(jnp.int32, (KEY_BLOCK, Q_BLOCK), 0)
        sel = (skey_ref[pl.ds(start, KEY_BLOCK), :] >= thr) & (pos <= q_pos)
        madd = jnp.where(sel, 0.0, NEG_INF)
        madd2 = jnp.concatenate([madd, madd], axis=1)
        kc = k_ref[0, pl.ds(start, KEY_BLOCK), :]
        s_list = [jnp.dot(kc[:, p * PAIR_WIDTH:(p + 1) * PAIR_WIDTH], wq_ref[p], preferred_element_type=F32)
                  + bias_ref[delta, p] + madd2 for p in range(HEAD_PAIRS)]
        attend(s_list, lambda p, i: vt_ref[0, kb, p * PAIR_WIDTH:(p + 1) * PAIR_WIDTH, :], 1)

    @pl.when(j > 0)
    def _():
        near_block(j - 1, 1)

    near_block(j, 0)

    outs = []
    for p in range(HEAD_PAIRS):
        o = acc_ref[p] / l_ref[p]
        outs.append(o[0:A_HEAD_DIM, 0:Q_BLOCK])
        outs.append(o[A_HEAD_DIM:PAIR_WIDTH, Q_BLOCK:2 * Q_BLOCK])
    o_ref[0] = jnp.concatenate(outs, axis=0).T.astype(o_ref.dtype)


def _rel_bias_tiles(rel_bias):
    tab = rel_bias[rel_bucket(jnp.arange(2 * Q_BLOCK, dtype=jnp.int32))] - rel_bias[REL_BUCKETS - 1]
    key = jnp.arange(KEY_BLOCK)[:, None]
    qry = jnp.arange(Q_BLOCK)[None, :]
    tiles = []
    for delta in (0, 1):
        dist = jnp.maximum(delta * Q_BLOCK + qry - key, 0)
        t = jnp.moveaxis(tab[dist], 2, 0)
        t = t.reshape(HEAD_PAIRS, 2, KEY_BLOCK, Q_BLOCK)
        tiles.append(jnp.concatenate([t[:, 0], t[:, 1]], axis=-1))
    return jnp.stack(tiles)


def dsa_attention(qiq_t, wt, k_idx, k, vt, bias_tiles, bn, s):
    assert A_WIDTH == IDX_HEADS * IDX_DIM
    scal_rows = wt.shape[1]
    assert scal_rows % IDX_HEADS == 0
    k_sel = min(TOPK_MAX, s // 4)
    nb = s // Q_BLOCK
    s_pad = s + DSA_CHUNK
    return pl.pallas_call(
        functools.partial(_dsa_kernel, k_sel=k_sel),
        grid=(bn, nb),
        in_specs=[pl.BlockSpec((1, A_WIDTH, Q_BLOCK), lambda b, j: (b, 0, j)),
                  pl.BlockSpec((1, IDX_HEADS * IDX_DIM, Q_BLOCK), lambda b, j: (b, 1, j)),
                  pl.BlockSpec((1, IDX_HEADS, Q_BLOCK), lambda b, j: (b, scal_rows // IDX_HEADS - 1, j)),
                  pl.BlockSpec((1, s, IDX_DIM), lambda b, j: (b, 0, 0)),
                  pl.BlockSpec((1, s, A_WIDTH), lambda b, j: (b, 0, 0)),
                  pl.BlockSpec((1, s // KEY_BLOCK, A_WIDTH, KEY_BLOCK), lambda b, j: (b, 0, 0, 0)),
                  pl.BlockSpec((2, HEAD_PAIRS, KEY_BLOCK, 2 * Q_BLOCK), lambda b, j: (0, 0, 0, 0))],
        out_specs=pl.BlockSpec((1, Q_BLOCK, A_WIDTH), lambda b, j: (b, j, 0)),
        out_shape=jax.ShapeDtypeStruct((bn, s, A_WIDTH), BF16),
        scratch_shapes=[pltpu.VMEM((s_pad, Q_BLOCK), jnp.int32),
                        pltpu.VMEM((IDX_DIM, IDX_HEADS * Q_BLOCK), BF16),
                        pltpu.VMEM((HEAD_PAIRS, PAIR_WIDTH, 2 * Q_BLOCK), BF16),
                        pltpu.VMEM((HEAD_PAIRS, PAIR_WIDTH, 2 * Q_BLOCK), F32),
                        pltpu.VMEM((HEAD_PAIRS, 1, 2 * Q_BLOCK), F32),
                        pltpu.VMEM((HEAD_PAIRS, 1, 2 * Q_BLOCK), F32)],
        compiler_params=pltpu.CompilerParams(dimension_semantics=("arbitrary", "arbitrary"), vmem_limit_bytes=56 * 1024 * 1024),
        name="dsa_attention",
    )(qiq_t, qiq_t, wt, k_idx, k, vt, bias_tiles)


GDN_CHUNK = B_HEAD_DIM
GDN_BLOCK = 512
GDN_HALO = 8
NEUMANN_STEPS = 6


def _split_bf16(x):
    hi = x.astype(BF16)
    return hi, (x - hi.astype(F32)).astype(BF16)


def _dot_f32(a, b):
    a_hi, a_lo = _split_bf16(a)
    b_hi, b_lo = _split_bf16(b)
    d = lambda x, y: jnp.dot(x, y, preferred_element_type=F32)
    return d(a_hi, b_hi) + (d(a_hi, b_lo) + d(a_lo, b_hi))


def _bdot(a, b):
    return jnp.dot(a.astype(BF16), b.astype(BF16), preferred_element_type=F32)


def _bdot_nt(a, b):
    return lax.dot_general(a.astype(BF16), b.astype(BF16), (((1,), (1,)), ((), ())), preferred_element_type=F32)


def _bdot_tn(a, b):
    return lax.dot_general(a.astype(BF16), b.astype(BF16), (((0,), (0,)), ((), ())), preferred_element_type=F32)


def _softplus(x):
    return jnp.maximum(x, 0.0) + jnp.log(1.0 + jnp.exp(-jnp.abs(x)))


def _gdn_kernel(hq_ref, hk_ref, hv_ref, hz_ref, wq_ref, wk_ref, wv_ref, ab_ref, const_ref, gout_ref, tri_ref,
                o_ref, state_ref, xbuf_ref):
    head = pl.program_id(1)

    @pl.when(pl.program_id(2) == 0)
    def _():
        state_ref[...] = jnp.zeros_like(state_ref)
        xbuf_ref[:, 0:GDN_HALO, :] = jnp.zeros((3, GDN_HALO, B_HEAD_DIM), F32)

    def conv_silu(n, x_ref, w_ref):
        x = x_ref[...]
        xbuf_ref[n, GDN_HALO:, :] = x
        acc = xbuf_ref[n, GDN_HALO - CONV_WIDTH + 1:GDN_HALO - CONV_WIDTH + 1 + GDN_BLOCK, :] * w_ref[0:1, :]
        for j in range(1, CONV_WIDTH):
            lo = GDN_HALO - CONV_WIDTH + 1 + j
            acc = acc + xbuf_ref[n, lo:lo + GDN_BLOCK, :] * w_ref[j:j + 1, :]
        xbuf_ref[n, 0:GDN_HALO, :] = x[GDN_BLOCK - GDN_HALO:, :]
        return acc * jax.nn.sigmoid(acc)

    q = conv_silu(0, hq_ref, wq_ref)
    k = conv_silu(1, hk_ref, wk_ref)
    v = conv_silu(2, hv_ref, wv_ref)
    q = q * lax.rsqrt(jnp.sum(q * q, -1, keepdims=True) + RMS_EPS) * B_HEAD_DIM ** -0.5
    k = k * lax.rsqrt(jnp.sum(k * k, -1, keepdims=True) + RMS_EPS)

    a_row = ab_ref[0, pl.ds(head, 1), :]
    b_row = ab_ref[0, pl.ds(B_HEADS + head, 1), :]
    beta_row = jax.nn.sigmoid(b_row)
    g_row = const_ref[0, 0:1, :] * _softplus(a_row + const_ref[0, 1:2, :])
    gc_rows = _dot_f32(jnp.broadcast_to(g_row, (8, GDN_BLOCK)), tri_ref[...])[0:1, :]

    row_i = lax.broadcasted_iota(jnp.int32, (GDN_CHUNK, GDN_CHUNK), 0)
    col_j = lax.broadcasted_iota(jnp.int32, (GDN_CHUNK, GDN_CHUNK), 1)
    eye = jnp.where(row_i == col_j, 1.0, 0.0)
    z_all = hz_ref[...]
    for c in range(GDN_BLOCK // GDN_CHUNK):
        sl = slice(c * GDN_CHUNK, (c + 1) * GDN_CHUNK)
        qc, kc, vc = q[sl], k[sl], v[sl]
        g_lane = jnp.broadcast_to(gc_rows[:, sl], (GDN_CHUNK, GDN_CHUNK))
        g_sub = g_lane.T
        beta_sub = jnp.broadcast_to(beta_row[:, sl], (GDN_CHUNK, GDN_CHUNK)).T
        g_end = jnp.broadcast_to(g_lane[:, GDN_CHUNK - 1:GDN_CHUNK], (GDN_CHUNK, GDN_CHUNK))
        decay = jnp.exp(jnp.where(row_i >= col_j, g_sub - g_lane, -jnp.inf))
        kb = kc * beta_sub
        m = jnp.where(row_i > col_j, _bdot_nt(kb, kc) * decay, 0.0)
        x = -m
        t_inv = eye + x
        for _ in range(NEUMANN_STEPS):
            x = _dot_f32(x, x)
            t_inv = t_inv + _dot_f32(t_inv, x)
        g_exp = jnp.exp(g_sub)
        u = _bdot(t_inv, vc * beta_sub)
        w = _bdot(t_inv, kb * g_exp)
        qk = _bdot_nt(qc, kc) * decay
        state = state_ref[...]
        v_new = u - _bdot(w, state)
        o = _bdot(qc * g_exp, state) + _bdot(qk, v_new)
        state_ref[...] = state * jnp.exp(g_end) + _bdot_tn(kc * jnp.exp(g_end - g_sub), v_new)
        o = o * lax.rsqrt(jnp.mean(o * o, -1, keepdims=True) + RMS_EPS) * gout_ref[...]
        zc = z_all[sl]
        o_ref[sl, :] = (o * (zc * jax.nn.sigmoid(zc))).astype(o_ref.dtype)


def gated_deltanet(h, col0, ab_t, conv_w, a_log, dt_bias, out_norm_g, bn, s):
    nblk = s // GDN_BLOCK
    tok = jnp.arange(GDN_BLOCK)
    tri = ((tok[:, None] // GDN_CHUNK == tok[None, :] // GDN_CHUNK) & (tok[:, None] <= tok[None, :])).astype(F32)
    const = jnp.stack([jnp.broadcast_to(-jnp.exp(a_log)[:, None], (B_HEADS, GDN_BLOCK)),
                       jnp.broadcast_to(dt_bias[:, None], (B_HEADS, GDN_BLOCK))], axis=1)
    const = jnp.pad(const, ((0, 0), (0, 6), (0, 0)))
    hcol = lambda part: pl.BlockSpec((GDN_BLOCK, B_HEAD_DIM), lambda b, hh, i: (b * nblk + i, col0 + part * B_HEADS + hh))
    wcol = lambda part: pl.BlockSpec((CONV_WIDTH, B_HEAD_DIM), lambda b, hh, i: (0, part * B_HEADS + hh))
    return pl.pallas_call(
        _gdn_kernel,
        grid=(bn, B_HEADS, nblk),
        in_specs=[hcol(0), hcol(1), hcol(2), hcol(3), wcol(0), wcol(1), wcol(2),
                  pl.BlockSpec((1, 16, GDN_BLOCK), lambda b, hh, i: (b, 0, i)),
                  pl.BlockSpec((1, 8, GDN_BLOCK), lambda b, hh, i: (hh, 0, 0)),
                  pl.BlockSpec((1, B_HEAD_DIM), lambda b, hh, i: (0, 0)),
                  pl.BlockSpec((GDN_BLOCK, GDN_BLOCK), lambda b, hh, i: (0, 0))],
        out_specs=pl.BlockSpec((GDN_BLOCK, B_HEAD_DIM), lambda b, hh, i: (b * nblk + i, hh)),
        out_shape=jax.ShapeDtypeStruct((bn * s, B_WIDTH), BF16),
        scratch_shapes=[pltpu.VMEM((B_HEAD_DIM, B_HEAD_DIM), F32),
                        pltpu.VMEM((3, GDN_HALO + GDN_BLOCK, B_HEAD_DIM), F32)],
        compiler_params=pltpu.CompilerParams(dimension_semantics=("arbitrary", "arbitrary", "arbitrary"), vmem_limit_bytes=VMEM_LIMIT_BYTES),
        name="gated_deltanet",
    )(h, h, h, h, conv_w, conv_w, conv_w, ab_t, const, out_norm_g.reshape(1, B_HEAD_DIM), tri)


S5_CHUNK = 32
S5_ROW = S5_CHUNK * C_GROUP
STATE2 = 2 * C_STATE


def _s5_tables(lam_re, lam_im, log_dt, b_re, b_im, c_re, c_im, d_skip):
    hp = lax.Precision.HIGHEST
    ell = S5_CHUNK
    dt = jnp.exp(log_dt)[:, None]
    lr, li = lam_re, lam_im
    mag = jnp.exp(lr * dt)
    a_re, a_im = mag * jnp.cos(li * dt), mag * jnp.sin(li * dt)
    den = lr * lr + li * li
    f_re = ((a_re - 1.0) * lr + a_im * li) / den
    f_im = (a_im * lr - (a_re - 1.0) * li) / den
    bb_re = f_re[..., None] * b_re - f_im[..., None] * b_im
    bb_im = f_re[..., None] * b_im + f_im[..., None] * b_re
    tau = jnp.arange(ell + 1, dtype=F32)[:, None, None]
    pmag, ang = jnp.exp(tau * (lr * dt)), tau * (li * dt)
    p_re, p_im = pmag * jnp.cos(ang), pmag * jnp.sin(ang)
    ab_re = p_re[:ell, ..., None] * bb_re - p_im[:ell, ..., None] * bb_im
    ab_im = p_re[:ell, ..., None] * bb_im + p_im[:ell, ..., None] * bb_re
    kern = (jnp.einsum('gip,tgpj->tgij', c_re, ab_re, precision=hp)
            - jnp.einsum('gip,tgpj->tgij', c_im, ab_im, precision=hp))
    lag = jnp.arange(ell)[None, :] - jnp.arange(ell)[:, None]
    toep = jnp.where((lag >= 0)[:, :, None, None, None], kern[jnp.maximum(lag, 0)], 0.0)
    toep = jnp.transpose(toep, (2, 0, 4, 1, 3)).reshape(C_GROUPS, S5_ROW, S5_ROW)
    w_re = jnp.transpose(ab_re[::-1], (1, 0, 3, 2)).reshape(C_GROUPS, S5_ROW, C_STATE)
    w_im = jnp.transpose(ab_im[::-1], (1, 0, 3, 2)).reshape(C_GROUPS, S5_ROW, C_STATE)
    tw = jnp.concatenate([toep, w_re, w_im, w_im, w_re], axis=-1).astype(BF16)
    pt_re = jnp.swapaxes(p_re[1:], 0, 1)[:, :, None, :]
    pt_im = jnp.swapaxes(p_im[1:], 0, 1)[:, :, None, :]
    ca_re = c_re[:, None] * pt_re - c_im[:, None] * pt_im
    ca_im = c_re[:, None] * pt_im + c_im[:, None] * pt_re
    v_re = jnp.transpose(ca_re, (0, 3, 1, 2)).reshape(C_GROUPS, C_STATE, S5_ROW)
    v_im = jnp.transpose(-ca_im, (0, 3, 1, 2)).reshape(C_GROUPS, C_STATE, S5_ROW)
    v = jnp.concatenate([v_re, v_im], axis=1).astype(BF16)
    al_re, al_im = p_re[ell], p_im[ell]
    coef = jnp.stack([jnp.concatenate([al_re, al_re], -1), jnp.concatenate([-al_im, al_im], -1),
                      jnp.concatenate([al_im, -al_im], -1)], axis=1)
    d_flat = jnp.tile(d_skip, (1, ell)).reshape(C_GROUPS, 1, S5_ROW)
    return tw, v, coef, d_flat


def _s5_kernel(u_ref, tw_ref, v_ref, coef_ref, d_ref, o_ref, sin_ref, xin_ref, *, bn):
    rows = u_ref.shape[1]
    n_chunks = rows // bn
    u = u_ref[0]
    tw = jnp.dot(u.astype(BF16), tw_ref[0], preferred_element_type=F32)
    o_ref[0] = tw[:, :S5_ROW] + u * d_ref[0]
    sin_ref[...] = tw[:, S5_ROW:]
    a1, a2, a2s = coef_ref[0, 0:1, :], coef_ref[0, 1:2, :], coef_ref[0, 2:3, :]

    def step(c, carry):
        new = []
        for b in range(bn):
            x, xs = carry[2 * b], carry[2 * b + 1]
            row = b * n_chunks + c
            xin_ref[pl.ds(row, 1), :] = x
            s_row = sin_ref[pl.ds(row, 1), :]
            new.append(a1 * x + a2 * xs + s_row[:, :STATE2])
            new.append(a1 * xs + a2s * x + s_row[:, STATE2:])
        return tuple(new)

    zero = jnp.zeros((1, STATE2), F32)
    lax.fori_loop(0, n_chunks, step, (zero,) * (2 * bn))
    y = o_ref[0] + jnp.dot(xin_ref[...].astype(BF16), v_ref[0], preferred_element_type=F32)
    o_ref[0] = jax.nn.gelu(y)


def _glu_kernel(y_ref, w_ref, b_ref, o_ref):
    y = y_ref[...]
    gate = jnp.dot(y.astype(BF16), w_ref[...], preferred_element_type=F32) + b_ref[...]
    o_ref[...] = (y * jax.nn.sigmoid(gate)).astype(o_ref.dtype)


def s5_branch(u, lam_re, lam_im, log_dt, b_re, b_im, c_re, c_im, d_skip, glu_w, glu_b):
    bn, s, _ = u.shape
    n_chunks = s // S5_CHUNK
    rows = bn * n_chunks
    tw, v, coef, d_flat = _s5_tables(lam_re, lam_im, log_dt, b_re, b_im, c_re, c_im, d_skip)
    uf = jnp.transpose(u.reshape(bn, n_chunks, S5_CHUNK, C_GROUPS, C_GROUP), (3, 0, 1, 2, 4)).reshape(C_GROUPS, rows, S5_ROW)
    per_group = lambda shape: pl.BlockSpec((1,) + shape, lambda g: (g, 0, 0))
    y = pl.pallas_call(
        functools.partial(_s5_kernel, bn=bn),
        grid=(C_GROUPS,),
        in_specs=[per_group((rows, S5_ROW)), per_group((S5_ROW, S5_ROW + 2 * STATE2)), per_group((STATE2, S5_ROW)),
                  per_group((3, STATE2)), per_group((1, S5_ROW))],
        out_specs=per_group((rows, S5_ROW)),
        out_shape=jax.ShapeDtypeStruct((C_GROUPS, rows, S5_ROW), F32),
        scratch_shapes=[pltpu.VMEM((rows, 2 * STATE2), F32), pltpu.VMEM((rows, STATE2), F32)],
        compiler_params=pltpu.CompilerParams(dimension_semantics=("arbitrary",), vmem_limit_bytes=VMEM_LIMIT_BYTES),
        name="s5_scan",
    )(uf, tw, v, coef, d_flat)
    y = jnp.transpose(y.reshape(C_GROUPS, bn, n_chunks, S5_CHUNK, C_GROUP), (1, 2, 3, 0, 4)).reshape(bn * s, C_WIDTH)
    t = bn * s
    tm = _pick_tile(t, (1024, 512, 256, 128, 8))
    return pl.pallas_call(
        _glu_kernel,
        grid=(t // tm,),
        in_specs=[pl.BlockSpec((tm, C_WIDTH), lambda i: (i, 0)), pl.BlockSpec((C_WIDTH, C_WIDTH), lambda i: (0, 0)),
                  pl.BlockSpec((1, C_WIDTH), lambda i: (0, 0))],
        out_specs=pl.BlockSpec((tm, C_WIDTH), lambda i: (i, 0)),
        out_shape=jax.ShapeDtypeStruct((t, C_WIDTH), BF16),
        compiler_params=pltpu.CompilerParams(dimension_semantics=("arbitrary",), vmem_limit_bytes=VMEM_LIMIT_BYTES),
        name="s5_glu",
    )(y, glu_w.astype(BF16), glu_b.reshape(1, C_WIDTH))


def hybrid_mixer(x, rel_bias, w_in, a_kv_norm, a_kv_up, b_conv, b_a_log, b_dt_bias, b_out_norm,
                 c_lambda_re, c_lambda_im, c_log_dt, c_b_re, c_b_im, c_c_re, c_c_im, c_d, c_glu_w, c_glu_b,
                 w_br_a, w_br_b, w_br_c, w_o, ln_g, ln_b):
    bn, s, d = x.shape
    t = bn * s
    xt = x.reshape(t, d)
    offsets = [0] + np.cumsum(SPLITS).tolist()
    w_cols = [w_in[:, offsets[i]:offsets[i + 1]] for i in range(len(SPLITS))]
    (w_aq, w_ckv, w_iq, w_ik, w_iw, w_bqkv, w_bz, w_ba, w_bb, w_cu, w_gate) = w_cols
    w_ik_pad = jnp.pad(w_ik, ((0, 0), (0, LANE - IDX_DIM)))
    h = matmul(xt, jnp.concatenate([w_gate, w_ckv, w_ik_pad, w_bqkv, w_bz, w_cu], axis=1))
    o = np.cumsum([0, 3 * D_MODEL, A_KV_RANK, LANE, 3 * B_WIDTH, B_WIDTH, C_WIDTH]).tolist()
    idx_k = h[:, o[2]:o[2] + IDX_DIM].astype(BF16).reshape(bn, s, IDX_DIM)
    c_u = h[:, o[5]:o[6]].reshape(bn, s, C_WIDTH)
    qiq_t = matmul_nt(jnp.concatenate([w_aq * A_HEAD_DIM ** -0.5, w_iq], axis=1).T, xt, bn, s)
    scal_t = matmul_nt(jnp.concatenate([w_ba, w_bb, w_iw], axis=1).T, xt, bn, s, out_dtype=F32)
    k, vt = kv_project(h, o[1] // LANE, a_kv_norm, a_kv_up, bn, s)
    ya = dsa_attention(qiq_t, scal_t, idx_k, k, vt, _rel_bias_tiles(rel_bias), bn, s)
    yb = gated_deltanet(h, o[3] // LANE, scal_t, b_conv, b_a_log, b_dt_bias, b_out_norm, bn, s)
    yc = s5_branch(c_u, c_lambda_re, c_lambda_im, c_log_dt, c_b_re, c_b_im, c_c_re, c_c_im, c_d, c_glu_w, c_glu_b)
    return merge_branches(ya.reshape(t, A_WIDTH), yb, yc, h, w_br_a, w_br_b, w_br_c, w_o, xt, ln_g, ln_b)


def _merge_kernel(ya_ref, yb_ref, yc_ref, ga_ref, gb_ref, gc_ref, wa_ref, wb_ref, wc_ref, wo_ref, x_ref, g_ref, b_ref, o_ref):
    branch = lambda y_ref, w_ref, gate_ref: jax.nn.sigmoid(gate_ref[...]) * jnp.dot(y_ref[...], w_ref[...], preferred_element_type=F32)
    merged = branch(ya_ref, wa_ref, ga_ref) + branch(yb_ref, wb_ref, gb_ref) + branch(yc_ref, wc_ref, gc_ref)
    mix = jnp.dot(merged.astype(BF16), wo_ref[...], preferred_element_type=F32)
    z = DEEPNORM_ALPHA * x_ref[...] + mix
    mu = jnp.mean(z, -1, keepdims=True)
    zc = z - mu
    var = jnp.mean(zc * zc, -1, keepdims=True)
    o_ref[...] = zc * lax.rsqrt(var + LN_EPS) * g_ref[...] + b_ref[...]


def merge_branches(ya, yb, yc, h, w_br_a, w_br_b, w_br_c, w_o, x, ln_g, ln_b):
    t, d = x.shape
    tm = _pick_tile(t, (512, 256, 128, 8))
    row = lambda width, col=0: pl.BlockSpec((tm, width), lambda i: (i, col))
    full = lambda a: pl.BlockSpec(a.shape, lambda i: (0, 0))
    ws = [w.astype(BF16) for w in (w_br_a, w_br_b, w_br_c, w_o)]
    return pl.pallas_call(
        _merge_kernel,
        grid=(t // tm,),
        in_specs=[row(A_WIDTH), row(B_WIDTH), row(C_WIDTH), row(d, 0), row(d, 1), row(d, 2)] + [full(w) for w in ws]
                 + [row(d), pl.BlockSpec((1, d), lambda i: (0, 0)), pl.BlockSpec((1, d), lambda i: (0, 0))],
        out_specs=row(d),
        out_shape=jax.ShapeDtypeStruct((t, d), F32),
        compiler_params=pltpu.CompilerParams(dimension_semantics=("arbitrary",), vmem_limit_bytes=VMEM_LIMIT_BYTES),
        name="merge_branches",
    )(ya, yb, yc, h, h, h, *ws, x, ln_g.reshape(1, d), ln_b.reshape(1, d))


ROUTER_TILE = 512


def _router_kernel(x_ref, rwt_ref, bias_ref, tri_ref, idx_ref, gate_ref, pos_ref, cnt_ref, carry_ref):
    @pl.when(pl.program_id(0) == 0)
    def _():
        carry_ref[...] = jnp.zeros_like(carry_ref)

    logits = lax.dot_general(rwt_ref[...], x_ref[...].astype(BF16), (((1,), (1,)), ((), ())), preferred_element_type=F32)
    scores = jax.nn.sigmoid(logits)
    remaining = scores + bias_ref[...]
    expert = lax.broadcasted_iota(jnp.int32, scores.shape, 0)
    picks = []
    for _ in range(TOP_K):
        best = jnp.max(remaining, axis=0, keepdims=True)
        first = jnp.min(jnp.where(remaining == best, expert, N_EXPERTS), axis=0, keepdims=True)
        pick = expert == first
        picks.append((first, pick))
        remaining = jnp.where(pick, -jnp.inf, remaining)
    chosen = picks[0][1]
    for _, pick in picks[1:]:
        chosen = chosen | pick
    total = jnp.sum(jnp.where(chosen, scores, 0.0), axis=0, keepdims=True)
    prefix = jnp.dot(jnp.where(chosen, 1.0, 0.0).astype(BF16), tri_ref[...], preferred_element_type=F32)
    rank = carry_ref[...] + prefix.astype(jnp.int32) - 1
    carry_ref[...] = carry_ref[...] + prefix[:, -1:].astype(jnp.int32)
    cnt_ref[...] = carry_ref[...]
    for k, (first, pick) in enumerate(picks):
        idx_ref[k:k + 1, :] = first
        gate_ref[k:k + 1, :] = jnp.sum(jnp.where(pick, scores, 0.0), axis=0, keepdims=True) / total * ROUTED_SCALE
        pos_ref[k:k + 1, :] = jnp.sum(jnp.where(pick, rank, 0), axis=0, keepdims=True)


def moe_router(xt, router_w, router_bias):
    t, d = xt.shape
    tile = ROUTER_TILE
    tri = (jnp.arange(tile)[:, None] <= jnp.arange(tile)[None, :]).astype(BF16)
    row = pl.BlockSpec((TOP_K, tile), lambda i: (0, i))
    return pl.pallas_call(
        _router_kernel,
        grid=(t // tile,),
        in_specs=[pl.BlockSpec((tile, d), lambda i: (i, 0)), pl.BlockSpec((N_EXPERTS, d), lambda i: (0, 0)),
                  pl.BlockSpec((N_EXPERTS, 1), lambda i: (0, 0)), pl.BlockSpec((tile, tile), lambda i: (0, 0))],
        out_specs=[row, row, row, pl.BlockSpec((N_EXPERTS, 1), lambda i: (0, 0))],
        out_shape=[jax.ShapeDtypeStruct((TOP_K, t), jnp.int32), jax.ShapeDtypeStruct((TOP_K, t), F32),
                   jax.ShapeDtypeStruct((TOP_K, t), jnp.int32), jax.ShapeDtypeStruct((N_EXPERTS, 1), jnp.int32)],
        scratch_shapes=[pltpu.VMEM((N_EXPERTS, 1), jnp.int32)],
        compiler_params=pltpu.CompilerParams(dimension_semantics=("arbitrary",), vmem_limit_bytes=VMEM_LIMIT_BYTES),
        name="moe_router",
    )(xt, router_w.T.astype(BF16), router_bias.reshape(N_EXPERTS, 1), tri)


def _expert_kernel(blk_expert_ref, blk_rows_ref, x_ref, wgu_ref, wdn_ref, o_ref):
    n_valid = blk_rows_ref[pl.program_id(0)]

    @pl.when(n_valid > 0)
    def _():
        h = jnp.dot(x_ref[...], wgu_ref[0], preferred_element_type=F32)
        hg, hu = h[:, :D_EXPERT], h[:, D_EXPERT:]
        act = (hg * jax.nn.sigmoid(hg) * hu).astype(BF16)
        y = jnp.dot(act, wdn_ref[0], preferred_element_type=F32)
        row = lax.broadcasted_iota(jnp.int32, y.shape, 0)
        o_ref[...] = jnp.where(row < n_valid, y, 0.0)

    @pl.when(n_valid <= 0)
    def _():
        o_ref[...] = jnp.zeros_like(o_ref)


def expert_ffn(xs, blk_expert, blk_rows, w_gu, w_dn):
    n_pad, d = xs.shape
    n_blk = n_pad // MOE_BLOCK
    return pl.pallas_call(
        _expert_kernel,
        grid_spec=pltpu.PrefetchScalarGridSpec(
            num_scalar_prefetch=2,
            grid=(n_blk,),
            in_specs=[pl.BlockSpec((MOE_BLOCK, d), lambda b, be, br: (b, 0)),
                      pl.BlockSpec((1, d, 2 * D_EXPERT), lambda b, be, br: (be[b], 0, 0)),
                      pl.BlockSpec((1, D_EXPERT, d), lambda b, be, br: (be[b], 0, 0))],
            out_specs=pl.BlockSpec((MOE_BLOCK, d), lambda b, be, br: (b, 0))),
        out_shape=jax.ShapeDtypeStruct((n_pad, d), F32),
        compiler_params=pltpu.CompilerParams(dimension_semantics=("arbitrary",), vmem_limit_bytes=VMEM_LIMIT_BYTES),
        name="expert_ffn",
    )(blk_expert, blk_rows, xs, w_gu, w_dn)


def moe_ffn(xt, router_w, router_bias, exp_w_gu, exp_w_down, sh_w_gu, sh_w_down):
    n_tok, d = xt.shape
    idx, gate, pos, counts = moe_router(xt, router_w, router_bias)
    counts = counts[:, 0]
    padded = (counts + MOE_BLOCK - 1) // MOE_BLOCK * MOE_BLOCK
    pad_ends = jnp.cumsum(padded)
    pad_start = pad_ends - padded
    n_blk = n_tok * TOP_K // MOE_BLOCK + N_EXPERTS
    n_pad = n_blk * MOE_BLOCK
    dest = pad_start[idx] + pos
    blk_lo = jnp.arange(n_blk, dtype=jnp.int32) * MOE_BLOCK
    blk_expert = jnp.minimum(jnp.searchsorted(pad_ends, blk_lo, side='right'), N_EXPERTS - 1).astype(jnp.int32)
    blk_rows = jnp.clip(pad_start[blk_expert] + counts[blk_expert] - blk_lo, 0, MOE_BLOCK).astype(jnp.int32)
    tok_of_row = jnp.zeros((n_pad,), jnp.int32).at[dest.reshape(-1)].set(jnp.tile(jnp.arange(n_tok, dtype=jnp.int32), TOP_K))
    xs = xt.astype(BF16)[tok_of_row]
    ys = expert_ffn(xs, blk_expert, blk_rows, exp_w_gu.astype(BF16), exp_w_down.astype(BF16))
    routed = jnp.sum(ys[dest] * gate[:, :, None], axis=0)
    sg, su = jnp.split(matmul(xt, sh_w_gu), 2, axis=-1)
    shared = matmul(jax.nn.silu(sg) * su, sh_w_down)
    return routed + shared


def kernel(x, rel_bias, w_in, a_kv_norm, a_kv_up, b_conv, b_a_log, b_dt_bias, b_out_norm, c_lambda_re, c_lambda_im, c_log_dt, c_b_re, c_b_im, c_c_re, c_c_im, c_d, c_glu_w, c_glu_b, w_br_a, w_br_b, w_br_c, w_o, ln1_g, ln1_b, router_w, router_bias, exp_w_gu, exp_w_down, sh_w_gu, sh_w_down, ln2_g, ln2_b):
    bn, s, d = x.shape
    t = bn * s
    for i in range(w_in.shape[0]):
        xt = hybrid_mixer(x, rel_bias, w_in[i], a_kv_norm[i], a_kv_up[i], b_conv[i], b_a_log[i], b_dt_bias[i],
                          b_out_norm[i], c_lambda_re[i], c_lambda_im[i], c_log_dt[i], c_b_re[i], c_b_im[i],
                          c_c_re[i], c_c_im[i], c_d[i], c_glu_w[i], c_glu_b[i], w_br_a[i], w_br_b[i], w_br_c[i], w_o[i],
                          ln1_g[i], ln1_b[i])
        ffn = moe_ffn(xt, router_w[i], router_bias[i], exp_w_gu[i], exp_w_down[i], sh_w_gu[i], sh_w_down[i])
        x = residual_layer_norm(xt, ffn, ln2_g[i], ln2_b[i]).reshape(bn, s, d)
    return x
```

```python
import functools
import math

import jax
import jax.numpy as jnp
import numpy as np
from jax import lax
from jax.experimental import pallas as pl
from jax.experimental.pallas import tpu as pltpu

F32 = jnp.float32
BF16 = jnp.bfloat16

D_MODEL = 1024
DEPTH = 4
A_HEADS = 8
A_HEAD_DIM = 64
A_WIDTH = A_HEADS * A_HEAD_DIM
A_KV_RANK = 128
IDX_HEADS = 8
IDX_DIM = 64
TOPK_MAX = 256
Q_BLOCK = 128
REL_BUCKETS = 32
REL_MAX_DIST = 128
B_HEADS = 4
B_HEAD_DIM = 128
B_WIDTH = B_HEADS * B_HEAD_DIM
CONV_WIDTH = 4
CHUNK = 64
C_WIDTH = 512
C_GROUP = 16
C_GROUPS = C_WIDTH // C_GROUP
C_STATE = 64
N_EXPERTS = 64
TOP_K = 8
D_EXPERT = 256
D_SHARED = 256
ROUTED_SCALE = 2.5
MOE_BLOCK = 256
SPLITS = (A_WIDTH, A_KV_RANK, IDX_HEADS * IDX_DIM, IDX_DIM, IDX_HEADS, 3 * B_WIDTH, B_WIDTH, B_HEADS, B_HEADS, C_WIDTH, 3 * D_MODEL)
DEEPNORM_ALPHA = (2 * DEPTH) ** 0.25
LN_EPS = 1e-5
RMS_EPS = 1e-6
NEG_INF = -1e30

VMEM_LIMIT_BYTES = 48 * 1024 * 1024
LANE = 128


def _pick_tile(n, candidates):
    for c in candidates:
        if n % c == 0:
            return c
    return n


def _mm_kernel(a_ref, b_ref, o_ref):
    o_ref[...] = jnp.dot(a_ref[...].astype(BF16), b_ref[...], preferred_element_type=F32).astype(o_ref.dtype)


def matmul(a, b, out_dtype=F32):
    m, k = a.shape
    _, n = b.shape
    tm = _pick_tile(m, (1024, 512, 256, 128, 8))
    tn = _pick_tile(n, (1024, 768, 512, 384, 256, 128))
    return pl.pallas_call(
        _mm_kernel,
        grid=(m // tm, n // tn),
        in_specs=[pl.BlockSpec((tm, k), lambda i, j: (i, 0)), pl.BlockSpec((k, tn), lambda i, j: (0, j))],
        out_specs=pl.BlockSpec((tm, tn), lambda i, j: (i, j)),
        out_shape=jax.ShapeDtypeStruct((m, n), out_dtype),
        compiler_params=pltpu.CompilerParams(dimension_semantics=("arbitrary", "arbitrary"), vmem_limit_bytes=VMEM_LIMIT_BYTES),
        name="matmul",
    )(a, b.astype(BF16))


def _res_ln_kernel(x_ref, y_ref, g_ref, b_ref, o_ref):
    z = DEEPNORM_ALPHA * x_ref[...] + y_ref[...]
    mu = jnp.mean(z, -1, keepdims=True)
    zc = z - mu
    var = jnp.mean(zc * zc, -1, keepdims=True)
    o_ref[...] = zc * lax.rsqrt(var + LN_EPS) * g_ref[...] + b_ref[...]


def residual_layer_norm(x, y, g, b):
    t, d = x.shape
    tm = _pick_tile(t, (512, 256, 128, 8))
    row = pl.BlockSpec((tm, d), lambda i: (i, 0))
    vec = pl.BlockSpec((1, d), lambda i: (0, 0))
    return pl.pallas_call(
        _res_ln_kernel,
        grid=(t // tm,),
        in_specs=[row, row, vec, vec],
        out_specs=row,
        out_shape=jax.ShapeDtypeStruct((t, d), F32),
        compiler_params=pltpu.CompilerParams(dimension_semantics=("arbitrary",), vmem_limit_bytes=VMEM_LIMIT_BYTES),
        name="residual_layer_norm",
    )(x, y, g.reshape(1, d), b.reshape(1, d))


def rms_norm(x, g):
    return x * lax.rsqrt(jnp.mean(x * x, -1, keepdims=True) + RMS_EPS) * g


def l2_normalize(x):
    return x * lax.rsqrt(jnp.sum(x * x, -1, keepdims=True) + RMS_EPS)


def rel_bucket(dist):
    n_exact = REL_BUCKETS // 2
    d = jnp.maximum(dist, 1).astype(F32)
    large = n_exact + (jnp.log(d / n_exact) / math.log(REL_MAX_DIST / n_exact) * (REL_BUCKETS - n_exact)).astype(jnp.int32)
    return jnp.where(dist < n_exact, dist, jnp.minimum(large, REL_BUCKETS - 1))


KEY_BLOCK = 128
DSA_CHUNK = 512
BLOCKS_PER_CHUNK = DSA_CHUNK // KEY_BLOCK
HEAD_PAIRS = A_HEADS // 2
PAIR_WIDTH = 2 * A_HEAD_DIM
INT_MIN = -2 ** 31


def _kv_kernel(c_ref, g_ref, wk_ref, wvt_ref, k_ref, vt_ref):
    c = c_ref[...]
    cn = (c * lax.rsqrt(jnp.mean(c * c, -1, keepdims=True) + RMS_EPS) * g_ref[...]).astype(BF16)
    k_ref[...] = jnp.dot(cn, wk_ref[...], preferred_element_type=F32).astype(BF16)
    vt = lax.dot_general(wvt_ref[...], cn, (((1,), (1,)), ((), ())), preferred_element_type=F32).astype(BF16)
    for i in range(vt_ref.shape[1]):
        vt_ref[0, i] = vt[:, i * KEY_BLOCK:(i + 1) * KEY_BLOCK]


def kv_project(c_kv, col, g, w_kv_up, bn, s):
    ts = 512
    w = w_kv_up.astype(BF16)
    return pl.pallas_call(
        _kv_kernel,
        grid=(bn, s // ts),
        in_specs=[pl.BlockSpec((ts, A_KV_RANK), lambda b, i: (b * (s // ts) + i, col)),
                  pl.BlockSpec((1, A_KV_RANK), lambda b, i: (0, 0)),
                  pl.BlockSpec((A_KV_RANK, A_WIDTH), lambda b, i: (0, 0)),
                  pl.BlockSpec((A_WIDTH, A_KV_RANK), lambda b, i: (0, 0))],
        out_specs=[pl.BlockSpec((None, ts, A_WIDTH), lambda b, i: (b, i, 0)),
                   pl.BlockSpec((1, ts // KEY_BLOCK, A_WIDTH, KEY_BLOCK), lambda b, i: (b, i, 0, 0))],
        out_shape=[jax.ShapeDtypeStruct((bn, s, A_WIDTH), BF16),
                   jax.ShapeDtypeStruct((bn, s // KEY_BLOCK, A_WIDTH, KEY_BLOCK), BF16)],
        compiler_params=pltpu.CompilerParams(dimension_semantics=("arbitrary", "arbitrary"), vmem_limit_bytes=VMEM_LIMIT_BYTES),
        name="kv_project",
    )(c_kv, g.reshape(1, A_KV_RANK), w[:, :A_WIDTH], w[:, A_WIDTH:].T)


def _mm_nt_kernel(wt_ref, x_ref, o_ref):
    o_ref[0] = lax.dot_general(wt_ref[...], x_ref[...].astype(BF16), (((1,), (1,)), ((), ())),
                               preferred_element_type=F32).astype(o_ref.dtype)


def matmul_nt(wt, x, bn, s, out_dtype=BF16):
    n, k = wt.shape
    ts = 512
    return pl.pallas_call(
        _mm_nt_kernel,
        grid=(bn, s // ts),
        in_specs=[pl.BlockSpec((n, k), lambda b, i: (0, 0)),
                  pl.BlockSpec((ts, k), lambda b, i: (b * (s // ts) + i, 0))],
        out_specs=pl.BlockSpec((1, n, ts), lambda b, i: (b, 0, i)),
        out_shape=jax.ShapeDtypeStruct((bn, n, s), out_dtype),
        compiler_params=pltpu.CompilerParams(dimension_semantics=("arbitrary", "arbitrary"), vmem_limit_bytes=VMEM_LIMIT_BYTES),
        name="matmul_nt",
    )(wt.astype(BF16), x)


def _sortable_key(score):
    bits = pltpu.bitcast(score, jnp.int32)
    return bits ^ (lax.shift_right_arithmetic(bits, 31) & jnp.int32(0x7FFFFFFF))


def _dsa_kernel(qt_ref, iqt_ref, wt_ref, kidx_ref, k_ref, vt_ref, bias_ref, o_ref,
                skey_ref, widx_ref, wq_ref, acc_ref, m_ref, l_ref, *, k_sel):
    j = pl.program_id(1)
    n_chunks = (j + BLOCKS_PER_CHUNK) // BLOCKS_PER_CHUNK
    q_pos = j * Q_BLOCK + lax.broadcasted_iota(jnp.int32, (1, Q_BLOCK), 1)

    for h in range(IDX_HEADS):
        widx_ref[:, h * Q_BLOCK:(h + 1) * Q_BLOCK] = iqt_ref[0, h * IDX_DIM:(h + 1) * IDX_DIM, :]
    wq_ref[...] = jnp.zeros_like(wq_ref)
    for p in range(HEAD_PAIRS):
        wq_ref[p, 0:A_HEAD_DIM, 0:Q_BLOCK] = qt_ref[0, (2 * p) * A_HEAD_DIM:(2 * p + 1) * A_HEAD_DIM, :]
        wq_ref[p, A_HEAD_DIM:PAIR_WIDTH, Q_BLOCK:2 * Q_BLOCK] = qt_ref[0, (2 * p + 1) * A_HEAD_DIM:(2 * p + 2) * A_HEAD_DIM, :]

    w_rows = [wt_ref[0, h:h + 1, :] * (IDX_HEADS ** -0.5 * IDX_DIM ** -0.5) for h in range(IDX_HEADS)]

    def idx_chunk(c, carry):
        start = pl.multiple_of(c * DSA_CHUNK, DSA_CHUNK)
        kc = kidx_ref[0, pl.ds(start, DSA_CHUNK), :]
        score = jnp.zeros((DSA_CHUNK, Q_BLOCK), F32)
        for p in range(IDX_HEADS // 2):
            z = jnp.dot(kc, widx_ref[:, p * 2 * Q_BLOCK:(p + 1) * 2 * Q_BLOCK], preferred_element_type=F32)
            score = score + jnp.maximum(z[:, :Q_BLOCK], 0.0) * w_rows[2 * p]
            score = score + jnp.maximum(z[:, Q_BLOCK:], 0.0) * w_rows[2 * p + 1]
        pos = start + lax.broadcasted_iota(jnp.int32, (DSA_CHUNK, Q_BLOCK), 0)
        skey_ref[pl.ds(start, DSA_CHUNK), :] = jnp.where(pos <= q_pos, _sortable_key(score), INT_MIN)
        return carry

    lax.fori_loop(0, n_chunks, idx_chunk, 0)

    def count_ge(cand):
        def body(c, cnt):
            start = pl.multiple_of(c * DSA_CHUNK, DSA_CHUNK)
            hit = (skey_ref[pl.ds(start, DSA_CHUNK), :] >= cand).astype(jnp.int32)
            return cnt + jnp.sum(hit.reshape(DSA_CHUNK // 8, 8, Q_BLOCK), axis=0)
        cnt = lax.fori_loop(0, n_chunks, body, jnp.zeros((8, Q_BLOCK), jnp.int32))
        return jnp.sum(cnt, axis=0, keepdims=True)

    def bit_step(i, thr_biased):
        cand_biased = thr_biased | lax.shift_left(jnp.int32(1), 31 - i)
        return jnp.where(count_ge(cand_biased ^ INT_MIN) >= k_sel, cand_biased, thr_biased)

    thr = lax.fori_loop(0, 32, bit_step, jnp.zeros((1, Q_BLOCK), jnp.int32)) ^ INT_MIN

    n_gt = count_ge(thr + 1)
    n_ge = count_ge(thr)
    tied = (n_ge > k_sel) & (thr > INT_MIN)

    @pl.when(jnp.max(tied.astype(jnp.int32)) > 0)
    def _():
        need = k_sel - n_gt

        def count_tied_upto(limit):
            def body(c, cnt):
                start = pl.multiple_of(c * DSA_CHUNK, DSA_CHUNK)
                pos = start + lax.broadcasted_iota(jnp.int32, (DSA_CHUNK, Q_BLOCK), 0)
                hit = jnp.where(skey_ref[pl.ds(start, DSA_CHUNK), :] == thr, (pos <= limit).astype(jnp.int32), 0)
                return cnt + jnp.sum(hit.reshape(DSA_CHUNK // 8, 8, Q_BLOCK), axis=0)
            cnt = lax.fori_loop(0, n_chunks, body, jnp.zeros((8, Q_BLOCK), jnp.int32))
            return jnp.sum(cnt, axis=0, keepdims=True)

        def pos_step(i, lim):
            cand = lim - lax.shift_left(jnp.int32(1), 30 - i)
            return jnp.where(count_tied_upto(cand) >= need, cand, lim)

        limit = lax.fori_loop(0, 31, pos_step, jnp.full((1, Q_BLOCK), 2 ** 31 - 1, jnp.int32))

        def demote(c, carry):
            start = pl.multiple_of(c * DSA_CHUNK, DSA_CHUNK)
            pos = start + lax.broadcasted_iota(jnp.int32, (DSA_CHUNK, Q_BLOCK), 0)
            sk = skey_ref[pl.ds(start, DSA_CHUNK), :]
            skey_ref[pl.ds(start, DSA_CHUNK), :] = jnp.where(tied & (sk == thr) & (pos > limit), thr - 1, sk)
            return carry

        lax.fori_loop(0, n_chunks, demote, 0)

    m_ref[...] = jnp.full_like(m_ref, NEG_INF)
    l_ref[...] = jnp.zeros_like(l_ref)
    acc_ref[...] = jnp.zeros_like(acc_ref)

    def attend(s_list, vt_of, n_blocks):
        for p in range(HEAD_PAIRS):
            s = s_list[p]
            m_old = m_ref[p]
            m_new = jnp.maximum(m_old, jnp.max(s, axis=0, keepdims=True))
            alpha = jnp.exp(m_old - m_new)
            pexp = jnp.exp(s - m_new)
            l_ref[p] = l_ref[p] * alpha + jnp.sum(pexp, axis=0, keepdims=True)
            pb = pexp.astype(BF16)
            pv = jnp.dot(vt_of(p, 0), pb[0:KEY_BLOCK], preferred_element_type=F32)
            for i in range(1, n_blocks):
                pv = pv + jnp.dot(vt_of(p, i), pb[i * KEY_BLOCK:(i + 1) * KEY_BLOCK], preferred_element_type=F32)
            acc_ref[p] = acc_ref[p] * alpha + pv
            m_ref[p] = m_new

    far_end = (j - 1) * KEY_BLOCK
    n_far = (jnp.maximum(j - 1, 0) + BLOCKS_PER_CHUNK - 1) // BLOCKS_PER_CHUNK

    def far_chunk(c, carry):
        start = pl.multiple_of(c * DSA_CHUNK, DSA_CHUNK)
        pos = start + lax.broadcasted_iota(jnp.int32, (DSA_CHUNK, Q_BLOCK), 0)
        sel = (skey_ref[pl.ds(start, DSA_CHUNK), :] >= thr) & (pos < far_end)
        madd = jnp.where(sel, 0.0, NEG_INF)
        madd2 = jnp.concatenate([madd, madd], axis=1)
        kc = k_ref[0, pl.ds(start, DSA_CHUNK), :]
        s_list = [jnp.dot(kc[:, p * PAIR_WIDTH:(p + 1) * PAIR_WIDTH], wq_ref[p], preferred_element_type=F32) + madd2
                  for p in range(HEAD_PAIRS)]
        blk0 = c * BLOCKS_PER_CHUNK
        attend(s_list, lambda p, i: vt_ref[0, blk0 + i, p * PAIR_WIDTH:(p + 1) * PAIR_WIDTH, :], BLOCKS_PER_CHUNK)
        return carry

    lax.fori_loop(0, n_far, far_chunk, 0)

    def near_block(kb, delta):
        start = pl.multiple_of(kb * KEY_BLOCK, KEY_BLOCK)
        pos = start + lax.broadcasted_iota(jnp.int32, (KEY_BLOCK, Q_BLOCK), 0)
        sel = (skey_ref[pl.ds(start, KEY_BLOCK), :] >= thr) & (pos <= q_pos)
        madd = jnp.where(sel, 0.0, NEG_INF)
        madd2 = jnp.concatenate([madd, madd], axis=1)
        kc = k_ref[0, pl.ds(start, KEY_BLOCK), :]
        s_list = [jnp.dot(kc[:, p * PAIR_WIDTH:(p + 1) * PAIR_WIDTH], wq_ref[p], preferred_element_type=F32)
                  + bias_ref[delta, p] + madd2 for p in range(HEAD_PAIRS)]
        attend(s_list, lambda p, i: vt_ref[0, kb, p * PAIR_WIDTH:(p + 1) * PAIR_WIDTH, :], 1)

    @pl.when(j > 0)
    def _():
        near_block(j - 1, 1)

    near_block(j, 0)

    outs = []
    for p in range(HEAD_PAIRS):
        o = acc_ref[p] / l_ref[p]
        outs.append(o[0:A_HEAD_DIM, 0:Q_BLOCK])
        outs.append(o[A_HEAD_DIM:PAIR_WIDTH, Q_BLOCK:2 * Q_BLOCK])
    o_ref[0] = jnp.concatenate(outs, axis=0).T.astype(o_ref.dtype)


def _rel_bias_tiles(rel_bias):
    tab = rel_bias[rel_bucket(jnp.arange(2 * Q_BLOCK, dtype=jnp.int32))] - rel_bias[REL_BUCKETS - 1]
    key = jnp.arange(KEY_BLOCK)[:, None]
    qry = jnp.arange(Q_BLOCK)[None, :]
    tiles = []
    for delta in (0, 1):
        dist = jnp.maximum(delta * Q_BLOCK + qry - key, 0)
        t = jnp.moveaxis(tab[dist], 2, 0)
        t = t.reshape(HEAD_PAIRS, 2, KEY_BLOCK, Q_BLOCK)
        tiles.append(jnp.concatenate([t[:, 0], t[:, 1]], axis=-1))
    return jnp.stack(tiles)


def dsa_attention(qiq_t, wt, k_idx, k, vt, bias_tiles, bn, s):
    assert A_WIDTH == IDX_HEADS * IDX_DIM
    scal_rows = wt.shape[1]
    assert scal_rows % IDX_HEADS == 0
    k_sel = min(TOPK_MAX, s // 4)
    nb = s // Q_BLOCK
    s_pad = s + DSA_CHUNK
    return pl.pallas_call(
        functools.partial(_dsa_kernel, k_sel=k_sel),
        grid=(bn, nb),
        in_specs=[pl.BlockSpec((1, A_WIDTH, Q_BLOCK), lambda b, j: (b, 0, j)),
                  pl.BlockSpec((1, IDX_HEADS * IDX_DIM, Q_BLOCK), lambda b, j: (b, 1, j)),
                  pl.BlockSpec((1, IDX_HEADS, Q_BLOCK), lambda b, j: (b, scal_rows // IDX_HEADS - 1, j)),
                  pl.BlockSpec((1, s, IDX_DIM), lambda b, j: (b, 0, 0)),
                  pl.BlockSpec((1, s, A_WIDTH), lambda b, j: (b, 0, 0)),
                  pl.BlockSpec((1, s // KEY_BLOCK, A_WIDTH, KEY_BLOCK), lambda b, j: (b, 0, 0, 0)),
                  pl.BlockSpec((2, HEAD_PAIRS, KEY_BLOCK, 2 * Q_BLOCK), lambda b, j: (0, 0, 0, 0))],
        out_specs=pl.BlockSpec((1, Q_BLOCK, A_WIDTH), lambda b, j: (b, j, 0)),
        out_shape=jax.ShapeDtypeStruct((bn, s, A_WIDTH), BF16),
        scratch_shapes=[pltpu.VMEM((s_pad, Q_BLOCK), jnp.int32),
                        pltpu.VMEM((IDX_DIM, IDX_HEADS * Q_BLOCK), BF16),
                        pltpu.VMEM((HEAD_PAIRS, PAIR_WIDTH, 2 * Q_BLOCK), BF16),
                        pltpu.VMEM((HEAD_PAIRS, PAIR_WIDTH, 2 * Q_BLOCK), F32),
                        pltpu.VMEM((HEAD_PAIRS, 1, 2 * Q_BLOCK), F32),
                        pltpu.VMEM((HEAD_PAIRS, 1, 2 * Q_BLOCK), F32)],
        compiler_params=pltpu.CompilerParams(dimension_semantics=("arbitrary", "arbitrary"), vmem_limit_bytes=56 * 1024 * 1024),
        name="dsa_attention",
    )(qiq_t, qiq_t, wt, k_idx, k, vt, bias_tiles)


GDN_CHUNK = B_HEAD_DIM
GDN_BLOCK = 512
GDN_HALO = 8
NEUMANN_STEPS = 6


def _split_bf16(x):
    hi = x.astype(BF16)
    return hi, (x - hi.astype(F32)).astype(BF16)


def _dot_f32(a, b):
    a_hi, a_lo = _split_bf16(a)
    b_hi, b_lo = _split_bf16(b)
    d = lambda x, y: jnp.dot(x, y, preferred_element_type=F32)
    return d(a_hi, b_hi) + (d(a_hi, b_lo) + d(a_lo, b_hi))


def _bdot(a, b):
    return jnp.dot(a.astype(BF16), b.astype(BF16), preferred_element_type=F32)


def _bdot_nt(a, b):
    return lax.dot_general(a.astype(BF16), b.astype(BF16), (((1,), (1,)), ((), ())), preferred_element_type=F32)


def _bdot_tn(a, b):
    return lax.dot_general(a.astype(BF16), b.astype(BF16), (((0,), (0,)), ((), ())), preferred_element_type=F32)


def _softplus(x):
    return jnp.maximum(x, 0.0) + jnp.log(1.0 + jnp.exp(-jnp.abs(x)))


def _gdn_kernel(hq_ref, hk_ref, hv_ref, hz_ref, wq_ref, wk_ref, wv_ref, ab_ref, const_ref, gout_ref, tri_ref,
                o_ref, state_ref, xbuf_ref):
    head = pl.program_id(1)

    @pl.when(pl.program_id(2) == 0)
    def _():
        state_ref[...] = jnp.zeros_like(state_ref)
        xbuf_ref[:, 0:GDN_HALO, :] = jnp.zeros((3, GDN_HALO, B_HEAD_DIM), F32)

    def conv_silu(n, x_ref, w_ref):
        x = x_ref[...]
        xbuf_ref[n, GDN_HALO:, :] = x
        acc = xbuf_ref[n, GDN_HALO - CONV_WIDTH + 1:GDN_HALO - CONV_WIDTH + 1 + GDN_BLOCK, :] * w_ref[0:1, :]
        for j in range(1, CONV_WIDTH):
            lo = GDN_HALO - CONV_WIDTH + 1 + j
            acc = acc + xbuf_ref[n, lo:lo + GDN_BLOCK, :] * w_ref[j:j + 1, :]
        xbuf_ref[n, 0:GDN_HALO, :] = x[GDN_BLOCK - GDN_HALO:, :]
        return acc * jax.nn.sigmoid(acc)

    q = conv_silu(0, hq_ref, wq_ref)
    k = conv_silu(1, hk_ref, wk_ref)
    v = conv_silu(2, hv_ref, wv_ref)
    q = q * lax.rsqrt(jnp.sum(q * q, -1, keepdims=True) + RMS_EPS) * B_HEAD_DIM ** -0.5
    k = k * lax.rsqrt(jnp.sum(k * k, -1, keepdims=True) + RMS_EPS)

    a_row = ab_ref[0, pl.ds(head, 1), :]
    b_row = ab_ref[0, pl.ds(B_HEADS + head, 1), :]
    beta_row = jax.nn.sigmoid(b_row)
    g_row = const_ref[0, 0:1, :] * _softplus(a_row + const_ref[0, 1:2, :])
    gc_rows = _dot_f32(jnp.broadcast_to(g_row, (8, GDN_BLOCK)), tri_ref[...])[0:1, :]

    row_i = lax.broadcasted_iota(jnp.int32, (GDN_CHUNK, GDN_CHUNK), 0)
    col_j = lax.broadcasted_iota(jnp.int32, (GDN_CHUNK, GDN_CHUNK), 1)
    eye = jnp.where(row_i == col_j, 1.0, 0.0)
    z_all = hz_ref[...]
    for c in range(GDN_BLOCK // GDN_CHUNK):
        sl = slice(c * GDN_CHUNK, (c + 1) * GDN_CHUNK)
        qc, kc, vc = q[sl], k[sl], v[sl]
        g_lane = jnp.broadcast_to(gc_rows[:, sl], (GDN_CHUNK, GDN_CHUNK))
        g_sub = g_lane.T
        beta_sub = jnp.broadcast_to(beta_row[:, sl], (GDN_CHUNK, GDN_CHUNK)).T
        g_end = jnp.broadcast_to(g_lane[:, GDN_CHUNK - 1:GDN_CHUNK], (GDN_CHUNK, GDN_CHUNK))
        decay = jnp.exp(jnp.where(row_i >= col_j, g_sub - g_lane, -jnp.inf))
        kb = kc * beta_sub
        m = jnp.where(row_i > col_j, _bdot_nt(kb, kc) * decay, 0.0)
        x = -m
        t_inv = eye + x
        for _ in range(NEUMANN_STEPS):
            x = _dot_f32(x, x)
            t_inv = t_inv + _dot_f32(t_inv, x)
        g_exp = jnp.exp(g_sub)
        u = _bdot(t_inv, vc * beta_sub)
        w = _bdot(t_inv, kb * g_exp)
        qk = _bdot_nt(qc, kc) * decay
        state = state_ref[...]
        v_new = u - _bdot(w, state)
        o = _bdot(qc * g_exp, state) + _bdot(qk, v_new)
        state_ref[...] = state * jnp.exp(g_end) + _bdot_tn(kc * jnp.exp(g_end - g_sub), v_new)
        o = o * lax.rsqrt(jnp.mean(o * o, -1, keepdims=True) + RMS_EPS) * gout_ref[...]
        zc = z_all[sl]
        o_ref[sl, :] = (o * (zc * jax.nn.sigmoid(zc))).astype(o_ref.dtype)


def gated_deltanet(h, col0, ab_t, conv_w, a_log, dt_bias, out_norm_g, bn, s):
    nblk = s // GDN_BLOCK
    tok = jnp.arange(GDN_BLOCK)
    tri = ((tok[:, None] // GDN_CHUNK == tok[None, :] // GDN_CHUNK) & (tok[:, None] <= tok[None, :])).astype(F32)
    const = jnp.stack([jnp.broadcast_to(-jnp.exp(a_log)[:, None], (B_HEADS, GDN_BLOCK)),
                       jnp.broadcast_to(dt_bias[:, None], (B_HEADS, GDN_BLOCK))], axis=1)
    const = jnp.pad(const, ((0, 0), (0, 6), (0, 0)))
    hcol = lambda part: pl.BlockSpec((GDN_BLOCK, B_HEAD_DIM), lambda b, hh, i: (b * nblk + i, col0 + part * B_HEADS + hh))
    wcol = lambda part: pl.BlockSpec((CONV_WIDTH, B_HEAD_DIM), lambda b, hh, i: (0, part * B_HEADS + hh))
    return pl.pallas_call(
        _gdn_kernel,
        grid=(bn, B_HEADS, nblk),
        in_specs=[hcol(0), hcol(1), hcol(2), hcol(3), wcol(0), wcol(1), wcol(2),
                  pl.BlockSpec((1, 16, GDN_BLOCK), lambda b, hh, i: (b, 0, i)),
                  pl.BlockSpec((1, 8, GDN_BLOCK), lambda b, hh, i: (hh, 0, 0)),
                  pl.BlockSpec((1, B_HEAD_DIM), lambda b, hh, i: (0, 0)),
                  pl.BlockSpec((GDN_BLOCK, GDN_BLOCK), lambda b, hh, i: (0, 0))],
        out_specs=pl.BlockSpec((GDN_BLOCK, B_HEAD_DIM), lambda b, hh, i: (b * nblk + i, hh)),
        out_shape=jax.ShapeDtypeStruct((bn * s, B_WIDTH), BF16),
        scratch_shapes=[pltpu.VMEM((B_HEAD_DIM, B_HEAD_DIM), F32),
                        pltpu.VMEM((3, GDN_HALO + GDN_BLOCK, B_HEAD_DIM), F32)],
        compiler_params=pltpu.CompilerParams(dimension_semantics=("arbitrary", "arbitrary", "arbitrary"), vmem_limit_bytes=VMEM_LIMIT_BYTES),
        name="gated_deltanet",
    )(h, h, h, h, conv_w, conv_w, conv_w, ab_t, const, out_norm_g.reshape(1, B_HEAD_DIM), tri)


S5_CHUNK = 32
S5_ROW = S5_CHUNK * C_GROUP
STATE2 = 2 * C_STATE


def _s5_tables(lam_re, lam_im, log_dt, b_re, b_im, c_re, c_im, d_skip):
    hp = lax.Precision.HIGHEST
    ell = S5_CHUNK
    dt = jnp.exp(log_dt)[:, None]
    lr, li = lam_re, lam_im
    mag = jnp.exp(lr * dt)
    a_re, a_im = mag * jnp.cos(li * dt), mag * jnp.sin(li * dt)
    den = lr * lr + li * li
    f_re = ((a_re - 1.0) * lr + a_im * li) / den
    f_im = (a_im * lr - (a_re - 1.0) * li) / den
    bb_re = f_re[..., None] * b_re - f_im[..., None] * b_im
    bb_im = f_re[..., None] * b_im + f_im[..., None] * b_re
    tau = jnp.arange(ell + 1, dtype=F32)[:, None, None]
    pmag, ang = jnp.exp(tau * (lr * dt)), tau * (li * dt)
    p_re, p_im = pmag * jnp.cos(ang), pmag * jnp.sin(ang)
    ab_re = p_re[:ell, ..., None] * bb_re - p_im[:ell, ..., None] * bb_im
    ab_im = p_re[:ell, ..., None] * bb_im + p_im[:ell, ..., None] * bb_re
    kern = (jnp.einsum('gip,tgpj->tgij', c_re, ab_re, precision=hp)
            - jnp.einsum('gip,tgpj->tgij', c_im, ab_im, precision=hp))
    lag = jnp.arange(ell)[None, :] - jnp.arange(ell)[:, None]
    toep = jnp.where((lag >= 0)[:, :, None, None, None], kern[jnp.maximum(lag, 0)], 0.0)
    toep = jnp.transpose(toep, (2, 0, 4, 1, 3)).reshape(C_GROUPS, S5_ROW, S5_ROW)
    w_re = jnp.transpose(ab_re[::-1], (1, 0, 3, 2)).reshape(C_GROUPS, S5_ROW, C_STATE)
    w_im = jnp.transpose(ab_im[::-1], (1, 0, 3, 2)).reshape(C_GROUPS, S5_ROW, C_STATE)
    tw = jnp.concatenate([toep, w_re, w_im, w_im, w_re], axis=-1).astype(BF16)
    pt_re = jnp.swapaxes(p_re[1:], 0, 1)[:, :, None, :]
    pt_im = jnp.swapaxes(p_im[1:], 0, 1)[:, :, None, :]
    ca_re = c_re[:, None] * pt_re - c_im[:, None] * pt_im
    ca_im = c_re[:, None] * pt_im + c_im[:, None] * pt_re
    v_re = jnp.transpose(ca_re, (0, 3, 1, 2)).reshape(C_GROUPS, C_STATE, S5_ROW)
    v_im = jnp.transpose(-ca_im, (0, 3, 1, 2)).reshape(C_GROUPS, C_STATE, S5_ROW)
    v = jnp.concatenate([v_re, v_im], axis=1).astype(BF16)
    al_re, al_im = p_re[ell], p_im[ell]
    coef = jnp.stack([jnp.concatenate([al_re, al_re], -1), jnp.concatenate([-al_im, al_im], -1),
                      jnp.concatenate([al_im, -al_im], -1)], axis=1)
    d_flat = jnp.tile(d_skip, (1, ell)).reshape(C_GROUPS, 1, S5_ROW)
    return tw, v, coef, d_flat


def _s5_kernel(u_ref, tw_ref, v_ref, coef_ref, d_ref, o_ref, sin_ref, xin_ref, *, bn):
    rows = u_ref.shape[1]
    n_chunks = rows // bn
    u = u_ref[0]
    tw = jnp.dot(u.astype(BF16), tw_ref[0], preferred_element_type=F32)
    o_ref[0] = tw[:, :S5_ROW] + u * d_ref[0]
    sin_ref[...] = tw[:, S5_ROW:]
    a1, a2, a2s = coef_ref[0, 0:1, :], coef_ref[0, 1:2, :], coef_ref[0, 2:3, :]

    def step(c, carry):
        new = []
        for b in range(bn):
            x, xs = carry[2 * b], carry[2 * b + 1]
            row = b * n_chunks + c
            xin_ref[pl.ds(row, 1), :] = x
            s_row = sin_ref[pl.ds(row, 1), :]
            new.append(a1 * x + a2 * xs + s_row[:, :STATE2])
            new.append(a1 * xs + a2s * x + s_row[:, STATE2:])
        return tuple(new)

    zero = jnp.zeros((1, STATE2), F32)
    lax.fori_loop(0, n_chunks, step, (zero,) * (2 * bn))
    y = o_ref[0] + jnp.dot(xin_ref[...].astype(BF16), v_ref[0], preferred_element_type=F32)
    o_ref[0] = jax.nn.gelu(y)


def _glu_kernel(y_ref, w_ref, b_ref, o_ref):
    y = y_ref[...]
    gate = jnp.dot(y.astype(BF16), w_ref[...], preferred_element_type=F32) + b_ref[...]
    o_ref[...] = (y * jax.nn.sigmoid(gate)).astype(o_ref.dtype)


def s5_branch(u, lam_re, lam_im, log_dt, b_re, b_im, c_re, c_im, d_skip, glu_w, glu_b):
    bn, s, _ = u.shape
    n_chunks = s // S5_CHUNK
    rows = bn * n_chunks
    tw, v, coef, d_flat = _s5_tables(lam_re, lam_im, log_dt, b_re, b_im, c_re, c_im, d_skip)
    uf = jnp.transpose(u.reshape(bn, n_chunks, S5_CHUNK, C_GROUPS, C_GROUP), (3, 0, 1, 2, 4)).reshape(C_GROUPS, rows, S5_ROW)
    per_group = lambda shape: pl.BlockSpec((1,) + shape, lambda g: (g, 0, 0))
    y = pl.pallas_call(
        functools.partial(_s5_kernel, bn=bn),
        grid=(C_GROUPS,),
        in_specs=[per_group((rows, S5_ROW)), per_group((S5_ROW, S5_ROW + 2 * STATE2)), per_group((STATE2, S5_ROW)),
                  per_group((3, STATE2)), per_group((1, S5_ROW))],
        out_specs=per_group((rows, S5_ROW)),
        out_shape=jax.ShapeDtypeStruct((C_GROUPS, rows, S5_ROW), F32),
        scratch_shapes=[pltpu.VMEM((rows, 2 * STATE2), F32), pltpu.VMEM((rows, STATE2), F32)],
        compiler_params=pltpu.CompilerParams(dimension_semantics=("arbitrary",), vmem_limit_bytes=VMEM_LIMIT_BYTES),
        name="s5_scan",
    )(uf, tw, v, coef, d_flat)
    y = jnp.transpose(y.reshape(C_GROUPS, bn, n_chunks, S5_CHUNK, C_GROUP), (1, 2, 3, 0, 4)).reshape(bn * s, C_WIDTH)
    t = bn * s
    tm = _pick_tile(t, (1024, 512, 256, 128, 8))
    return pl.pallas_call(
        _glu_kernel,
        grid=(t // tm,),
        in_specs=[pl.BlockSpec((tm, C_WIDTH), lambda i: (i, 0)), pl.BlockSpec((C_WIDTH, C_WIDTH), lambda i: (0, 0)),
                  pl.BlockSpec((1, C_WIDTH), lambda i: (0, 0))],
        out_specs=pl.BlockSpec((tm, C_WIDTH), lambda i: (i, 0)),
        out_shape=jax.ShapeDtypeStruct((t, C_WIDTH), BF16),
        compiler_params=pltpu.CompilerParams(dimension_semantics=("arbitrary",), vmem_limit_bytes=VMEM_LIMIT_BYTES),
        name="s5_glu",
    )(y, glu_w.astype(BF16), glu_b.reshape(1, C_WIDTH))


def hybrid_mixer(x, rel_bias, w_in, a_kv_norm, a_kv_up, b_conv, b_a_log, b_dt_bias, b_out_norm,
                 c_lambda_re, c_lambda_im, c_log_dt, c_b_re, c_b_im, c_c_re, c_c_im, c_d, c_glu_w, c_glu_b,
                 w_br_a, w_br_b, w_br_c, w_o, ln_g, ln_b):
    bn, s, d = x.shape
    t = bn * s
    xt = x.reshape(t, d)
    offsets = [0] + np.cumsum(SPLITS).tolist()
    w_cols = [w_in[:, offsets[i]:offsets[i + 1]] for i in range(len(SPLITS))]
    (w_aq, w_ckv, w_iq, w_ik, w_iw, w_bqkv, w_bz, w_ba, w_bb, w_cu, w_gate) = w_cols
    w_ik_pad = jnp.pad(w_ik, ((0, 0), (0, LANE - IDX_DIM)))
    h = matmul(xt, jnp.concatenate([w_gate, w_ckv, w_ik_pad, w_bqkv, w_bz, w_cu], axis=1))
    o = np.cumsum([0, 3 * D_MODEL, A_KV_RANK, LANE, 3 * B_WIDTH, B_WIDTH, C_WIDTH]).tolist()
    idx_k = h[:, o[2]:o[2] + IDX_DIM].astype(BF16).reshape(bn, s, IDX_DIM)
    c_u = h[:, o[5]:o[6]].reshape(bn, s, C_WIDTH)
    qiq_t = matmul_nt(jnp.concatenate([w_aq * A_HEAD_DIM ** -0.5, w_iq], axis=1).T, xt, bn, s)
    scal_t = matmul_nt(jnp.concatenate([w_ba, w_bb, w_iw], axis=1).T, xt, bn, s, out_dtype=F32)
    k, vt = kv_project(h, o[1] // LANE, a_kv_norm, a_kv_up, bn, s)
    ya = dsa_attention(qiq_t, scal_t, idx_k, k, vt, _rel_bias_tiles(rel_bias), bn, s)
    yb = gated_deltanet(h, o[3] // LANE, scal_t, b_conv, b_a_log, b_dt_bias, b_out_norm, bn, s)
    yc = s5_branch(c_u, c_lambda_re, c_lambda_im, c_log_dt, c_b_re, c_b_im, c_c_re, c_c_im, c_d, c_glu_w, c_glu_b)
    return merge_branches(ya.reshape(t, A_WIDTH), yb, yc, h, w_br_a, w_br_b, w_br_c, w_o, xt, ln_g, ln_b)


def _merge_kernel(ya_ref, yb_ref, yc_ref, ga_ref, gb_ref, gc_ref, wa_ref, wb_ref, wc_ref, wo_ref, x_ref, g_ref, b_ref, o_ref):
    branch = lambda y_ref, w_ref, gate_ref: jax.nn.sigmoid(gate_ref[...]) * jnp.dot(y_ref[...], w_ref[...], preferred_element_type=F32)
    merged = branch(ya_ref, wa_ref, ga_ref) + branch(yb_ref, wb_ref, gb_ref) + branch(yc_ref, wc_ref, gc_ref)
    mix = jnp.dot(merged.astype(BF16), wo_ref[...], preferred_element_type=F32)
    z = DEEPNORM_ALPHA * x_ref[...] + mix
    mu = jnp.mean(z, -1, keepdims=True)
    zc = z - mu
    var = jnp.mean(zc * zc, -1, keepdims=True)
    o_ref[...] = zc * lax.rsqrt(var + LN_EPS) * g_ref[...] + b_ref[...]


def merge_branches(ya, yb, yc, h, w_br_a, w_br_b, w_br_c, w_o, x, ln_g, ln_b):
    t, d = x.shape
    tm = _pick_tile(t, (512, 256, 128, 8))
    row = lambda width, col=0: pl.BlockSpec((tm, width), lambda i: (i, col))
    full = lambda a: pl.BlockSpec(a.shape, lambda i: (0, 0))
    ws = [w.astype(BF16) for w in (w_br_a, w_br_b, w_br_c, w_o)]
    return pl.pallas_call(
        _merge_kernel,
        grid=(t // tm,),
        in_specs=[row(A_WIDTH), row(B_WIDTH), row(C_WIDTH), row(d, 0), row(d, 1), row(d, 2)] + [full(w) for w in ws]
                 + [row(d), pl.BlockSpec((1, d), lambda i: (0, 0)), pl.BlockSpec((1, d), lambda i: (0, 0))],
        out_specs=row(d),
        out_shape=jax.ShapeDtypeStruct((t, d), F32),
        compiler_params=pltpu.CompilerParams(dimension_semantics=("arbitrary",), vmem_limit_bytes=VMEM_LIMIT_BYTES),
        name="merge_branches",
    )(ya, yb, yc, h, h, h, *ws, x, ln_g.reshape(1, d), ln_b.reshape(1, d))


ROUTER_TILE = 512


def _router_kernel(x_ref, rwt_ref, bias_ref, tri_ref, idx_ref, gate_ref, pos_ref, cnt_ref, carry_ref):
    @pl.when(pl.program_id(0) == 0)
    def _():
        carry_ref[...] = jnp.zeros_like(carry_ref)

    logits = lax.dot_general(rwt_ref[...], x_ref[...].astype(BF16), (((1,), (1,)), ((), ())), preferred_element_type=F32)
    scores = jax.nn.sigmoid(logits)
    remaining = scores + bias_ref[...]
    expert = lax.broadcasted_iota(jnp.int32, scores.shape, 0)
    picks = []
    for _ in range(TOP_K):
        best = jnp.max(remaining, axis=0, keepdims=True)
        first = jnp.min(jnp.where(remaining == best, expert, N_EXPERTS), axis=0, keepdims=True)
        pick = expert == first
        picks.append((first, pick))
        remaining = jnp.where(pick, -jnp.inf, remaining)
    chosen = picks[0][1]
    for _, pick in picks[1:]:
        chosen = chosen | pick
    total = jnp.sum(jnp.where(chosen, scores, 0.0), axis=0, keepdims=True)
    prefix = jnp.dot(jnp.where(chosen, 1.0, 0.0).astype(BF16), tri_ref[...], preferred_element_type=F32)
    rank = carry_ref[...] + prefix.astype(jnp.int32) - 1
    carry_ref[...] = carry_ref[...] + prefix[:, -1:].astype(jnp.int32)
    cnt_ref[...] = carry_ref[...]
    for k, (first, pick) in enumerate(picks):
        idx_ref[k:k + 1, :] = first
        gate_ref[k:k + 1, :] = jnp.sum(jnp.where(pick, scores, 0.0), axis=0, keepdims=True) / total * ROUTED_SCALE
        pos_ref[k:k + 1, :] = jnp.sum(jnp.where(pick, rank, 0), axis=0, keepdims=True)


def moe_router(xt, router_w, router_bias):
    t, d = xt.shape
    tile = ROUTER_TILE
    tri = (jnp.arange(tile)[:, None] <= jnp.arange(tile)[None, :]).astype(BF16)
    row = pl.BlockSpec((TOP_K, tile), lambda i: (0, i))
    return pl.pallas_call(
        _router_kernel,
        grid=(t // tile,),
        in_specs=[pl.BlockSpec((tile, d), lambda i: (i, 0)), pl.BlockSpec((N_EXPERTS, d), lambda i: (0, 0)),
                  pl.BlockSpec((N_EXPERTS, 1), lambda i: (0, 0)), pl.BlockSpec((tile, tile), lambda i: (0, 0))],
        out_specs=[row, row, row, pl.BlockSpec((N_EXPERTS, 1), lambda i: (0, 0))],
        out_shape=[jax.ShapeDtypeStruct((TOP_K, t), jnp.int32), jax.ShapeDtypeStruct((TOP_K, t), F32),
                   jax.ShapeDtypeStruct((TOP_K, t), jnp.int32), jax.ShapeDtypeStruct((N_EXPERTS, 1), jnp.int32)],
        scratch_shapes=[pltpu.VMEM((N_EXPERTS, 1), jnp.int32)],
        compiler_params=pltpu.CompilerParams(dimension_semantics=("arbitrary",), vmem_limit_bytes=VMEM_LIMIT_BYTES),
        name="moe_router",
    )(xt, router_w.T.astype(BF16), router_bias.reshape(N_EXPERTS, 1), tri)


def _expert_kernel(blk_expert_ref, blk_rows_ref, x_ref, wgu_ref, wdn_ref, o_ref):
    n_valid = blk_rows_ref[pl.program_id(0)]

    @pl.when(n_valid > 0)
    def _():
        row = lax.broadcasted_iota(jnp.int32, x_ref.shape, 0)
        x = jnp.where(row < n_valid, x_ref[...], 0.0).astype(BF16)
        h = jnp.dot(x, wgu_ref[0], preferred_element_type=F32)
        hg, hu = h[:, :D_EXPERT], h[:, D_EXPERT:]
        act = (hg * jax.nn.sigmoid(hg) * hu).astype(BF16)
        o_ref[...] = jnp.dot(act, wdn_ref[0], preferred_element_type=F32)

    @pl.when(n_valid <= 0)
    def _():
        o_ref[...] = jnp.zeros_like(o_ref)


def expert_ffn(xs, blk_expert, blk_rows, w_gu, w_dn):
    n_pad, d = xs.shape
    n_blk = n_pad // MOE_BLOCK
    return pl.pallas_call(
        _expert_kernel,
        grid_spec=pltpu.PrefetchScalarGridSpec(
            num_scalar_prefetch=2,
            grid=(n_blk,),
            in_specs=[pl.BlockSpec((MOE_BLOCK, d), lambda b, be, br: (b, 0)),
                      pl.BlockSpec((1, d, 2 * D_EXPERT), lambda b, be, br: (be[b], 0, 0)),
                      pl.BlockSpec((1, D_EXPERT, d), lambda b, be, br: (be[b], 0, 0))],
            out_specs=pl.BlockSpec((MOE_BLOCK, d), lambda b, be, br: (b, 0))),
        out_shape=jax.ShapeDtypeStruct((n_pad, d), F32),
        compiler_params=pltpu.CompilerParams(dimension_semantics=("arbitrary",), vmem_limit_bytes=VMEM_LIMIT_BYTES),
        name="expert_ffn",
    )(blk_expert, blk_rows, xs, w_gu, w_dn)


DISPATCH_TILE = 256
COMBINE_TILE = 128


def _row_copy(src_hbm, src_row, dst_hbm, dst_row, sem):
    return pltpu.make_async_copy(src_hbm.at[pl.ds(src_row, 1)], dst_hbm.at[pl.ds(dst_row, 1)], sem)


def _dispatch_kernel(dest_ref, x_hbm, xs_hbm, sem):
    base = pl.program_id(0) * DISPATCH_TILE

    def issue(r, carry):
        for k in range(TOP_K):
            _row_copy(x_hbm, base + r, xs_hbm, dest_ref[0, 0, r * TOP_K + k], sem).start()
        return carry

    lax.fori_loop(0, DISPATCH_TILE, issue, 0)
    n_rows = DISPATCH_TILE * TOP_K
    pltpu.make_async_copy(xs_hbm.at[pl.ds(0, n_rows)], xs_hbm.at[pl.ds(0, n_rows)], sem).wait()


def moe_dispatch(xt, dest_tok_major, n_pad):
    t, d = xt.shape
    n_tiles = t // DISPATCH_TILE
    return pl.pallas_call(
        _dispatch_kernel,
        grid=(n_tiles,),
        in_specs=[pl.BlockSpec((1, 1, DISPATCH_TILE * TOP_K), lambda i: (i, 0, 0), memory_space=pltpu.SMEM),
                  pl.BlockSpec(memory_space=pl.ANY)],
        out_specs=pl.BlockSpec(memory_space=pl.ANY),
        out_shape=jax.ShapeDtypeStruct((n_pad, d), xt.dtype),
        scratch_shapes=[pltpu.SemaphoreType.DMA(())],
        compiler_params=pltpu.CompilerParams(dimension_semantics=("arbitrary",), has_side_effects=True),
        name="moe_dispatch",
    )(dest_tok_major.reshape(n_tiles, 1, DISPATCH_TILE * TOP_K), xt)


def _combine_kernel(dest_ref, dest_next_ref, ys_hbm, gate_ref, x_ref, wgu_ref, wdn_ref, g_ref, b_ref, o_ref, buf_ref, sem):
    i = pl.program_id(0)
    n = pl.num_programs(0)
    slot = lax.rem(i, 2)

    def gather(d_ref, into):
        def issue(r, carry):
            for k in range(TOP_K):
                pltpu.make_async_copy(ys_hbm.at[pl.ds(d_ref[0, 0, r * TOP_K + k], 1)],
                                      buf_ref.at[into, k, pl.ds(r, 1)], sem.at[into]).start()
            return carry
        lax.fori_loop(0, COMBINE_TILE, issue, 0)

    @pl.when(i == 0)
    def _():
        gather(dest_ref, 0)

    @pl.when(i + 1 < n)
    def _():
        gather(dest_next_ref, 1 - slot)

    x = x_ref[...]
    h = jnp.dot(x.astype(BF16), wgu_ref[...], preferred_element_type=F32)
    hg, hu = h[:, :D_SHARED], h[:, D_SHARED:]
    acc = DEEPNORM_ALPHA * x + jnp.dot((hg * jax.nn.sigmoid(hg) * hu).astype(BF16), wdn_ref[...], preferred_element_type=F32)
    pltpu.make_async_copy(buf_ref.at[slot], buf_ref.at[slot], sem.at[slot]).wait()
    gate = gate_ref[...]
    for k in range(TOP_K):
        acc = acc + buf_ref[slot, k] * gate[:, k:k + 1]
    mu = jnp.mean(acc, -1, keepdims=True)
    zc = acc - mu
    var = jnp.mean(zc * zc, -1, keepdims=True)
    o_ref[...] = zc * lax.rsqrt(var + LN_EPS) * g_ref[...] + b_ref[...]


def moe_combine(ys, dest_tok_major, gate_tok_major, xt, sh_w_gu, sh_w_down, ln_g, ln_b):
    t, d = xt.shape
    tm = COMBINE_TILE
    n_tiles = t // tm
    dest3 = dest_tok_major.reshape(n_tiles, 1, tm * TOP_K)
    row = pl.BlockSpec((tm, d), lambda i: (i, 0))
    full = lambda shape: pl.BlockSpec(shape, lambda i: (0, 0))
    return pl.pallas_call(
        _combine_kernel,
        grid=(n_tiles,),
        in_specs=[pl.BlockSpec((1, 1, tm * TOP_K), lambda i: (i, 0, 0), memory_space=pltpu.SMEM),
                  pl.BlockSpec((1, 1, tm * TOP_K), lambda i: (jnp.minimum(i + 1, n_tiles - 1), 0, 0), memory_space=pltpu.SMEM),
                  pl.BlockSpec(memory_space=pl.ANY),
                  pl.BlockSpec((tm, TOP_K), lambda i: (i, 0)), row,
                  full((d, 2 * D_SHARED)), full((D_SHARED, d)), full((1, d)), full((1, d))],
        out_specs=row,
        out_shape=jax.ShapeDtypeStruct((t, d), F32),
        scratch_shapes=[pltpu.VMEM((2, TOP_K, tm, d), F32), pltpu.SemaphoreType.DMA((2,))],
        compiler_params=pltpu.CompilerParams(dimension_semantics=("arbitrary",), vmem_limit_bytes=VMEM_LIMIT_BYTES),
        name="moe_combine",
    )(dest3, dest3, ys, gate_tok_major, xt, sh_w_gu.astype(BF16), sh_w_down.astype(BF16), ln_g.reshape(1, d), ln_b.reshape(1, d))


def moe_ffn(xt, router_w, router_bias, exp_w_gu, exp_w_down, sh_w_gu, sh_w_down, ln_g, ln_b):
    n_tok, d = xt.shape
    idx, gate, pos, counts = moe_router(xt, router_w, router_bias)
    counts = counts[:, 0]
    padded = (counts + MOE_BLOCK - 1) // MOE_BLOCK * MOE_BLOCK
    pad_ends = jnp.cumsum(padded)
    pad_start = pad_ends - padded
    n_blk = n_tok * TOP_K // MOE_BLOCK + N_EXPERTS
    n_pad = n_blk * MOE_BLOCK
    dest = (pad_start[idx] + pos).T.reshape(-1)
    blk_lo = jnp.arange(n_blk, dtype=jnp.int32) * MOE_BLOCK
    blk_expert = jnp.minimum(jnp.sum(pad_ends[None, :] <= blk_lo[:, None], axis=1), N_EXPERTS - 1).astype(jnp.int32)
    blk_rows = jnp.clip(pad_start[blk_expert] + counts[blk_expert] - blk_lo, 0, MOE_BLOCK).astype(jnp.int32)
    xs = moe_dispatch(xt, dest, n_pad)
    ys = expert_ffn(xs, blk_expert, blk_rows, exp_w_gu.astype(BF16), exp_w_down.astype(BF16))
    return moe_combine(ys, dest, gate.T, xt, sh_w_gu, sh_w_down, ln_g, ln_b)


def kernel(x, rel_bias, w_in, a_kv_norm, a_kv_up, b_conv, b_a_log, b_dt_bias, b_out_norm, c_lambda_re, c_lambda_im, c_log_dt, c_b_re, c_b_im, c_c_re, c_c_im, c_d, c_glu_w, c_glu_b, w_br_a, w_br_b, w_br_c, w_o, ln1_g, ln1_b, router_w, router_bias, exp_w_gu, exp_w_down, sh_w_gu, sh_w_down, ln2_g, ln2_b):
    bn, s, d = x.shape
    t = bn * s
    for i in range(w_in.shape[0]):
        xt = hybrid_mixer(x, rel_bias, w_in[i], a_kv_norm[i], a_kv_up[i], b_conv[i], b_a_log[i], b_dt_bias[i],
                          b_out_norm[i], c_lambda_re[i], c_lambda_im[i], c_log_dt[i], c_b_re[i], c_b_im[i],
                          c_c_re[i], c_c_im[i], c_d[i], c_glu_w[i], c_glu_b[i], w_br_a[i], w_br_b[i], w_br_c[i], w_o[i],
                          ln1_g[i], ln1_b[i])
        x = moe_ffn(xt, router_w[i], router_bias[i], exp_w_gu[i], exp_w_down[i], sh_w_gu[i], sh_w_down[i],
                    ln2_g[i], ln2_b[i]).reshape(bn, s, d)
    return x
```

```python
import functools
import math

import jax
import jax.numpy as jnp
import numpy as np
from jax import lax
from jax.experimental import pallas as pl
from jax.experimental.pallas import tpu as pltpu

F32 = jnp.float32
BF16 = jnp.bfloat16

D_MODEL = 1024
DEPTH = 4
A_HEADS = 8
A_HEAD_DIM = 64
A_WIDTH = A_HEADS * A_HEAD_DIM
A_KV_RANK = 128
IDX_HEADS = 8
IDX_DIM = 64
TOPK_MAX = 256
Q_BLOCK = 128
REL_BUCKETS = 32
REL_MAX_DIST = 128
B_HEADS = 4
B_HEAD_DIM = 128
B_WIDTH = B_HEADS * B_HEAD_DIM
CONV_WIDTH = 4
CHUNK = 64
C_WIDTH = 512
C_GROUP = 16
C_GROUPS = C_WIDTH // C_GROUP
C_STATE = 64
N_EXPERTS = 64
TOP_K = 8
D_EXPERT = 256
D_SHARED = 256
ROUTED_SCALE = 2.5
MOE_BLOCK = 256
SPLITS = (A_WIDTH, A_KV_RANK, IDX_HEADS * IDX_DIM, IDX_DIM, IDX_HEADS, 3 * B_WIDTH, B_WIDTH, B_HEADS, B_HEADS, C_WIDTH, 3 * D_MODEL)
DEEPNORM_ALPHA = (2 * DEPTH) ** 0.25
LN_EPS = 1e-5
RMS_EPS = 1e-6
NEG_INF = -1e30

VMEM_LIMIT_BYTES = 48 * 1024 * 1024
LANE = 128


def _pick_tile(n, candidates):
    for c in candidates:
        if n % c == 0:
            return c
    return n


def _mm_kernel(a_ref, b_ref, o_ref):
    o_ref[...] = jnp.dot(a_ref[...].astype(BF16), b_ref[...], preferred_element_type=F32).astype(o_ref.dtype)


def matmul(a, b, out_dtype=F32):
    m, k = a.shape
    _, n = b.shape
    tm = _pick_tile(m, (1024, 512, 256, 128, 8))
    tn = _pick_tile(n, (1024, 768, 512, 384, 256, 128))
    return pl.pallas_call(
        _mm_kernel,
        grid=(m // tm, n // tn),
        in_specs=[pl.BlockSpec((tm, k), lambda i, j: (i, 0)), pl.BlockSpec((k, tn), lambda i, j: (0, j))],
        out_specs=pl.BlockSpec((tm, tn), lambda i, j: (i, j)),
        out_shape=jax.ShapeDtypeStruct((m, n), out_dtype),
        compiler_params=pltpu.CompilerParams(dimension_semantics=("arbitrary", "arbitrary"), vmem_limit_bytes=VMEM_LIMIT_BYTES),
        name="matmul",
    )(a, b.astype(BF16))


def _res_ln_kernel(x_ref, y_ref, g_ref, b_ref, o_ref):
    z = DEEPNORM_ALPHA * x_ref[...] + y_ref[...]
    mu = jnp.mean(z, -1, keepdims=True)
    zc = z - mu
    var = jnp.mean(zc * zc, -1, keepdims=True)
    o_ref[...] = zc * lax.rsqrt(var + LN_EPS) * g_ref[...] + b_ref[...]


def residual_layer_norm(x, y, g, b):
    t, d = x.shape
    tm = _pick_tile(t, (512, 256, 128, 8))
    row = pl.BlockSpec((tm, d), lambda i: (i, 0))
    vec = pl.BlockSpec((1, d), lambda i: (0, 0))
    return pl.pallas_call(
        _res_ln_kernel,
        grid=(t // tm,),
        in_specs=[row, row, vec, vec],
        out_specs=row,
        out_shape=jax.ShapeDtypeStruct((t, d), F32),
        compiler_params=pltpu.CompilerParams(dimension_semantics=("arbitrary",), vmem_limit_bytes=VMEM_LIMIT_BYTES),
        name="residual_layer_norm",
    )(x, y, g.reshape(1, d), b.reshape(1, d))


def rms_norm(x, g):
    return x * lax.rsqrt(jnp.mean(x * x, -1, keepdims=True) + RMS_EPS) * g


def l2_normalize(x):
    return x * lax.rsqrt(jnp.sum(x * x, -1, keepdims=True) + RMS_EPS)


def rel_bucket(dist):
    n_exact = REL_BUCKETS // 2
    d = jnp.maximum(dist, 1).astype(F32)
    large = n_exact + (jnp.log(d / n_exact) / math.log(REL_MAX_DIST / n_exact) * (REL_BUCKETS - n_exact)).astype(jnp.int32)
    return jnp.where(dist < n_exact, dist, jnp.minimum(large, REL_BUCKETS - 1))


KEY_BLOCK = 128
DSA_CHUNK = 512
BLOCKS_PER_CHUNK = DSA_CHUNK // KEY_BLOCK
HEAD_PAIRS = A_HEADS // 2
PAIR_WIDTH = 2 * A_HEAD_DIM
INT_MIN = -2 ** 31


def _kv_kernel(c_ref, g_ref, wk_ref, wvt_ref, k_ref, vt_ref):
    c = c_ref[...]
    cn = (c * lax.rsqrt(jnp.mean(c * c, -1, keepdims=True) + RMS_EPS) * g_ref[...]).astype(BF16)
    k_ref[...] = jnp.dot(cn, wk_ref[...], preferred_element_type=F32).astype(BF16)
    vt = lax.dot_general(wvt_ref[...], cn, (((1,), (1,)), ((), ())), preferred_element_type=F32).astype(BF16)
    for i in range(vt_ref.shape[1]):
        vt_ref[0, i] = vt[:, i * KEY_BLOCK:(i + 1) * KEY_BLOCK]


def kv_project(c_kv, col, g, w_kv_up, bn, s):
    ts = 512
    w = w_kv_up.astype(BF16)
    return pl.pallas_call(
        _kv_kernel,
        grid=(bn, s // ts),
        in_specs=[pl.BlockSpec((ts, A_KV_RANK), lambda b, i: (b * (s // ts) + i, col)),
                  pl.BlockSpec((1, A_KV_RANK), lambda b, i: (0, 0)),
                  pl.BlockSpec((A_KV_RANK, A_WIDTH), lambda b, i: (0, 0)),
                  pl.BlockSpec((A_WIDTH, A_KV_RANK), lambda b, i: (0, 0))],
        out_specs=[pl.BlockSpec((None, ts, A_WIDTH), lambda b, i: (b, i, 0)),
                   pl.BlockSpec((1, ts // KEY_BLOCK, A_WIDTH, KEY_BLOCK), lambda b, i: (b, i, 0, 0))],
        out_shape=[jax.ShapeDtypeStruct((bn, s, A_WIDTH), BF16),
                   jax.ShapeDtypeStruct((bn, s // KEY_BLOCK, A_WIDTH, KEY_BLOCK), BF16)],
        compiler_params=pltpu.CompilerParams(dimension_semantics=("arbitrary", "arbitrary"), vmem_limit_bytes=VMEM_LIMIT_BYTES),
        name="kv_project",
    )(c_kv, g.reshape(1, A_KV_RANK), w[:, :A_WIDTH], w[:, A_WIDTH:].T)


def _mm_nt_kernel(wt_ref, x_ref, o_ref):
    o_ref[0] = lax.dot_general(wt_ref[...], x_ref[...].astype(BF16), (((1,), (1,)), ((), ())),
                               preferred_element_type=F32).astype(o_ref.dtype)


def matmul_nt(wt, x, bn, s, out_dtype=BF16):
    n, k = wt.shape
    ts = 512
    return pl.pallas_call(
        _mm_nt_kernel,
        grid=(bn, s // ts),
        in_specs=[pl.BlockSpec((n, k), lambda b, i: (0, 0)),
                  pl.BlockSpec((ts, k), lambda b, i: (b * (s // ts) + i, 0))],
        out_specs=pl.BlockSpec((1, n, ts), lambda b, i: (b, 0, i)),
        out_shape=jax.ShapeDtypeStruct((bn, n, s), out_dtype),
        compiler_params=pltpu.CompilerParams(dimension_semantics=("arbitrary", "arbitrary"), vmem_limit_bytes=VMEM_LIMIT_BYTES),
        name="matmul_nt",
    )(wt.astype(BF16), x)


def _sortable_key(score):
    bits = pltpu.bitcast(score, jnp.int32)
    return bits ^ (lax.shift_right_arithmetic(bits, 31) & jnp.int32(0x7FFFFFFF))


def _dsa_kernel(qt_ref, iqt_ref, wt_ref, kidx_ref, k_ref, vt_ref, bias_ref, o_ref,
                skey_ref, widx_ref, wq_ref, acc_ref, m_ref, l_ref, *, k_sel):
    j = pl.program_id(1)
    n_chunks = (j + BLOCKS_PER_CHUNK) // BLOCKS_PER_CHUNK
    q_pos = j * Q_BLOCK + lax.broadcasted_iota(jnp.int32, (1, Q_BLOCK), 1)

    for h in range(IDX_HEADS):
        widx_ref[:, h * Q_BLOCK:(h + 1) * Q_BLOCK] = iqt_ref[0, h * IDX_DIM:(h + 1) * IDX_DIM, :]
    wq_ref[...] = jnp.zeros_like(wq_ref)
    for p in range(HEAD_PAIRS):
        wq_ref[p, 0:A_HEAD_DIM, 0:Q_BLOCK] = qt_ref[0, (2 * p) * A_HEAD_DIM:(2 * p + 1) * A_HEAD_DIM, :]
        wq_ref[p, A_HEAD_DIM:PAIR_WIDTH, Q_BLOCK:2 * Q_BLOCK] = qt_ref[0, (2 * p + 1) * A_HEAD_DIM:(2 * p + 2) * A_HEAD_DIM, :]

    w_rows = [wt_ref[0, h:h + 1, :] * (IDX_HEADS ** -0.5 * IDX_DIM ** -0.5) for h in range(IDX_HEADS)]

    def idx_chunk(c, carry):
        start = pl.multiple_of(c * DSA_CHUNK, DSA_CHUNK)
        kc = kidx_ref[0, pl.ds(start, DSA_CHUNK), :]
        score = jnp.zeros((DSA_CHUNK, Q_BLOCK), F32)
        for p in range(IDX_HEADS // 2):
            z = jnp.dot(kc, widx_ref[:, p * 2 * Q_BLOCK:(p + 1) * 2 * Q_BLOCK], preferred_element_type=F32)
            score = score + jnp.maximum(z[:, :Q_BLOCK], 0.0) * w_rows[2 * p]
            score = score + jnp.maximum(z[:, Q_BLOCK:], 0.0) * w_rows[2 * p + 1]
        pos = start + lax.broadcasted_iota(jnp.int32, (DSA_CHUNK, Q_BLOCK), 0)
        skey_ref[pl.ds(start, DSA_CHUNK), :] = jnp.where(pos <= q_pos, _sortable_key(score), INT_MIN)
        return carry

    lax.fori_loop(0, n_chunks, idx_chunk, 0)

    def count_ge(cand):
        def body(c, cnt):
            start = pl.multiple_of(c * DSA_CHUNK, DSA_CHUNK)
            hit = (skey_ref[pl.ds(start, DSA_CHUNK), :] >= cand).astype(jnp.int32)
            return cnt + jnp.sum(hit.reshape(DSA_CHUNK // 8, 8, Q_BLOCK), axis=0)
        cnt = lax.fori_loop(0, n_chunks, body, jnp.zeros((8, Q_BLOCK), jnp.int32))
        return jnp.sum(cnt, axis=0, keepdims=True)

    def bit_step(i, thr_biased):
        cand_biased = thr_biased | lax.shift_left(jnp.int32(1), 31 - i)
        return jnp.where(count_ge(cand_biased ^ INT_MIN) >= k_sel, cand_biased, thr_biased)

    thr = lax.fori_loop(0, 32, bit_step, jnp.zeros((1, Q_BLOCK), jnp.int32)) ^ INT_MIN

    n_gt = count_ge(thr + 1)
    n_ge = count_ge(thr)
    tied = (n_ge > k_sel) & (thr > INT_MIN)

    @pl.when(jnp.max(tied.astype(jnp.int32)) > 0)
    def _():
        need = k_sel - n_gt

        def count_tied_upto(limit):
            def body(c, cnt):
                start = pl.multiple_of(c * DSA_CHUNK, DSA_CHUNK)
                pos = start + lax.broadcasted_iota(jnp.int32, (DSA_CHUNK, Q_BLOCK), 0)
                hit = jnp.where(skey_ref[pl.ds(start, DSA_CHUNK), :] == thr, (pos <= limit).astype(jnp.int32), 0)
                return cnt + jnp.sum(hit.reshape(DSA_CHUNK // 8, 8, Q_BLOCK), axis=0)
            cnt = lax.fori_loop(0, n_chunks, body, jnp.zeros((8, Q_BLOCK), jnp.int32))
            return jnp.sum(cnt, axis=0, keepdims=True)

        def pos_step(i, lim):
            cand = lim - lax.shift_left(jnp.int32(1), 30 - i)
            return jnp.where(count_tied_upto(cand) >= need, cand, lim)

        limit = lax.fori_loop(0, 31, pos_step, jnp.full((1, Q_BLOCK), 2 ** 31 - 1, jnp.int32))

        def demote(c, carry):
            start = pl.multiple_of(c * DSA_CHUNK, DSA_CHUNK)
            pos = start + lax.broadcasted_iota(jnp.int32, (DSA_CHUNK, Q_BLOCK), 0)
            sk = skey_ref[pl.ds(start, DSA_CHUNK), :]
            skey_ref[pl.ds(start, DSA_CHUNK), :] = jnp.where(tied & (sk == thr) & (pos > limit), thr - 1, sk)
            return carry

        lax.fori_loop(0, n_chunks, demote, 0)

    m_ref[...] = jnp.full_like(m_ref, NEG_INF)
    l_ref[...] = jnp.zeros_like(l_ref)
    acc_ref[...] = jnp.zeros_like(acc_ref)

    def attend(s_list, vt_of, n_blocks):
        for p in range(HEAD_PAIRS):
            s = s_list[p]
            m_old = m_ref[p]
            m_new = jnp.maximum(m_old, jnp.max(s, axis=0, keepdims=True))
            alpha = jnp.exp(m_old - m_new)
            pexp = jnp.exp(s - m_new)
            l_ref[p] = l_ref[p] * alpha + jnp.sum(pexp, axis=0, keepdims=True)
            pb = pexp.astype(BF16)
            pv = jnp.dot(vt_of(p, 0), pb[0:KEY_BLOCK], preferred_element_type=F32)
            for i in range(1, n_blocks):
                pv = pv + jnp.dot(vt_of(p, i), pb[i * KEY_BLOCK:(i + 1) * KEY_BLOCK], preferred_element_type=F32)
            acc_ref[p] = acc_ref[p] * alpha + pv
            m_ref[p] = m_new

    far_end = (j - 1) * KEY_BLOCK
    n_far = (jnp.maximum(j - 1, 0) + BLOCKS_PER_CHUNK - 1) // BLOCKS_PER_CHUNK

    def far_chunk(c, carry):
        start = pl.multiple_of(c * DSA_CHUNK, DSA_CHUNK)
        pos = start + lax.broadcasted_iota(jnp.int32, (DSA_CHUNK, Q_BLOCK), 0)
        sel = (skey_ref[pl.ds(start, DSA_CHUNK), :] >= thr) & (pos < far_end)
        madd = jnp.where(sel, 0.0, NEG_INF)
        madd2 = jnp.concatenate([madd, madd], axis=1)
        kc = k_ref[0, pl.ds(start, DSA_CHUNK), :]
        s_list = [jnp.dot(kc[:, p * PAIR_WIDTH:(p + 1) * PAIR_WIDTH], wq_ref[p], preferred_element_type=F32) + madd2
                  for p in range(HEAD_PAIRS)]
        blk0 = c * BLOCKS_PER_CHUNK
        attend(s_list, lambda p, i: vt_ref[0, blk0 + i, p * PAIR_WIDTH:(p + 1) * PAIR_WIDTH, :], BLOCKS_PER_CHUNK)
        return carry

    lax.fori_loop(0, n_far, far_chunk, 0)

    def near_block(kb, delta):
        start = pl.multiple_of(kb * KEY_BLOCK, KEY_BLOCK)
        pos = start + lax.broadcasted_iota(jnp.int32, (KEY_BLOCK, Q_BLOCK), 0)
        sel = (skey_ref[pl.ds(start, KEY_BLOCK), :] >= thr) & (pos <= q_pos)
        madd = jnp.where(sel, 0.0, NEG_INF)
        madd2 = jnp.concatenate([madd, madd], axis=1)
        kc = k_ref[0, pl.ds(start, KEY_BLOCK), :]
        s_list = [jnp.dot(kc[:, p * PAIR_WIDTH:(p + 1) * PAIR_WIDTH], wq_ref[p], preferred_element_type=F32)
                  + bias_ref[delta, p] + madd2 for p in range(HEAD_PAIRS)]
        attend(s_list, lambda p, i: vt_ref[0, kb, p * PAIR_WIDTH:(p + 1) * PAIR_WIDTH, :], 1)

    @pl.when(j > 0)
    def _():
        near_block(j - 1, 1)

    near_block(j, 0)

    outs = []
    for p in range(HEAD_PAIRS):
        o = acc_ref[p] / l_ref[p]
        outs.append(o[0:A_HEAD_DIM, 0:Q_BLOCK])
        outs.append(o[A_HEAD_DIM:PAIR_WIDTH, Q_BLOCK:2 * Q_BLOCK])
    o_ref[0] = jnp.concatenate(outs, axis=0).T.astype(o_ref.dtype)


def _rel_bias_tiles(rel_bias):
    tab = rel_bias[rel_bucket(jnp.arange(2 * Q_BLOCK, dtype=jnp.int32))] - rel_bias[REL_BUCKETS - 1]
    key = jnp.arange(KEY_BLOCK)[:, None]
    qry = jnp.arange(Q_BLOCK)[None, :]
    tiles = []
    for delta in (0, 1):
        dist = jnp.maximum(delta * Q_BLOCK + qry - key, 0)
        t = jnp.moveaxis(tab[dist], 2, 0)
        t = t.reshape(HEAD_PAIRS, 2, KEY_BLOCK, Q_BLOCK)
        tiles.append(jnp.concatenate([t[:, 0], t[:, 1]], axis=-1))
    return jnp.stack(tiles)


def dsa_attention(qiq_t, wt, k_idx, k, vt, bias_tiles, bn, s):
    assert A_WIDTH == IDX_HEADS * IDX_DIM
    scal_rows = wt.shape[1]
    assert scal_rows % IDX_HEADS == 0
    k_sel = min(TOPK_MAX, s // 4)
    nb = s // Q_BLOCK
    s_pad = s + DSA_CHUNK
    return pl.pallas_call(
        functools.partial(_dsa_kernel, k_sel=k_sel),
        grid=(bn, nb),
        in_specs=[pl.BlockSpec((1, A_WIDTH, Q_BLOCK), lambda b, j: (b, 0, j)),
                  pl.BlockSpec((1, IDX_HEADS * IDX_DIM, Q_BLOCK), lambda b, j: (b, 1, j)),
                  pl.BlockSpec((1, IDX_HEADS, Q_BLOCK), lambda b, j: (b, scal_rows // IDX_HEADS - 1, j)),
                  pl.BlockSpec((1, s, IDX_DIM), lambda b, j: (b, 0, 0)),
                  pl.BlockSpec((1, s, A_WIDTH), lambda b, j: (b, 0, 0)),
                  pl.BlockSpec((1, s // KEY_BLOCK, A_WIDTH, KEY_BLOCK), lambda b, j: (b, 0, 0, 0)),
                  pl.BlockSpec((2, HEAD_PAIRS, KEY_BLOCK, 2 * Q_BLOCK), lambda b, j: (0, 0, 0, 0))],
        out_specs=pl.BlockSpec((1, Q_BLOCK, A_WIDTH), lambda b, j: (b, j, 0)),
        out_shape=jax.ShapeDtypeStruct((bn, s, A_WIDTH), BF16),
        scratch_shapes=[pltpu.VMEM((s_pad, Q_BLOCK), jnp.int32),
                        pltpu.VMEM((IDX_DIM, IDX_HEADS * Q_BLOCK), BF16),
                        pltpu.VMEM((HEAD_PAIRS, PAIR_WIDTH, 2 * Q_BLOCK), BF16),
                        pltpu.VMEM((HEAD_PAIRS, PAIR_WIDTH, 2 * Q_BLOCK), F32),
                        pltpu.VMEM((HEAD_PAIRS, 1, 2 * Q_BLOCK), F32),
                        pltpu.VMEM((HEAD_PAIRS, 1, 2 * Q_BLOCK), F32)],
        compiler_params=pltpu.CompilerParams(dimension_semantics=("arbitrary", "arbitrary"), vmem_limit_bytes=56 * 1024 * 1024),
        name="dsa_attention",
    )(qiq_t, qiq_t, wt, k_idx, k, vt, bias_tiles)


GDN_CHUNK = B_HEAD_DIM
GDN_BLOCK = 512
GDN_HALO = 8
NEUMANN_STEPS = 6


def _split_bf16(x):
    hi = x.astype(BF16)
    return hi, (x - hi.astype(F32)).astype(BF16)


def _dot_f32(a, b):
    a_hi, a_lo = _split_bf16(a)
    b_hi, b_lo = _split_bf16(b)
    d = lambda x, y: jnp.dot(x, y, preferred_element_type=F32)
    return d(a_hi, b_hi) + (d(a_hi, b_lo) + d(a_lo, b_hi))


def _bdot(a, b):
    return jnp.dot(a.astype(BF16), b.astype(BF16), preferred_element_type=F32)


def _bdot_nt(a, b):
    return lax.dot_general(a.astype(BF16), b.astype(BF16), (((1,), (1,)), ((), ())), preferred_element_type=F32)


def _bdot_tn(a, b):
    return lax.dot_general(a.astype(BF16), b.astype(BF16), (((0,), (0,)), ((), ())), preferred_element_type=F32)


def _softplus(x):
    return jnp.maximum(x, 0.0) + jnp.log(1.0 + jnp.exp(-jnp.abs(x)))


def _gdn_kernel(hq_ref, hk_ref, hv_ref, hz_ref, wq_ref, wk_ref, wv_ref, ab_ref, const_ref, gout_ref, tri_ref,
                o_ref, state_ref, xbuf_ref):
    head = pl.program_id(1)

    @pl.when(pl.program_id(2) == 0)
    def _():
        state_ref[...] = jnp.zeros_like(state_ref)
        xbuf_ref[:, 0:GDN_HALO, :] = jnp.zeros((3, GDN_HALO, B_HEAD_DIM), F32)

    def conv_silu(n, x_ref, w_ref):
        x = x_ref[...]
        xbuf_ref[n, GDN_HALO:, :] = x
        acc = xbuf_ref[n, GDN_HALO - CONV_WIDTH + 1:GDN_HALO - CONV_WIDTH + 1 + GDN_BLOCK, :] * w_ref[0:1, :]
        for j in range(1, CONV_WIDTH):
            lo = GDN_HALO - CONV_WIDTH + 1 + j
            acc = acc + xbuf_ref[n, lo:lo + GDN_BLOCK, :] * w_ref[j:j + 1, :]
        xbuf_ref[n, 0:GDN_HALO, :] = x[GDN_BLOCK - GDN_HALO:, :]
        return acc * jax.nn.sigmoid(acc)

    q = conv_silu(0, hq_ref, wq_ref)
    k = conv_silu(1, hk_ref, wk_ref)
    v = conv_silu(2, hv_ref, wv_ref)
    q = q * lax.rsqrt(jnp.sum(q * q, -1, keepdims=True) + RMS_EPS) * B_HEAD_DIM ** -0.5
    k = k * lax.rsqrt(jnp.sum(k * k, -1, keepdims=True) + RMS_EPS)

    a_row = ab_ref[0, pl.ds(head, 1), :]
    b_row = ab_ref[0, pl.ds(B_HEADS + head, 1), :]
    beta_row = jax.nn.sigmoid(b_row)
    g_row = const_ref[0, 0:1, :] * _softplus(a_row + const_ref[0, 1:2, :])
    gc_rows = _dot_f32(jnp.broadcast_to(g_row, (8, GDN_BLOCK)), tri_ref[...])[0:1, :]

    row_i = lax.broadcasted_iota(jnp.int32, (GDN_CHUNK, GDN_CHUNK), 0)
    col_j = lax.broadcasted_iota(jnp.int32, (GDN_CHUNK, GDN_CHUNK), 1)
    eye = jnp.where(row_i == col_j, 1.0, 0.0)
    z_all = hz_ref[...]
    for c in range(GDN_BLOCK // GDN_CHUNK):
        sl = slice(c * GDN_CHUNK, (c + 1) * GDN_CHUNK)
        qc, kc, vc = q[sl], k[sl], v[sl]
        g_lane = jnp.broadcast_to(gc_rows[:, sl], (GDN_CHUNK, GDN_CHUNK))
        g_sub = g_lane.T
        beta_sub = jnp.broadcast_to(beta_row[:, sl], (GDN_CHUNK, GDN_CHUNK)).T
        g_end = jnp.broadcast_to(g_lane[:, GDN_CHUNK - 1:GDN_CHUNK], (GDN_CHUNK, GDN_CHUNK))
        decay = jnp.exp(jnp.where(row_i >= col_j, g_sub - g_lane, -jnp.inf))
        kb = kc * beta_sub
        m = jnp.where(row_i > col_j, _bdot_nt(kb, kc) * decay, 0.0)
        x = -m
        t_inv = eye + x
        for _ in range(NEUMANN_STEPS):
            x = _dot_f32(x, x)
            t_inv = t_inv + _dot_f32(t_inv, x)
        g_exp = jnp.exp(g_sub)
        u = _bdot(t_inv, vc * beta_sub)
        w = _bdot(t_inv, kb * g_exp)
        qk = _bdot_nt(qc, kc) * decay
        state = state_ref[...]
        v_new = u - _bdot(w, state)
        o = _bdot(qc * g_exp, state) + _bdot(qk, v_new)
        state_ref[...] = state * jnp.exp(g_end) + _bdot_tn(kc * jnp.exp(g_end - g_sub), v_new)
        o = o * lax.rsqrt(jnp.mean(o * o, -1, keepdims=True) + RMS_EPS) * gout_ref[...]
        zc = z_all[sl]
        o_ref[sl, :] = (o * (zc * jax.nn.sigmoid(zc))).astype(o_ref.dtype)


def gated_deltanet(h, col0, ab_t, conv_w, a_log, dt_bias, out_norm_g, bn, s):
    nblk = s // GDN_BLOCK
    tok = jnp.arange(GDN_BLOCK)
    tri = ((tok[:, None] // GDN_CHUNK == tok[None, :] // GDN_CHUNK) & (tok[:, None] <= tok[None, :])).astype(F32)
    const = jnp.stack([jnp.broadcast_to(-jnp.exp(a_log)[:, None], (B_HEADS, GDN_BLOCK)),
                       jnp.broadcast_to(dt_bias[:, None], (B_HEADS, GDN_BLOCK))], axis=1)
    const = jnp.pad(const, ((0, 0), (0, 6), (0, 0)))
    hcol = lambda part: pl.BlockSpec((GDN_BLOCK, B_HEAD_DIM), lambda b, hh, i: (b * nblk + i, col0 + part * B_HEADS + hh))
    wcol = lambda part: pl.BlockSpec((CONV_WIDTH, B_HEAD_DIM), lambda b, hh, i: (0, part * B_HEADS + hh))
    return pl.pallas_call(
        _gdn_kernel,
        grid=(bn, B_HEADS, nblk),
        in_specs=[hcol(0), hcol(1), hcol(2), hcol(3), wcol(0), wcol(1), wcol(2),
                  pl.BlockSpec((1, 16, GDN_BLOCK), lambda b, hh, i: (b, 0, i)),
                  pl.BlockSpec((1, 8, GDN_BLOCK), lambda b, hh, i: (hh, 0, 0)),
                  pl.BlockSpec((1, B_HEAD_DIM), lambda b, hh, i: (0, 0)),
                  pl.BlockSpec((GDN_BLOCK, GDN_BLOCK), lambda b, hh, i: (0, 0))],
        out_specs=pl.BlockSpec((GDN_BLOCK, B_HEAD_DIM), lambda b, hh, i: (b * nblk + i, hh)),
        out_shape=jax.ShapeDtypeStruct((bn * s, B_WIDTH), BF16),
        scratch_shapes=[pltpu.VMEM((B_HEAD_DIM, B_HEAD_DIM), F32),
                        pltpu.VMEM((3, GDN_HALO + GDN_BLOCK, B_HEAD_DIM), F32)],
        compiler_params=pltpu.CompilerParams(dimension_semantics=("arbitrary", "arbitrary", "arbitrary"), vmem_limit_bytes=VMEM_LIMIT_BYTES),
        name="gated_deltanet",
    )(h, h, h, h, conv_w, conv_w, conv_w, ab_t, const, out_norm_g.reshape(1, B_HEAD_DIM), tri)


S5_CHUNK = 32
S5_ROW = S5_CHUNK * C_GROUP
STATE2 = 2 * C_STATE


def _s5_tables(lam_re, lam_im, log_dt, b_re, b_im, c_re, c_im, d_skip):
    hp = lax.Precision.HIGHEST
    ell = S5_CHUNK
    dt = jnp.exp(log_dt)[:, None]
    lr, li = lam_re, lam_im
    mag = jnp.exp(lr * dt)
    a_re, a_im = mag * jnp.cos(li * dt), mag * jnp.sin(li * dt)
    den = lr * lr + li * li
    f_re = ((a_re - 1.0) * lr + a_im * li) / den
    f_im = (a_im * lr - (a_re - 1.0) * li) / den
    bb_re = f_re[..., None] * b_re - f_im[..., None] * b_im
    bb_im = f_re[..., None] * b_im + f_im[..., None] * b_re
    tau = jnp.arange(ell + 1, dtype=F32)[:, None, None]
    pmag, ang = jnp.exp(tau * (lr * dt)), tau * (li * dt)
    p_re, p_im = pmag * jnp.cos(ang), pmag * jnp.sin(ang)
    ab_re = p_re[:ell, ..., None] * bb_re - p_im[:ell, ..., None] * bb_im
    ab_im = p_re[:ell, ..., None] * bb_im + p_im[:ell, ..., None] * bb_re
    kern = (jnp.einsum('gip,tgpj->tgij', c_re, ab_re, precision=hp)
            - jnp.einsum('gip,tgpj->tgij', c_im, ab_im, precision=hp))
    lag = jnp.arange(ell)[None, :] - jnp.arange(ell)[:, None]
    toep = jnp.where((lag >= 0)[:, :, None, None, None], kern[jnp.maximum(lag, 0)], 0.0)
    toep = jnp.transpose(toep, (2, 0, 4, 1, 3)).reshape(C_GROUPS, S5_ROW, S5_ROW)
    w_re = jnp.transpose(ab_re[::-1], (1, 0, 3, 2)).reshape(C_GROUPS, S5_ROW, C_STATE)
    w_im = jnp.transpose(ab_im[::-1], (1, 0, 3, 2)).reshape(C_GROUPS, S5_ROW, C_STATE)
    tw = jnp.concatenate([toep, w_re, w_im, w_im, w_re], axis=-1).astype(BF16)
    pt_re = jnp.swapaxes(p_re[1:], 0, 1)[:, :, None, :]
    pt_im = jnp.swapaxes(p_im[1:], 0, 1)[:, :, None, :]
    ca_re = c_re[:, None] * pt_re - c_im[:, None] * pt_im
    ca_im = c_re[:, None] * pt_im + c_im[:, None] * pt_re
    v_re = jnp.transpose(ca_re, (0, 3, 1, 2)).reshape(C_GROUPS, C_STATE, S5_ROW)
    v_im = jnp.transpose(-ca_im, (0, 3, 1, 2)).reshape(C_GROUPS, C_STATE, S5_ROW)
    v = jnp.concatenate([v_re, v_im], axis=1).astype(BF16)
    al_re, al_im = p_re[ell], p_im[ell]
    coef = jnp.stack([jnp.concatenate([al_re, al_re], -1), jnp.concatenate([-al_im, al_im], -1),
                      jnp.concatenate([al_im, -al_im], -1)], axis=1)
    d_flat = jnp.tile(d_skip, (1, ell)).reshape(C_GROUPS, 1, S5_ROW)
    return tw, v, coef, d_flat


def _s5_kernel(u_ref, tw_ref, v_ref, coef_ref, d_ref, o_ref, sin_ref, xin_ref, *, bn):
    rows = u_ref.shape[1]
    n_chunks = rows // bn
    u = u_ref[0]
    tw = jnp.dot(u.astype(BF16), tw_ref[0], preferred_element_type=F32)
    o_ref[0] = tw[:, :S5_ROW] + u * d_ref[0]
    sin_ref[...] = tw[:, S5_ROW:]
    a1, a2, a2s = coef_ref[0, 0:1, :], coef_ref[0, 1:2, :], coef_ref[0, 2:3, :]

    def step(c, carry):
        new = []
        for b in range(bn):
            x, xs = carry[2 * b], carry[2 * b + 1]
            row = b * n_chunks + c
            xin_ref[pl.ds(row, 1), :] = x
            s_row = sin_ref[pl.ds(row, 1), :]
            new.append(a1 * x + a2 * xs + s_row[:, :STATE2])
            new.append(a1 * xs + a2s * x + s_row[:, STATE2:])
        return tuple(new)

    zero = jnp.zeros((1, STATE2), F32)
    lax.fori_loop(0, n_chunks, step, (zero,) * (2 * bn))
    y = o_ref[0] + jnp.dot(xin_ref[...].astype(BF16), v_ref[0], preferred_element_type=F32)
    o_ref[0] = jax.nn.gelu(y)


def _glu_kernel(y_ref, w_ref, b_ref, o_ref):
    y = y_ref[...]
    gate = jnp.dot(y.astype(BF16), w_ref[...], preferred_element_type=F32) + b_ref[...]
    o_ref[...] = (y * jax.nn.sigmoid(gate)).astype(o_ref.dtype)


def s5_branch(u, lam_re, lam_im, log_dt, b_re, b_im, c_re, c_im, d_skip, glu_w, glu_b):
    bn, s, _ = u.shape
    n_chunks = s // S5_CHUNK
    rows = bn * n_chunks
    tw, v, coef, d_flat = _s5_tables(lam_re, lam_im, log_dt, b_re, b_im, c_re, c_im, d_skip)
    uf = jnp.transpose(u.reshape(bn, n_chunks, S5_CHUNK, C_GROUPS, C_GROUP), (3, 0, 1, 2, 4)).reshape(C_GROUPS, rows, S5_ROW)
    per_group = lambda shape: pl.BlockSpec((1,) + shape, lambda g: (g, 0, 0))
    y = pl.pallas_call(
        functools.partial(_s5_kernel, bn=bn),
        grid=(C_GROUPS,),
        in_specs=[per_group((rows, S5_ROW)), per_group((S5_ROW, S5_ROW + 2 * STATE2)), per_group((STATE2, S5_ROW)),
                  per_group((3, STATE2)), per_group((1, S5_ROW))],
        out_specs=per_group((rows, S5_ROW)),
        out_shape=jax.ShapeDtypeStruct((C_GROUPS, rows, S5_ROW), F32),
        scratch_shapes=[pltpu.VMEM((rows, 2 * STATE2), F32), pltpu.VMEM((rows, STATE2), F32)],
        compiler_params=pltpu.CompilerParams(dimension_semantics=("arbitrary",), vmem_limit_bytes=VMEM_LIMIT_BYTES),
        name="s5_scan",
    )(uf, tw, v, coef, d_flat)
    y = jnp.transpose(y.reshape(C_GROUPS, bn, n_chunks, S5_CHUNK, C_GROUP), (1, 2, 3, 0, 4)).reshape(bn * s, C_WIDTH)
    t = bn * s
    tm = _pick_tile(t, (1024, 512, 256, 128, 8))
    return pl.pallas_call(
        _glu_kernel,
        grid=(t // tm,),
        in_specs=[pl.BlockSpec((tm, C_WIDTH), lambda i: (i, 0)), pl.BlockSpec((C_WIDTH, C_WIDTH), lambda i: (0, 0)),
                  pl.BlockSpec((1, C_WIDTH), lambda i: (0, 0))],
        out_specs=pl.BlockSpec((tm, C_WIDTH), lambda i: (i, 0)),
        out_shape=jax.ShapeDtypeStruct((t, C_WIDTH), BF16),
        compiler_params=pltpu.CompilerParams(dimension_semantics=("arbitrary",), vmem_limit_bytes=VMEM_LIMIT_BYTES),
        name="s5_glu",
    )(y, glu_w.astype(BF16), glu_b.reshape(1, C_WIDTH))


def hybrid_mixer(x, rel_bias, w_in, a_kv_norm, a_kv_up, b_conv, b_a_log, b_dt_bias, b_out_norm,
                 c_lambda_re, c_lambda_im, c_log_dt, c_b_re, c_b_im, c_c_re, c_c_im, c_d, c_glu_w, c_glu_b,
                 w_br_a, w_br_b, w_br_c, w_o, ln_g, ln_b):
    bn, s, d = x.shape
    t = bn * s
    xt = x.reshape(t, d)
    offsets = [0] + np.cumsum(SPLITS).tolist()
    w_cols = [w_in[:, offsets[i]:offsets[i + 1]] for i in range(len(SPLITS))]
    (w_aq, w_ckv, w_iq, w_ik, w_iw, w_bqkv, w_bz, w_ba, w_bb, w_cu, w_gate) = w_cols
    w_ik_pad = jnp.pad(w_ik, ((0, 0), (0, LANE - IDX_DIM)))
    h = matmul(xt, jnp.concatenate([w_gate, w_ckv, w_ik_pad, w_bqkv, w_bz, w_cu], axis=1))
    o = np.cumsum([0, 3 * D_MODEL, A_KV_RANK, LANE, 3 * B_WIDTH, B_WIDTH, C_WIDTH]).tolist()
    idx_k = h[:, o[2]:o[2] + IDX_DIM].astype(BF16).reshape(bn, s, IDX_DIM)
    c_u = h[:, o[5]:o[6]].reshape(bn, s, C_WIDTH)
    qiq_t = matmul_nt(jnp.concatenate([w_aq * A_HEAD_DIM ** -0.5, w_iq], axis=1).T, xt, bn, s)
    scal_t = matmul_nt(jnp.concatenate([w_ba, w_bb, w_iw], axis=1).T, xt, bn, s, out_dtype=F32)
    k, vt = kv_project(h, o[1] // LANE, a_kv_norm, a_kv_up, bn, s)
    ya = dsa_attention(qiq_t, scal_t, idx_k, k, vt, _rel_bias_tiles(rel_bias), bn, s)
    yb = gated_deltanet(h, o[3] // LANE, scal_t, b_conv, b_a_log, b_dt_bias, b_out_norm, bn, s)
    yc = s5_branch(c_u, c_lambda_re, c_lambda_im, c_log_dt, c_b_re, c_b_im, c_c_re, c_c_im, c_d, c_glu_w, c_glu_b)
    return merge_branches(ya.reshape(t, A_WIDTH), yb, yc, h, w_br_a, w_br_b, w_br_c, w_o, xt, ln_g, ln_b)


def _merge_kernel(ya_ref, yb_ref, yc_ref, ga_ref, gb_ref, gc_ref, wa_ref, wb_ref, wc_ref, wo_ref, x_ref, g_ref, b_ref, o_ref):
    branch = lambda y_ref, w_ref, gate_ref: jax.nn.sigmoid(gate_ref[...]) * jnp.dot(y_ref[...], w_ref[...], preferred_element_type=F32)
    merged = branch(ya_ref, wa_ref, ga_ref) + branch(yb_ref, wb_ref, gb_ref) + branch(yc_ref, wc_ref, gc_ref)
    mix = jnp.dot(merged.astype(BF16), wo_ref[...], preferred_element_type=F32)
    z = DEEPNORM_ALPHA * x_ref[...] + mix
    mu = jnp.mean(z, -1, keepdims=True)
    zc = z - mu
    var = jnp.mean(zc * zc, -1, keepdims=True)
    o_ref[...] = zc * lax.rsqrt(var + LN_EPS) * g_ref[...] + b_ref[...]


def merge_branches(ya, yb, yc, h, w_br_a, w_br_b, w_br_c, w_o, x, ln_g, ln_b):
    t, d = x.shape
    tm = _pick_tile(t, (512, 256, 128, 8))
    row = lambda width, col=0: pl.BlockSpec((tm, width), lambda i: (i, col))
    full = lambda a: pl.BlockSpec(a.shape, lambda i: (0, 0))
    ws = [w.astype(BF16) for w in (w_br_a, w_br_b, w_br_c, w_o)]
    return pl.pallas_call(
        _merge_kernel,
        grid=(t // tm,),
        in_specs=[row(A_WIDTH), row(B_WIDTH), row(C_WIDTH), row(d, 0), row(d, 1), row(d, 2)] + [full(w) for w in ws]
                 + [row(d), pl.BlockSpec((1, d), lambda i: (0, 0)), pl.BlockSpec((1, d), lambda i: (0, 0))],
        out_specs=row(d),
        out_shape=jax.ShapeDtypeStruct((t, d), F32),
        compiler_params=pltpu.CompilerParams(dimension_semantics=("arbitrary",), vmem_limit_bytes=VMEM_LIMIT_BYTES),
        name="merge_branches",
    )(ya, yb, yc, h, h, h, *ws, x, ln_g.reshape(1, d), ln_b.reshape(1, d))


ROUTER_TILE = 512


def _router_kernel(x_ref, rwt_ref, bias_ref, tri_ref, idx_ref, gate_ref, pos_ref, cnt_ref, carry_ref):
    @pl.when(pl.program_id(0) == 0)
    def _():
        carry_ref[...] = jnp.zeros_like(carry_ref)

    logits = lax.dot_general(rwt_ref[...], x_ref[...].astype(BF16), (((1,), (1,)), ((), ())), preferred_element_type=F32)
    scores = jax.nn.sigmoid(logits)
    remaining = scores + bias_ref[...]
    expert = lax.broadcasted_iota(jnp.int32, scores.shape, 0)
    picks = []
    for _ in range(TOP_K):
        best = jnp.max(remaining, axis=0, keepdims=True)
        first = jnp.min(jnp.where(remaining == best, expert, N_EXPERTS), axis=0, keepdims=True)
        pick = expert == first
        picks.append((first, pick))
        remaining = jnp.where(pick, -jnp.inf, remaining)
    chosen = picks[0][1]
    for _, pick in picks[1:]:
        chosen = chosen | pick
    total = jnp.sum(jnp.where(chosen, scores, 0.0), axis=0, keepdims=True)
    prefix = jnp.dot(jnp.where(chosen, 1.0, 0.0).astype(BF16), tri_ref[...], preferred_element_type=F32)
    rank = carry_ref[...] + prefix.astype(jnp.int32) - 1
    carry_ref[...] = carry_ref[...] + prefix[:, -1:].astype(jnp.int32)
    cnt_ref[...] = carry_ref[...]
    for k, (first, pick) in enumerate(picks):
        idx_ref[k:k + 1, :] = first
        gate_ref[k:k + 1, :] = jnp.sum(jnp.where(pick, scores, 0.0), axis=0, keepdims=True) / total * ROUTED_SCALE
        pos_ref[k:k + 1, :] = jnp.sum(jnp.where(pick, rank, 0), axis=0, keepdims=True)


def moe_router(xt, router_w, router_bias):
    t, d = xt.shape
    tile = ROUTER_TILE
    tri = (jnp.arange(tile)[:, None] <= jnp.arange(tile)[None, :]).astype(BF16)
    row = pl.BlockSpec((TOP_K, tile), lambda i: (0, i))
    return pl.pallas_call(
        _router_kernel,
        grid=(t // tile,),
        in_specs=[pl.BlockSpec((tile, d), lambda i: (i, 0)), pl.BlockSpec((N_EXPERTS, d), lambda i: (0, 0)),
                  pl.BlockSpec((N_EXPERTS, 1), lambda i: (0, 0)), pl.BlockSpec((tile, tile), lambda i: (0, 0))],
        out_specs=[row, row, row, pl.BlockSpec((N_EXPERTS, 1), lambda i: (0, 0))],
        out_shape=[jax.ShapeDtypeStruct((TOP_K, t), jnp.int32), jax.ShapeDtypeStruct((TOP_K, t), F32),
                   jax.ShapeDtypeStruct((TOP_K, t), jnp.int32), jax.ShapeDtypeStruct((N_EXPERTS, 1), jnp.int32)],
        scratch_shapes=[pltpu.VMEM((N_EXPERTS, 1), jnp.int32)],
        compiler_params=pltpu.CompilerParams(dimension_semantics=("arbitrary",), vmem_limit_bytes=VMEM_LIMIT_BYTES),
        name="moe_router",
    )(xt, router_w.T.astype(BF16), router_bias.reshape(N_EXPERTS, 1), tri)


def _expert_kernel(blk_expert_ref, blk_rows_ref, x_ref, wgu_ref, wdn_ref, o_ref):
    n_valid = blk_rows_ref[pl.program_id(0)]

    @pl.when(n_valid > 0)
    def _():
        row = lax.broadcasted_iota(jnp.int32, x_ref.shape, 0)
        x = jnp.where(row < n_valid, x_ref[...], 0.0).astype(BF16)
        h = jnp.dot(x, wgu_ref[0], preferred_element_type=F32)
        hg, hu = h[:, :D_EXPERT], h[:, D_EXPERT:]
        act = (hg * jax.nn.sigmoid(hg) * hu).astype(BF16)
        o_ref[...] = jnp.dot(act, wdn_ref[0], preferred_element_type=F32)

    @pl.when(n_valid <= 0)
    def _():
        o_ref[...] = jnp.zeros_like(o_ref)


def expert_ffn(xs, blk_expert, blk_rows, w_gu, w_dn):
    n_pad, d = xs.shape
    n_blk = n_pad // MOE_BLOCK
    return pl.pallas_call(
        _expert_kernel,
        grid_spec=pltpu.PrefetchScalarGridSpec(
            num_scalar_prefetch=2,
            grid=(n_blk,),
            in_specs=[pl.BlockSpec((MOE_BLOCK, d), lambda b, be, br: (b, 0)),
                      pl.BlockSpec((1, d, 2 * D_EXPERT), lambda b, be, br: (be[b], 0, 0)),
                      pl.BlockSpec((1, D_EXPERT, d), lambda b, be, br: (be[b], 0, 0))],
            out_specs=pl.BlockSpec((MOE_BLOCK, d), lambda b, be, br: (b, 0))),
        out_shape=jax.ShapeDtypeStruct((n_pad, d), F32),
        compiler_params=pltpu.CompilerParams(dimension_semantics=("arbitrary",), vmem_limit_bytes=VMEM_LIMIT_BYTES),
        name="expert_ffn",
    )(blk_expert, blk_rows, xs, w_gu, w_dn)


DISPATCH_TILE = 256
COMBINE_TILE = 128


def _assignment_row(pad_start_ref, idx_ref, pos_ref, a):
    return pad_start_ref[idx_ref[0, 0, a]] + pos_ref[0, 0, a]


def _dispatch_kernel(pad_start_ref, idx_ref, pos_ref, x_ref, xs_hbm, sem):
    def issue(r, carry):
        for k in range(TOP_K):
            row = _assignment_row(pad_start_ref, idx_ref, pos_ref, r * TOP_K + k)
            pltpu.make_async_copy(x_ref.at[pl.ds(r, 1)], xs_hbm.at[pl.ds(row, 1)], sem).start()
        return carry

    lax.fori_loop(0, DISPATCH_TILE, issue, 0)
    n_rows = DISPATCH_TILE * TOP_K
    pltpu.make_async_copy(xs_hbm.at[pl.ds(0, n_rows)], xs_hbm.at[pl.ds(0, n_rows)], sem).wait()


def moe_dispatch(xt, pad_start, idx_tok_major, pos_tok_major, n_pad):
    t, d = xt.shape
    n_tiles = t // DISPATCH_TILE
    smem = pl.BlockSpec((1, 1, DISPATCH_TILE * TOP_K), lambda i, ps: (i, 0, 0), memory_space=pltpu.SMEM)
    return pl.pallas_call(
        _dispatch_kernel,
        grid_spec=pltpu.PrefetchScalarGridSpec(
            num_scalar_prefetch=1,
            grid=(n_tiles,),
            in_specs=[smem, smem, pl.BlockSpec((DISPATCH_TILE, d), lambda i, ps: (i, 0))],
            out_specs=pl.BlockSpec(memory_space=pl.ANY),
            scratch_shapes=[pltpu.SemaphoreType.DMA(())]),
        out_shape=jax.ShapeDtypeStruct((n_pad, d), xt.dtype),
        compiler_params=pltpu.CompilerParams(dimension_semantics=("arbitrary",), has_side_effects=True),
        name="moe_dispatch",
    )(pad_start, idx_tok_major.reshape(n_tiles, 1, -1), pos_tok_major.reshape(n_tiles, 1, -1), xt)


def _combine_kernel(pad_start_ref, idx_ref, pos_ref, idx_next_ref, pos_next_ref, ys_hbm, gate_ref, x_ref,
                    wgu_ref, wdn_ref, g_ref, b_ref, o_ref, buf_ref, sem):
    i = pl.program_id(0)
    n = pl.num_programs(0)
    slot = lax.rem(i, 2)

    def gather(i_ref, p_ref, into):
        def issue(r, carry):
            for k in range(TOP_K):
                row = _assignment_row(pad_start_ref, i_ref, p_ref, r * TOP_K + k)
                pltpu.make_async_copy(ys_hbm.at[pl.ds(row, 1)], buf_ref.at[into, k, pl.ds(r, 1)], sem.at[into]).start()
            return carry
        lax.fori_loop(0, COMBINE_TILE, issue, 0)

    @pl.when(i == 0)
    def _():
        gather(idx_ref, pos_ref, 0)

    @pl.when(i + 1 < n)
    def _():
        gather(idx_next_ref, pos_next_ref, 1 - slot)

    x = x_ref[...]
    h = jnp.dot(x.astype(BF16), wgu_ref[...], preferred_element_type=F32)
    hg, hu = h[:, :D_SHARED], h[:, D_SHARED:]
    acc = DEEPNORM_ALPHA * x + jnp.dot((hg * jax.nn.sigmoid(hg) * hu).astype(BF16), wdn_ref[...], preferred_element_type=F32)
    pltpu.make_async_copy(buf_ref.at[slot], buf_ref.at[slot], sem.at[slot]).wait()
    gate = gate_ref[...]
    for k in range(TOP_K):
        acc = acc + buf_ref[slot, k] * gate[:, k:k + 1]
    mu = jnp.mean(acc, -1, keepdims=True)
    zc = acc - mu
    var = jnp.mean(zc * zc, -1, keepdims=True)
    o_ref[...] = zc * lax.rsqrt(var + LN_EPS) * g_ref[...] + b_ref[...]


def moe_combine(ys, pad_start, idx_tok_major, pos_tok_major, gate_tok_major, xt, sh_w_gu, sh_w_down, ln_g, ln_b):
    t, d = xt.shape
    tm = COMBINE_TILE
    n_tiles = t // tm
    idx3, pos3 = idx_tok_major.reshape(n_tiles, 1, -1), pos_tok_major.reshape(n_tiles, 1, -1)
    row = pl.BlockSpec((tm, d), lambda i, ps: (i, 0))
    full = lambda shape: pl.BlockSpec(shape, lambda i, ps: (0, 0))
    this = pl.BlockSpec((1, 1, tm * TOP_K), lambda i, ps: (i, 0, 0), memory_space=pltpu.SMEM)
    nxt = pl.BlockSpec((1, 1, tm * TOP_K), lambda i, ps: (jnp.minimum(i + 1, n_tiles - 1), 0, 0), memory_space=pltpu.SMEM)
    return pl.pallas_call(
        _combine_kernel,
        grid_spec=pltpu.PrefetchScalarGridSpec(
            num_scalar_prefetch=1,
            grid=(n_tiles,),
            in_specs=[this, this, nxt, nxt, pl.BlockSpec(memory_space=pl.ANY),
                      pl.BlockSpec((tm, TOP_K), lambda i, ps: (i, 0)), row,
                      full((d, 2 * D_SHARED)), full((D_SHARED, d)), full((1, d)), full((1, d))],
            out_specs=row,
            scratch_shapes=[pltpu.VMEM((2, TOP_K, tm, d), F32), pltpu.SemaphoreType.DMA((2,))]),
        out_shape=jax.ShapeDtypeStruct((t, d), F32),
        compiler_params=pltpu.CompilerParams(dimension_semantics=("arbitrary",), vmem_limit_bytes=VMEM_LIMIT_BYTES),
        name="moe_combine",
    )(pad_start, idx3, pos3, idx3, pos3, ys, gate_tok_major, xt, sh_w_gu.astype(BF16), sh_w_down.astype(BF16),
      ln_g.reshape(1, d), ln_b.reshape(1, d))


def moe_ffn(xt, router_w, router_bias, exp_w_gu, exp_w_down, sh_w_gu, sh_w_down, ln_g, ln_b):
    n_tok, d = xt.shape
    idx, gate, pos, counts = moe_router(xt, router_w, router_bias)
    counts = counts[:, 0]
    padded = (counts + MOE_BLOCK - 1) // MOE_BLOCK * MOE_BLOCK
    pad_ends = jnp.cumsum(padded)
    pad_start = pad_ends - padded
    n_blk = n_tok * TOP_K // MOE_BLOCK + N_EXPERTS
    n_pad = n_blk * MOE_BLOCK
    idx_tm, pos_tm = idx.T.reshape(-1), pos.T.reshape(-1)
    blk_lo = jnp.arange(n_blk, dtype=jnp.int32) * MOE_BLOCK
    owner = (pad_start[None, :] <= blk_lo[:, None]) & (blk_lo[:, None] < pad_ends[None, :])
    blk_expert = jnp.minimum(jnp.sum(pad_ends[None, :] <= blk_lo[:, None], axis=1), N_EXPERTS - 1).astype(jnp.int32)
    blk_rows = jnp.clip(jnp.sum(jnp.where(owner, (pad_start + counts)[None, :] - blk_lo[:, None], 0), axis=1),
                        0, MOE_BLOCK).astype(jnp.int32)
    pad_start = pad_start.astype(jnp.int32)
    xs = moe_dispatch(xt, pad_start, idx_tm, pos_tm, n_pad)
    ys = expert_ffn(xs, blk_expert, blk_rows, exp_w_gu.astype(BF16), exp_w_down.astype(BF16))
    return moe_combine(ys, pad_start, idx_tm, pos_tm, gate.T, xt, sh_w_gu, sh_w_down, ln_g, ln_b)


def kernel(x, rel_bias, w_in, a_kv_norm, a_kv_up, b_conv, b_a_log, b_dt_bias, b_out_norm, c_lambda_re, c_lambda_im, c_log_dt, c_b_re, c_b_im, c_c_re, c_c_im, c_d, c_glu_w, c_glu_b, w_br_a, w_br_b, w_br_c, w_o, ln1_g, ln1_b, router_w, router_bias, exp_w_gu, exp_w_down, sh_w_gu, sh_w_down, ln2_g, ln2_b):
    bn, s, d = x.shape
    t = bn * s
    for i in range(w_in.shape[0]):
        xt = hybrid_mixer(x, rel_bias, w_in[i], a_kv_norm[i], a_kv_up[i], b_conv[i], b_a_log[i], b_dt_bias[i],
                          b_out_norm[i], c_lambda_re[i], c_lambda_im[i], c_log_dt[i], c_b_re[i], c_b_im[i],
                          c_c_re[i], c_c_im[i], c_d[i], c_glu_w[i], c_glu_b[i], w_br_a[i], w_br_b[i], w_br_c[i], w_o[i],
                          ln1_g[i], ln1_b[i])
        x = moe_ffn(xt, router_w[i], router_bias[i], exp_w_gu[i], exp_w_down[i], sh_w_gu[i], sh_w_down[i],
                    ln2_g[i], ln2_b[i]).reshape(bn, s, d)
    return x
```

```python
import functools
import math

import jax
import jax.numpy as jnp
import numpy as np
from jax import lax
from jax.experimental import pallas as pl
from jax.experimental.pallas import tpu as pltpu

F32 = jnp.float32
BF16 = jnp.bfloat16

D_MODEL = 1024
DEPTH = 4
A_HEADS = 8
A_HEAD_DIM = 64
A_WIDTH = A_HEADS * A_HEAD_DIM
A_KV_RANK = 128
IDX_HEADS = 8
IDX_DIM = 64
TOPK_MAX = 256
Q_BLOCK = 128
REL_BUCKETS = 32
REL_MAX_DIST = 128
B_HEADS = 4
B_HEAD_DIM = 128
B_WIDTH = B_HEADS * B_HEAD_DIM
CONV_WIDTH = 4
CHUNK = 64
C_WIDTH = 512
C_GROUP = 16
C_GROUPS = C_WIDTH // C_GROUP
C_STATE = 64
N_EXPERTS = 64
TOP_K = 8
D_EXPERT = 256
D_SHARED = 256
ROUTED_SCALE = 2.5
MOE_BLOCK = 512
SPLITS = (A_WIDTH, A_KV_RANK, IDX_HEADS * IDX_DIM, IDX_DIM, IDX_HEADS, 3 * B_WIDTH, B_WIDTH, B_HEADS, B_HEADS, C_WIDTH, 3 * D_MODEL)
DEEPNORM_ALPHA = (2 * DEPTH) ** 0.25
LN_EPS = 1e-5
RMS_EPS = 1e-6
NEG_INF = -1e30

VMEM_LIMIT_BYTES = 48 * 1024 * 1024
LANE = 128


def _pick_tile(n, candidates):
    for c in candidates:
        if n % c == 0:
            return c
    return n


def _mm_kernel(a_ref, b_ref, o_ref):
    o_ref[...] = jnp.dot(a_ref[...].astype(BF16), b_ref[...], preferred_element_type=F32).astype(o_ref.dtype)


def matmul(a, b, out_dtype=F32):
    m, k = a.shape
    _, n = b.shape
    tm = _pick_tile(m, (1024, 512, 256, 128, 8))
    tn = _pick_tile(n, (1024, 768, 512, 384, 256, 128))
    return pl.pallas_call(
        _mm_kernel,
        grid=(m // tm, n // tn),
        in_specs=[pl.BlockSpec((tm, k), lambda i, j: (i, 0)), pl.BlockSpec((k, tn), lambda i, j: (0, j))],
        out_specs=pl.BlockSpec((tm, tn), lambda i, j: (i, j)),
        out_shape=jax.ShapeDtypeStruct((m, n), out_dtype),
        compiler_params=pltpu.CompilerParams(dimension_semantics=("arbitrary", "arbitrary"), vmem_limit_bytes=VMEM_LIMIT_BYTES),
        name="matmul",
    )(a, b.astype(BF16))


def _res_ln_kernel(x_ref, y_ref, g_ref, b_ref, o_ref):
    z = DEEPNORM_ALPHA * x_ref[...] + y_ref[...]
    mu = jnp.mean(z, -1, keepdims=True)
    zc = z - mu
    var = jnp.mean(zc * zc, -1, keepdims=True)
    o_ref[...] = zc * lax.rsqrt(var + LN_EPS) * g_ref[...] + b_ref[...]


def residual_layer_norm(x, y, g, b):
    t, d = x.shape
    tm = _pick_tile(t, (512, 256, 128, 8))
    row = pl.BlockSpec((tm, d), lambda i: (i, 0))
    vec = pl.BlockSpec((1, d), lambda i: (0, 0))
    return pl.pallas_call(
        _res_ln_kernel,
        grid=(t // tm,),
        in_specs=[row, row, vec, vec],
        out_specs=row,
        out_shape=jax.ShapeDtypeStruct((t, d), F32),
        compiler_params=pltpu.CompilerParams(dimension_semantics=("arbitrary",), vmem_limit_bytes=VMEM_LIMIT_BYTES),
        name="residual_layer_norm",
    )(x, y, g.reshape(1, d), b.reshape(1, d))


def rms_norm(x, g):
    return x * lax.rsqrt(jnp.mean(x * x, -1, keepdims=True) + RMS_EPS) * g


def l2_normalize(x):
    return x * lax.rsqrt(jnp.sum(x * x, -1, keepdims=True) + RMS_EPS)


def rel_bucket(dist):
    n_exact = REL_BUCKETS // 2
    d = jnp.maximum(dist, 1).astype(F32)
    large = n_exact + (jnp.log(d / n_exact) / math.log(REL_MAX_DIST / n_exact) * (REL_BUCKETS - n_exact)).astype(jnp.int32)
    return jnp.where(dist < n_exact, dist, jnp.minimum(large, REL_BUCKETS - 1))


KEY_BLOCK = 128
DSA_CHUNK = 512
BLOCKS_PER_CHUNK = DSA_CHUNK // KEY_BLOCK
HEAD_PAIRS = A_HEADS // 2
PAIR_WIDTH = 2 * A_HEAD_DIM
INT_MIN = -2 ** 31
HALF_MASK = 0xFFFF
HALF_BIAS = 0x8000
PACK = 16
IDX_W_SCALE = IDX_HEADS ** -0.5 * IDX_DIM ** -0.5


def _kv_kernel(c_ref, g_ref, wk_ref, wvt_ref, k_ref, vt_ref):
    c = c_ref[...]
    cn = (c * lax.rsqrt(jnp.mean(c * c, -1, keepdims=True) + RMS_EPS) * g_ref[...]).astype(BF16)
    k_ref[...] = jnp.dot(cn, wk_ref[...], preferred_element_type=F32).astype(BF16)
    vt = lax.dot_general(wvt_ref[...], cn, (((1,), (1,)), ((), ())), preferred_element_type=F32).astype(BF16)
    for i in range(vt_ref.shape[1]):
        vt_ref[0, i] = vt[:, i * KEY_BLOCK:(i + 1) * KEY_BLOCK]


def kv_project(c_kv, col, g, w_kv_up, bn, s):
    ts = 512
    w = w_kv_up.astype(BF16)
    return pl.pallas_call(
        _kv_kernel,
        grid=(bn, s // ts),
        in_specs=[pl.BlockSpec((ts, A_KV_RANK), lambda b, i: (b * (s // ts) + i, col)),
                  pl.BlockSpec((1, A_KV_RANK), lambda b, i: (0, 0)),
                  pl.BlockSpec((A_KV_RANK, A_WIDTH), lambda b, i: (0, 0)),
                  pl.BlockSpec((A_WIDTH, A_KV_RANK), lambda b, i: (0, 0))],
        out_specs=[pl.BlockSpec((None, ts, A_WIDTH), lambda b, i: (b, i, 0)),
                   pl.BlockSpec((1, ts // KEY_BLOCK, A_WIDTH, KEY_BLOCK), lambda b, i: (b, i, 0, 0))],
        out_shape=[jax.ShapeDtypeStruct((bn, s, A_WIDTH), BF16),
                   jax.ShapeDtypeStruct((bn, s // KEY_BLOCK, A_WIDTH, KEY_BLOCK), BF16)],
        compiler_params=pltpu.CompilerParams(dimension_semantics=("arbitrary", "arbitrary"), vmem_limit_bytes=VMEM_LIMIT_BYTES),
        name="kv_project",
    )(c_kv, g.reshape(1, A_KV_RANK), w[:, :A_WIDTH], w[:, A_WIDTH:].T)


def _mm_nt_kernel(wt_ref, x_ref, o_ref):
    o_ref[0] = lax.dot_general(wt_ref[...], x_ref[...].astype(BF16), (((1,), (1,)), ((), ())),
                               preferred_element_type=F32).astype(o_ref.dtype)


def matmul_nt(wt, x, bn, s, out_dtype=BF16):
    n, k = wt.shape
    ts = 512
    return pl.pallas_call(
        _mm_nt_kernel,
        grid=(bn, s // ts),
        in_specs=[pl.BlockSpec((n, k), lambda b, i: (0, 0)),
                  pl.BlockSpec((ts, k), lambda b, i: (b * (s // ts) + i, 0))],
        out_specs=pl.BlockSpec((1, n, ts), lambda b, i: (b, 0, i)),
        out_shape=jax.ShapeDtypeStruct((bn, n, s), out_dtype),
        compiler_params=pltpu.CompilerParams(dimension_semantics=("arbitrary", "arbitrary"), vmem_limit_bytes=VMEM_LIMIT_BYTES),
        name="matmul_nt",
    )(wt.astype(BF16), x)


def _sortable_key(score):
    bits = pltpu.bitcast(score, jnp.int32)
    return bits ^ (lax.shift_right_arithmetic(bits, 31) & jnp.int32(0x7FFFFFFF))


def _dsa_kernel(qt_ref, iqt_ref, wt_ref, kidx_ref, k_ref, vt_ref, bias_ref, o_ref,
                skey_ref, hi_ref, lo_ref, sel_ref, widx_ref, wq_ref, acc_ref, m_ref, l_ref, *, k_sel):
    j = pl.program_id(1)
    n_chunks = (j + BLOCKS_PER_CHUNK) // BLOCKS_PER_CHUNK
    q_pos = j * Q_BLOCK + lax.broadcasted_iota(jnp.int32, (1, Q_BLOCK), 1)

    for h in range(IDX_HEADS):
        widx_ref[:, h * Q_BLOCK:(h + 1) * Q_BLOCK] = iqt_ref[0, h * IDX_DIM:(h + 1) * IDX_DIM, :]
    wq_ref[...] = jnp.zeros_like(wq_ref)
    for p in range(HEAD_PAIRS):
        wq_ref[p, 0:A_HEAD_DIM, 0:Q_BLOCK] = qt_ref[0, (2 * p) * A_HEAD_DIM:(2 * p + 1) * A_HEAD_DIM, :]
        wq_ref[p, A_HEAD_DIM:PAIR_WIDTH, Q_BLOCK:2 * Q_BLOCK] = qt_ref[0, (2 * p + 1) * A_HEAD_DIM:(2 * p + 2) * A_HEAD_DIM, :]

    w_rows = [wt_ref[0, h:h + 1, :] * IDX_W_SCALE for h in range(IDX_HEADS)]

    def idx_chunk(c, carry):
        start = pl.multiple_of(c * DSA_CHUNK, DSA_CHUNK)
        kc = kidx_ref[0, pl.ds(start, DSA_CHUNK), :]
        score = jnp.zeros((DSA_CHUNK, Q_BLOCK), F32)
        for p in range(IDX_HEADS // 2):
            z = jnp.dot(kc, widx_ref[:, p * 2 * Q_BLOCK:(p + 1) * 2 * Q_BLOCK], preferred_element_type=F32)
            score = score + jnp.maximum(z[:, :Q_BLOCK], 0.0) * w_rows[2 * p]
            score = score + jnp.maximum(z[:, Q_BLOCK:], 0.0) * w_rows[2 * p + 1]
        pos = start + lax.broadcasted_iota(jnp.int32, (DSA_CHUNK, Q_BLOCK), 0)
        key = jnp.where(pos <= q_pos, _sortable_key(score), INT_MIN)
        skey_ref[pl.ds(start, DSA_CHUNK), :] = key
        hi_ref[pl.ds(start, DSA_CHUNK), :] = lax.shift_right_arithmetic(key, 16).astype(jnp.int16)
        lo_ref[pl.ds(start, DSA_CHUNK), :] = ((key & HALF_MASK) - HALF_BIAS).astype(jnp.int16)
        return carry

    lax.fori_loop(0, n_chunks, idx_chunk, 0)

    def count_ge(cand):
        def body(c, cnt):
            start = pl.multiple_of(c * DSA_CHUNK, DSA_CHUNK)
            hit = (skey_ref[pl.ds(start, DSA_CHUNK), :] >= cand).astype(jnp.int32)
            return cnt + jnp.sum(hit.reshape(DSA_CHUNK // 8, 8, Q_BLOCK), axis=0)
        cnt = lax.fori_loop(0, n_chunks, body, jnp.zeros((8, Q_BLOCK), jnp.int32))
        return jnp.sum(cnt, axis=0, keepdims=True)

    def packed(value):
        return jnp.broadcast_to(value, (PACK, Q_BLOCK)).astype(jnp.int16)

    def count16(ref, test):
        def body(c, cnt):
            start = pl.multiple_of(c * DSA_CHUNK, DSA_CHUNK)
            hit = jnp.where(test(ref[pl.ds(start, DSA_CHUNK), :].reshape(DSA_CHUNK // PACK, PACK, Q_BLOCK)),
                            jnp.int16(1), jnp.int16(0))
            for r in range(DSA_CHUNK // PACK):
                cnt = cnt + hit[r]
            return cnt
        cnt = lax.fori_loop(0, n_chunks, body, jnp.zeros((PACK, Q_BLOCK), jnp.int16))
        return jnp.sum(cnt.astype(jnp.int32), axis=0, keepdims=True)

    def bisect16(ref, want):
        def bit_step(i, biased):
            cand_biased = biased | lax.shift_left(jnp.int32(1), 15 - i)
            cand = packed(cand_biased - HALF_BIAS)[None]
            return jnp.where(count16(ref, lambda v: v >= cand) >= want, cand_biased, biased)
        return lax.fori_loop(0, 16, bit_step, jnp.zeros((1, Q_BLOCK), jnp.int32)) - HALF_BIAS

    thr_hi = bisect16(hi_ref, k_sel)
    thr_hi16 = packed(thr_hi)[None]
    above = count16(hi_ref, lambda v: v > thr_hi16)

    def restrict_low(c, carry):
        start = pl.multiple_of(c * DSA_CHUNK, DSA_CHUNK)
        hi = hi_ref[pl.ds(start, DSA_CHUNK), :].reshape(DSA_CHUNK // PACK, PACK, Q_BLOCK)
        lo = lo_ref[pl.ds(start, DSA_CHUNK), :].reshape(DSA_CHUNK // PACK, PACK, Q_BLOCK)
        sel_ref[pl.ds(start, DSA_CHUNK), :] = jnp.where(hi == thr_hi16, lo, jnp.int16(-HALF_BIAS)).reshape(DSA_CHUNK, Q_BLOCK)
        return carry

    lax.fori_loop(0, n_chunks, restrict_low, 0)
    thr_lo = bisect16(sel_ref, k_sel - above)
    thr = lax.shift_left(thr_hi, 16) | (thr_lo + HALF_BIAS)

    n_gt = count_ge(thr + 1)
    n_ge = count_ge(thr)
    tied = (n_ge > k_sel) & (thr > INT_MIN)

    @pl.when(jnp.max(tied.astype(jnp.int32)) > 0)
    def _():
        need = k_sel - n_gt

        def count_tied_upto(limit):
            def body(c, cnt):
                start = pl.multiple_of(c * DSA_CHUNK, DSA_CHUNK)
                pos = start + lax.broadcasted_iota(jnp.int32, (DSA_CHUNK, Q_BLOCK), 0)
                hit = jnp.where(skey_ref[pl.ds(start, DSA_CHUNK), :] == thr, (pos <= limit).astype(jnp.int32), 0)
                return cnt + jnp.sum(hit.reshape(DSA_CHUNK // 8, 8, Q_BLOCK), axis=0)
            cnt = lax.fori_loop(0, n_chunks, body, jnp.zeros((8, Q_BLOCK), jnp.int32))
            return jnp.sum(cnt, axis=0, keepdims=True)

        def pos_step(i, lim):
            cand = lim - lax.shift_left(jnp.int32(1), 30 - i)
            return jnp.where(count_tied_upto(cand) >= need, cand, lim)

        limit = lax.fori_loop(0, 31, pos_step, jnp.full((1, Q_BLOCK), 2 ** 31 - 1, jnp.int32))

        def demote(c, carry):
            start = pl.multiple_of(c * DSA_CHUNK, DSA_CHUNK)
            pos = start + lax.broadcasted_iota(jnp.int32, (DSA_CHUNK, Q_BLOCK), 0)
            sk = skey_ref[pl.ds(start, DSA_CHUNK), :]
            skey_ref[pl.ds(start, DSA_CHUNK), :] = jnp.where(tied & (sk == thr) & (pos > limit), thr - 1, sk)
            return carry

        lax.fori_loop(0, n_chunks, demote, 0)

    m_ref[...] = jnp.full_like(m_ref, NEG_INF)
    l_ref[...] = jnp.zeros_like(l_ref)
    acc_ref[...] = jnp.zeros_like(acc_ref)

    def attend(s_list, vt_of, n_blocks):
        for p in range(HEAD_PAIRS):
            s = s_list[p]
            m_old = m_ref[p]
            m_new = jnp.maximum(m_old, jnp.max(s, axis=0, keepdims=True))
            alpha = jnp.exp(m_old - m_new)
            pexp = jnp.exp(s - m_new)
            l_ref[p] = l_ref[p] * alpha + jnp.sum(pexp, axis=0, keepdims=True)
            pb = pexp.astype(BF16)
            pv = jnp.dot(vt_of(p, 0), pb[0:KEY_BLOCK], preferred_element_type=F32)
            for i in range(1, n_blocks):
                pv = pv + jnp.dot(vt_of(p, i), pb[i * KEY_BLOCK:(i + 1) * KEY_BLOCK], preferred_element_type=F32)
            acc_ref[p] = acc_ref[p] * alpha + pv
            m_ref[p] = m_new

    far_end = (j - 1) * KEY_BLOCK
    n_far = (jnp.maximum(j - 1, 0) + BLOCKS_PER_CHUNK - 1) // BLOCKS_PER_CHUNK

    def far_chunk(c, carry):
        start = pl.multiple_of(c * DSA_CHUNK, DSA_CHUNK)
        pos = start + lax.broadcasted_iota(jnp.int32, (DSA_CHUNK, Q_BLOCK), 0)
        sel = (skey_ref[pl.ds(start, DSA_CHUNK), :] >= thr) & (pos < far_end)
        madd = jnp.where(sel, 0.0, NEG_INF)
        madd2 = jnp.concatenate([madd, madd], axis=1)
        kc = k_ref[0, pl.ds(start, DSA_CHUNK), :]
        s_list = [jnp.dot(kc[:, p * PAIR_WIDTH:(p + 1) * PAIR_WIDTH], wq_ref[p], preferred_element_type=F32) + madd2
                  for p in range(HEAD_PAIRS)]
        blk0 = c * BLOCKS_PER_CHUNK
        attend(s_list, lambda p, i: vt_ref[0, blk0 + i, p * PAIR_WIDTH:(p + 1) * PAIR_WIDTH, :], BLOCKS_PER_CHUNK)
        return carry

    lax.fori_loop(0, n_far, far_chunk, 0)

    def near_block(kb, delta):
        start = pl.multiple_of(kb * KEY_BLOCK, KEY_BLOCK)
        pos = start + lax.broadcasted_iota(jnp.int32, (KEY_BLOCK, Q_BLOCK), 0)
        sel = (skey_ref[pl.ds(start, KEY_BLOCK), :] >= thr) & (pos <= q_pos)
        madd = jnp.where(sel, 0.0, NEG_INF)
        madd2 = jnp.concatenate([madd, madd], axis=1)
        kc = k_ref[0, pl.ds(start, KEY_BLOCK), :]
        s_list = [jnp.dot(kc[:, p * PAIR_WIDTH:(p + 1) * PAIR_WIDTH], wq_ref[p], preferred_element_type=F32)
                  + bias_ref[delta, p] + madd2 for p in range(HEAD_PAIRS)]
        attend(s_list, lambda p, i: vt_ref[0, kb, p * PAIR_WIDTH:(p + 1) * PAIR_WIDTH, :], 1)

    @pl.when(j > 0)
    def _():
        near_block(j - 1, 1)

    near_block(j, 0)

    outs = []
    for p in range(HEAD_PAIRS):
        o = acc_ref[p] / l_ref[p]
        outs.append(o[0:A_HEAD_DIM, 0:Q_BLOCK])
        outs.append(o[A_HEAD_DIM:PAIR_WIDTH, Q_BLOCK:2 * Q_BLOCK])
    o_ref[0] = jnp.concatenate(outs, axis=0).T.astype(o_ref.dtype)


def _rel_bias_tiles(rel_bias):
    tab = rel_bias[rel_bucket(jnp.arange(2 * Q_BLOCK, dtype=jnp.int32))] - rel_bias[REL_BUCKETS - 1]
    key = jnp.arange(KEY_BLOCK)[:, None]
    qry = jnp.arange(Q_BLOCK)[None, :]
    tiles = []
    for delta in (0, 1):
        dist = jnp.maximum(delta * Q_BLOCK + qry - key, 0)
        t = jnp.moveaxis(tab[dist], 2, 0)
        t = t.reshape(HEAD_PAIRS, 2, KEY_BLOCK, Q_BLOCK)
        tiles.append(jnp.concatenate([t[:, 0], t[:, 1]], axis=-1))
    return jnp.stack(tiles)


def dsa_attention(qiq_t, wt, k_idx, k, vt, bias_tiles, bn, s):
    assert A_WIDTH == IDX_HEADS * IDX_DIM
    scal_rows = wt.shape[1]
    assert scal_rows % IDX_HEADS == 0
    k_sel = min(TOPK_MAX, s // 4)
    nb = s // Q_BLOCK
    s_pad = s + DSA_CHUNK
    return pl.pallas_call(
        functools.partial(_dsa_kernel, k_sel=k_sel),
        grid=(bn, nb),
        in_specs=[pl.BlockSpec((1, A_WIDTH, Q_BLOCK), lambda b, j: (b, 0, j)),
                  pl.BlockSpec((1, IDX_HEADS * IDX_DIM, Q_BLOCK), lambda b, j: (b, 1, j)),
                  pl.BlockSpec((1, IDX_HEADS, Q_BLOCK), lambda b, j: (b, scal_rows // IDX_HEADS - 1, j)),
                  pl.BlockSpec((1, s, IDX_DIM), lambda b, j: (b, 0, 0)),
                  pl.BlockSpec((1, s, A_WIDTH), lambda b, j: (b, 0, 0)),
                  pl.BlockSpec((1, s // KEY_BLOCK, A_WIDTH, KEY_BLOCK), lambda b, j: (b, 0, 0, 0)),
                  pl.BlockSpec((2, HEAD_PAIRS, KEY_BLOCK, 2 * Q_BLOCK), lambda b, j: (0, 0, 0, 0))],
        out_specs=pl.BlockSpec((1, Q_BLOCK, A_WIDTH), lambda b, j: (b, j, 0)),
        out_shape=jax.ShapeDtypeStruct((bn, s, A_WIDTH), BF16),
        scratch_shapes=[pltpu.VMEM((s_pad, Q_BLOCK), jnp.int32),
                        pltpu.VMEM((s_pad, Q_BLOCK), jnp.int16), pltpu.VMEM((s_pad, Q_BLOCK), jnp.int16),
                        pltpu.VMEM((s_pad, Q_BLOCK), jnp.int16),
                        pltpu.VMEM((IDX_DIM, IDX_HEADS * Q_BLOCK), BF16),
                        pltpu.VMEM((HEAD_PAIRS, PAIR_WIDTH, 2 * Q_BLOCK), BF16),
                        pltpu.VMEM((HEAD_PAIRS, PAIR_WIDTH, 2 * Q_BLOCK), F32),
                        pltpu.VMEM((HEAD_PAIRS, 1, 2 * Q_BLOCK), F32),
                        pltpu.VMEM((HEAD_PAIRS, 1, 2 * Q_BLOCK), F32)],
        compiler_params=pltpu.CompilerParams(dimension_semantics=("arbitrary", "arbitrary"), vmem_limit_bytes=60 * 1024 * 1024),
        name="dsa_attention",
    )(qiq_t, qiq_t, wt, k_idx, k, vt, bias_tiles)


GDN_CHUNK = B_HEAD_DIM
GDN_BLOCK = 512
GDN_HALO = 8
NEUMANN_STEPS = 6


def _split_bf16(x):
    hi = x.astype(BF16)
    return hi, (x - hi.astype(F32)).astype(BF16)


def _dot_f32(a, b):
    a_hi, a_lo = _split_bf16(a)
    b_hi, b_lo = _split_bf16(b)
    d = lambda x, y: jnp.dot(x, y, preferred_element_type=F32)
    return d(a_hi, b_hi) + (d(a_hi, b_lo) + d(a_lo, b_hi))


def _bdot(a, b):
    return jnp.dot(a.astype(BF16), b.astype(BF16), preferred_element_type=F32)


def _bdot_nt(a, b):
    return lax.dot_general(a.astype(BF16), b.astype(BF16), (((1,), (1,)), ((), ())), preferred_element_type=F32)


def _bdot_tn(a, b):
    return lax.dot_general(a.astype(BF16), b.astype(BF16), (((0,), (0,)), ((), ())), preferred_element_type=F32)


def _softplus(x):
    return jnp.maximum(x, 0.0) + jnp.log(1.0 + jnp.exp(-jnp.abs(x)))


def _gdn_kernel(hq_ref, hk_ref, hv_ref, hz_ref, wq_ref, wk_ref, wv_ref, ab_ref, const_ref, gout_ref, tri_ref,
                o_ref, state_ref, xbuf_ref):
    head = pl.program_id(1)

    @pl.when(pl.program_id(2) == 0)
    def _():
        state_ref[...] = jnp.zeros_like(state_ref)
        xbuf_ref[:, 0:GDN_HALO, :] = jnp.zeros((3, GDN_HALO, B_HEAD_DIM), F32)

    def conv_silu(n, x_ref, w_ref):
        x = x_ref[...]
        xbuf_ref[n, GDN_HALO:, :] = x
        acc = xbuf_ref[n, GDN_HALO - CONV_WIDTH + 1:GDN_HALO - CONV_WIDTH + 1 + GDN_BLOCK, :] * w_ref[0:1, :]
        for j in range(1, CONV_WIDTH):
            lo = GDN_HALO - CONV_WIDTH + 1 + j
            acc = acc + xbuf_ref[n, lo:lo + GDN_BLOCK, :] * w_ref[j:j + 1, :]
        xbuf_ref[n, 0:GDN_HALO, :] = x[GDN_BLOCK - GDN_HALO:, :]
        return acc * jax.nn.sigmoid(acc)

    q = conv_silu(0, hq_ref, wq_ref)
    k = conv_silu(1, hk_ref, wk_ref)
    v = conv_silu(2, hv_ref, wv_ref)
    q = q * lax.rsqrt(jnp.sum(q * q, -1, keepdims=True) + RMS_EPS) * B_HEAD_DIM ** -0.5
    k = k * lax.rsqrt(jnp.sum(k * k, -1, keepdims=True) + RMS_EPS)

    a_row = ab_ref[0, pl.ds(head, 1), :]
    b_row = ab_ref[0, pl.ds(B_HEADS + head, 1), :]
    beta_row = jax.nn.sigmoid(b_row)
    g_row = const_ref[0, 0:1, :] * _softplus(a_row + const_ref[0, 1:2, :])
    gc_rows = _dot_f32(jnp.broadcast_to(g_row, (8, GDN_BLOCK)), tri_ref[...])[0:1, :]

    row_i = lax.broadcasted_iota(jnp.int32, (GDN_CHUNK, GDN_CHUNK), 0)
    col_j = lax.broadcasted_iota(jnp.int32, (GDN_CHUNK, GDN_CHUNK), 1)
    eye = jnp.where(row_i == col_j, 1.0, 0.0)
    z_all = hz_ref[...]
    for c in range(GDN_BLOCK // GDN_CHUNK):
        sl = slice(c * GDN_CHUNK, (c + 1) * GDN_CHUNK)
        qc, kc, vc = q[sl], k[sl], v[sl]
        g_lane = jnp.broadcast_to(gc_rows[:, sl], (GDN_CHUNK, GDN_CHUNK))
        g_sub = g_lane.T
        beta_sub = jnp.broadcast_to(beta_row[:, sl], (GDN_CHUNK, GDN_CHUNK)).T
        g_end = jnp.broadcast_to(g_lane[:, GDN_CHUNK - 1:GDN_CHUNK], (GDN_CHUNK, GDN_CHUNK))
        decay = jnp.exp(jnp.where(row_i >= col_j, g_sub - g_lane, -jnp.inf))
        kb = kc * beta_sub
        m = jnp.where(row_i > col_j, _bdot_nt(kb, kc) * decay, 0.0)
        x = -m
        t_inv = eye + x
        for _ in range(NEUMANN_STEPS):
            x = _dot_f32(x, x)
            t_inv = t_inv + _bdot(t_inv, x)
        g_exp = jnp.exp(g_sub)
        u = _bdot(t_inv, vc * beta_sub)
        w = _bdot(t_inv, kb * g_exp)
        qk = _bdot_nt(qc, kc) * decay
        state = state_ref[...]
        v_new = u - _bdot(w, state)
        o = _bdot(qc * g_exp, state) + _bdot(qk, v_new)
        state_ref[...] = state * jnp.exp(g_end) + _bdot_tn(kc * jnp.exp(g_end - g_sub), v_new)
        o = o * lax.rsqrt(jnp.mean(o * o, -1, keepdims=True) + RMS_EPS) * gout_ref[...]
        zc = z_all[sl]
        o_ref[sl, :] = (o * (zc * jax.nn.sigmoid(zc))).astype(o_ref.dtype)


def gated_deltanet(h, col0, ab_t, conv_w, a_log, dt_bias, out_norm_g, bn, s):
    nblk = s // GDN_BLOCK
    tok = jnp.arange(GDN_BLOCK)
    tri = ((tok[:, None] // GDN_CHUNK == tok[None, :] // GDN_CHUNK) & (tok[:, None] <= tok[None, :])).astype(F32)
    const = jnp.stack([jnp.broadcast_to(-jnp.exp(a_log)[:, None], (B_HEADS, GDN_BLOCK)),
                       jnp.broadcast_to(dt_bias[:, None], (B_HEADS, GDN_BLOCK))], axis=1)
    const = jnp.pad(const, ((0, 0), (0, 6), (0, 0)))
    hcol = lambda part: pl.BlockSpec((GDN_BLOCK, B_HEAD_DIM), lambda b, hh, i: (b * nblk + i, col0 + part * B_HEADS + hh))
    wcol = lambda part: pl.BlockSpec((CONV_WIDTH, B_HEAD_DIM), lambda b, hh, i: (0, part * B_HEADS + hh))
    return pl.pallas_call(
        _gdn_kernel,
        grid=(bn, B_HEADS, nblk),
        in_specs=[hcol(0), hcol(1), hcol(2), hcol(3), wcol(0), wcol(1), wcol(2),
                  pl.BlockSpec((1, 16, GDN_BLOCK), lambda b, hh, i: (b, 0, i)),
                  pl.BlockSpec((1, 8, GDN_BLOCK), lambda b, hh, i: (hh, 0, 0)),
                  pl.BlockSpec((1, B_HEAD_DIM), lambda b, hh, i: (0, 0)),
                  pl.BlockSpec((GDN_BLOCK, GDN_BLOCK), lambda b, hh, i: (0, 0))],
        out_specs=pl.BlockSpec((GDN_BLOCK, B_HEAD_DIM), lambda b, hh, i: (b * nblk + i, hh)),
        out_shape=jax.ShapeDtypeStruct((bn * s, B_WIDTH), BF16),
        scratch_shapes=[pltpu.VMEM((B_HEAD_DIM, B_HEAD_DIM), F32),
                        pltpu.VMEM((3, GDN_HALO + GDN_BLOCK, B_HEAD_DIM), F32)],
        compiler_params=pltpu.CompilerParams(dimension_semantics=("arbitrary", "arbitrary", "arbitrary"), vmem_limit_bytes=VMEM_LIMIT_BYTES),
        name="gated_deltanet",
    )(h, h, h, h, conv_w, conv_w, conv_w, ab_t, const, out_norm_g.reshape(1, B_HEAD_DIM), tri)


S5_CHUNK = 32
S5_ROW = S5_CHUNK * C_GROUP
STATE2 = 2 * C_STATE


def _s5_tables(lam_re, lam_im, log_dt, b_re, b_im, c_re, c_im, d_skip):
    hp = lax.Precision.HIGHEST
    ell = S5_CHUNK
    dt = jnp.exp(log_dt)[:, None]
    lr, li = lam_re, lam_im
    mag = jnp.exp(lr * dt)
    a_re, a_im = mag * jnp.cos(li * dt), mag * jnp.sin(li * dt)
    den = lr * lr + li * li
    f_re = ((a_re - 1.0) * lr + a_im * li) / den
    f_im = (a_im * lr - (a_re - 1.0) * li) / den
    bb_re = f_re[..., None] * b_re - f_im[..., None] * b_im
    bb_im = f_re[..., None] * b_im + f_im[..., None] * b_re
    tau = jnp.arange(ell + 1, dtype=F32)[:, None, None]
    pmag, ang = jnp.exp(tau * (lr * dt)), tau * (li * dt)
    p_re, p_im = pmag * jnp.cos(ang), pmag * jnp.sin(ang)
    ab_re = p_re[:ell, ..., None] * bb_re - p_im[:ell, ..., None] * bb_im
    ab_im = p_re[:ell, ..., None] * bb_im + p_im[:ell, ..., None] * bb_re
    kern = (jnp.einsum('gip,tgpj->tgij', c_re, ab_re, precision=hp)
            - jnp.einsum('gip,tgpj->tgij', c_im, ab_im, precision=hp))
    lag = jnp.arange(ell)[None, :] - jnp.arange(ell)[:, None]
    toep = jnp.where((lag >= 0)[:, :, None, None, None], kern[jnp.maximum(lag, 0)], 0.0)
    toep = jnp.transpose(toep, (2, 0, 4, 1, 3)).reshape(C_GROUPS, S5_ROW, S5_ROW)
    w_re = jnp.transpose(ab_re[::-1], (1, 0, 3, 2)).reshape(C_GROUPS, S5_ROW, C_STATE)
    w_im = jnp.transpose(ab_im[::-1], (1, 0, 3, 2)).reshape(C_GROUPS, S5_ROW, C_STATE)
    tw = jnp.concatenate([toep, w_re, w_im, w_im, w_re], axis=-1).astype(BF16)
    pt_re = jnp.swapaxes(p_re[1:], 0, 1)[:, :, None, :]
    pt_im = jnp.swapaxes(p_im[1:], 0, 1)[:, :, None, :]
    ca_re = c_re[:, None] * pt_re - c_im[:, None] * pt_im
    ca_im = c_re[:, None] * pt_im + c_im[:, None] * pt_re
    v_re = jnp.transpose(ca_re, (0, 3, 1, 2)).reshape(C_GROUPS, C_STATE, S5_ROW)
    v_im = jnp.transpose(-ca_im, (0, 3, 1, 2)).reshape(C_GROUPS, C_STATE, S5_ROW)
    v = jnp.concatenate([v_re, v_im], axis=1).astype(BF16)
    al_re, al_im = p_re[ell], p_im[ell]
    coef = jnp.stack([jnp.concatenate([al_re, al_re], -1), jnp.concatenate([-al_im, al_im], -1),
                      jnp.concatenate([al_im, -al_im], -1)], axis=1)
    d_flat = jnp.tile(d_skip, (1, ell)).reshape(C_GROUPS, 1, S5_ROW)
    return tw, v, coef, d_flat


def _s5_kernel(u_ref, tw_ref, v_ref, coef_ref, d_ref, o_ref, sin_ref, xin_ref, *, bn):
    rows = u_ref.shape[1]
    n_chunks = rows // bn
    u = u_ref[0]
    tw = jnp.dot(u.astype(BF16), tw_ref[0], preferred_element_type=F32)
    o_ref[0] = tw[:, :S5_ROW] + u * d_ref[0]
    sin_ref[...] = tw[:, S5_ROW:]
    a1, a2, a2s = coef_ref[0, 0:1, :], coef_ref[0, 1:2, :], coef_ref[0, 2:3, :]

    def step(c, carry):
        new = []
        for b in range(bn):
            x, xs = carry[2 * b], carry[2 * b + 1]
            row = b * n_chunks + c
            xin_ref[pl.ds(row, 1), :] = x
            s_row = sin_ref[pl.ds(row, 1), :]
            new.append(a1 * x + a2 * xs + s_row[:, :STATE2])
            new.append(a1 * xs + a2s * x + s_row[:, STATE2:])
        return tuple(new)

    zero = jnp.zeros((1, STATE2), F32)
    lax.fori_loop(0, n_chunks, step, (zero,) * (2 * bn))
    y = o_ref[0] + jnp.dot(xin_ref[...].astype(BF16), v_ref[0], preferred_element_type=F32)
    o_ref[0] = jax.nn.gelu(y)


def _glu_kernel(y_ref, w_ref, b_ref, o_ref):
    y = y_ref[...]
    gate = jnp.dot(y.astype(BF16), w_ref[...], preferred_element_type=F32) + b_ref[...]
    o_ref[...] = (y * jax.nn.sigmoid(gate)).astype(o_ref.dtype)


def s5_branch(u, lam_re, lam_im, log_dt, b_re, b_im, c_re, c_im, d_skip, glu_w, glu_b):
    bn, s, _ = u.shape
    n_chunks = s // S5_CHUNK
    rows = bn * n_chunks
    tw, v, coef, d_flat = _s5_tables(lam_re, lam_im, log_dt, b_re, b_im, c_re, c_im, d_skip)
    uf = jnp.transpose(u.reshape(bn, n_chunks, S5_CHUNK, C_GROUPS, C_GROUP), (3, 0, 1, 2, 4)).reshape(C_GROUPS, rows, S5_ROW)
    per_group = lambda shape: pl.BlockSpec((1,) + shape, lambda g: (g, 0, 0))
    y = pl.pallas_call(
        functools.partial(_s5_kernel, bn=bn),
        grid=(C_GROUPS,),
        in_specs=[per_group((rows, S5_ROW)), per_group((S5_ROW, S5_ROW + 2 * STATE2)), per_group((STATE2, S5_ROW)),
                  per_group((3, STATE2)), per_group((1, S5_ROW))],
        out_specs=per_group((rows, S5_ROW)),
        out_shape=jax.ShapeDtypeStruct((C_GROUPS, rows, S5_ROW), F32),
        scratch_shapes=[pltpu.VMEM((rows, 2 * STATE2), F32), pltpu.VMEM((rows, STATE2), F32)],
        compiler_params=pltpu.CompilerParams(dimension_semantics=("arbitrary",), vmem_limit_bytes=VMEM_LIMIT_BYTES),
        name="s5_scan",
    )(uf, tw, v, coef, d_flat)
    y = jnp.transpose(y.reshape(C_GROUPS, bn, n_chunks, S5_CHUNK, C_GROUP), (1, 2, 3, 0, 4)).reshape(bn * s, C_WIDTH)
    t = bn * s
    tm = _pick_tile(t, (1024, 512, 256, 128, 8))
    return pl.pallas_call(
        _glu_kernel,
        grid=(t // tm,),
        in_specs=[pl.BlockSpec((tm, C_WIDTH), lambda i: (i, 0)), pl.BlockSpec((C_WIDTH, C_WIDTH), lambda i: (0, 0)),
                  pl.BlockSpec((1, C_WIDTH), lambda i: (0, 0))],
        out_specs=pl.BlockSpec((tm, C_WIDTH), lambda i: (i, 0)),
        out_shape=jax.ShapeDtypeStruct((t, C_WIDTH), BF16),
        compiler_params=pltpu.CompilerParams(dimension_semantics=("arbitrary",), vmem_limit_bytes=VMEM_LIMIT_BYTES),
        name="s5_glu",
    )(y, glu_w.astype(BF16), glu_b.reshape(1, C_WIDTH))


def hybrid_mixer(x, rel_bias, w_in, a_kv_norm, a_kv_up, b_conv, b_a_log, b_dt_bias, b_out_norm,
                 c_lambda_re, c_lambda_im, c_log_dt, c_b_re, c_b_im, c_c_re, c_c_im, c_d, c_glu_w, c_glu_b,
                 w_br_a, w_br_b, w_br_c, w_o, ln_g, ln_b):
    bn, s, d = x.shape
    t = bn * s
    xt = x.reshape(t, d)
    offsets = [0] + np.cumsum(SPLITS).tolist()
    w_cols = [w_in[:, offsets[i]:offsets[i + 1]] for i in range(len(SPLITS))]
    (w_aq, w_ckv, w_iq, w_ik, w_iw, w_bqkv, w_bz, w_ba, w_bb, w_cu, w_gate) = w_cols
    w_ik_pad = jnp.pad(w_ik, ((0, 0), (0, LANE - IDX_DIM)))
    h = matmul(xt, jnp.concatenate([w_gate, w_ckv, w_ik_pad, w_bqkv, w_bz, w_cu], axis=1))
    o = np.cumsum([0, 3 * D_MODEL, A_KV_RANK, LANE, 3 * B_WIDTH, B_WIDTH, C_WIDTH]).tolist()
    idx_k = h[:, o[2]:o[2] + IDX_DIM].astype(BF16).reshape(bn, s, IDX_DIM)
    c_u = h[:, o[5]:o[6]].reshape(bn, s, C_WIDTH)
    qiq_t = matmul_nt(jnp.concatenate([w_aq * A_HEAD_DIM ** -0.5, w_iq], axis=1).T, xt, bn, s)
    scal_t = matmul_nt(jnp.concatenate([w_ba, w_bb, w_iw], axis=1).T, xt, bn, s, out_dtype=F32)
    k, vt = kv_project(h, o[1] // LANE, a_kv_norm, a_kv_up, bn, s)
    ya = dsa_attention(qiq_t, scal_t, idx_k, k, vt, _rel_bias_tiles(rel_bias), bn, s)
    yb = gated_deltanet(h, o[3] // LANE, scal_t, b_conv, b_a_log, b_dt_bias, b_out_norm, bn, s)
    yc = s5_branch(c_u, c_lambda_re, c_lambda_im, c_log_dt, c_b_re, c_b_im, c_c_re, c_c_im, c_d, c_glu_w, c_glu_b)
    return merge_branches(ya.reshape(t, A_WIDTH), yb, yc, h, w_br_a, w_br_b, w_br_c, w_o, xt, ln_g, ln_b)


def _merge_kernel(ya_ref, yb_ref, yc_ref, ga_ref, gb_ref, gc_ref, wa_ref, wb_ref, wc_ref, wo_ref, x_ref, g_ref, b_ref, o_ref):
    branch = lambda y_ref, w_ref, gate_ref: jax.nn.sigmoid(gate_ref[...]) * jnp.dot(y_ref[...], w_ref[...], preferred_element_type=F32)
    merged = branch(ya_ref, wa_ref, ga_ref) + branch(yb_ref, wb_ref, gb_ref) + branch(yc_ref, wc_ref, gc_ref)
    mix = jnp.dot(merged.astype(BF16), wo_ref[...], preferred_element_type=F32)
    z = DEEPNORM_ALPHA * x_ref[...] + mix
    mu = jnp.mean(z, -1, keepdims=True)
    zc = z - mu
    var = jnp.mean(zc * zc, -1, keepdims=True)
    o_ref[...] = zc * lax.rsqrt(var + LN_EPS) * g_ref[...] + b_ref[...]


def merge_branches(ya, yb, yc, h, w_br_a, w_br_b, w_br_c, w_o, x, ln_g, ln_b):
    t, d = x.shape
    tm = _pick_tile(t, (512, 256, 128, 8))
    row = lambda width, col=0: pl.BlockSpec((tm, width), lambda i: (i, col))
    full = lambda a: pl.BlockSpec(a.shape, lambda i: (0, 0))
    ws = [w.astype(BF16) for w in (w_br_a, w_br_b, w_br_c, w_o)]
    return pl.pallas_call(
        _merge_kernel,
        grid=(t // tm,),
        in_specs=[row(A_WIDTH), row(B_WIDTH), row(C_WIDTH), row(d, 0), row(d, 1), row(d, 2)] + [full(w) for w in ws]
                 + [row(d), pl.BlockSpec((1, d), lambda i: (0, 0)), pl.BlockSpec((1, d), lambda i: (0, 0))],
        out_specs=row(d),
        out_shape=jax.ShapeDtypeStruct((t, d), F32),
        compiler_params=pltpu.CompilerParams(dimension_semantics=("arbitrary",), vmem_limit_bytes=VMEM_LIMIT_BYTES),
        name="merge_branches",
    )(ya, yb, yc, h, h, h, *ws, x, ln_g.reshape(1, d), ln_b.reshape(1, d))


ROUTER_TILE = 512


def _router_kernel(x_ref, rwt_ref, bias_ref, tri_ref, idx_ref, gate_ref, pos_ref, cnt_ref, carry_ref):
    @pl.when(pl.program_id(0) == 0)
    def _():
        carry_ref[...] = jnp.zeros_like(carry_ref)

    logits = lax.dot_general(rwt_ref[...], x_ref[...].astype(BF16), (((1,), (1,)), ((), ())), preferred_element_type=F32)
    scores = jax.nn.sigmoid(logits)
    remaining = scores + bias_ref[...]
    expert = lax.broadcasted_iota(jnp.int32, scores.shape, 0)
    picks = []
    for _ in range(TOP_K):
        best = jnp.max(remaining, axis=0, keepdims=True)
        first = jnp.min(jnp.where(remaining == best, expert, N_EXPERTS), axis=0, keepdims=True)
        pick = expert == first
        picks.append((first, pick))
        remaining = jnp.where(pick, -jnp.inf, remaining)
    chosen = picks[0][1]
    for _, pick in picks[1:]:
        chosen = chosen | pick
    total = jnp.sum(jnp.where(chosen, scores, 0.0), axis=0, keepdims=True)
    prefix = jnp.dot(jnp.where(chosen, 1.0, 0.0).astype(BF16), tri_ref[...], preferred_element_type=F32)
    rank = carry_ref[...] + prefix.astype(jnp.int32) - 1
    carry_ref[...] = carry_ref[...] + prefix[:, -1:].astype(jnp.int32)
    cnt_ref[...] = carry_ref[...]
    for k, (first, pick) in enumerate(picks):
        idx_ref[k:k + 1, :] = first
        gate_ref[k:k + 1, :] = jnp.sum(jnp.where(pick, scores, 0.0), axis=0, keepdims=True) / total * ROUTED_SCALE
        pos_ref[k:k + 1, :] = jnp.sum(jnp.where(pick, rank, 0), axis=0, keepdims=True)


def moe_router(xt, router_w, router_bias):
    t, d = xt.shape
    tile = ROUTER_TILE
    tri = (jnp.arange(tile)[:, None] <= jnp.arange(tile)[None, :]).astype(BF16)
    row = pl.BlockSpec((TOP_K, tile), lambda i: (0, i))
    return pl.pallas_call(
        _router_kernel,
        grid=(t // tile,),
        in_specs=[pl.BlockSpec((tile, d), lambda i: (i, 0)), pl.BlockSpec((N_EXPERTS, d), lambda i: (0, 0)),
                  pl.BlockSpec((N_EXPERTS, 1), lambda i: (0, 0)), pl.BlockSpec((tile, tile), lambda i: (0, 0))],
        out_specs=[row, row, row, pl.BlockSpec((N_EXPERTS, 1), lambda i: (0, 0))],
        out_shape=[jax.ShapeDtypeStruct((TOP_K, t), jnp.int32), jax.ShapeDtypeStruct((TOP_K, t), F32),
                   jax.ShapeDtypeStruct((TOP_K, t), jnp.int32), jax.ShapeDtypeStruct((N_EXPERTS, 1), jnp.int32)],
        scratch_shapes=[pltpu.VMEM((N_EXPERTS, 1), jnp.int32)],
        compiler_params=pltpu.CompilerParams(dimension_semantics=("arbitrary",), vmem_limit_bytes=VMEM_LIMIT_BYTES),
        name="moe_router",
    )(xt, router_w.T.astype(BF16), router_bias.reshape(N_EXPERTS, 1), tri)


def _expert_kernel(blk_expert_ref, blk_rows_ref, x_ref, wgu_ref, wdn_ref, o_ref):
    n_valid = blk_rows_ref[pl.program_id(0)]

    @pl.when(n_valid > 0)
    def _():
        row = lax.broadcasted_iota(jnp.int32, x_ref.shape, 0)
        x = jnp.where(row < n_valid, x_ref[...], 0.0).astype(BF16)
        h = jnp.dot(x, wgu_ref[0], preferred_element_type=F32)
        hg, hu = h[:, :D_EXPERT], h[:, D_EXPERT:]
        act = (hg * jax.nn.sigmoid(hg) * hu).astype(BF16)
        o_ref[...] = jnp.dot(act, wdn_ref[0], preferred_element_type=F32)

    @pl.when(n_valid <= 0)
    def _():
        o_ref[...] = jnp.zeros_like(o_ref)


def expert_ffn(xs, blk_expert, blk_rows, w_gu, w_dn):
    n_pad, d = xs.shape
    n_blk = n_pad // MOE_BLOCK
    return pl.pallas_call(
        _expert_kernel,
        grid_spec=pltpu.PrefetchScalarGridSpec(
            num_scalar_prefetch=2,
            grid=(n_blk,),
            in_specs=[pl.BlockSpec((MOE_BLOCK, d), lambda b, be, br: (b, 0)),
                      pl.BlockSpec((1, d, 2 * D_EXPERT), lambda b, be, br: (be[b], 0, 0)),
                      pl.BlockSpec((1, D_EXPERT, d), lambda b, be, br: (be[b], 0, 0))],
            out_specs=pl.BlockSpec((MOE_BLOCK, d), lambda b, be, br: (b, 0))),
        out_shape=jax.ShapeDtypeStruct((n_pad, d), F32),
        compiler_params=pltpu.CompilerParams(dimension_semantics=("arbitrary",), vmem_limit_bytes=VMEM_LIMIT_BYTES),
        name="expert_ffn",
    )(blk_expert, blk_rows, xs, w_gu, w_dn)


DISPATCH_TILE = 256
COMBINE_TILE = 128


def _assignment_row(pad_start_ref, idx_ref, pos_ref, a):
    return pad_start_ref[idx_ref[0, 0, a]] + pos_ref[0, 0, a]


def _dispatch_kernel(pad_start_ref, idx_ref, pos_ref, x_ref, xs_hbm, sem):
    def issue(r, carry):
        for k in range(TOP_K):
            row = _assignment_row(pad_start_ref, idx_ref, pos_ref, r * TOP_K + k)
            pltpu.make_async_copy(x_ref.at[pl.ds(r, 1)], xs_hbm.at[pl.ds(row, 1)], sem).start(priority=k % 2)
        return carry

    lax.fori_loop(0, DISPATCH_TILE, issue, 0)
    n_rows = DISPATCH_TILE * TOP_K
    pltpu.make_async_copy(xs_hbm.at[pl.ds(0, n_rows)], xs_hbm.at[pl.ds(0, n_rows)], sem).wait()


def moe_dispatch(xt, pad_start, idx_tok_major, pos_tok_major, n_pad):
    t, d = xt.shape
    n_tiles = t // DISPATCH_TILE
    smem = pl.BlockSpec((1, 1, DISPATCH_TILE * TOP_K), lambda i, ps: (i, 0, 0), memory_space=pltpu.SMEM)
    return pl.pallas_call(
        _dispatch_kernel,
        grid_spec=pltpu.PrefetchScalarGridSpec(
            num_scalar_prefetch=1,
            grid=(n_tiles,),
            in_specs=[smem, smem, pl.BlockSpec((DISPATCH_TILE, d), lambda i, ps: (i, 0))],
            out_specs=pl.BlockSpec(memory_space=pl.ANY),
            scratch_shapes=[pltpu.SemaphoreType.DMA(())]),
        out_shape=jax.ShapeDtypeStruct((n_pad, d), xt.dtype),
        compiler_params=pltpu.CompilerParams(dimension_semantics=("arbitrary",), has_side_effects=True),
        name="moe_dispatch",
    )(pad_start, idx_tok_major.reshape(n_tiles, 1, -1), pos_tok_major.reshape(n_tiles, 1, -1), xt)


def _combine_kernel(pad_start_ref, idx_ref, pos_ref, idx_next_ref, pos_next_ref, ys_hbm, gate_ref, x_ref,
                    wgu_ref, wdn_ref, g_ref, b_ref, o_ref, buf_ref, sem):
    i = pl.program_id(0)
    n = pl.num_programs(0)
    slot = lax.rem(i, 2)

    def gather(i_ref, p_ref, into):
        def issue(r, carry):
            for k in range(TOP_K):
                row = _assignment_row(pad_start_ref, i_ref, p_ref, r * TOP_K + k)
                pltpu.make_async_copy(ys_hbm.at[pl.ds(row, 1)], buf_ref.at[into, k, pl.ds(r, 1)],
                                      sem.at[into]).start(priority=k % 2)
            return carry
        lax.fori_loop(0, COMBINE_TILE, issue, 0)

    @pl.when(i == 0)
    def _():
        gather(idx_ref, pos_ref, 0)

    @pl.when(i + 1 < n)
    def _():
        gather(idx_next_ref, pos_next_ref, 1 - slot)

    x = x_ref[...]
    h = jnp.dot(x.astype(BF16), wgu_ref[...], preferred_element_type=F32)
    hg, hu = h[:, :D_SHARED], h[:, D_SHARED:]
    acc = DEEPNORM_ALPHA * x + jnp.dot((hg * jax.nn.sigmoid(hg) * hu).astype(BF16), wdn_ref[...], preferred_element_type=F32)
    pltpu.make_async_copy(buf_ref.at[slot], buf_ref.at[slot], sem.at[slot]).wait()
    gate = gate_ref[...]
    for k in range(TOP_K):
        acc = acc + buf_ref[slot, k] * gate[:, k:k + 1]
    mu = jnp.mean(acc, -1, keepdims=True)
    zc = acc - mu
    var = jnp.mean(zc * zc, -1, keepdims=True)
    o_ref[...] = zc * lax.rsqrt(var + LN_EPS) * g_ref[...] + b_ref[...]


def moe_combine(ys, pad_start, idx_tok_major, pos_tok_major, gate_tok_major, xt, sh_w_gu, sh_w_down, ln_g, ln_b):
    t, d = xt.shape
    tm = COMBINE_TILE
    n_tiles = t // tm
    idx3, pos3 = idx_tok_major.reshape(n_tiles, 1, -1), pos_tok_major.reshape(n_tiles, 1, -1)
    row = pl.BlockSpec((tm, d), lambda i, ps: (i, 0))
    full = lambda shape: pl.BlockSpec(shape, lambda i, ps: (0, 0))
    this = pl.BlockSpec((1, 1, tm * TOP_K), lambda i, ps: (i, 0, 0), memory_space=pltpu.SMEM)
    nxt = pl.BlockSpec((1, 1, tm * TOP_K), lambda i, ps: (jnp.minimum(i + 1, n_tiles - 1), 0, 0), memory_space=pltpu.SMEM)
    return pl.pallas_call(
        _combine_kernel,
        grid_spec=pltpu.PrefetchScalarGridSpec(
            num_scalar_prefetch=1,
            grid=(n_tiles,),
            in_specs=[this, this, nxt, nxt, pl.BlockSpec(memory_space=pl.ANY),
                      pl.BlockSpec((tm, TOP_K), lambda i, ps: (i, 0)), row,
                      full((d, 2 * D_SHARED)), full((D_SHARED, d)), full((1, d)), full((1, d))],
            out_specs=row,
            scratch_shapes=[pltpu.VMEM((2, TOP_K, tm, d), F32), pltpu.SemaphoreType.DMA((2,))]),
        out_shape=jax.ShapeDtypeStruct((t, d), F32),
        compiler_params=pltpu.CompilerParams(dimension_semantics=("arbitrary",), vmem_limit_bytes=VMEM_LIMIT_BYTES),
        name="moe_combine",
    )(pad_start, idx3, pos3, idx3, pos3, ys, gate_tok_major, xt, sh_w_gu.astype(BF16), sh_w_down.astype(BF16),
      ln_g.reshape(1, d), ln_b.reshape(1, d))


def moe_ffn(xt, router_w, router_bias, exp_w_gu, exp_w_down, sh_w_gu, sh_w_down, ln_g, ln_b):
    n_tok, d = xt.shape
    idx, gate, pos, counts = moe_router(xt, router_w, router_bias)
    counts = counts[:, 0]
    padded = (counts + MOE_BLOCK - 1) // MOE_BLOCK * MOE_BLOCK
    pad_ends = jnp.cumsum(padded)
    pad_start = pad_ends - padded
    n_blk = n_tok * TOP_K // MOE_BLOCK + N_EXPERTS
    n_pad = n_blk * MOE_BLOCK
    idx_tm, pos_tm = idx.T.reshape(-1), pos.T.reshape(-1)
    blk_lo = jnp.arange(n_blk, dtype=jnp.int32) * MOE_BLOCK
    owner = (pad_start[None, :] <= blk_lo[:, None]) & (blk_lo[:, None] < pad_ends[None, :])
    blk_expert = jnp.minimum(jnp.sum(pad_ends[None, :] <= blk_lo[:, None], axis=1), N_EXPERTS - 1).astype(jnp.int32)
    blk_rows = jnp.clip(jnp.sum(jnp.where(owner, (pad_start + counts)[None, :] - blk_lo[:, None], 0), axis=1),
                        0, MOE_BLOCK).astype(jnp.int32)
    pad_start = pad_start.astype(jnp.int32)
    xs = moe_dispatch(xt, pad_start, idx_tm, pos_tm, n_pad)
    ys = expert_ffn(xs, blk_expert, blk_rows, exp_w_gu.astype(BF16), exp_w_down.astype(BF16))
    return moe_combine(ys, pad_start, idx_tm, pos_tm, gate.T, xt, sh_w_gu, sh_w_down, ln_g, ln_b)


def kernel(x, rel_bias, w_in, a_kv_norm, a_kv_up, b_conv, b_a_log, b_dt_bias, b_out_norm, c_lambda_re, c_lambda_im, c_log_dt, c_b_re, c_b_im, c_c_re, c_c_im, c_d, c_glu_w, c_glu_b, w_br_a, w_br_b, w_br_c, w_o, ln1_g, ln1_b, router_w, router_bias, exp_w_gu, exp_w_down, sh_w_gu, sh_w_down, ln2_g, ln2_b):
    bn, s, d = x.shape
    t = bn * s
    for i in range(w_in.shape[0]):
        xt = hybrid_mixer(x, rel_bias, w_in[i], a_kv_norm[i], a_kv_up[i], b_conv[i], b_a_log[i], b_dt_bias[i],
                          b_out_norm[i], c_lambda_re[i], c_lambda_im[i], c_log_dt[i], c_b_re[i], c_b_im[i],
                          c_c_re[i], c_c_im[i], c_d[i], c_glu_w[i], c_glu_b[i], w_br_a[i], w_br_b[i], w_br_c[i], w_o[i],
                          ln1_g[i], ln1_b[i])
        x = moe_ffn(xt, router_w[i], router_bias[i], exp_w_gu[i], exp_w_down[i], sh_w_gu[i], sh_w_down[i],
                    ln2_g[i], ln2_b[i]).reshape(bn, s, d)
    return x
```

```python
import functools
import math

import jax
import jax.numpy as jnp
import numpy as np
from jax import lax
from jax.experimental import pallas as pl
from jax.experimental.pallas import tpu as pltpu

F32 = jnp.float32
BF16 = jnp.bfloat16

D_MODEL = 1024
DEPTH = 4
A_HEADS = 8
A_HEAD_DIM = 64
A_WIDTH = A_HEADS * A_HEAD_DIM
A_KV_RANK = 128
IDX_HEADS = 8
IDX_DIM = 64
TOPK_MAX = 256
Q_BLOCK = 128
REL_BUCKETS = 32
REL_MAX_DIST = 128
B_HEADS = 4
B_HEAD_DIM = 128
B_WIDTH = B_HEADS * B_HEAD_DIM
CONV_WIDTH = 4
CHUNK = 64
C_WIDTH = 512
C_GROUP = 16
C_GROUPS = C_WIDTH // C_GROUP
C_STATE = 64
N_EXPERTS = 64
TOP_K = 8
D_EXPERT = 256
D_SHARED = 256
ROUTED_SCALE = 2.5
MOE_BLOCK = 512
SPLITS = (A_WIDTH, A_KV_RANK, IDX_HEADS * IDX_DIM, IDX_DIM, IDX_HEADS, 3 * B_WIDTH, B_WIDTH, B_HEADS, B_HEADS, C_WIDTH, 3 * D_MODEL)
DEEPNORM_ALPHA = (2 * DEPTH) ** 0.25
LN_EPS = 1e-5
RMS_EPS = 1e-6
NEG_INF = -1e30

VMEM_LIMIT_BYTES = 48 * 1024 * 1024
LANE = 128


def _pick_tile(n, candidates):
    for c in candidates:
        if n % c == 0:
            return c
    return n


def _mm_kernel(a_ref, b_ref, o_ref):
    o_ref[...] = jnp.dot(a_ref[...].astype(BF16), b_ref[...], preferred_element_type=F32).astype(o_ref.dtype)


def matmul(a, b, out_dtype=F32):
    m, k = a.shape
    _, n = b.shape
    tm = _pick_tile(m, (1024, 512, 256, 128, 8))
    tn = _pick_tile(n, (1024, 768, 512, 384, 256, 128))
    return pl.pallas_call(
        _mm_kernel,
        grid=(m // tm, n // tn),
        in_specs=[pl.BlockSpec((tm, k), lambda i, j: (i, 0)), pl.BlockSpec((k, tn), lambda i, j: (0, j))],
        out_specs=pl.BlockSpec((tm, tn), lambda i, j: (i, j)),
        out_shape=jax.ShapeDtypeStruct((m, n), out_dtype),
        compiler_params=pltpu.CompilerParams(dimension_semantics=("arbitrary", "arbitrary"), vmem_limit_bytes=VMEM_LIMIT_BYTES),
        name="matmul",
    )(a, b.astype(BF16))


def _res_ln_kernel(x_ref, y_ref, g_ref, b_ref, o_ref):
    z = DEEPNORM_ALPHA * x_ref[...] + y_ref[...]
    mu = jnp.mean(z, -1, keepdims=True)
    zc = z - mu
    var = jnp.mean(zc * zc, -1, keepdims=True)
    o_ref[...] = zc * lax.rsqrt(var + LN_EPS) * g_ref[...] + b_ref[...]


def residual_layer_norm(x, y, g, b):
    t, d = x.shape
    tm = _pick_tile(t, (512, 256, 128, 8))
    row = pl.BlockSpec((tm, d), lambda i: (i, 0))
    vec = pl.BlockSpec((1, d), lambda i: (0, 0))
    return pl.pallas_call(
        _res_ln_kernel,
        grid=(t // tm,),
        in_specs=[row, row, vec, vec],
        out_specs=row,
        out_shape=jax.ShapeDtypeStruct((t, d), F32),
        compiler_params=pltpu.CompilerParams(dimension_semantics=("arbitrary",), vmem_limit_bytes=VMEM_LIMIT_BYTES),
        name="residual_layer_norm",
    )(x, y, g.reshape(1, d), b.reshape(1, d))


def rms_norm(x, g):
    return x * lax.rsqrt(jnp.mean(x * x, -1, keepdims=True) + RMS_EPS) * g


def l2_normalize(x):
    return x * lax.rsqrt(jnp.sum(x * x, -1, keepdims=True) + RMS_EPS)


def rel_bucket(dist):
    n_exact = REL_BUCKETS // 2
    d = jnp.maximum(dist, 1).astype(F32)
    large = n_exact + (jnp.log(d / n_exact) / math.log(REL_MAX_DIST / n_exact) * (REL_BUCKETS - n_exact)).astype(jnp.int32)
    return jnp.where(dist < n_exact, dist, jnp.minimum(large, REL_BUCKETS - 1))


KEY_BLOCK = 128
DSA_CHUNK = 512
BLOCKS_PER_CHUNK = DSA_CHUNK // KEY_BLOCK
HEAD_PAIRS = A_HEADS // 2
PAIR_WIDTH = 2 * A_HEAD_DIM
INT_MIN = -2 ** 31
HALF_MASK = 0xFFFF
HALF_BIAS = 0x8000
PACK = 16
IDX_W_SCALE = IDX_HEADS ** -0.5 * IDX_DIM ** -0.5


def _kv_kernel(c_ref, g_ref, wk_ref, wvt_ref, k_ref, vt_ref):
    c = c_ref[...]
    cn = (c * lax.rsqrt(jnp.mean(c * c, -1, keepdims=True) + RMS_EPS) * g_ref[...]).astype(BF16)
    k_ref[...] = jnp.dot(cn, wk_ref[...], preferred_element_type=F32).astype(BF16)
    vt = lax.dot_general(wvt_ref[...], cn, (((1,), (1,)), ((), ())), preferred_element_type=F32).astype(BF16)
    for i in range(vt_ref.shape[1]):
        vt_ref[0, i] = vt[:, i * KEY_BLOCK:(i + 1) * KEY_BLOCK]


def kv_project(c_kv, col, g, w_kv_up, bn, s):
    ts = 512
    w = w_kv_up.astype(BF16)
    return pl.pallas_call(
        _kv_kernel,
        grid=(bn, s // ts),
        in_specs=[pl.BlockSpec((ts, A_KV_RANK), lambda b, i: (b * (s // ts) + i, col)),
                  pl.BlockSpec((1, A_KV_RANK), lambda b, i: (0, 0)),
                  pl.BlockSpec((A_KV_RANK, A_WIDTH), lambda b, i: (0, 0)),
                  pl.BlockSpec((A_WIDTH, A_KV_RANK), lambda b, i: (0, 0))],
        out_specs=[pl.BlockSpec((None, ts, A_WIDTH), lambda b, i: (b, i, 0)),
                   pl.BlockSpec((1, ts // KEY_BLOCK, A_WIDTH, KEY_BLOCK), lambda b, i: (b, i, 0, 0))],
        out_shape=[jax.ShapeDtypeStruct((bn, s, A_WIDTH), BF16),
                   jax.ShapeDtypeStruct((bn, s // KEY_BLOCK, A_WIDTH, KEY_BLOCK), BF16)],
        compiler_params=pltpu.CompilerParams(dimension_semantics=("arbitrary", "arbitrary"), vmem_limit_bytes=VMEM_LIMIT_BYTES),
        name="kv_project",
    )(c_kv, g.reshape(1, A_KV_RANK), w[:, :A_WIDTH], w[:, A_WIDTH:].T)


def _mm_nt_kernel(wt_ref, x_ref, o_ref):
    o_ref[0] = lax.dot_general(wt_ref[...], x_ref[...].astype(BF16), (((1,), (1,)), ((), ())),
                               preferred_element_type=F32).astype(o_ref.dtype)


def matmul_nt(wt, x, bn, s, out_dtype=BF16):
    n, k = wt.shape
    ts = 512
    return pl.pallas_call(
        _mm_nt_kernel,
        grid=(bn, s // ts),
        in_specs=[pl.BlockSpec((n, k), lambda b, i: (0, 0)),
                  pl.BlockSpec((ts, k), lambda b, i: (b * (s // ts) + i, 0))],
        out_specs=pl.BlockSpec((1, n, ts), lambda b, i: (b, 0, i)),
        out_shape=jax.ShapeDtypeStruct((bn, n, s), out_dtype),
        compiler_params=pltpu.CompilerParams(dimension_semantics=("arbitrary", "arbitrary"), vmem_limit_bytes=VMEM_LIMIT_BYTES),
        name="matmul_nt",
    )(wt.astype(BF16), x)


def _sortable_key(score):
    bits = pltpu.bitcast(score, jnp.int32)
    return bits ^ (lax.shift_right_arithmetic(bits, 31) & jnp.int32(0x7FFFFFFF))


def _dsa_kernel(qt_ref, iqt_ref, wt_ref, kidx_ref, k_ref, vt_ref, bias_ref, o_ref,
                skey_ref, hi_ref, lo_ref, sel_ref, widx_ref, wq_ref, acc_ref, m_ref, l_ref, *, k_sel):
    j = pl.program_id(1)
    n_chunks = (j + BLOCKS_PER_CHUNK) // BLOCKS_PER_CHUNK
    q_pos = j * Q_BLOCK + lax.broadcasted_iota(jnp.int32, (1, Q_BLOCK), 1)

    for h in range(IDX_HEADS):
        widx_ref[:, h * Q_BLOCK:(h + 1) * Q_BLOCK] = iqt_ref[0, h * IDX_DIM:(h + 1) * IDX_DIM, :]
    wq_ref[...] = jnp.zeros_like(wq_ref)
    for p in range(HEAD_PAIRS):
        wq_ref[p, 0:A_HEAD_DIM, 0:Q_BLOCK] = qt_ref[0, (2 * p) * A_HEAD_DIM:(2 * p + 1) * A_HEAD_DIM, :]
        wq_ref[p, A_HEAD_DIM:PAIR_WIDTH, Q_BLOCK:2 * Q_BLOCK] = qt_ref[0, (2 * p + 1) * A_HEAD_DIM:(2 * p + 2) * A_HEAD_DIM, :]

    w_rows = [wt_ref[0, h:h + 1, :] * IDX_W_SCALE for h in range(IDX_HEADS)]

    def idx_chunk(c, carry):
        start = pl.multiple_of(c * DSA_CHUNK, DSA_CHUNK)
        kc = kidx_ref[0, pl.ds(start, DSA_CHUNK), :]
        score = jnp.zeros((DSA_CHUNK, Q_BLOCK), F32)
        for p in range(IDX_HEADS // 2):
            z = jnp.dot(kc, widx_ref[:, p * 2 * Q_BLOCK:(p + 1) * 2 * Q_BLOCK], preferred_element_type=F32)
            score = score + jnp.maximum(z[:, :Q_BLOCK], 0.0) * w_rows[2 * p]
            score = score + jnp.maximum(z[:, Q_BLOCK:], 0.0) * w_rows[2 * p + 1]
        pos = start + lax.broadcasted_iota(jnp.int32, (DSA_CHUNK, Q_BLOCK), 0)
        key = jnp.where(pos <= q_pos, _sortable_key(score), INT_MIN)
        skey_ref[pl.ds(start, DSA_CHUNK), :] = key
        hi_ref[pl.ds(start, DSA_CHUNK), :] = lax.shift_right_arithmetic(key, 16).astype(jnp.int16)
        lo_ref[pl.ds(start, DSA_CHUNK), :] = ((key & HALF_MASK) - HALF_BIAS).astype(jnp.int16)
        return carry

    lax.fori_loop(0, n_chunks, idx_chunk, 0)

    def count_ge(cand):
        def body(c, cnt):
            start = pl.multiple_of(c * DSA_CHUNK, DSA_CHUNK)
            hit = (skey_ref[pl.ds(start, DSA_CHUNK), :] >= cand).astype(jnp.int32)
            return cnt + jnp.sum(hit.reshape(DSA_CHUNK // 8, 8, Q_BLOCK), axis=0)
        cnt = lax.fori_loop(0, n_chunks, body, jnp.zeros((8, Q_BLOCK), jnp.int32))
        return jnp.sum(cnt, axis=0, keepdims=True)

    def packed(value):
        return jnp.broadcast_to(value, (PACK, Q_BLOCK)).astype(jnp.int16)

    def count16(ref, test):
        def body(c, cnt):
            start = pl.multiple_of(c * DSA_CHUNK, DSA_CHUNK)
            hit = jnp.where(test(ref[pl.ds(start, DSA_CHUNK), :].reshape(DSA_CHUNK // PACK, PACK, Q_BLOCK)),
                            jnp.int16(1), jnp.int16(0))
            parts = [hit[r] for r in range(DSA_CHUNK // PACK)]
            while len(parts) > 1:
                parts = [parts[i] + parts[i + 1] for i in range(0, len(parts), 2)]
            return cnt + parts[0]
        cnt = lax.fori_loop(0, n_chunks, body, jnp.zeros((PACK, Q_BLOCK), jnp.int16))
        return jnp.sum(cnt.astype(jnp.int32), axis=0, keepdims=True)

    def bisect16(ref, want):
        def bit_step(i, biased):
            cand_biased = biased | lax.shift_left(jnp.int32(1), 15 - i)
            cand = packed(cand_biased - HALF_BIAS)[None]
            return jnp.where(count16(ref, lambda v: v >= cand) >= want, cand_biased, biased)
        return lax.fori_loop(0, 16, bit_step, jnp.zeros((1, Q_BLOCK), jnp.int32)) - HALF_BIAS

    thr_hi = bisect16(hi_ref, k_sel)
    thr_hi16 = packed(thr_hi)[None]
    above = count16(hi_ref, lambda v: v > thr_hi16)

    def restrict_low(c, carry):
        start = pl.multiple_of(c * DSA_CHUNK, DSA_CHUNK)
        hi = hi_ref[pl.ds(start, DSA_CHUNK), :].reshape(DSA_CHUNK // PACK, PACK, Q_BLOCK)
        lo = lo_ref[pl.ds(start, DSA_CHUNK), :].reshape(DSA_CHUNK // PACK, PACK, Q_BLOCK)
        sel_ref[pl.ds(start, DSA_CHUNK), :] = jnp.where(hi == thr_hi16, lo, jnp.int16(-HALF_BIAS)).reshape(DSA_CHUNK, Q_BLOCK)
        return carry

    lax.fori_loop(0, n_chunks, restrict_low, 0)
    thr_lo = bisect16(sel_ref, k_sel - above)
    thr = lax.shift_left(thr_hi, 16) | (thr_lo + HALF_BIAS)

    n_gt = count_ge(thr + 1)
    n_ge = count_ge(thr)
    tied = (n_ge > k_sel) & (thr > INT_MIN)

    @pl.when(jnp.max(tied.astype(jnp.int32)) > 0)
    def _():
        need = k_sel - n_gt

        def count_tied_upto(limit):
            def body(c, cnt):
                start = pl.multiple_of(c * DSA_CHUNK, DSA_CHUNK)
                pos = start + lax.broadcasted_iota(jnp.int32, (DSA_CHUNK, Q_BLOCK), 0)
                hit = jnp.where(skey_ref[pl.ds(start, DSA_CHUNK), :] == thr, (pos <= limit).astype(jnp.int32), 0)
                return cnt + jnp.sum(hit.reshape(DSA_CHUNK // 8, 8, Q_BLOCK), axis=0)
            cnt = lax.fori_loop(0, n_chunks, body, jnp.zeros((8, Q_BLOCK), jnp.int32))
            return jnp.sum(cnt, axis=0, keepdims=True)

        def pos_step(i, lim):
            cand = lim - lax.shift_left(jnp.int32(1), 30 - i)
            return jnp.where(count_tied_upto(cand) >= need, cand, lim)

        limit = lax.fori_loop(0, 31, pos_step, jnp.full((1, Q_BLOCK), 2 ** 31 - 1, jnp.int32))

        def demote(c, carry):
            start = pl.multiple_of(c * DSA_CHUNK, DSA_CHUNK)
            pos = start + lax.broadcasted_iota(jnp.int32, (DSA_CHUNK, Q_BLOCK), 0)
            sk = skey_ref[pl.ds(start, DSA_CHUNK), :]
            skey_ref[pl.ds(start, DSA_CHUNK), :] = jnp.where(tied & (sk == thr) & (pos > limit), thr - 1, sk)
            return carry

        lax.fori_loop(0, n_chunks, demote, 0)

    m_ref[...] = jnp.full_like(m_ref, NEG_INF)
    l_ref[...] = jnp.zeros_like(l_ref)
    acc_ref[...] = jnp.zeros_like(acc_ref)

    def attend(s_list, vt_of, n_blocks):
        for p in range(HEAD_PAIRS):
            s = s_list[p]
            m_old = m_ref[p]
            m_new = jnp.maximum(m_old, jnp.max(s, axis=0, keepdims=True))
            alpha = jnp.exp(m_old - m_new)
            pexp = jnp.exp(s - m_new)
            l_ref[p] = l_ref[p] * alpha + jnp.sum(pexp, axis=0, keepdims=True)
            pb = pexp.astype(BF16)
            pv = jnp.dot(vt_of(p, 0), pb[0:KEY_BLOCK], preferred_element_type=F32)
            for i in range(1, n_blocks):
                pv = pv + jnp.dot(vt_of(p, i), pb[i * KEY_BLOCK:(i + 1) * KEY_BLOCK], preferred_element_type=F32)
            acc_ref[p] = acc_ref[p] * alpha + pv
            m_ref[p] = m_new

    far_end = (j - 1) * KEY_BLOCK
    n_far = (jnp.maximum(j - 1, 0) + BLOCKS_PER_CHUNK - 1) // BLOCKS_PER_CHUNK

    def far_chunk(c, carry):
        start = pl.multiple_of(c * DSA_CHUNK, DSA_CHUNK)
        pos = start + lax.broadcasted_iota(jnp.int32, (DSA_CHUNK, Q_BLOCK), 0)
        sel = (skey_ref[pl.ds(start, DSA_CHUNK), :] >= thr) & (pos < far_end)
        madd = jnp.where(sel, 0.0, NEG_INF)
        madd2 = jnp.concatenate([madd, madd], axis=1)
        kc = k_ref[0, pl.ds(start, DSA_CHUNK), :]
        s_list = [jnp.dot(kc[:, p * PAIR_WIDTH:(p + 1) * PAIR_WIDTH], wq_ref[p], preferred_element_type=F32) + madd2
                  for p in range(HEAD_PAIRS)]
        blk0 = c * BLOCKS_PER_CHUNK
        attend(s_list, lambda p, i: vt_ref[0, blk0 + i, p * PAIR_WIDTH:(p + 1) * PAIR_WIDTH, :], BLOCKS_PER_CHUNK)
        return carry

    lax.fori_loop(0, n_far, far_chunk, 0)

    def near_block(kb, delta):
        start = pl.multiple_of(kb * KEY_BLOCK, KEY_BLOCK)
        pos = start + lax.broadcasted_iota(jnp.int32, (KEY_BLOCK, Q_BLOCK), 0)
        sel = (skey_ref[pl.ds(start, KEY_BLOCK), :] >= thr) & (pos <= q_pos)
        madd = jnp.where(sel, 0.0, NEG_INF)
        madd2 = jnp.concatenate([madd, madd], axis=1)
        kc = k_ref[0, pl.ds(start, KEY_BLOCK), :]
        s_list = [jnp.dot(kc[:, p * PAIR_WIDTH:(p + 1) * PAIR_WIDTH], wq_ref[p], preferred_element_type=F32)
                  + bias_ref[delta, p] + madd2 for p in range(HEAD_PAIRS)]
        attend(s_list, lambda p, i: vt_ref[0, kb, p * PAIR_WIDTH:(p + 1) * PAIR_WIDTH, :], 1)

    @pl.when(j > 0)
    def _():
        near_block(j - 1, 1)

    near_block(j, 0)

    outs = []
    for p in range(HEAD_PAIRS):
        o = acc_ref[p] / l_ref[p]
        outs.append(o[0:A_HEAD_DIM, 0:Q_BLOCK])
        outs.append(o[A_HEAD_DIM:PAIR_WIDTH, Q_BLOCK:2 * Q_BLOCK])
    o_ref[0] = jnp.concatenate(outs, axis=0).T.astype(o_ref.dtype)


def _rel_bias_tiles(rel_bias):
    tab = rel_bias[rel_bucket(jnp.arange(2 * Q_BLOCK, dtype=jnp.int32))] - rel_bias[REL_BUCKETS - 1]
    key = jnp.arange(KEY_BLOCK)[:, None]
    qry = jnp.arange(Q_BLOCK)[None, :]
    tiles = []
    for delta in (0, 1):
        dist = jnp.maximum(delta * Q_BLOCK + qry - key, 0)
        t = jnp.moveaxis(tab[dist], 2, 0)
        t = t.reshape(HEAD_PAIRS, 2, KEY_BLOCK, Q_BLOCK)
        tiles.append(jnp.concatenate([t[:, 0], t[:, 1]], axis=-1))
    return jnp.stack(tiles)


def dsa_attention(qiq_t, wt, k_idx, k, vt, bias_tiles, bn, s):
    assert A_WIDTH == IDX_HEADS * IDX_DIM
    scal_rows = wt.shape[1]
    assert scal_rows % IDX_HEADS == 0
    k_sel = min(TOPK_MAX, s // 4)
    nb = s // Q_BLOCK
    s_pad = s + DSA_CHUNK
    return pl.pallas_call(
        functools.partial(_dsa_kernel, k_sel=k_sel),
        grid=(bn, nb),
        in_specs=[pl.BlockSpec((1, A_WIDTH, Q_BLOCK), lambda b, j: (b, 0, j)),
                  pl.BlockSpec((1, IDX_HEADS * IDX_DIM, Q_BLOCK), lambda b, j: (b, 1, j)),
                  pl.BlockSpec((1, IDX_HEADS, Q_BLOCK), lambda b, j: (b, scal_rows // IDX_HEADS - 1, j)),
                  pl.BlockSpec((1, s, IDX_DIM), lambda b, j: (b, 0, 0)),
                  pl.BlockSpec((1, s, A_WIDTH), lambda b, j: (b, 0, 0)),
                  pl.BlockSpec((1, s // KEY_BLOCK, A_WIDTH, KEY_BLOCK), lambda b, j: (b, 0, 0, 0)),
                  pl.BlockSpec((2, HEAD_PAIRS, KEY_BLOCK, 2 * Q_BLOCK), lambda b, j: (0, 0, 0, 0))],
        out_specs=pl.BlockSpec((1, Q_BLOCK, A_WIDTH), lambda b, j: (b, j, 0)),
        out_shape=jax.ShapeDtypeStruct((bn, s, A_WIDTH), BF16),
        scratch_shapes=[pltpu.VMEM((s_pad, Q_BLOCK), jnp.int32),
                        pltpu.VMEM((s_pad, Q_BLOCK), jnp.int16), pltpu.VMEM((s_pad, Q_BLOCK), jnp.int16),
                        pltpu.VMEM((s_pad, Q_BLOCK), jnp.int16),
                        pltpu.VMEM((IDX_DIM, IDX_HEADS * Q_BLOCK), BF16),
                        pltpu.VMEM((HEAD_PAIRS, PAIR_WIDTH, 2 * Q_BLOCK), BF16),
                        pltpu.VMEM((HEAD_PAIRS, PAIR_WIDTH, 2 * Q_BLOCK), F32),
                        pltpu.VMEM((HEAD_PAIRS, 1, 2 * Q_BLOCK), F32),
                        pltpu.VMEM((HEAD_PAIRS, 1, 2 * Q_BLOCK), F32)],
        compiler_params=pltpu.CompilerParams(dimension_semantics=("arbitrary", "arbitrary"), vmem_limit_bytes=60 * 1024 * 1024),
        name="dsa_attention",
    )(qiq_t, qiq_t, wt, k_idx, k, vt, bias_tiles)


GDN_CHUNK = B_HEAD_DIM
GDN_BLOCK = 512
GDN_HALO = 8
NEUMANN_STEPS = 6


def _split_bf16(x):
    hi = x.astype(BF16)
    return hi, (x - hi.astype(F32)).astype(BF16)


def _dot_f32(a, b):
    a_hi, a_lo = _split_bf16(a)
    b_hi, b_lo = _split_bf16(b)
    d = lambda x, y: jnp.dot(x, y, preferred_element_type=F32)
    return d(a_hi, b_hi) + (d(a_hi, b_lo) + d(a_lo, b_hi))


def _bdot(a, b):
    return jnp.dot(a.astype(BF16), b.astype(BF16), preferred_element_type=F32)


def _bdot_nt(a, b):
    return lax.dot_general(a.astype(BF16), b.astype(BF16), (((1,), (1,)), ((), ())), preferred_element_type=F32)


def _bdot_tn(a, b):
    return lax.dot_general(a.astype(BF16), b.astype(BF16), (((0,), (0,)), ((), ())), preferred_element_type=F32)


def _softplus(x):
    return jnp.maximum(x, 0.0) + jnp.log(1.0 + jnp.exp(-jnp.abs(x)))


def _gdn_kernel(hq_ref, hk_ref, hv_ref, hz_ref, wq_ref, wk_ref, wv_ref, ab_ref, const_ref, gout_ref, tri_ref,
                o_ref, state_ref, xbuf_ref):
    head = pl.program_id(1)

    @pl.when(pl.program_id(2) == 0)
    def _():
        state_ref[...] = jnp.zeros_like(state_ref)
        xbuf_ref[:, 0:GDN_HALO, :] = jnp.zeros((3, GDN_HALO, B_HEAD_DIM), F32)

    def conv_silu(n, x_ref, w_ref):
        x = x_ref[...]
        xbuf_ref[n, GDN_HALO:, :] = x
        acc = xbuf_ref[n, GDN_HALO - CONV_WIDTH + 1:GDN_HALO - CONV_WIDTH + 1 + GDN_BLOCK, :] * w_ref[0:1, :]
        for j in range(1, CONV_WIDTH):
            lo = GDN_HALO - CONV_WIDTH + 1 + j
            acc = acc + xbuf_ref[n, lo:lo + GDN_BLOCK, :] * w_ref[j:j + 1, :]
        xbuf_ref[n, 0:GDN_HALO, :] = x[GDN_BLOCK - GDN_HALO:, :]
        return acc * jax.nn.sigmoid(acc)

    q = conv_silu(0, hq_ref, wq_ref)
    k = conv_silu(1, hk_ref, wk_ref)
    v = conv_silu(2, hv_ref, wv_ref)
    q = q * lax.rsqrt(jnp.sum(q * q, -1, keepdims=True) + RMS_EPS) * B_HEAD_DIM ** -0.5
    k = k * lax.rsqrt(jnp.sum(k * k, -1, keepdims=True) + RMS_EPS)

    a_row = ab_ref[0, pl.ds(head, 1), :]
    b_row = ab_ref[0, pl.ds(B_HEADS + head, 1), :]
    beta_row = jax.nn.sigmoid(b_row)
    g_row = const_ref[0, 0:1, :] * _softplus(a_row + const_ref[0, 1:2, :])
    gc_rows = _dot_f32(jnp.broadcast_to(g_row, (8, GDN_BLOCK)), tri_ref[...])[0:1, :]

    row_i = lax.broadcasted_iota(jnp.int32, (GDN_CHUNK, GDN_CHUNK), 0)
    col_j = lax.broadcasted_iota(jnp.int32, (GDN_CHUNK, GDN_CHUNK), 1)
    eye = jnp.where(row_i == col_j, 1.0, 0.0)
    z_all = hz_ref[...]
    chunks = []
    for c in range(GDN_BLOCK // GDN_CHUNK):
        sl = slice(c * GDN_CHUNK, (c + 1) * GDN_CHUNK)
        qc, kc, vc = q[sl], k[sl], v[sl]
        g_lane = jnp.broadcast_to(gc_rows[:, sl], (GDN_CHUNK, GDN_CHUNK))
        g_sub = g_lane.T
        beta_sub = jnp.broadcast_to(beta_row[:, sl], (GDN_CHUNK, GDN_CHUNK)).T
        g_end = jnp.broadcast_to(g_lane[:, GDN_CHUNK - 1:GDN_CHUNK], (GDN_CHUNK, GDN_CHUNK))
        decay = jnp.exp(jnp.where(row_i >= col_j, g_sub - g_lane, -jnp.inf))
        kb = kc * beta_sub
        x = -jnp.where(row_i > col_j, _bdot_nt(kb, kc) * decay, 0.0)
        chunks.append(dict(sl=sl, qc=qc, kc=kc, vc=vc, g_sub=g_sub, beta_sub=beta_sub, g_end=g_end, decay=decay,
                           kb=kb, x=x, t_inv=eye + x))
    for _ in range(NEUMANN_STEPS):
        for ch in chunks:
            ch['x'] = _dot_f32(ch['x'], ch['x'])
        for ch in chunks:
            ch['t_inv'] = ch['t_inv'] + _bdot(ch['t_inv'], ch['x'])
    for ch in chunks:
        g_exp = jnp.exp(ch['g_sub'])
        ch['u'] = _bdot(ch['t_inv'], ch['vc'] * ch['beta_sub'])
        ch['w'] = _bdot(ch['t_inv'], ch['kb'] * g_exp)
        ch['qk'] = _bdot_nt(ch['qc'], ch['kc']) * ch['decay']
        ch['q_dec'] = ch['qc'] * g_exp
        ch['k_dec'] = ch['kc'] * jnp.exp(ch['g_end'] - ch['g_sub'])
    for ch in chunks:
        state = state_ref[...]
        v_new = ch['u'] - _bdot(ch['w'], state)
        o = _bdot(ch['q_dec'], state) + _bdot(ch['qk'], v_new)
        state_ref[...] = state * jnp.exp(ch['g_end']) + _bdot_tn(ch['k_dec'], v_new)
        o = o * lax.rsqrt(jnp.mean(o * o, -1, keepdims=True) + RMS_EPS) * gout_ref[...]
        zc = z_all[ch['sl']]
        o_ref[ch['sl'], :] = (o * (zc * jax.nn.sigmoid(zc))).astype(o_ref.dtype)


def gated_deltanet(h, col0, ab_t, conv_w, a_log, dt_bias, out_norm_g, bn, s):
    nblk = s // GDN_BLOCK
    tok = jnp.arange(GDN_BLOCK)
    tri = ((tok[:, None] // GDN_CHUNK == tok[None, :] // GDN_CHUNK) & (tok[:, None] <= tok[None, :])).astype(F32)
    const = jnp.stack([jnp.broadcast_to(-jnp.exp(a_log)[:, None], (B_HEADS, GDN_BLOCK)),
                       jnp.broadcast_to(dt_bias[:, None], (B_HEADS, GDN_BLOCK))], axis=1)
    const = jnp.pad(const, ((0, 0), (0, 6), (0, 0)))
    hcol = lambda part: pl.BlockSpec((GDN_BLOCK, B_HEAD_DIM), lambda b, hh, i: (b * nblk + i, col0 + part * B_HEADS + hh))
    wcol = lambda part: pl.BlockSpec((CONV_WIDTH, B_HEAD_DIM), lambda b, hh, i: (0, part * B_HEADS + hh))
    return pl.pallas_call(
        _gdn_kernel,
        grid=(bn, B_HEADS, nblk),
        in_specs=[hcol(0), hcol(1), hcol(2), hcol(3), wcol(0), wcol(1), wcol(2),
                  pl.BlockSpec((1, 16, GDN_BLOCK), lambda b, hh, i: (b, 0, i)),
                  pl.BlockSpec((1, 8, GDN_BLOCK), lambda b, hh, i: (hh, 0, 0)),
                  pl.BlockSpec((1, B_HEAD_DIM), lambda b, hh, i: (0, 0)),
                  pl.BlockSpec((GDN_BLOCK, GDN_BLOCK), lambda b, hh, i: (0, 0))],
        out_specs=pl.BlockSpec((GDN_BLOCK, B_HEAD_DIM), lambda b, hh, i: (b * nblk + i, hh)),
        out_shape=jax.ShapeDtypeStruct((bn * s, B_WIDTH), BF16),
        scratch_shapes=[pltpu.VMEM((B_HEAD_DIM, B_HEAD_DIM), F32),
                        pltpu.VMEM((3, GDN_HALO + GDN_BLOCK, B_HEAD_DIM), F32)],
        compiler_params=pltpu.CompilerParams(dimension_semantics=("arbitrary", "arbitrary", "arbitrary"), vmem_limit_bytes=VMEM_LIMIT_BYTES),
        name="gated_deltanet",
    )(h, h, h, h, conv_w, conv_w, conv_w, ab_t, const, out_norm_g.reshape(1, B_HEAD_DIM), tri)


S5_CHUNK = 32
S5_ROW = S5_CHUNK * C_GROUP
STATE2 = 2 * C_STATE


def _s5_tables(lam_re, lam_im, log_dt, b_re, b_im, c_re, c_im, d_skip):
    hp = lax.Precision.HIGHEST
    ell = S5_CHUNK
    dt = jnp.exp(log_dt)[:, None]
    lr, li = lam_re, lam_im
    mag = jnp.exp(lr * dt)
    a_re, a_im = mag * jnp.cos(li * dt), mag * jnp.sin(li * dt)
    den = lr * lr + li * li
    f_re = ((a_re - 1.0) * lr + a_im * li) / den
    f_im = (a_im * lr - (a_re - 1.0) * li) / den
    bb_re = f_re[..., None] * b_re - f_im[..., None] * b_im
    bb_im = f_re[..., None] * b_im + f_im[..., None] * b_re
    tau = jnp.arange(ell + 1, dtype=F32)[:, None, None]
    pmag, ang = jnp.exp(tau * (lr * dt)), tau * (li * dt)
    p_re, p_im = pmag * jnp.cos(ang), pmag * jnp.sin(ang)
    ab_re = p_re[:ell, ..., None] * bb_re - p_im[:ell, ..., None] * bb_im
    ab_im = p_re[:ell, ..., None] * bb_im + p_im[:ell, ..., None] * bb_re
    kern = (jnp.einsum('gip,tgpj->tgij', c_re, ab_re, precision=hp)
            - jnp.einsum('gip,tgpj->tgij', c_im, ab_im, precision=hp))
    lag = jnp.arange(ell)[None, :] - jnp.arange(ell)[:, None]
    toep = jnp.where((lag >= 0)[:, :, None, None, None], kern[jnp.maximum(lag, 0)], 0.0)
    toep = jnp.transpose(toep, (2, 0, 4, 1, 3)).reshape(C_GROUPS, S5_ROW, S5_ROW)
    w_re = jnp.transpose(ab_re[::-1], (1, 0, 3, 2)).reshape(C_GROUPS, S5_ROW, C_STATE)
    w_im = jnp.transpose(ab_im[::-1], (1, 0, 3, 2)).reshape(C_GROUPS, S5_ROW, C_STATE)
    tw = jnp.concatenate([toep, w_re, w_im, w_im, w_re], axis=-1).astype(BF16)
    pt_re = jnp.swapaxes(p_re[1:], 0, 1)[:, :, None, :]
    pt_im = jnp.swapaxes(p_im[1:], 0, 1)[:, :, None, :]
    ca_re = c_re[:, None] * pt_re - c_im[:, None] * pt_im
    ca_im = c_re[:, None] * pt_im + c_im[:, None] * pt_re
    v_re = jnp.transpose(ca_re, (0, 3, 1, 2)).reshape(C_GROUPS, C_STATE, S5_ROW)
    v_im = jnp.transpose(-ca_im, (0, 3, 1, 2)).reshape(C_GROUPS, C_STATE, S5_ROW)
    v = jnp.concatenate([v_re, v_im], axis=1).astype(BF16)
    al_re, al_im = p_re[ell], p_im[ell]
    coef = jnp.stack([jnp.concatenate([al_re, al_re], -1), jnp.concatenate([-al_im, al_im], -1),
                      jnp.concatenate([al_im, -al_im], -1)], axis=1)
    d_flat = jnp.tile(d_skip, (1, ell)).reshape(C_GROUPS, 1, S5_ROW)
    return tw, v, coef, d_flat


def _s5_kernel(u_ref, tw_ref, v_ref, coef_ref, d_ref, o_ref, sin_ref, xin_ref, *, bn):
    rows = u_ref.shape[1]
    n_chunks = rows // bn
    u = u_ref[0]
    tw = jnp.dot(u.astype(BF16), tw_ref[0], preferred_element_type=F32)
    o_ref[0] = tw[:, :S5_ROW] + u * d_ref[0]
    sin_ref[...] = tw[:, S5_ROW:]
    a1, a2, a2s = coef_ref[0, 0:1, :], coef_ref[0, 1:2, :], coef_ref[0, 2:3, :]

    def step(c, carry):
        new = []
        for b in range(bn):
            x, xs = carry[2 * b], carry[2 * b + 1]
            row = b * n_chunks + c
            xin_ref[pl.ds(row, 1), :] = x
            s_row = sin_ref[pl.ds(row, 1), :]
            new.append(a1 * x + a2 * xs + s_row[:, :STATE2])
            new.append(a1 * xs + a2s * x + s_row[:, STATE2:])
        return tuple(new)

    zero = jnp.zeros((1, STATE2), F32)
    lax.fori_loop(0, n_chunks, step, (zero,) * (2 * bn))
    y = o_ref[0] + jnp.dot(xin_ref[...].astype(BF16), v_ref[0], preferred_element_type=F32)
    o_ref[0] = jax.nn.gelu(y)


def _glu_kernel(y_ref, w_ref, b_ref, o_ref):
    y = y_ref[...]
    gate = jnp.dot(y.astype(BF16), w_ref[...], preferred_element_type=F32) + b_ref[...]
    o_ref[...] = (y * jax.nn.sigmoid(gate)).astype(o_ref.dtype)


def s5_branch(u, lam_re, lam_im, log_dt, b_re, b_im, c_re, c_im, d_skip, glu_w, glu_b):
    bn, s, _ = u.shape
    n_chunks = s // S5_CHUNK
    rows = bn * n_chunks
    tw, v, coef, d_flat = _s5_tables(lam_re, lam_im, log_dt, b_re, b_im, c_re, c_im, d_skip)
    uf = jnp.transpose(u.reshape(bn, n_chunks, S5_CHUNK, C_GROUPS, C_GROUP), (3, 0, 1, 2, 4)).reshape(C_GROUPS, rows, S5_ROW)
    per_group = lambda shape: pl.BlockSpec((1,) + shape, lambda g: (g, 0, 0))
    y = pl.pallas_call(
        functools.partial(_s5_kernel, bn=bn),
        grid=(C_GROUPS,),
        in_specs=[per_group((rows, S5_ROW)), per_group((S5_ROW, S5_ROW + 2 * STATE2)), per_group((STATE2, S5_ROW)),
                  per_group((3, STATE2)), per_group((1, S5_ROW))],
        out_specs=per_group((rows, S5_ROW)),
        out_shape=jax.ShapeDtypeStruct((C_GROUPS, rows, S5_ROW), F32),
        scratch_shapes=[pltpu.VMEM((rows, 2 * STATE2), F32), pltpu.VMEM((rows, STATE2), F32)],
        compiler_params=pltpu.CompilerParams(dimension_semantics=("arbitrary",), vmem_limit_bytes=VMEM_LIMIT_BYTES),
        name="s5_scan",
    )(uf, tw, v, coef, d_flat)
    y = jnp.transpose(y.reshape(C_GROUPS, bn, n_chunks, S5_CHUNK, C_GROUP), (1, 2, 3, 0, 4)).reshape(bn * s, C_WIDTH)
    t = bn * s
    tm = _pick_tile(t, (1024, 512, 256, 128, 8))
    return pl.pallas_call(
        _glu_kernel,
        grid=(t // tm,),
        in_specs=[pl.BlockSpec((tm, C_WIDTH), lambda i: (i, 0)), pl.BlockSpec((C_WIDTH, C_WIDTH), lambda i: (0, 0)),
                  pl.BlockSpec((1, C_WIDTH), lambda i: (0, 0))],
        out_specs=pl.BlockSpec((tm, C_WIDTH), lambda i: (i, 0)),
        out_shape=jax.ShapeDtypeStruct((t, C_WIDTH), BF16),
        compiler_params=pltpu.CompilerParams(dimension_semantics=("arbitrary",), vmem_limit_bytes=VMEM_LIMIT_BYTES),
        name="s5_glu",
    )(y, glu_w.astype(BF16), glu_b.reshape(1, C_WIDTH))


def hybrid_mixer(x, rel_bias, w_in, a_kv_norm, a_kv_up, b_conv, b_a_log, b_dt_bias, b_out_norm,
                 c_lambda_re, c_lambda_im, c_log_dt, c_b_re, c_b_im, c_c_re, c_c_im, c_d, c_glu_w, c_glu_b,
                 w_br_a, w_br_b, w_br_c, w_o, ln_g, ln_b):
    bn, s, d = x.shape
    t = bn * s
    xt = x.reshape(t, d)
    offsets = [0] + np.cumsum(SPLITS).tolist()
    w_cols = [w_in[:, offsets[i]:offsets[i + 1]] for i in range(len(SPLITS))]
    (w_aq, w_ckv, w_iq, w_ik, w_iw, w_bqkv, w_bz, w_ba, w_bb, w_cu, w_gate) = w_cols
    w_ik_pad = jnp.pad(w_ik, ((0, 0), (0, LANE - IDX_DIM)))
    h = matmul(xt, jnp.concatenate([w_gate, w_ckv, w_ik_pad, w_bqkv, w_bz, w_cu], axis=1))
    o = np.cumsum([0, 3 * D_MODEL, A_KV_RANK, LANE, 3 * B_WIDTH, B_WIDTH, C_WIDTH]).tolist()
    idx_k = h[:, o[2]:o[2] + IDX_DIM].astype(BF16).reshape(bn, s, IDX_DIM)
    c_u = h[:, o[5]:o[6]].reshape(bn, s, C_WIDTH)
    qiq_t = matmul_nt(jnp.concatenate([w_aq * A_HEAD_DIM ** -0.5, w_iq], axis=1).T, xt, bn, s)
    scal_t = matmul_nt(jnp.concatenate([w_ba, w_bb, w_iw], axis=1).T, xt, bn, s, out_dtype=F32)
    k, vt = kv_project(h, o[1] // LANE, a_kv_norm, a_kv_up, bn, s)
    ya = dsa_attention(qiq_t, scal_t, idx_k, k, vt, _rel_bias_tiles(rel_bias), bn, s)
    yb = gated_deltanet(h, o[3] // LANE, scal_t, b_conv, b_a_log, b_dt_bias, b_out_norm, bn, s)
    yc = s5_branch(c_u, c_lambda_re, c_lambda_im, c_log_dt, c_b_re, c_b_im, c_c_re, c_c_im, c_d, c_glu_w, c_glu_b)
    return merge_branches(ya.reshape(t, A_WIDTH), yb, yc, h, w_br_a, w_br_b, w_br_c, w_o, xt, ln_g, ln_b)


def _merge_kernel(ya_ref, yb_ref, yc_ref, ga_ref, gb_ref, gc_ref, wa_ref, wb_ref, wc_ref, wo_ref, x_ref, g_ref, b_ref, o_ref):
    branch = lambda y_ref, w_ref, gate_ref: jax.nn.sigmoid(gate_ref[...]) * jnp.dot(y_ref[...], w_ref[...], preferred_element_type=F32)
    merged = branch(ya_ref, wa_ref, ga_ref) + branch(yb_ref, wb_ref, gb_ref) + branch(yc_ref, wc_ref, gc_ref)
    mix = jnp.dot(merged.astype(BF16), wo_ref[...], preferred_element_type=F32)
    z = DEEPNORM_ALPHA * x_ref[...] + mix
    mu = jnp.mean(z, -1, keepdims=True)
    zc = z - mu
    var = jnp.mean(zc * zc, -1, keepdims=True)
    o_ref[...] = zc * lax.rsqrt(var + LN_EPS) * g_ref[...] + b_ref[...]


def merge_branches(ya, yb, yc, h, w_br_a, w_br_b, w_br_c, w_o, x, ln_g, ln_b):
    t, d = x.shape
    tm = _pick_tile(t, (512, 256, 128, 8))
    row = lambda width, col=0: pl.BlockSpec((tm, width), lambda i: (i, col))
    full = lambda a: pl.BlockSpec(a.shape, lambda i: (0, 0))
    ws = [w.astype(BF16) for w in (w_br_a, w_br_b, w_br_c, w_o)]
    return pl.pallas_call(
        _merge_kernel,
        grid=(t // tm,),
        in_specs=[row(A_WIDTH), row(B_WIDTH), row(C_WIDTH), row(d, 0), row(d, 1), row(d, 2)] + [full(w) for w in ws]
                 + [row(d), pl.BlockSpec((1, d), lambda i: (0, 0)), pl.BlockSpec((1, d), lambda i: (0, 0))],
        out_specs=row(d),
        out_shape=jax.ShapeDtypeStruct((t, d), F32),
        compiler_params=pltpu.CompilerParams(dimension_semantics=("arbitrary",), vmem_limit_bytes=VMEM_LIMIT_BYTES),
        name="merge_branches",
    )(ya, yb, yc, h, h, h, *ws, x, ln_g.reshape(1, d), ln_b.reshape(1, d))


ROUTER_TILE = 512


def _router_kernel(x_ref, rwt_ref, bias_ref, tri_ref, idx_ref, gate_ref, pos_ref, cnt_ref, carry_ref):
    @pl.when(pl.program_id(0) == 0)
    def _():
        carry_ref[...] = jnp.zeros_like(carry_ref)

    logits = lax.dot_general(rwt_ref[...], x_ref[...].astype(BF16), (((1,), (1,)), ((), ())), preferred_element_type=F32)
    scores = jax.nn.sigmoid(logits)
    remaining = scores + bias_ref[...]
    expert = lax.broadcasted_iota(jnp.int32, scores.shape, 0)
    picks = []
    for _ in range(TOP_K):
        best = jnp.max(remaining, axis=0, keepdims=True)
        first = jnp.min(jnp.where(remaining == best, expert, N_EXPERTS), axis=0, keepdims=True)
        pick = expert == first
        picks.append((first, pick))
        remaining = jnp.where(pick, -jnp.inf, remaining)
    chosen = picks[0][1]
    for _, pick in picks[1:]:
        chosen = chosen | pick
    total = jnp.sum(jnp.where(chosen, scores, 0.0), axis=0, keepdims=True)
    prefix = jnp.dot(jnp.where(chosen, 1.0, 0.0).astype(BF16), tri_ref[...], preferred_element_type=F32)
    rank = carry_ref[...] + prefix.astype(jnp.int32) - 1
    carry_ref[...] = carry_ref[...] + prefix[:, -1:].astype(jnp.int32)
    cnt_ref[...] = carry_ref[...]
    for k, (first, pick) in enumerate(picks):
        idx_ref[k:k + 1, :] = first
        gate_ref[k:k + 1, :] = jnp.sum(jnp.where(pick, scores, 0.0), axis=0, keepdims=True) / total * ROUTED_SCALE
        pos_ref[k:k + 1, :] = jnp.sum(jnp.where(pick, rank, 0), axis=0, keepdims=True)


def moe_router(xt, router_w, router_bias):
    t, d = xt.shape
    tile = ROUTER_TILE
    tri = (jnp.arange(tile)[:, None] <= jnp.arange(tile)[None, :]).astype(BF16)
    row = pl.BlockSpec((TOP_K, tile), lambda i: (0, i))
    return pl.pallas_call(
        _router_kernel,
        grid=(t // tile,),
        in_specs=[pl.BlockSpec((tile, d), lambda i: (i, 0)), pl.BlockSpec((N_EXPERTS, d), lambda i: (0, 0)),
                  pl.BlockSpec((N_EXPERTS, 1), lambda i: (0, 0)), pl.BlockSpec((tile, tile), lambda i: (0, 0))],
        out_specs=[row, row, row, pl.BlockSpec((N_EXPERTS, 1), lambda i: (0, 0))],
        out_shape=[jax.ShapeDtypeStruct((TOP_K, t), jnp.int32), jax.ShapeDtypeStruct((TOP_K, t), F32),
                   jax.ShapeDtypeStruct((TOP_K, t), jnp.int32), jax.ShapeDtypeStruct((N_EXPERTS, 1), jnp.int32)],
        scratch_shapes=[pltpu.VMEM((N_EXPERTS, 1), jnp.int32)],
        compiler_params=pltpu.CompilerParams(dimension_semantics=("arbitrary",), vmem_limit_bytes=VMEM_LIMIT_BYTES),
        name="moe_router",
    )(xt, router_w.T.astype(BF16), router_bias.reshape(N_EXPERTS, 1), tri)


def _expert_kernel(blk_expert_ref, blk_rows_ref, x_ref, wgu_ref, wdn_ref, o_ref):
    n_valid = blk_rows_ref[pl.program_id(0)]

    @pl.when(n_valid > 0)
    def _():
        row = lax.broadcasted_iota(jnp.int32, x_ref.shape, 0)
        x = jnp.where(row < n_valid, x_ref[...], 0.0).astype(BF16)
        h = jnp.dot(x, wgu_ref[0], preferred_element_type=F32)
        hg, hu = h[:, :D_EXPERT], h[:, D_EXPERT:]
        act = (hg * jax.nn.sigmoid(hg) * hu).astype(BF16)
        o_ref[...] = jnp.dot(act, wdn_ref[0], preferred_element_type=F32)

    @pl.when(n_valid <= 0)
    def _():
        o_ref[...] = jnp.zeros_like(o_ref)


def expert_ffn(xs, blk_expert, blk_rows, w_gu, w_dn):
    n_pad, d = xs.shape
    n_blk = n_pad // MOE_BLOCK
    return pl.pallas_call(
        _expert_kernel,
        grid_spec=pltpu.PrefetchScalarGridSpec(
            num_scalar_prefetch=2,
            grid=(n_blk,),
            in_specs=[pl.BlockSpec((MOE_BLOCK, d), lambda b, be, br: (b, 0)),
                      pl.BlockSpec((1, d, 2 * D_EXPERT), lambda b, be, br: (be[b], 0, 0)),
                      pl.BlockSpec((1, D_EXPERT, d), lambda b, be, br: (be[b], 0, 0))],
            out_specs=pl.BlockSpec((MOE_BLOCK, d), lambda b, be, br: (b, 0))),
        out_shape=jax.ShapeDtypeStruct((n_pad, d), F32),
        compiler_params=pltpu.CompilerParams(dimension_semantics=("arbitrary",), vmem_limit_bytes=VMEM_LIMIT_BYTES),
        name="expert_ffn",
    )(blk_expert, blk_rows, xs, w_gu, w_dn)


DISPATCH_TILE = 256
COMBINE_TILE = 128


def _assignment_row(pad_start_ref, idx_ref, pos_ref, a):
    return pad_start_ref[idx_ref[0, 0, a]] + pos_ref[0, 0, a]


def _dispatch_kernel(pad_start_ref, idx_ref, pos_ref, x_ref, xs_hbm, sem):
    def issue(r, carry):
        for k in range(TOP_K):
            row = _assignment_row(pad_start_ref, idx_ref, pos_ref, r * TOP_K + k)
            pltpu.make_async_copy(x_ref.at[pl.ds(r, 1)], xs_hbm.at[pl.ds(row, 1)], sem).start(priority=k % 2)
        return carry

    lax.fori_loop(0, DISPATCH_TILE, issue, 0)
    n_rows = DISPATCH_TILE * TOP_K
    pltpu.make_async_copy(xs_hbm.at[pl.ds(0, n_rows)], xs_hbm.at[pl.ds(0, n_rows)], sem).wait()


def moe_dispatch(xt, pad_start, idx_tok_major, pos_tok_major, n_pad):
    t, d = xt.shape
    n_tiles = t // DISPATCH_TILE
    smem = pl.BlockSpec((1, 1, DISPATCH_TILE * TOP_K), lambda i, ps: (i, 0, 0), memory_space=pltpu.SMEM)
    return pl.pallas_call(
        _dispatch_kernel,
        grid_spec=pltpu.PrefetchScalarGridSpec(
            num_scalar_prefetch=1,
            grid=(n_tiles,),
            in_specs=[smem, smem, pl.BlockSpec((DISPATCH_TILE, d), lambda i, ps: (i, 0))],
            out_specs=pl.BlockSpec(memory_space=pl.ANY),
            scratch_shapes=[pltpu.SemaphoreType.DMA(())]),
        out_shape=jax.ShapeDtypeStruct((n_pad, d), xt.dtype),
        compiler_params=pltpu.CompilerParams(dimension_semantics=("arbitrary",), has_side_effects=True),
        name="moe_dispatch",
    )(pad_start, idx_tok_major.reshape(n_tiles, 1, -1), pos_tok_major.reshape(n_tiles, 1, -1), xt)


def _combine_kernel(pad_start_ref, idx_ref, pos_ref, idx_next_ref, pos_next_ref, ys_hbm, gate_ref, x_ref,
                    wgu_ref, wdn_ref, g_ref, b_ref, o_ref, buf_ref, sem):
    i = pl.program_id(0)
    n = pl.num_programs(0)
    slot = lax.rem(i, 2)

    def gather(i_ref, p_ref, into):
        def issue(r, carry):
            for k in range(TOP_K):
                row = _assignment_row(pad_start_ref, i_ref, p_ref, r * TOP_K + k)
                pltpu.make_async_copy(ys_hbm.at[pl.ds(row, 1)], buf_ref.at[into, k, pl.ds(r, 1)],
                                      sem.at[into]).start(priority=k % 2)
            return carry
        lax.fori_loop(0, COMBINE_TILE, issue, 0)

    @pl.when(i == 0)
    def _():
        gather(idx_ref, pos_ref, 0)

    @pl.when(i + 1 < n)
    def _():
        gather(idx_next_ref, pos_next_ref, 1 - slot)

    x = x_ref[...]
    h = jnp.dot(x.astype(BF16), wgu_ref[...], preferred_element_type=F32)
    hg, hu = h[:, :D_SHARED], h[:, D_SHARED:]
    acc = DEEPNORM_ALPHA * x + jnp.dot((hg * jax.nn.sigmoid(hg) * hu).astype(BF16), wdn_ref[...], preferred_element_type=F32)
    pltpu.make_async_copy(buf_ref.at[slot], buf_ref.at[slot], sem.at[slot]).wait()
    gate = gate_ref[...]
    for k in range(TOP_K):
        acc = acc + buf_ref[slot, k] * gate[:, k:k + 1]
    mu = jnp.mean(acc, -1, keepdims=True)
    zc = acc - mu
    var = jnp.mean(zc * zc, -1, keepdims=True)
    o_ref[...] = zc * lax.rsqrt(var + LN_EPS) * g_ref[...] + b_ref[...]


def moe_combine(ys, pad_start, idx_tok_major, pos_tok_major, gate_tok_major, xt, sh_w_gu, sh_w_down, ln_g, ln_b):
    t, d = xt.shape
    tm = COMBINE_TILE
    n_tiles = t // tm
    idx3, pos3 = idx_tok_major.reshape(n_tiles, 1, -1), pos_tok_major.reshape(n_tiles, 1, -1)
    row = pl.BlockSpec((tm, d), lambda i, ps: (i, 0))
    full = lambda shape: pl.BlockSpec(shape, lambda i, ps: (0, 0))
    this = pl.BlockSpec((1, 1, tm * TOP_K), lambda i, ps: (i, 0, 0), memory_space=pltpu.SMEM)
    nxt = pl.BlockSpec((1, 1, tm * TOP_K), lambda i, ps: (jnp.minimum(i + 1, n_tiles - 1), 0, 0), memory_space=pltpu.SMEM)
    return pl.pallas_call(
        _combine_kernel,
        grid_spec=pltpu.PrefetchScalarGridSpec(
            num_scalar_prefetch=1,
            grid=(n_tiles,),
            in_specs=[this, this, nxt, nxt, pl.BlockSpec(memory_space=pl.ANY),
                      pl.BlockSpec((tm, TOP_K), lambda i, ps: (i, 0)), row,
                      full((d, 2 * D_SHARED)), full((D_SHARED, d)), full((1, d)), full((1, d))],
            out_specs=row,
            scratch_shapes=[pltpu.VMEM((2, TOP_K, tm, d), F32), pltpu.SemaphoreType.DMA((2,))]),
        out_shape=jax.ShapeDtypeStruct((t, d), F32),
        compiler_params=pltpu.CompilerParams(dimension_semantics=("arbitrary",), vmem_limit_bytes=VMEM_LIMIT_BYTES),
        name="moe_combine",
    )(pad_start, idx3, pos3, idx3, pos3, ys, gate_tok_major, xt, sh_w_gu.astype(BF16), sh_w_down.astype(BF16),
      ln_g.reshape(1, d), ln_b.reshape(1, d))


def moe_ffn(xt, router_w, router_bias, exp_w_gu, exp_w_down, sh_w_gu, sh_w_down, ln_g, ln_b):
    n_tok, d = xt.shape
    idx, gate, pos, counts = moe_router(xt, router_w, router_bias)
    counts = counts[:, 0]
    padded = (counts + MOE_BLOCK - 1) // MOE_BLOCK * MOE_BLOCK
    pad_ends = jnp.cumsum(padded)
    pad_start = pad_ends - padded
    n_blk = n_tok * TOP_K // MOE_BLOCK + N_EXPERTS
    n_pad = n_blk * MOE_BLOCK
    idx_tm, pos_tm = idx.T.reshape(-1), pos.T.reshape(-1)
    blk_lo = jnp.arange(n_blk, dtype=jnp.int32) * MOE_BLOCK
    owner = (pad_start[None, :] <= blk_lo[:, None]) & (blk_lo[:, None] < pad_ends[None, :])
    blk_expert = jnp.minimum(jnp.sum(pad_ends[None, :] <= blk_lo[:, None], axis=1), N_EXPERTS - 1).astype(jnp.int32)
    blk_rows = jnp.clip(jnp.sum(jnp.where(owner, (pad_start + counts)[None, :] - blk_lo[:, None], 0), axis=1),
                        0, MOE_BLOCK).astype(jnp.int32)
    pad_start = pad_start.astype(jnp.int32)
    xs = moe_dispatch(xt, pad_start, idx_tm, pos_tm, n_pad)
    ys = expert_ffn(xs, blk_expert, blk_rows, exp_w_gu.astype(BF16), exp_w_down.astype(BF16))
    return moe_combine(ys, pad_start, idx_tm, pos_tm, gate.T, xt, sh_w_gu, sh_w_down, ln_g, ln_b)


def kernel(x, rel_bias, w_in, a_kv_norm, a_kv_up, b_conv, b_a_log, b_dt_bias, b_out_norm, c_lambda_re, c_lambda_im, c_log_dt, c_b_re, c_b_im, c_c_re, c_c_im, c_d, c_glu_w, c_glu_b, w_br_a, w_br_b, w_br_c, w_o, ln1_g, ln1_b, router_w, router_bias, exp_w_gu, exp_w_down, sh_w_gu, sh_w_down, ln2_g, ln2_b):
    bn, s, d = x.shape
    t = bn * s
    for i in range(w_in.shape[0]):
        xt = hybrid_mixer(x, rel_bias, w_in[i], a_kv_norm[i], a_kv_up[i], b_conv[i], b_a_log[i], b_dt_bias[i],
                          b_out_norm[i], c_lambda_re[i], c_lambda_im[i], c_log_dt[i], c_b_re[i], c_b_im[i],
                          c_c_re[i], c_c_im[i], c_d[i], c_glu_w[i], c_glu_b[i], w_br_a[i], w_br_b[i], w_br_c[i], w_o[i],
                          ln1_g[i], ln1_b[i])
        x = moe_ffn(xt, router_w[i], router_bias[i], exp_w_gu[i], exp_w_down[i], sh_w_gu[i], sh_w_down[i],
                    ln2_g[i], ln2_b[i]).reshape(bn, s, d)
    return x
```

```python
import functools
import math

import jax
import jax.numpy as jnp
import numpy as np
from jax import lax
from jax.experimental import pallas as pl
from jax.experimental.pallas import tpu as pltpu

F32 = jnp.float32
BF16 = jnp.bfloat16

D_MODEL = 1024
DEPTH = 4
A_HEADS = 8
A_HEAD_DIM = 64
A_WIDTH = A_HEADS * A_HEAD_DIM
A_KV_RANK = 128
IDX_HEADS = 8
IDX_DIM = 64
TOPK_MAX = 256
Q_BLOCK = 128
REL_BUCKETS = 32
REL_MAX_DIST = 128
B_HEADS = 4
B_HEAD_DIM = 128
B_WIDTH = B_HEADS * B_HEAD_DIM
CONV_WIDTH = 4
CHUNK = 64
C_WIDTH = 512
C_GROUP = 16
C_GROUPS = C_WIDTH // C_GROUP
C_STATE = 64
N_EXPERTS = 64
TOP_K = 8
D_EXPERT = 256
D_SHARED = 256
ROUTED_SCALE = 2.5
MOE_BLOCK = 512
SPLITS = (A_WIDTH, A_KV_RANK, IDX_HEADS * IDX_DIM, IDX_DIM, IDX_HEADS, 3 * B_WIDTH, B_WIDTH, B_HEADS, B_HEADS, C_WIDTH, 3 * D_MODEL)
DEEPNORM_ALPHA = (2 * DEPTH) ** 0.25
LN_EPS = 1e-5
RMS_EPS = 1e-6
NEG_INF = -1e30

VMEM_LIMIT_BYTES = 48 * 1024 * 1024
LANE = 128


def _pick_tile(n, candidates):
    for c in candidates:
        if n % c == 0:
            return c
    return n


def _mm_kernel(a_ref, b_ref, o_ref):
    o_ref[...] = jnp.dot(a_ref[...].astype(BF16), b_ref[...], preferred_element_type=F32).astype(o_ref.dtype)


def matmul(a, b, out_dtype=F32):
    m, k = a.shape
    _, n = b.shape
    tm = _pick_tile(m, (1024, 512, 256, 128, 8))
    tn = _pick_tile(n, (1024, 768, 512, 384, 256, 128))
    return pl.pallas_call(
        _mm_kernel,
        grid=(m // tm, n // tn),
        in_specs=[pl.BlockSpec((tm, k), lambda i, j: (i, 0)), pl.BlockSpec((k, tn), lambda i, j: (0, j))],
        out_specs=pl.BlockSpec((tm, tn), lambda i, j: (i, j)),
        out_shape=jax.ShapeDtypeStruct((m, n), out_dtype),
        compiler_params=pltpu.CompilerParams(dimension_semantics=("arbitrary", "arbitrary"), vmem_limit_bytes=VMEM_LIMIT_BYTES),
        name="matmul",
    )(a, b.astype(BF16))


def _res_ln_kernel(x_ref, y_ref, g_ref, b_ref, o_ref):
    z = DEEPNORM_ALPHA * x_ref[...] + y_ref[...]
    mu = jnp.mean(z, -1, keepdims=True)
    zc = z - mu
    var = jnp.mean(zc * zc, -1, keepdims=True)
    o_ref[...] = zc * lax.rsqrt(var + LN_EPS) * g_ref[...] + b_ref[...]


def residual_layer_norm(x, y, g, b):
    t, d = x.shape
    tm = _pick_tile(t, (512, 256, 128, 8))
    row = pl.BlockSpec((tm, d), lambda i: (i, 0))
    vec = pl.BlockSpec((1, d), lambda i: (0, 0))
    return pl.pallas_call(
        _res_ln_kernel,
        grid=(t // tm,),
        in_specs=[row, row, vec, vec],
        out_specs=row,
        out_shape=jax.ShapeDtypeStruct((t, d), F32),
        compiler_params=pltpu.CompilerParams(dimension_semantics=("arbitrary",), vmem_limit_bytes=VMEM_LIMIT_BYTES),
        name="residual_layer_norm",
    )(x, y, g.reshape(1, d), b.reshape(1, d))


def rms_norm(x, g):
    return x * lax.rsqrt(jnp.mean(x * x, -1, keepdims=True) + RMS_EPS) * g


def l2_normalize(x):
    return x * lax.rsqrt(jnp.sum(x * x, -1, keepdims=True) + RMS_EPS)


def rel_bucket(dist):
    n_exact = REL_BUCKETS // 2
    d = jnp.maximum(dist, 1).astype(F32)
    large = n_exact + (jnp.log(d / n_exact) / math.log(REL_MAX_DIST / n_exact) * (REL_BUCKETS - n_exact)).astype(jnp.int32)
    return jnp.where(dist < n_exact, dist, jnp.minimum(large, REL_BUCKETS - 1))


KEY_BLOCK = 128
DSA_CHUNK = 512
BLOCKS_PER_CHUNK = DSA_CHUNK // KEY_BLOCK
HEAD_PAIRS = A_HEADS // 2
PAIR_WIDTH = 2 * A_HEAD_DIM
INT_MIN = -2 ** 31
IDX_W_SCALE = IDX_HEADS ** -0.5 * IDX_DIM ** -0.5


def _kv_kernel(c_ref, g_ref, wk_ref, wvt_ref, k_ref, vt_ref):
    c = c_ref[...]
    cn = (c * lax.rsqrt(jnp.mean(c * c, -1, keepdims=True) + RMS_EPS) * g_ref[...]).astype(BF16)
    k_ref[...] = jnp.dot(cn, wk_ref[...], preferred_element_type=F32).astype(BF16)
    vt = lax.dot_general(wvt_ref[...], cn, (((1,), (1,)), ((), ())), preferred_element_type=F32).astype(BF16)
    for i in range(vt_ref.shape[1]):
        vt_ref[0, i] = vt[:, i * KEY_BLOCK:(i + 1) * KEY_BLOCK]


def kv_project(c_kv, col, g, w_kv_up, bn, s):
    ts = 512
    w = w_kv_up.astype(BF16)
    return pl.pallas_call(
        _kv_kernel,
        grid=(bn, s // ts),
        in_specs=[pl.BlockSpec((ts, A_KV_RANK), lambda b, i: (b * (s // ts) + i, col)),
                  pl.BlockSpec((1, A_KV_RANK), lambda b, i: (0, 0)),
                  pl.BlockSpec((A_KV_RANK, A_WIDTH), lambda b, i: (0, 0)),
                  pl.BlockSpec((A_WIDTH, A_KV_RANK), lambda b, i: (0, 0))],
        out_specs=[pl.BlockSpec((None, ts, A_WIDTH), lambda b, i: (b, i, 0)),
                   pl.BlockSpec((1, ts // KEY_BLOCK, A_WIDTH, KEY_BLOCK), lambda b, i: (b, i, 0, 0))],
        out_shape=[jax.ShapeDtypeStruct((bn, s, A_WIDTH), BF16),
                   jax.ShapeDtypeStruct((bn, s // KEY_BLOCK, A_WIDTH, KEY_BLOCK), BF16)],
        compiler_params=pltpu.CompilerParams(dimension_semantics=("arbitrary", "arbitrary"), vmem_limit_bytes=VMEM_LIMIT_BYTES),
        name="kv_project",
    )(c_kv, g.reshape(1, A_KV_RANK), w[:, :A_WIDTH], w[:, A_WIDTH:].T)


def _mm_nt_kernel(wt_ref, x_ref, o_ref):
    o_ref[0] = lax.dot_general(wt_ref[...], x_ref[...].astype(BF16), (((1,), (1,)), ((), ())),
                               preferred_element_type=F32).astype(o_ref.dtype)


def matmul_nt(wt, x, bn, s, out_dtype=BF16):
    n, k = wt.shape
    ts = 512
    return pl.pallas_call(
        _mm_nt_kernel,
        grid=(bn, s // ts),
        in_specs=[pl.BlockSpec((n, k), lambda b, i: (0, 0)),
                  pl.BlockSpec((ts, k), lambda b, i: (b * (s // ts) + i, 0))],
        out_specs=pl.BlockSpec((1, n, ts), lambda b, i: (b, 0, i)),
        out_shape=jax.ShapeDtypeStruct((bn, n, s), out_dtype),
        compiler_params=pltpu.CompilerParams(dimension_semantics=("arbitrary", "arbitrary"), vmem_limit_bytes=VMEM_LIMIT_BYTES),
        name="matmul_nt",
    )(wt.astype(BF16), x)


def _sortable_key(score):
    bits = pltpu.bitcast(score, jnp.int32)
    return bits ^ (lax.shift_right_arithmetic(bits, 31) & jnp.int32(0x7FFFFFFF))


def _dsa_kernel(qt_ref, iqt_ref, wt_ref, kidx_ref, k_ref, vt_ref, bias_ref, o_ref,
                skey_ref, widx_ref, wq_ref, acc_ref, m_ref, l_ref, *, k_sel):
    j = pl.program_id(1)
    n_chunks = (j + BLOCKS_PER_CHUNK) // BLOCKS_PER_CHUNK
    q_pos = j * Q_BLOCK + lax.broadcasted_iota(jnp.int32, (1, Q_BLOCK), 1)

    for h in range(IDX_HEADS):
        widx_ref[:, h * Q_BLOCK:(h + 1) * Q_BLOCK] = iqt_ref[0, h * IDX_DIM:(h + 1) * IDX_DIM, :]
    wq_ref[...] = jnp.zeros_like(wq_ref)
    for p in range(HEAD_PAIRS):
        wq_ref[p, 0:A_HEAD_DIM, 0:Q_BLOCK] = qt_ref[0, (2 * p) * A_HEAD_DIM:(2 * p + 1) * A_HEAD_DIM, :]
        wq_ref[p, A_HEAD_DIM:PAIR_WIDTH, Q_BLOCK:2 * Q_BLOCK] = qt_ref[0, (2 * p + 1) * A_HEAD_DIM:(2 * p + 2) * A_HEAD_DIM, :]

    w_rows = [wt_ref[0, h:h + 1, :] * IDX_W_SCALE for h in range(IDX_HEADS)]

    def idx_chunk(c, carry):
        start = pl.multiple_of(c * DSA_CHUNK, DSA_CHUNK)
        kc = kidx_ref[0, pl.ds(start, DSA_CHUNK), :]
        score = jnp.zeros((DSA_CHUNK, Q_BLOCK), F32)
        for p in range(IDX_HEADS // 2):
            z = jnp.dot(kc, widx_ref[:, p * 2 * Q_BLOCK:(p + 1) * 2 * Q_BLOCK], preferred_element_type=F32)
            score = score + jnp.maximum(z[:, :Q_BLOCK], 0.0) * w_rows[2 * p]
            score = score + jnp.maximum(z[:, Q_BLOCK:], 0.0) * w_rows[2 * p + 1]
        pos = start + lax.broadcasted_iota(jnp.int32, (DSA_CHUNK, Q_BLOCK), 0)
        skey_ref[pl.ds(start, DSA_CHUNK), :] = jnp.where(pos <= q_pos, _sortable_key(score), INT_MIN)
        return carry

    lax.fori_loop(0, n_chunks, idx_chunk, 0)

    def count_ge(cand):
        def body(c, cnt):
            start = pl.multiple_of(c * DSA_CHUNK, DSA_CHUNK)
            hit = (skey_ref[pl.ds(start, DSA_CHUNK), :] >= cand).astype(jnp.int32)
            return cnt + jnp.sum(hit.reshape(DSA_CHUNK // 8, 8, Q_BLOCK), axis=0)
        cnt = lax.fori_loop(0, n_chunks, body, jnp.zeros((8, Q_BLOCK), jnp.int32))
        return jnp.sum(cnt, axis=0, keepdims=True)

    def bit_step(i, thr_biased):
        cand_biased = thr_biased | lax.shift_left(jnp.int32(1), 31 - i)
        return jnp.where(count_ge(cand_biased ^ INT_MIN) >= k_sel, cand_biased, thr_biased)

    thr = lax.fori_loop(0, 32, bit_step, jnp.zeros((1, Q_BLOCK), jnp.int32)) ^ INT_MIN

    n_gt = count_ge(thr + 1)
    n_ge = count_ge(thr)
    tied = (n_ge > k_sel) & (thr > INT_MIN)

    @pl.when(jnp.max(tied.astype(jnp.int32)) > 0)
    def _():
        need = k_sel - n_gt

        def count_tied_upto(limit):
            def body(c, cnt):
                start = pl.multiple_of(c * DSA_CHUNK, DSA_CHUNK)
                pos = start + lax.broadcasted_iota(jnp.int32, (DSA_CHUNK, Q_BLOCK), 0)
                hit = jnp.where(skey_ref[pl.ds(start, DSA_CHUNK), :] == thr, (pos <= limit).astype(jnp.int32), 0)
                return cnt + jnp.sum(hit.reshape(DSA_CHUNK // 8, 8, Q_BLOCK), axis=0)
            cnt = lax.fori_loop(0, n_chunks, body, jnp.zeros((8, Q_BLOCK), jnp.int32))
            return jnp.sum(cnt, axis=0, keepdims=True)

        def pos_step(i, lim):
            cand = lim - lax.shift_left(jnp.int32(1), 30 - i)
            return jnp.where(count_tied_upto(cand) >= need, cand, lim)

        limit = lax.fori_loop(0, 31, pos_step, jnp.full((1, Q_BLOCK), 2 ** 31 - 1, jnp.int32))

        def demote(c, carry):
            start = pl.multiple_of(c * DSA_CHUNK, DSA_CHUNK)
            pos = start + lax.broadcasted_iota(jnp.int32, (DSA_CHUNK, Q_BLOCK), 0)
            sk = skey_ref[pl.ds(start, DSA_CHUNK), :]
            skey_ref[pl.ds(start, DSA_CHUNK), :] = jnp.where(tied & (sk == thr) & (pos > limit), thr - 1, sk)
            return carry

        lax.fori_loop(0, n_chunks, demote, 0)

    m_ref[...] = jnp.full_like(m_ref, NEG_INF)
    l_ref[...] = jnp.zeros_like(l_ref)
    acc_ref[...] = jnp.zeros_like(acc_ref)

    def attend(s_list, vt_of, n_blocks):
        for p in range(HEAD_PAIRS):
            s = s_list[p]
            m_old = m_ref[p]
            m_new = jnp.maximum(m_old, jnp.max(s, axis=0, keepdims=True))
            alpha = jnp.exp(m_old - m_new)
            pexp = jnp.exp(s - m_new)
            l_ref[p] = l_ref[p] * alpha + jnp.sum(pexp, axis=0, keepdims=True)
            pb = pexp.astype(BF16)
            pv = jnp.dot(vt_of(p, 0), pb[0:KEY_BLOCK], preferred_element_type=F32)
            for i in range(1, n_blocks):
                pv = pv + jnp.dot(vt_of(p, i), pb[i * KEY_BLOCK:(i + 1) * KEY_BLOCK], preferred_element_type=F32)
            acc_ref[p] = acc_ref[p] * alpha + pv
            m_ref[p] = m_new

    far_end = (j - 1) * KEY_BLOCK
    n_far = (jnp.maximum(j - 1, 0) + BLOCKS_PER_CHUNK - 1) // BLOCKS_PER_CHUNK

    def far_chunk(c, carry):
        start = pl.multiple_of(c * DSA_CHUNK, DSA_CHUNK)
        pos = start + lax.broadcasted_iota(jnp.int32, (DSA_CHUNK, Q_BLOCK), 0)
        sel = (skey_ref[pl.ds(start, DSA_CHUNK), :] >= thr) & (pos < far_end)
        madd = jnp.where(sel, 0.0, NEG_INF)
        madd2 = jnp.concatenate([madd, madd], axis=1)
        kc = k_ref[0, pl.ds(start, DSA_CHUNK), :]
        s_list = [jnp.dot(kc[:, p * PAIR_WIDTH:(p + 1) * PAIR_WIDTH], wq_ref[p], preferred_element_type=F32) + madd2
                  for p in range(HEAD_PAIRS)]
        blk0 = c * BLOCKS_PER_CHUNK
        attend(s_list, lambda p, i: vt_ref[0, blk0 + i, p * PAIR_WIDTH:(p + 1) * PAIR_WIDTH, :], BLOCKS_PER_CHUNK)
        return carry

    lax.fori_loop(0, n_far, far_chunk, 0)

    def near_block(kb, delta):
        start = pl.multiple_of(kb * KEY_BLOCK, KEY_BLOCK)
        pos = start + lax.broadcasted_iota(jnp.int32, (KEY_BLOCK, Q_BLOCK), 0)
        sel = (skey_ref[pl.ds(start, KEY_BLOCK), :] >= thr) & (pos <= q_pos)
        madd = jnp.where(sel, 0.0, NEG_INF)
        madd2 = jnp.concatenate([madd, madd], axis=1)
        kc = k_ref[0, pl.ds(start, KEY_BLOCK), :]
        s_list = [jnp.dot(kc[:, p * PAIR_WIDTH:(p + 1) * PAIR_WIDTH], wq_ref[p], preferred_element_type=F32)
                  + bias_ref[delta, p] + madd2 for p in range(HEAD_PAIRS)]
        attend(s_list, lambda p, i: vt_ref[0, kb, p * PAIR_WIDTH:(p + 1) * PAIR_WIDTH, :], 1)

    @pl.when(j > 0)
    def _():
        near_block(j - 1, 1)

    near_block(j, 0)

    outs = []
    for p in range(HEAD_PAIRS):
        o = acc_ref[p] / l_ref[p]
        outs.append(o[0:A_HEAD_DIM, 0:Q_BLOCK])
        outs.append(o[A_HEAD_DIM:PAIR_WIDTH, Q_BLOCK:2 * Q_BLOCK])
    o_ref[0] = jnp.concatenate(outs, axis=0).T.astype(o_ref.dtype)


def _rel_bias_tiles(rel_bias):
    tab = rel_bias[rel_bucket(jnp.arange(2 * Q_BLOCK, dtype=jnp.int32))] - rel_bias[REL_BUCKETS - 1]
    key = jnp.arange(KEY_BLOCK)[:, None]
    qry = jnp.arange(Q_BLOCK)[None, :]
    tiles = []
    for delta in (0, 1):
        dist = jnp.maximum(delta * Q_BLOCK + qry - key, 0)
        t = jnp.moveaxis(tab[dist], 2, 0)
        t = t.reshape(HEAD_PAIRS, 2, KEY_BLOCK, Q_BLOCK)
        tiles.append(jnp.concatenate([t[:, 0], t[:, 1]], axis=-1))
    return jnp.stack(tiles)


def dsa_attention(qiq_t, wt, k_idx, k, vt, bias_tiles, bn, s):
    assert A_WIDTH == IDX_HEADS * IDX_DIM
    scal_rows = wt.shape[1]
    assert scal_rows % IDX_HEADS == 0
    k_sel = min(TOPK_MAX, s // 4)
    nb = s // Q_BLOCK
    s_pad = s + DSA_CHUNK
    return pl.pallas_call(
        functools.partial(_dsa_kernel, k_sel=k_sel),
        grid=(bn, nb),
        in_specs=[pl.BlockSpec((1, A_WIDTH, Q_BLOCK), lambda b, j: (b, 0, j)),
                  pl.BlockSpec((1, IDX_HEADS * IDX_DIM, Q_BLOCK), lambda b, j: (b, 1, j)),
                  pl.BlockSpec((1, IDX_HEADS, Q_BLOCK), lambda b, j: (b, scal_rows // IDX_HEADS - 1, j)),
                  pl.BlockSpec((1, s, IDX_DIM), lambda b, j: (b, 0, 0)),
                  pl.BlockSpec((1, s, A_WIDTH), lambda b, j: (b, 0, 0)),
                  pl.BlockSpec((1, s // KEY_BLOCK, A_WIDTH, KEY_BLOCK), lambda b, j: (b, 0, 0, 0)),
                  pl.BlockSpec((2, HEAD_PAIRS, KEY_BLOCK, 2 * Q_BLOCK), lambda b, j: (0, 0, 0, 0))],
        out_specs=pl.BlockSpec((1, Q_BLOCK, A_WIDTH), lambda b, j: (b, j, 0)),
        out_shape=jax.ShapeDtypeStruct((bn, s, A_WIDTH), BF16),
        scratch_shapes=[pltpu.VMEM((s_pad, Q_BLOCK), jnp.int32),
                        pltpu.VMEM((IDX_DIM, IDX_HEADS * Q_BLOCK), BF16),
                        pltpu.VMEM((HEAD_PAIRS, PAIR_WIDTH, 2 * Q_BLOCK), BF16),
                        pltpu.VMEM((HEAD_PAIRS, PAIR_WIDTH, 2 * Q_BLOCK), F32),
                        pltpu.VMEM((HEAD_PAIRS, 1, 2 * Q_BLOCK), F32),
                        pltpu.VMEM((HEAD_PAIRS, 1, 2 * Q_BLOCK), F32)],
        compiler_params=pltpu.CompilerParams(dimension_semantics=("arbitrary", "arbitrary"), vmem_limit_bytes=56 * 1024 * 1024),
        name="dsa_attention",
    )(qiq_t, qiq_t, wt, k_idx, k, vt, bias_tiles)


GDN_CHUNK = B_HEAD_DIM
GDN_BLOCK = 512
GDN_HALO = 8
NEUMANN_STEPS = 6


def _split_bf16(x):
    hi = x.astype(BF16)
    return hi, (x - hi.astype(F32)).astype(BF16)


def _dot_f32(a, b):
    a_hi, a_lo = _split_bf16(a)
    b_hi, b_lo = _split_bf16(b)
    d = lambda x, y: jnp.dot(x, y, preferred_element_type=F32)
    return d(a_hi, b_hi) + (d(a_hi, b_lo) + d(a_lo, b_hi))


def _bdot(a, b):
    return jnp.dot(a.astype(BF16), b.astype(BF16), preferred_element_type=F32)


def _bdot_nt(a, b):
    return lax.dot_general(a.astype(BF16), b.astype(BF16), (((1,), (1,)), ((), ())), preferred_element_type=F32)


def _bdot_tn(a, b):
    return lax.dot_general(a.astype(BF16), b.astype(BF16), (((0,), (0,)), ((), ())), preferred_element_type=F32)


def _softplus(x):
    return jnp.maximum(x, 0.0) + jnp.log(1.0 + jnp.exp(-jnp.abs(x)))


def _gdn_kernel(hq_ref, hk_ref, hv_ref, hz_ref, wq_ref, wk_ref, wv_ref, ab_ref, const_ref, gout_ref, tri_ref,
                o_ref, state_ref, xbuf_ref):
    head = pl.program_id(1)

    @pl.when(pl.program_id(2) == 0)
    def _():
        state_ref[...] = jnp.zeros_like(state_ref)
        xbuf_ref[:, 0:GDN_HALO, :] = jnp.zeros((3, GDN_HALO, B_HEAD_DIM), F32)

    def conv_silu(n, x_ref, w_ref):
        x = x_ref[...]
        xbuf_ref[n, GDN_HALO:, :] = x
        acc = xbuf_ref[n, GDN_HALO - CONV_WIDTH + 1:GDN_HALO - CONV_WIDTH + 1 + GDN_BLOCK, :] * w_ref[0:1, :]
        for j in range(1, CONV_WIDTH):
            lo = GDN_HALO - CONV_WIDTH + 1 + j
            acc = acc + xbuf_ref[n, lo:lo + GDN_BLOCK, :] * w_ref[j:j + 1, :]
        xbuf_ref[n, 0:GDN_HALO, :] = x[GDN_BLOCK - GDN_HALO:, :]
        return acc * jax.nn.sigmoid(acc)

    q = conv_silu(0, hq_ref, wq_ref)
    k = conv_silu(1, hk_ref, wk_ref)
    v = conv_silu(2, hv_ref, wv_ref)
    q = q * lax.rsqrt(jnp.sum(q * q, -1, keepdims=True) + RMS_EPS) * B_HEAD_DIM ** -0.5
    k = k * lax.rsqrt(jnp.sum(k * k, -1, keepdims=True) + RMS_EPS)

    a_row = ab_ref[0, pl.ds(head, 1), :]
    b_row = ab_ref[0, pl.ds(B_HEADS + head, 1), :]
    beta_row = jax.nn.sigmoid(b_row)
    g_row = const_ref[0, 0:1, :] * _softplus(a_row + const_ref[0, 1:2, :])
    gc_rows = _dot_f32(jnp.broadcast_to(g_row, (8, GDN_BLOCK)), tri_ref[...])[0:1, :]

    row_i = lax.broadcasted_iota(jnp.int32, (GDN_CHUNK, GDN_CHUNK), 0)
    col_j = lax.broadcasted_iota(jnp.int32, (GDN_CHUNK, GDN_CHUNK), 1)
    eye = jnp.where(row_i == col_j, 1.0, 0.0)
    z_all = hz_ref[...]
    chunks = []
    for c in range(GDN_BLOCK // GDN_CHUNK):
        sl = slice(c * GDN_CHUNK, (c + 1) * GDN_CHUNK)
        qc, kc, vc = q[sl], k[sl], v[sl]
        g_lane = jnp.broadcast_to(gc_rows[:, sl], (GDN_CHUNK, GDN_CHUNK))
        g_sub = g_lane.T
        beta_sub = jnp.broadcast_to(beta_row[:, sl], (GDN_CHUNK, GDN_CHUNK)).T
        g_end = jnp.broadcast_to(g_lane[:, GDN_CHUNK - 1:GDN_CHUNK], (GDN_CHUNK, GDN_CHUNK))
        decay = jnp.exp(jnp.where(row_i >= col_j, g_sub - g_lane, -jnp.inf))
        kb = kc * beta_sub
        x = -jnp.where(row_i > col_j, _bdot_nt(kb, kc) * decay, 0.0)
        chunks.append(dict(sl=sl, qc=qc, kc=kc, vc=vc, g_sub=g_sub, beta_sub=beta_sub, g_end=g_end, decay=decay,
                           kb=kb, x=x, t_inv=eye + x))
    for _ in range(NEUMANN_STEPS):
        for ch in chunks:
            ch['x'] = _dot_f32(ch['x'], ch['x'])
        for ch in chunks:
            ch['t_inv'] = ch['t_inv'] + _bdot(ch['t_inv'], ch['x'])
    for ch in chunks:
        g_exp = jnp.exp(ch['g_sub'])
        ch['u'] = _bdot(ch['t_inv'], ch['vc'] * ch['beta_sub'])
        ch['w'] = _bdot(ch['t_inv'], ch['kb'] * g_exp)
        ch['qk'] = _bdot_nt(ch['qc'], ch['kc']) * ch['decay']
        ch['q_dec'] = ch['qc'] * g_exp
        ch['k_dec'] = ch['kc'] * jnp.exp(ch['g_end'] - ch['g_sub'])
    for ch in chunks:
        state = state_ref[...]
        v_new = ch['u'] - _bdot(ch['w'], state)
        o = _bdot(ch['q_dec'], state) + _bdot(ch['qk'], v_new)
        state_ref[...] = state * jnp.exp(ch['g_end']) + _bdot_tn(ch['k_dec'], v_new)
        o = o * lax.rsqrt(jnp.mean(o * o, -1, keepdims=True) + RMS_EPS) * gout_ref[...]
        zc = z_all[ch['sl']]
        o_ref[ch['sl'], :] = (o * (zc * jax.nn.sigmoid(zc))).astype(o_ref.dtype)


def gated_deltanet(h, col0, ab_t, conv_w, a_log, dt_bias, out_norm_g, bn, s):
    nblk = s // GDN_BLOCK
    tok = jnp.arange(GDN_BLOCK)
    tri = ((tok[:, None] // GDN_CHUNK == tok[None, :] // GDN_CHUNK) & (tok[:, None] <= tok[None, :])).astype(F32)
    const = jnp.stack([jnp.broadcast_to(-jnp.exp(a_log)[:, None], (B_HEADS, GDN_BLOCK)),
                       jnp.broadcast_to(dt_bias[:, None], (B_HEADS, GDN_BLOCK))], axis=1)
    const = jnp.pad(const, ((0, 0), (0, 6), (0, 0)))
    hcol = lambda part: pl.BlockSpec((GDN_BLOCK, B_HEAD_DIM), lambda b, hh, i: (b * nblk + i, col0 + part * B_HEADS + hh))
    wcol = lambda part: pl.BlockSpec((CONV_WIDTH, B_HEAD_DIM), lambda b, hh, i: (0, part * B_HEADS + hh))
    return pl.pallas_call(
        _gdn_kernel,
        grid=(bn, B_HEADS, nblk),
        in_specs=[hcol(0), hcol(1), hcol(2), hcol(3), wcol(0), wcol(1), wcol(2),
                  pl.BlockSpec((1, 16, GDN_BLOCK), lambda b, hh, i: (b, 0, i)),
                  pl.BlockSpec((1, 8, GDN_BLOCK), lambda b, hh, i: (hh, 0, 0)),
                  pl.BlockSpec((1, B_HEAD_DIM), lambda b, hh, i: (0, 0)),
                  pl.BlockSpec((GDN_BLOCK, GDN_BLOCK), lambda b, hh, i: (0, 0))],
        out_specs=pl.BlockSpec((GDN_BLOCK, B_HEAD_DIM), lambda b, hh, i: (b * nblk + i, hh)),
        out_shape=jax.ShapeDtypeStruct((bn * s, B_WIDTH), BF16),
        scratch_shapes=[pltpu.VMEM((B_HEAD_DIM, B_HEAD_DIM), F32),
                        pltpu.VMEM((3, GDN_HALO + GDN_BLOCK, B_HEAD_DIM), F32)],
        compiler_params=pltpu.CompilerParams(dimension_semantics=("arbitrary", "arbitrary", "arbitrary"), vmem_limit_bytes=VMEM_LIMIT_BYTES),
        name="gated_deltanet",
    )(h, h, h, h, conv_w, conv_w, conv_w, ab_t, const, out_norm_g.reshape(1, B_HEAD_DIM), tri)


S5_CHUNK = 32
S5_ROW = S5_CHUNK * C_GROUP
STATE2 = 2 * C_STATE


def _s5_tables(lam_re, lam_im, log_dt, b_re, b_im, c_re, c_im, d_skip):
    hp = lax.Precision.HIGHEST
    ell = S5_CHUNK
    dt = jnp.exp(log_dt)[:, None]
    lr, li = lam_re, lam_im
    mag = jnp.exp(lr * dt)
    a_re, a_im = mag * jnp.cos(li * dt), mag * jnp.sin(li * dt)
    den = lr * lr + li * li
    f_re = ((a_re - 1.0) * lr + a_im * li) / den
    f_im = (a_im * lr - (a_re - 1.0) * li) / den
    bb_re = f_re[..., None] * b_re - f_im[..., None] * b_im
    bb_im = f_re[..., None] * b_im + f_im[..., None] * b_re
    tau = jnp.arange(ell + 1, dtype=F32)[:, None, None]
    pmag, ang = jnp.exp(tau * (lr * dt)), tau * (li * dt)
    p_re, p_im = pmag * jnp.cos(ang), pmag * jnp.sin(ang)
    ab_re = p_re[:ell, ..., None] * bb_re - p_im[:ell, ..., None] * bb_im
    ab_im = p_re[:ell, ..., None] * bb_im + p_im[:ell, ..., None] * bb_re
    kern = (jnp.einsum('gip,tgpj->tgij', c_re, ab_re, precision=hp)
            - jnp.einsum('gip,tgpj->tgij', c_im, ab_im, precision=hp))
    lag = jnp.arange(ell)[None, :] - jnp.arange(ell)[:, None]
    toep = jnp.where((lag >= 0)[:, :, None, None, None], kern[jnp.maximum(lag, 0)], 0.0)
    toep = jnp.transpose(toep, (2, 0, 4, 1, 3)).reshape(C_GROUPS, S5_ROW, S5_ROW)
    w_re = jnp.transpose(ab_re[::-1], (1, 0, 3, 2)).reshape(C_GROUPS, S5_ROW, C_STATE)
    w_im = jnp.transpose(ab_im[::-1], (1, 0, 3, 2)).reshape(C_GROUPS, S5_ROW, C_STATE)
    tw = jnp.concatenate([toep, w_re, w_im, w_im, w_re], axis=-1).astype(BF16)
    pt_re = jnp.swapaxes(p_re[1:], 0, 1)[:, :, None, :]
    pt_im = jnp.swapaxes(p_im[1:], 0, 1)[:, :, None, :]
    ca_re = c_re[:, None] * pt_re - c_im[:, None] * pt_im
    ca_im = c_re[:, None] * pt_im + c_im[:, None] * pt_re
    v_re = jnp.transpose(ca_re, (0, 3, 1, 2)).reshape(C_GROUPS, C_STATE, S5_ROW)
    v_im = jnp.transpose(-ca_im, (0, 3, 1, 2)).reshape(C_GROUPS, C_STATE, S5_ROW)
    v = jnp.concatenate([v_re, v_im], axis=1).astype(BF16)
    al_re, al_im = p_re[ell], p_im[ell]
    coef = jnp.stack([jnp.concatenate([al_re, al_re], -1), jnp.concatenate([-al_im, al_im], -1),
                      jnp.concatenate([al_im, -al_im], -1)], axis=1)
    d_flat = jnp.tile(d_skip, (1, ell)).reshape(C_GROUPS, 1, S5_ROW)
    return tw, v, coef, d_flat


def _s5_kernel(u_ref, tw_ref, v_ref, coef_ref, d_ref, o_ref, sin_ref, xin_ref, *, bn):
    rows = u_ref.shape[1]
    n_chunks = rows // bn
    u = u_ref[0]
    tw = jnp.dot(u.astype(BF16), tw_ref[0], preferred_element_type=F32)
    o_ref[0] = tw[:, :S5_ROW] + u * d_ref[0]
    sin_ref[...] = tw[:, S5_ROW:]
    a1, a2, a2s = coef_ref[0, 0:1, :], coef_ref[0, 1:2, :], coef_ref[0, 2:3, :]

    def step(c, carry):
        new = []
        for b in range(bn):
            x, xs = carry[2 * b], carry[2 * b + 1]
            row = b * n_chunks + c
            xin_ref[pl.ds(row, 1), :] = x
            s_row = sin_ref[pl.ds(row, 1), :]
            new.append(a1 * x + a2 * xs + s_row[:, :STATE2])
            new.append(a1 * xs + a2s * x + s_row[:, STATE2:])
        return tuple(new)

    zero = jnp.zeros((1, STATE2), F32)
    lax.fori_loop(0, n_chunks, step, (zero,) * (2 * bn))
    y = o_ref[0] + jnp.dot(xin_ref[...].astype(BF16), v_ref[0], preferred_element_type=F32)
    o_ref[0] = jax.nn.gelu(y)


def _glu_kernel(y_ref, w_ref, b_ref, o_ref):
    y = y_ref[...]
    gate = jnp.dot(y.astype(BF16), w_ref[...], preferred_element_type=F32) + b_ref[...]
    o_ref[...] = (y * jax.nn.sigmoid(gate)).astype(o_ref.dtype)


def s5_branch(u, lam_re, lam_im, log_dt, b_re, b_im, c_re, c_im, d_skip, glu_w, glu_b):
    bn, s, _ = u.shape
    n_chunks = s // S5_CHUNK
    rows = bn * n_chunks
    tw, v, coef, d_flat = _s5_tables(lam_re, lam_im, log_dt, b_re, b_im, c_re, c_im, d_skip)
    uf = jnp.transpose(u.reshape(bn, n_chunks, S5_CHUNK, C_GROUPS, C_GROUP), (3, 0, 1, 2, 4)).reshape(C_GROUPS, rows, S5_ROW)
    per_group = lambda shape: pl.BlockSpec((1,) + shape, lambda g: (g, 0, 0))
    y = pl.pallas_call(
        functools.partial(_s5_kernel, bn=bn),
        grid=(C_GROUPS,),
        in_specs=[per_group((rows, S5_ROW)), per_group((S5_ROW, S5_ROW + 2 * STATE2)), per_group((STATE2, S5_ROW)),
                  per_group((3, STATE2)), per_group((1, S5_ROW))],
        out_specs=per_group((rows, S5_ROW)),
        out_shape=jax.ShapeDtypeStruct((C_GROUPS, rows, S5_ROW), F32),
        scratch_shapes=[pltpu.VMEM((rows, 2 * STATE2), F32), pltpu.VMEM((rows, STATE2), F32)],
        compiler_params=pltpu.CompilerParams(dimension_semantics=("arbitrary",), vmem_limit_bytes=VMEM_LIMIT_BYTES),
        name="s5_scan",
    )(uf, tw, v, coef, d_flat)
    y = jnp.transpose(y.reshape(C_GROUPS, bn, n_chunks, S5_CHUNK, C_GROUP), (1, 2, 3, 0, 4)).reshape(bn * s, C_WIDTH)
    t = bn * s
    tm = _pick_tile(t, (1024, 512, 256, 128, 8))
    return pl.pallas_call(
        _glu_kernel,
        grid=(t // tm,),
        in_specs=[pl.BlockSpec((tm, C_WIDTH), lambda i: (i, 0)), pl.BlockSpec((C_WIDTH, C_WIDTH), lambda i: (0, 0)),
                  pl.BlockSpec((1, C_WIDTH), lambda i: (0, 0))],
        out_specs=pl.BlockSpec((tm, C_WIDTH), lambda i: (i, 0)),
        out_shape=jax.ShapeDtypeStruct((t, C_WIDTH), BF16),
        compiler_params=pltpu.CompilerParams(dimension_semantics=("arbitrary",), vmem_limit_bytes=VMEM_LIMIT_BYTES),
        name="s5_glu",
    )(y, glu_w.astype(BF16), glu_b.reshape(1, C_WIDTH))


def hybrid_mixer(x, rel_bias, w_in, a_kv_norm, a_kv_up, b_conv, b_a_log, b_dt_bias, b_out_norm,
                 c_lambda_re, c_lambda_im, c_log_dt, c_b_re, c_b_im, c_c_re, c_c_im, c_d, c_glu_w, c_glu_b,
                 w_br_a, w_br_b, w_br_c, w_o, ln_g, ln_b):
    bn, s, d = x.shape
    t = bn * s
    xt = x.reshape(t, d)
    offsets = [0] + np.cumsum(SPLITS).tolist()
    w_cols = [w_in[:, offsets[i]:offsets[i + 1]] for i in range(len(SPLITS))]
    (w_aq, w_ckv, w_iq, w_ik, w_iw, w_bqkv, w_bz, w_ba, w_bb, w_cu, w_gate) = w_cols
    w_ik_pad = jnp.pad(w_ik, ((0, 0), (0, LANE - IDX_DIM)))
    h = matmul(xt, jnp.concatenate([w_gate, w_ckv, w_ik_pad, w_bqkv, w_bz, w_cu], axis=1))
    o = np.cumsum([0, 3 * D_MODEL, A_KV_RANK, LANE, 3 * B_WIDTH, B_WIDTH, C_WIDTH]).tolist()
    idx_k = h[:, o[2]:o[2] + IDX_DIM].astype(BF16).reshape(bn, s, IDX_DIM)
    c_u = h[:, o[5]:o[6]].reshape(bn, s, C_WIDTH)
    qiq_t = matmul_nt(jnp.concatenate([w_aq * A_HEAD_DIM ** -0.5, w_iq], axis=1).T, xt, bn, s)
    scal_t = matmul_nt(jnp.concatenate([w_ba, w_bb, w_iw], axis=1).T, xt, bn, s, out_dtype=F32)
    k, vt = kv_project(h, o[1] // LANE, a_kv_norm, a_kv_up, bn, s)
    ya = dsa_attention(qiq_t, scal_t, idx_k, k, vt, _rel_bias_tiles(rel_bias), bn, s)
    yb = gated_deltanet(h, o[3] // LANE, scal_t, b_conv, b_a_log, b_dt_bias, b_out_norm, bn, s)
    yc = s5_branch(c_u, c_lambda_re, c_lambda_im, c_log_dt, c_b_re, c_b_im, c_c_re, c_c_im, c_d, c_glu_w, c_glu_b)
    return merge_branches(ya.reshape(t, A_WIDTH), yb, yc, h, w_br_a, w_br_b, w_br_c, w_o, xt, ln_g, ln_b)


def _merge_kernel(ya_ref, yb_ref, yc_ref, ga_ref, gb_ref, gc_ref, wa_ref, wb_ref, wc_ref, wo_ref, x_ref, g_ref, b_ref, o_ref):
    branch = lambda y_ref, w_ref, gate_ref: jax.nn.sigmoid(gate_ref[...]) * jnp.dot(y_ref[...], w_ref[...], preferred_element_type=F32)
    merged = branch(ya_ref, wa_ref, ga_ref) + branch(yb_ref, wb_ref, gb_ref) + branch(yc_ref, wc_ref, gc_ref)
    mix = jnp.dot(merged.astype(BF16), wo_ref[...], preferred_element_type=F32)
    z = DEEPNORM_ALPHA * x_ref[...] + mix
    mu = jnp.mean(z, -1, keepdims=True)
    zc = z - mu
    var = jnp.mean(zc * zc, -1, keepdims=True)
    o_ref[...] = zc * lax.rsqrt(var + LN_EPS) * g_ref[...] + b_ref[...]


def merge_branches(ya, yb, yc, h, w_br_a, w_br_b, w_br_c, w_o, x, ln_g, ln_b):
    t, d = x.shape
    tm = _pick_tile(t, (512, 256, 128, 8))
    row = lambda width, col=0: pl.BlockSpec((tm, width), lambda i: (i, col))
    full = lambda a: pl.BlockSpec(a.shape, lambda i: (0, 0))
    ws = [w.astype(BF16) for w in (w_br_a, w_br_b, w_br_c, w_o)]
    return pl.pallas_call(
        _merge_kernel,
        grid=(t // tm,),
        in_specs=[row(A_WIDTH), row(B_WIDTH), row(C_WIDTH), row(d, 0), row(d, 1), row(d, 2)] + [full(w) for w in ws]
                 + [row(d), pl.BlockSpec((1, d), lambda i: (0, 0)), pl.BlockSpec((1, d), lambda i: (0, 0))],
        out_specs=row(d),
        out_shape=jax.ShapeDtypeStruct((t, d), F32),
        compiler_params=pltpu.CompilerParams(dimension_semantics=("arbitrary",), vmem_limit_bytes=VMEM_LIMIT_BYTES),
        name="merge_branches",
    )(ya, yb, yc, h, h, h, *ws, x, ln_g.reshape(1, d), ln_b.reshape(1, d))


ROUTER_TILE = 512


def _router_kernel(x_ref, rwt_ref, bias_ref, tri_ref, idx_ref, gate_ref, pos_ref, cnt_ref, carry_ref):
    @pl.when(pl.program_id(0) == 0)
    def _():
        carry_ref[...] = jnp.zeros_like(carry_ref)

    logits = lax.dot_general(rwt_ref[...], x_ref[...].astype(BF16), (((1,), (1,)), ((), ())), preferred_element_type=F32)
    scores = jax.nn.sigmoid(logits)
    remaining = scores + bias_ref[...]
    expert = lax.broadcasted_iota(jnp.int32, scores.shape, 0)
    picks = []
    for _ in range(TOP_K):
        best = jnp.max(remaining, axis=0, keepdims=True)
        first = jnp.min(jnp.where(remaining == best, expert, N_EXPERTS), axis=0, keepdims=True)
        pick = expert == first
        picks.append((first, pick))
        remaining = jnp.where(pick, -jnp.inf, remaining)
    chosen = picks[0][1]
    for _, pick in picks[1:]:
        chosen = chosen | pick
    total = jnp.sum(jnp.where(chosen, scores, 0.0), axis=0, keepdims=True)
    prefix = jnp.dot(jnp.where(chosen, 1.0, 0.0).astype(BF16), tri_ref[...], preferred_element_type=F32)
    rank = carry_ref[...] + prefix.astype(jnp.int32) - 1
    carry_ref[...] = carry_ref[...] + prefix[:, -1:].astype(jnp.int32)
    cnt_ref[...] = carry_ref[...]
    for k, (first, pick) in enumerate(picks):
        idx_ref[k:k + 1, :] = first
        gate_ref[k:k + 1, :] = jnp.sum(jnp.where(pick, scores, 0.0), axis=0, keepdims=True) / total * ROUTED_SCALE
        pos_ref[k:k + 1, :] = jnp.sum(jnp.where(pick, rank, 0), axis=0, keepdims=True)


def moe_router(xt, router_w, router_bias):
    t, d = xt.shape
    tile = ROUTER_TILE
    tri = (jnp.arange(tile)[:, None] <= jnp.arange(tile)[None, :]).astype(BF16)
    row = pl.BlockSpec((TOP_K, tile), lambda i: (0, i))
    return pl.pallas_call(
        _router_kernel,
        grid=(t // tile,),
        in_specs=[pl.BlockSpec((tile, d), lambda i: (i, 0)), pl.BlockSpec((N_EXPERTS, d), lambda i: (0, 0)),
                  pl.BlockSpec((N_EXPERTS, 1), lambda i: (0, 0)), pl.BlockSpec((tile, tile), lambda i: (0, 0))],
        out_specs=[row, row, row, pl.BlockSpec((N_EXPERTS, 1), lambda i: (0, 0))],
        out_shape=[jax.ShapeDtypeStruct((TOP_K, t), jnp.int32), jax.ShapeDtypeStruct((TOP_K, t), F32),
                   jax.ShapeDtypeStruct((TOP_K, t), jnp.int32), jax.ShapeDtypeStruct((N_EXPERTS, 1), jnp.int32)],
        scratch_shapes=[pltpu.VMEM((N_EXPERTS, 1), jnp.int32)],
        compiler_params=pltpu.CompilerParams(dimension_semantics=("arbitrary",), vmem_limit_bytes=VMEM_LIMIT_BYTES),
        name="moe_router",
    )(xt, router_w.T.astype(BF16), router_bias.reshape(N_EXPERTS, 1), tri)


def _expert_kernel(blk_expert_ref, blk_rows_ref, x_ref, wgu_ref, wdn_ref, o_ref):
    n_valid = blk_rows_ref[pl.program_id(0)]

    @pl.when(n_valid > 0)
    def _():
        row = lax.broadcasted_iota(jnp.int32, x_ref.shape, 0)
        x = jnp.where(row < n_valid, x_ref[...], 0.0).astype(BF16)
        h = jnp.dot(x, wgu_ref[0], preferred_element_type=F32)
        hg, hu = h[:, :D_EXPERT], h[:, D_EXPERT:]
        act = (hg * jax.nn.sigmoid(hg) * hu).astype(BF16)
        o_ref[...] = jnp.dot(act, wdn_ref[0], preferred_element_type=F32)

    @pl.when(n_valid <= 0)
    def _():
        o_ref[...] = jnp.zeros_like(o_ref)


def expert_ffn(xs, blk_expert, blk_rows, w_gu, w_dn):
    n_pad, d = xs.shape
    n_blk = n_pad // MOE_BLOCK
    return pl.pallas_call(
        _expert_kernel,
        grid_spec=pltpu.PrefetchScalarGridSpec(
            num_scalar_prefetch=2,
            grid=(n_blk,),
            in_specs=[pl.BlockSpec((MOE_BLOCK, d), lambda b, be, br: (b, 0)),
                      pl.BlockSpec((1, d, 2 * D_EXPERT), lambda b, be, br: (be[b], 0, 0)),
                      pl.BlockSpec((1, D_EXPERT, d), lambda b, be, br: (be[b], 0, 0))],
            out_specs=pl.BlockSpec((MOE_BLOCK, d), lambda b, be, br: (b, 0))),
        out_shape=jax.ShapeDtypeStruct((n_pad, d), F32),
        compiler_params=pltpu.CompilerParams(dimension_semantics=("arbitrary",), vmem_limit_bytes=VMEM_LIMIT_BYTES),
        name="expert_ffn",
    )(blk_expert, blk_rows, xs, w_gu, w_dn)


DISPATCH_TILE = 256
COMBINE_TILE = 128


def _rows_kernel(pad_start_ref, idx_ref, pos_ref, o_ref):
    idx = idx_ref[...]
    row = pos_ref[...]
    for e in range(N_EXPERTS):
        row = row + jnp.where(idx == e, pad_start_ref[e], 0)
    o_ref[...] = row


def moe_rows(pad_start, idx, pos):
    k, t = idx.shape
    tile = _pick_tile(t, (4096, 2048, 1024, 512, 256, 128))
    blk = pl.BlockSpec((k, tile), lambda i, ps: (0, i))
    return pl.pallas_call(
        _rows_kernel,
        grid_spec=pltpu.PrefetchScalarGridSpec(num_scalar_prefetch=1, grid=(t // tile,), in_specs=[blk, blk], out_specs=blk),
        out_shape=jax.ShapeDtypeStruct((k, t), jnp.int32),
        compiler_params=pltpu.CompilerParams(dimension_semantics=("arbitrary",)),
        name="moe_rows",
    )(pad_start, idx, pos)


def _dispatch_kernel(dest_ref, x_ref, xs_hbm, sem):
    def issue(r, carry):
        for k in range(TOP_K):
            row = dest_ref[0, 0, r * TOP_K + k]
            pltpu.make_async_copy(x_ref.at[pl.ds(r, 1)], xs_hbm.at[pl.ds(row, 1)], sem).start(priority=k % 2)
        return carry

    lax.fori_loop(0, DISPATCH_TILE, issue, 0)
    n_rows = DISPATCH_TILE * TOP_K
    pltpu.make_async_copy(xs_hbm.at[pl.ds(0, n_rows)], xs_hbm.at[pl.ds(0, n_rows)], sem).wait()


def moe_dispatch(xt, dest_tok_major, n_pad):
    t, d = xt.shape
    n_tiles = t // DISPATCH_TILE
    return pl.pallas_call(
        _dispatch_kernel,
        grid=(n_tiles,),
        in_specs=[pl.BlockSpec((1, 1, DISPATCH_TILE * TOP_K), lambda i: (i, 0, 0), memory_space=pltpu.SMEM),
                  pl.BlockSpec((DISPATCH_TILE, d), lambda i: (i, 0))],
        out_specs=pl.BlockSpec(memory_space=pl.ANY),
        scratch_shapes=[pltpu.SemaphoreType.DMA(())],
        out_shape=jax.ShapeDtypeStruct((n_pad, d), xt.dtype),
        compiler_params=pltpu.CompilerParams(dimension_semantics=("arbitrary",), has_side_effects=True),
        name="moe_dispatch",
    )(dest_tok_major.reshape(n_tiles, 1, -1), xt)


def _combine_kernel(dest_ref, dest_next_ref, ys_hbm, gate_ref, x_ref, wgu_ref, wdn_ref, g_ref, b_ref, o_ref, buf_ref, sem):
    i = pl.program_id(0)
    n = pl.num_programs(0)
    slot = lax.rem(i, 2)

    def gather(d_ref, into):
        def issue(r, carry):
            for k in range(TOP_K):
                row = d_ref[0, 0, r * TOP_K + k]
                pltpu.make_async_copy(ys_hbm.at[pl.ds(row, 1)], buf_ref.at[into, k, pl.ds(r, 1)],
                                      sem.at[into]).start(priority=k % 2)
            return carry
        lax.fori_loop(0, COMBINE_TILE, issue, 0)

    @pl.when(i == 0)
    def _():
        gather(dest_ref, 0)

    @pl.when(i + 1 < n)
    def _():
        gather(dest_next_ref, 1 - slot)

    x = x_ref[...]
    h = jnp.dot(x.astype(BF16), wgu_ref[...], preferred_element_type=F32)
    hg, hu = h[:, :D_SHARED], h[:, D_SHARED:]
    acc = DEEPNORM_ALPHA * x + jnp.dot((hg * jax.nn.sigmoid(hg) * hu).astype(BF16), wdn_ref[...], preferred_element_type=F32)
    pltpu.make_async_copy(buf_ref.at[slot], buf_ref.at[slot], sem.at[slot]).wait()
    gate = gate_ref[...]
    for k in range(TOP_K):
        acc = acc + buf_ref[slot, k] * gate[:, k:k + 1]
    mu = jnp.mean(acc, -1, keepdims=True)
    zc = acc - mu
    var = jnp.mean(zc * zc, -1, keepdims=True)
    o_ref[...] = zc * lax.rsqrt(var + LN_EPS) * g_ref[...] + b_ref[...]


def moe_combine(ys, dest_tok_major, gate_tok_major, xt, sh_w_gu, sh_w_down, ln_g, ln_b):
    t, d = xt.shape
    tm = COMBINE_TILE
    n_tiles = t // tm
    dest3 = dest_tok_major.reshape(n_tiles, 1, -1)
    row = pl.BlockSpec((tm, d), lambda i: (i, 0))
    full = lambda shape: pl.BlockSpec(shape, lambda i: (0, 0))
    this = pl.BlockSpec((1, 1, tm * TOP_K), lambda i: (i, 0, 0), memory_space=pltpu.SMEM)
    nxt = pl.BlockSpec((1, 1, tm * TOP_K), lambda i: (jnp.minimum(i + 1, n_tiles - 1), 0, 0), memory_space=pltpu.SMEM)
    return pl.pallas_call(
        _combine_kernel,
        grid=(n_tiles,),
        in_specs=[this, nxt, pl.BlockSpec(memory_space=pl.ANY), pl.BlockSpec((tm, TOP_K), lambda i: (i, 0)), row,
                  full((d, 2 * D_SHARED)), full((D_SHARED, d)), full((1, d)), full((1, d))],
        out_specs=row,
        scratch_shapes=[pltpu.VMEM((2, TOP_K, tm, d), F32), pltpu.SemaphoreType.DMA((2,))],
        out_shape=jax.ShapeDtypeStruct((t, d), F32),
        compiler_params=pltpu.CompilerParams(dimension_semantics=("arbitrary",), vmem_limit_bytes=VMEM_LIMIT_BYTES),
        name="moe_combine",
    )(dest3, dest3, ys, gate_tok_major, xt, sh_w_gu.astype(BF16), sh_w_down.astype(BF16),
      ln_g.reshape(1, d), ln_b.reshape(1, d))


def moe_ffn(xt, router_w, router_bias, exp_w_gu, exp_w_down, sh_w_gu, sh_w_down, ln_g, ln_b):
    n_tok, d = xt.shape
    idx, gate, pos, counts = moe_router(xt, router_w, router_bias)
    counts = counts[:, 0]
    padded = (counts + MOE_BLOCK - 1) // MOE_BLOCK * MOE_BLOCK
    pad_ends = jnp.cumsum(padded)
    pad_start = pad_ends - padded
    n_blk = n_tok * TOP_K // MOE_BLOCK + N_EXPERTS
    n_pad = n_blk * MOE_BLOCK
    blk_lo = jnp.arange(n_blk, dtype=jnp.int32) * MOE_BLOCK
    owner = (pad_start[None, :] <= blk_lo[:, None]) & (blk_lo[:, None] < pad_ends[None, :])
    blk_expert = jnp.minimum(jnp.sum(pad_ends[None, :] <= blk_lo[:, None], axis=1), N_EXPERTS - 1).astype(jnp.int32)
    blk_rows = jnp.clip(jnp.sum(jnp.where(owner, (pad_start + counts)[None, :] - blk_lo[:, None], 0), axis=1),
                        0, MOE_BLOCK).astype(jnp.int32)
    dest = moe_rows(pad_start.astype(jnp.int32), idx, pos).T.reshape(-1)
    xs = moe_dispatch(xt, dest, n_pad)
    ys = expert_ffn(xs, blk_expert, blk_rows, exp_w_gu.astype(BF16), exp_w_down.astype(BF16))
    return moe_combine(ys, dest, gate.T, xt, sh_w_gu, sh_w_down, ln_g, ln_b)


def kernel(x, rel_bias, w_in, a_kv_norm, a_kv_up, b_conv, b_a_log, b_dt_bias, b_out_norm, c_lambda_re, c_lambda_im, c_log_dt, c_b_re, c_b_im, c_c_re, c_c_im, c_d, c_glu_w, c_glu_b, w_br_a, w_br_b, w_br_c, w_o, ln1_g, ln1_b, router_w, router_bias, exp_w_gu, exp_w_down, sh_w_gu, sh_w_down, ln2_g, ln2_b):
    bn, s, d = x.shape
    t = bn * s
    for i in range(w_in.shape[0]):
        xt = hybrid_mixer(x, rel_bias, w_in[i], a_kv_norm[i], a_kv_up[i], b_conv[i], b_a_log[i], b_dt_bias[i],
                          b_out_norm[i], c_lambda_re[i], c_lambda_im[i], c_log_dt[i], c_b_re[i], c_b_im[i],
                          c_c_re[i], c_c_im[i], c_d[i], c_glu_w[i], c_glu_b[i], w_br_a[i], w_br_b[i], w_br_c[i], w_o[i],
                          ln1_g[i], ln1_b[i])
        x = moe_ffn(xt, router_w[i], router_bias[i], exp_w_gu[i], exp_w_down[i], sh_w_gu[i], sh_w_down[i],
                    ln2_g[i], ln2_b[i]).reshape(bn, s, d)
    return x
```

```python
import functools
import math

import jax
import jax.numpy as jnp
import numpy as np
from jax import lax
from jax.experimental import pallas as pl
from jax.experimental.pallas import tpu as pltpu

F32 = jnp.float32
BF16 = jnp.bfloat16

D_MODEL = 1024
DEPTH = 4
A_HEADS = 8
A_HEAD_DIM = 64
A_WIDTH = A_HEADS * A_HEAD_DIM
A_KV_RANK = 128
IDX_HEADS = 8
IDX_DIM = 64
TOPK_MAX = 256
Q_BLOCK = 128
REL_BUCKETS = 32
REL_MAX_DIST = 128
B_HEADS = 4
B_HEAD_DIM = 128
B_WIDTH = B_HEADS * B_HEAD_DIM
CONV_WIDTH = 4
CHUNK = 64
C_WIDTH = 512
C_GROUP = 16
C_GROUPS = C_WIDTH // C_GROUP
C_STATE = 64
N_EXPERTS = 64
TOP_K = 8
D_EXPERT = 256
D_SHARED = 256
ROUTED_SCALE = 2.5
MOE_BLOCK = 512
SPLITS = (A_WIDTH, A_KV_RANK, IDX_HEADS * IDX_DIM, IDX_DIM, IDX_HEADS, 3 * B_WIDTH, B_WIDTH, B_HEADS, B_HEADS, C_WIDTH, 3 * D_MODEL)
DEEPNORM_ALPHA = (2 * DEPTH) ** 0.25
LN_EPS = 1e-5
RMS_EPS = 1e-6
NEG_INF = -1e30

VMEM_LIMIT_BYTES = 48 * 1024 * 1024
LANE = 128
PROJ_TILE_N = 1024


def _pick_tile(n, candidates):
    for c in candidates:
        if n % c == 0:
            return c
    return n


def _mm_kernel(a_ref, b_ref, o_ref):
    o_ref[...] = jnp.dot(a_ref[...].astype(BF16), b_ref[...], preferred_element_type=F32).astype(o_ref.dtype)


def matmul(a, b, out_dtype=F32):
    m, k = a.shape
    _, n = b.shape
    tm = _pick_tile(m, (1024, 512, 256, 128, 8))
    tn = _pick_tile(n, (1024, 768, 512, 384, 256, 128))
    return pl.pallas_call(
        _mm_kernel,
        grid=(m // tm, n // tn),
        in_specs=[pl.BlockSpec((tm, k), lambda i, j: (i, 0)), pl.BlockSpec((k, tn), lambda i, j: (0, j))],
        out_specs=pl.BlockSpec((tm, tn), lambda i, j: (i, j)),
        out_shape=jax.ShapeDtypeStruct((m, n), out_dtype),
        compiler_params=pltpu.CompilerParams(dimension_semantics=("arbitrary", "arbitrary"), vmem_limit_bytes=VMEM_LIMIT_BYTES),
        name="matmul",
    )(a, b.astype(BF16))


def rel_bucket(dist):
    n_exact = REL_BUCKETS // 2
    d = jnp.maximum(dist, 1).astype(F32)
    large = n_exact + (jnp.log(d / n_exact) / math.log(REL_MAX_DIST / n_exact) * (REL_BUCKETS - n_exact)).astype(jnp.int32)
    return jnp.where(dist < n_exact, dist, jnp.minimum(large, REL_BUCKETS - 1))


KEY_BLOCK = 128
DSA_CHUNK = 512
BLOCKS_PER_CHUNK = DSA_CHUNK // KEY_BLOCK
HEAD_PAIRS = A_HEADS // 2
PAIR_WIDTH = 2 * A_HEAD_DIM
INT_MIN = -2 ** 31
IDX_W_SCALE = IDX_HEADS ** -0.5 * IDX_DIM ** -0.5


def _kv_kernel(c_ref, g_ref, wk_ref, wvt_ref, k_ref, vt_ref):
    c = c_ref[...]
    cn = (c * lax.rsqrt(jnp.mean(c * c, -1, keepdims=True) + RMS_EPS) * g_ref[...]).astype(BF16)
    k_ref[...] = jnp.dot(cn, wk_ref[...], preferred_element_type=F32).astype(BF16)
    vt = lax.dot_general(wvt_ref[...], cn, (((1,), (1,)), ((), ())), preferred_element_type=F32).astype(BF16)
    for i in range(vt_ref.shape[1]):
        vt_ref[0, i] = vt[:, i * KEY_BLOCK:(i + 1) * KEY_BLOCK]


def kv_project(c_kv, col, g, w_kv_up, bn, s):
    ts = 512
    w = w_kv_up.astype(BF16)
    return pl.pallas_call(
        _kv_kernel,
        grid=(bn, s // ts),
        in_specs=[pl.BlockSpec((ts, A_KV_RANK), lambda b, i: (b * (s // ts) + i, col)),
                  pl.BlockSpec((1, A_KV_RANK), lambda b, i: (0, 0)),
                  pl.BlockSpec((A_KV_RANK, A_WIDTH), lambda b, i: (0, 0)),
                  pl.BlockSpec((A_WIDTH, A_KV_RANK), lambda b, i: (0, 0))],
        out_specs=[pl.BlockSpec((None, ts, A_WIDTH), lambda b, i: (b, i, 0)),
                   pl.BlockSpec((1, ts // KEY_BLOCK, A_WIDTH, KEY_BLOCK), lambda b, i: (b, i, 0, 0))],
        out_shape=[jax.ShapeDtypeStruct((bn, s, A_WIDTH), BF16),
                   jax.ShapeDtypeStruct((bn, s // KEY_BLOCK, A_WIDTH, KEY_BLOCK), BF16)],
        compiler_params=pltpu.CompilerParams(dimension_semantics=("arbitrary", "arbitrary"), vmem_limit_bytes=VMEM_LIMIT_BYTES),
        name="kv_project",
    )(c_kv, g.reshape(1, A_KV_RANK), w[:, :A_WIDTH], w[:, A_WIDTH:].T)


def _mm_nt_kernel(wt_ref, x_ref, o_ref):
    o_ref[0] = lax.dot_general(wt_ref[...], x_ref[...].astype(BF16), (((1,), (1,)), ((), ())),
                               preferred_element_type=F32).astype(o_ref.dtype)


def matmul_nt(wt, x, bn, s, out_dtype=BF16):
    n, k = wt.shape
    ts = 512
    return pl.pallas_call(
        _mm_nt_kernel,
        grid=(bn, s // ts),
        in_specs=[pl.BlockSpec((n, k), lambda b, i: (0, 0)),
                  pl.BlockSpec((ts, k), lambda b, i: (b * (s // ts) + i, 0))],
        out_specs=pl.BlockSpec((1, n, ts), lambda b, i: (b, 0, i)),
        out_shape=jax.ShapeDtypeStruct((bn, n, s), out_dtype),
        compiler_params=pltpu.CompilerParams(dimension_semantics=("arbitrary", "arbitrary"), vmem_limit_bytes=VMEM_LIMIT_BYTES),
        name="matmul_nt",
    )(wt.astype(BF16), x)


def _sortable_key(score):
    bits = pltpu.bitcast(score, jnp.int32)
    return bits ^ (lax.shift_right_arithmetic(bits, 31) & jnp.int32(0x7FFFFFFF))


def _dsa_kernel(qt_ref, iqt_ref, wt_ref, kidx_ref, k_ref, vt_ref, bias_ref, o_ref,
                skey_ref, widx_ref, wq_ref, acc_ref, m_ref, l_ref, *, k_sel):
    j = pl.program_id(1)
    n_chunks = (j + BLOCKS_PER_CHUNK) // BLOCKS_PER_CHUNK
    q_pos = j * Q_BLOCK + lax.broadcasted_iota(jnp.int32, (1, Q_BLOCK), 1)

    for h in range(IDX_HEADS):
        widx_ref[:, h * Q_BLOCK:(h + 1) * Q_BLOCK] = iqt_ref[0, h * IDX_DIM:(h + 1) * IDX_DIM, :]
    wq_ref[...] = jnp.zeros_like(wq_ref)
    for p in range(HEAD_PAIRS):
        wq_ref[p, 0:A_HEAD_DIM, 0:Q_BLOCK] = qt_ref[0, (2 * p) * A_HEAD_DIM:(2 * p + 1) * A_HEAD_DIM, :]
        wq_ref[p, A_HEAD_DIM:PAIR_WIDTH, Q_BLOCK:2 * Q_BLOCK] = qt_ref[0, (2 * p + 1) * A_HEAD_DIM:(2 * p + 2) * A_HEAD_DIM, :]

    w_rows = [wt_ref[0, h:h + 1, :] * IDX_W_SCALE for h in range(IDX_HEADS)]

    def idx_chunk(c, carry):
        start = pl.multiple_of(c * DSA_CHUNK, DSA_CHUNK)
        kc = kidx_ref[0, pl.ds(start, DSA_CHUNK), :]
        score = jnp.zeros((DSA_CHUNK, Q_BLOCK), F32)
        for p in range(IDX_HEADS // 2):
            z = jnp.dot(kc, widx_ref[:, p * 2 * Q_BLOCK:(p + 1) * 2 * Q_BLOCK], preferred_element_type=F32)
            score = score + jnp.maximum(z[:, :Q_BLOCK], 0.0) * w_rows[2 * p]
            score = score + jnp.maximum(z[:, Q_BLOCK:], 0.0) * w_rows[2 * p + 1]
        pos = start + lax.broadcasted_iota(jnp.int32, (DSA_CHUNK, Q_BLOCK), 0)
        skey_ref[pl.ds(start, DSA_CHUNK), :] = jnp.where(pos <= q_pos, _sortable_key(score), INT_MIN)
        return carry

    lax.fori_loop(0, n_chunks, idx_chunk, 0)

    def count_ge(cand):
        def body(c, cnt):
            start = pl.multiple_of(c * DSA_CHUNK, DSA_CHUNK)
            hit = (skey_ref[pl.ds(start, DSA_CHUNK), :] >= cand).astype(jnp.int32)
            return cnt + jnp.sum(hit.reshape(DSA_CHUNK // 8, 8, Q_BLOCK), axis=0)
        cnt = lax.fori_loop(0, n_chunks, body, jnp.zeros((8, Q_BLOCK), jnp.int32))
        return jnp.sum(cnt, axis=0, keepdims=True)

    def bit_step(i, carry):
        thr_biased, n_ge = carry
        cand_biased = thr_biased | lax.shift_left(jnp.int32(1), 31 - i)
        cnt = count_ge(cand_biased ^ INT_MIN)
        keep = cnt >= k_sel
        return jnp.where(keep, cand_biased, thr_biased), jnp.where(keep, cnt, n_ge)

    thr, n_ge = lax.fori_loop(0, 32, bit_step, (jnp.zeros((1, Q_BLOCK), jnp.int32),
                                                jnp.full((1, Q_BLOCK), k_sel, jnp.int32)))
    thr = thr ^ INT_MIN

    tied = (n_ge > k_sel) & (thr > INT_MIN)

    @pl.when(jnp.max(tied.astype(jnp.int32)) > 0)
    def _():
        need = k_sel - count_ge(thr + 1)

        def count_tied_upto(limit):
            def body(c, cnt):
                start = pl.multiple_of(c * DSA_CHUNK, DSA_CHUNK)
                pos = start + lax.broadcasted_iota(jnp.int32, (DSA_CHUNK, Q_BLOCK), 0)
                hit = jnp.where(skey_ref[pl.ds(start, DSA_CHUNK), :] == thr, (pos <= limit).astype(jnp.int32), 0)
                return cnt + jnp.sum(hit.reshape(DSA_CHUNK // 8, 8, Q_BLOCK), axis=0)
            cnt = lax.fori_loop(0, n_chunks, body, jnp.zeros((8, Q_BLOCK), jnp.int32))
            return jnp.sum(cnt, axis=0, keepdims=True)

        def pos_step(i, lim):
            cand = lim - lax.shift_left(jnp.int32(1), 30 - i)
            return jnp.where(count_tied_upto(cand) >= need, cand, lim)

        limit = lax.fori_loop(0, 31, pos_step, jnp.full((1, Q_BLOCK), 2 ** 31 - 1, jnp.int32))

        def demote(c, carry):
            start = pl.multiple_of(c * DSA_CHUNK, DSA_CHUNK)
            pos = start + lax.broadcasted_iota(jnp.int32, (DSA_CHUNK, Q_BLOCK), 0)
            sk = skey_ref[pl.ds(start, DSA_CHUNK), :]
            skey_ref[pl.ds(start, DSA_CHUNK), :] = jnp.where(tied & (sk == thr) & (pos > limit), thr - 1, sk)
            return carry

        lax.fori_loop(0, n_chunks, demote, 0)

    m_ref[...] = jnp.full_like(m_ref, NEG_INF)
    l_ref[...] = jnp.zeros_like(l_ref)
    acc_ref[...] = jnp.zeros_like(acc_ref)

    def attend(s_list, vt_of, n_blocks):
        for p in range(HEAD_PAIRS):
            s = s_list[p]
            m_old = m_ref[p]
            m_new = jnp.maximum(m_old, jnp.max(s, axis=0, keepdims=True))
            alpha = jnp.exp(m_old - m_new)
            pexp = jnp.exp(s - m_new)
            l_ref[p] = l_ref[p] * alpha + jnp.sum(pexp, axis=0, keepdims=True)
            pb = pexp.astype(BF16)
            pv = jnp.dot(vt_of(p, 0), pb[0:KEY_BLOCK], preferred_element_type=F32)
            for i in range(1, n_blocks):
                pv = pv + jnp.dot(vt_of(p, i), pb[i * KEY_BLOCK:(i + 1) * KEY_BLOCK], preferred_element_type=F32)
            acc_ref[p] = acc_ref[p] * alpha + pv
            m_ref[p] = m_new

    far_end = (j - 1) * KEY_BLOCK
    n_far = (jnp.maximum(j - 1, 0) + BLOCKS_PER_CHUNK - 1) // BLOCKS_PER_CHUNK

    def far_chunk(c, carry):
        start = pl.multiple_of(c * DSA_CHUNK, DSA_CHUNK)
        pos = start + lax.broadcasted_iota(jnp.int32, (DSA_CHUNK, Q_BLOCK), 0)
        sel = (skey_ref[pl.ds(start, DSA_CHUNK), :] >= thr) & (pos < far_end)
        madd = jnp.where(sel, 0.0, NEG_INF)
        madd2 = jnp.concatenate([madd, madd], axis=1)
        kc = k_ref[0, pl.ds(start, DSA_CHUNK), :]
        s_list = [jnp.dot(kc[:, p * PAIR_WIDTH:(p + 1) * PAIR_WIDTH], wq_ref[p], preferred_element_type=F32) + madd2
                  for p in range(HEAD_PAIRS)]
        blk0 = c * BLOCKS_PER_CHUNK
        attend(s_list, lambda p, i: vt_ref[0, blk0 + i, p * PAIR_WIDTH:(p + 1) * PAIR_WIDTH, :], BLOCKS_PER_CHUNK)
        return carry

    lax.fori_loop(0, n_far, far_chunk, 0)

    def near_block(kb, delta):
        start = pl.multiple_of(kb * KEY_BLOCK, KEY_BLOCK)
        pos = start + lax.broadcasted_iota(jnp.int32, (KEY_BLOCK, Q_BLOCK), 0)
        sel = (skey_ref[pl.ds(start, KEY_BLOCK), :] >= thr) & (pos <= q_pos)
        madd = jnp.where(sel, 0.0, NEG_INF)
        madd2 = jnp.concatenate([madd, madd], axis=1)
        kc = k_ref[0, pl.ds(start, KEY_BLOCK), :]
        s_list = [jnp.dot(kc[:, p * PAIR_WIDTH:(p + 1) * PAIR_WIDTH], wq_ref[p], preferred_element_type=F32)
                  + bias_ref[delta, p] + madd2 for p in range(HEAD_PAIRS)]
        attend(s_list, lambda p, i: vt_ref[0, kb, p * PAIR_WIDTH:(p + 1) * PAIR_WIDTH, :], 1)

    @pl.when(j > 0)
    def _():
        near_block(j - 1, 1)

    near_block(j, 0)

    outs = []
    for p in range(HEAD_PAIRS):
        o = acc_ref[p] / l_ref[p]
        outs.append(o[0:A_HEAD_DIM, 0:Q_BLOCK])
        outs.append(o[A_HEAD_DIM:PAIR_WIDTH, Q_BLOCK:2 * Q_BLOCK])
    o_ref[0] = jnp.concatenate(outs, axis=0).T.astype(o_ref.dtype)


def _rel_bias_tiles(rel_bias):
    tab = rel_bias[rel_bucket(jnp.arange(2 * Q_BLOCK, dtype=jnp.int32))] - rel_bias[REL_BUCKETS - 1]
    key = jnp.arange(KEY_BLOCK)[:, None]
    qry = jnp.arange(Q_BLOCK)[None, :]
    tiles = []
    for delta in (0, 1):
        dist = jnp.maximum(delta * Q_BLOCK + qry - key, 0)
        t = jnp.moveaxis(tab[dist], 2, 0)
        t = t.reshape(HEAD_PAIRS, 2, KEY_BLOCK, Q_BLOCK)
        tiles.append(jnp.concatenate([t[:, 0], t[:, 1]], axis=-1))
    return jnp.stack(tiles)


def dsa_attention(qiq_t, wt, k_idx, k, vt, bias_tiles, bn, s):
    assert A_WIDTH == IDX_HEADS * IDX_DIM
    scal_rows = wt.shape[1]
    assert scal_rows % IDX_HEADS == 0
    k_sel = min(TOPK_MAX, s // 4)
    nb = s // Q_BLOCK
    s_pad = s + DSA_CHUNK
    return pl.pallas_call(
        functools.partial(_dsa_kernel, k_sel=k_sel),
        grid=(bn, nb),
        in_specs=[pl.BlockSpec((1, A_WIDTH, Q_BLOCK), lambda b, j: (b, 0, j)),
                  pl.BlockSpec((1, IDX_HEADS * IDX_DIM, Q_BLOCK), lambda b, j: (b, 1, j)),
                  pl.BlockSpec((1, IDX_HEADS, Q_BLOCK), lambda b, j: (b, scal_rows // IDX_HEADS - 1, j)),
                  pl.BlockSpec((1, s, IDX_DIM), lambda b, j: (b, 0, 0)),
                  pl.BlockSpec((1, s, A_WIDTH), lambda b, j: (b, 0, 0)),
                  pl.BlockSpec((1, s // KEY_BLOCK, A_WIDTH, KEY_BLOCK), lambda b, j: (b, 0, 0, 0)),
                  pl.BlockSpec((2, HEAD_PAIRS, KEY_BLOCK, 2 * Q_BLOCK), lambda b, j: (0, 0, 0, 0))],
        out_specs=pl.BlockSpec((1, Q_BLOCK, A_WIDTH), lambda b, j: (b, j, 0)),
        out_shape=jax.ShapeDtypeStruct((bn, s, A_WIDTH), BF16),
        scratch_shapes=[pltpu.VMEM((s_pad, Q_BLOCK), jnp.int32),
                        pltpu.VMEM((IDX_DIM, IDX_HEADS * Q_BLOCK), BF16),
                        pltpu.VMEM((HEAD_PAIRS, PAIR_WIDTH, 2 * Q_BLOCK), BF16),
                        pltpu.VMEM((HEAD_PAIRS, PAIR_WIDTH, 2 * Q_BLOCK), F32),
                        pltpu.VMEM((HEAD_PAIRS, 1, 2 * Q_BLOCK), F32),
                        pltpu.VMEM((HEAD_PAIRS, 1, 2 * Q_BLOCK), F32)],
        compiler_params=pltpu.CompilerParams(dimension_semantics=("arbitrary", "arbitrary"), vmem_limit_bytes=56 * 1024 * 1024),
        name="dsa_attention",
    )(qiq_t, qiq_t, wt, k_idx, k, vt, bias_tiles)


GDN_CHUNK = B_HEAD_DIM
GDN_BLOCK = 512
GDN_HALO = 8
NEUMANN_STEPS = 6


def _split_bf16(x):
    hi = x.astype(BF16)
    return hi, (x - hi.astype(F32)).astype(BF16)


def _dot_f32(a, b):
    a_hi, a_lo = _split_bf16(a)
    b_hi, b_lo = _split_bf16(b)
    d = lambda x, y: jnp.dot(x, y, preferred_element_type=F32)
    return d(a_hi, b_hi) + (d(a_hi, b_lo) + d(a_lo, b_hi))


def _bdot(a, b):
    return jnp.dot(a.astype(BF16), b.astype(BF16), preferred_element_type=F32)


def _bdot_nt(a, b):
    return lax.dot_general(a.astype(BF16), b.astype(BF16), (((1,), (1,)), ((), ())), preferred_element_type=F32)


def _bdot_tn(a, b):
    return lax.dot_general(a.astype(BF16), b.astype(BF16), (((0,), (0,)), ((), ())), preferred_element_type=F32)


def _softplus(x):
    return jnp.maximum(x, 0.0) + jnp.log(1.0 + jnp.exp(-jnp.abs(x)))


def _gdn_kernel(hq_ref, hk_ref, hv_ref, hz_ref, wq_ref, wk_ref, wv_ref, ab_ref, const_ref, gout_ref, tri_ref,
                o_ref, state_ref, xbuf_ref):
    head = pl.program_id(1)

    @pl.when(pl.program_id(2) == 0)
    def _():
        state_ref[...] = jnp.zeros_like(state_ref)
        xbuf_ref[:, 0:GDN_HALO, :] = jnp.zeros((3, GDN_HALO, B_HEAD_DIM), F32)

    def conv_silu(n, x_ref, w_ref):
        x = x_ref[...]
        xbuf_ref[n, GDN_HALO:, :] = x
        acc = xbuf_ref[n, GDN_HALO - CONV_WIDTH + 1:GDN_HALO - CONV_WIDTH + 1 + GDN_BLOCK, :] * w_ref[0:1, :]
        for j in range(1, CONV_WIDTH):
            lo = GDN_HALO - CONV_WIDTH + 1 + j
            acc = acc + xbuf_ref[n, lo:lo + GDN_BLOCK, :] * w_ref[j:j + 1, :]
        xbuf_ref[n, 0:GDN_HALO, :] = x[GDN_BLOCK - GDN_HALO:, :]
        return acc * jax.nn.sigmoid(acc)

    q = conv_silu(0, hq_ref, wq_ref)
    k = conv_silu(1, hk_ref, wk_ref)
    v = conv_silu(2, hv_ref, wv_ref)
    q = q * lax.rsqrt(jnp.sum(q * q, -1, keepdims=True) + RMS_EPS) * B_HEAD_DIM ** -0.5
    k = k * lax.rsqrt(jnp.sum(k * k, -1, keepdims=True) + RMS_EPS)

    a_row = ab_ref[0, pl.ds(head, 1), :]
    b_row = ab_ref[0, pl.ds(B_HEADS + head, 1), :]
    beta_row = jax.nn.sigmoid(b_row)
    g_row = const_ref[0, 0:1, :] * _softplus(a_row + const_ref[0, 1:2, :])
    gc_rows = _dot_f32(jnp.broadcast_to(g_row, (8, GDN_BLOCK)), tri_ref[...])[0:1, :]

    row_i = lax.broadcasted_iota(jnp.int32, (GDN_CHUNK, GDN_CHUNK), 0)
    col_j = lax.broadcasted_iota(jnp.int32, (GDN_CHUNK, GDN_CHUNK), 1)
    eye = jnp.where(row_i == col_j, 1.0, 0.0)
    z_all = hz_ref[...]
    chunks = []
    for c in range(GDN_BLOCK // GDN_CHUNK):
        sl = slice(c * GDN_CHUNK, (c + 1) * GDN_CHUNK)
        qc, kc, vc = q[sl], k[sl], v[sl]
        g_lane = jnp.broadcast_to(gc_rows[:, sl], (GDN_CHUNK, GDN_CHUNK))
        g_sub = g_lane.T
        beta_sub = jnp.broadcast_to(beta_row[:, sl], (GDN_CHUNK, GDN_CHUNK)).T
        g_end = jnp.broadcast_to(g_lane[:, GDN_CHUNK - 1:GDN_CHUNK], (GDN_CHUNK, GDN_CHUNK))
        decay = jnp.exp(jnp.where(row_i >= col_j, g_sub - g_lane, -jnp.inf))
        kb = kc * beta_sub
        x = -jnp.where(row_i > col_j, _bdot_nt(kb, kc) * decay, 0.0)
        chunks.append(dict(sl=sl, qc=qc, kc=kc, vc=vc, g_sub=g_sub, beta_sub=beta_sub, g_end=g_end, decay=decay,
                           kb=kb, x=x, t_inv=eye + x))
    for _ in range(NEUMANN_STEPS):
        for ch in chunks:
            ch['x'] = _dot_f32(ch['x'], ch['x'])
        for ch in chunks:
            ch['t_inv'] = ch['t_inv'] + _bdot(ch['t_inv'], ch['x'])
    for ch in chunks:
        g_exp = jnp.exp(ch['g_sub'])
        ch['u'] = _bdot(ch['t_inv'], ch['vc'] * ch['beta_sub'])
        ch['w'] = _bdot(ch['t_inv'], ch['kb'] * g_exp)
        ch['qk'] = _bdot_nt(ch['qc'], ch['kc']) * ch['decay']
        ch['q_dec'] = ch['qc'] * g_exp
        ch['k_dec'] = ch['kc'] * jnp.exp(ch['g_end'] - ch['g_sub'])
    for ch in chunks:
        state = state_ref[...]
        v_new = ch['u'] - _bdot(ch['w'], state)
        o = _bdot(ch['q_dec'], state) + _bdot(ch['qk'], v_new)
        state_ref[...] = state * jnp.exp(ch['g_end']) + _bdot_tn(ch['k_dec'], v_new)
        o = o * lax.rsqrt(jnp.mean(o * o, -1, keepdims=True) + RMS_EPS) * gout_ref[...]
        zc = z_all[ch['sl']]
        o_ref[ch['sl'], :] = (o * (zc * jax.nn.sigmoid(zc))).astype(o_ref.dtype)


def gated_deltanet(h, col0, ab_t, conv_w, a_log, dt_bias, out_norm_g, bn, s):
    nblk = s // GDN_BLOCK
    tok = jnp.arange(GDN_BLOCK)
    tri = ((tok[:, None] // GDN_CHUNK == tok[None, :] // GDN_CHUNK) & (tok[:, None] <= tok[None, :])).astype(F32)
    const = jnp.stack([jnp.broadcast_to(-jnp.exp(a_log)[:, None], (B_HEADS, GDN_BLOCK)),
                       jnp.broadcast_to(dt_bias[:, None], (B_HEADS, GDN_BLOCK))], axis=1)
    const = jnp.pad(const, ((0, 0), (0, 6), (0, 0)))
    hcol = lambda part: pl.BlockSpec((GDN_BLOCK, B_HEAD_DIM), lambda b, hh, i: (b * nblk + i, col0 + part * B_HEADS + hh))
    wcol = lambda part: pl.BlockSpec((CONV_WIDTH, B_HEAD_DIM), lambda b, hh, i: (0, part * B_HEADS + hh))
    return pl.pallas_call(
        _gdn_kernel,
        grid=(bn, B_HEADS, nblk),
        in_specs=[hcol(0), hcol(1), hcol(2), hcol(3), wcol(0), wcol(1), wcol(2),
                  pl.BlockSpec((1, 16, GDN_BLOCK), lambda b, hh, i: (b, 0, i)),
                  pl.BlockSpec((1, 8, GDN_BLOCK), lambda b, hh, i: (hh, 0, 0)),
                  pl.BlockSpec((1, B_HEAD_DIM), lambda b, hh, i: (0, 0)),
                  pl.BlockSpec((GDN_BLOCK, GDN_BLOCK), lambda b, hh, i: (0, 0))],
        out_specs=pl.BlockSpec((GDN_BLOCK, B_HEAD_DIM), lambda b, hh, i: (b * nblk + i, hh)),
        out_shape=jax.ShapeDtypeStruct((bn * s, B_WIDTH), BF16),
        scratch_shapes=[pltpu.VMEM((B_HEAD_DIM, B_HEAD_DIM), F32),
                        pltpu.VMEM((3, GDN_HALO + GDN_BLOCK, B_HEAD_DIM), F32)],
        compiler_params=pltpu.CompilerParams(dimension_semantics=("arbitrary", "arbitrary", "arbitrary"), vmem_limit_bytes=VMEM_LIMIT_BYTES),
        name="gated_deltanet",
    )(h, h, h, h, conv_w, conv_w, conv_w, ab_t, const, out_norm_g.reshape(1, B_HEAD_DIM), tri)


S5_CHUNK = 32
S5_ROW = S5_CHUNK * C_GROUP
STATE2 = 2 * C_STATE


def _s5_tables(lam_re, lam_im, log_dt, b_re, b_im, c_re, c_im, d_skip):
    hp = lax.Precision.HIGHEST
    ell = S5_CHUNK
    dt = jnp.exp(log_dt)[:, None]
    lr, li = lam_re, lam_im
    mag = jnp.exp(lr * dt)
    a_re, a_im = mag * jnp.cos(li * dt), mag * jnp.sin(li * dt)
    den = lr * lr + li * li
    f_re = ((a_re - 1.0) * lr + a_im * li) / den
    f_im = (a_im * lr - (a_re - 1.0) * li) / den
    bb_re = f_re[..., None] * b_re - f_im[..., None] * b_im
    bb_im = f_re[..., None] * b_im + f_im[..., None] * b_re
    tau = jnp.arange(ell + 1, dtype=F32)[:, None, None]
    pmag, ang = jnp.exp(tau * (lr * dt)), tau * (li * dt)
    p_re, p_im = pmag * jnp.cos(ang), pmag * jnp.sin(ang)
    ab_re = p_re[:ell, ..., None] * bb_re - p_im[:ell, ..., None] * bb_im
    ab_im = p_re[:ell, ..., None] * bb_im + p_im[:ell, ..., None] * bb_re
    kern = (jnp.einsum('gip,tgpj->tgij', c_re, ab_re, precision=hp)
            - jnp.einsum('gip,tgpj->tgij', c_im, ab_im, precision=hp))
    lag = jnp.arange(ell)[None, :] - jnp.arange(ell)[:, None]
    toep = jnp.where((lag >= 0)[:, :, None, None, None], kern[jnp.maximum(lag, 0)], 0.0)
    toep = jnp.transpose(toep, (2, 0, 4, 1, 3)).reshape(C_GROUPS, S5_ROW, S5_ROW)
    w_re = jnp.transpose(ab_re[::-1], (1, 0, 3, 2)).reshape(C_GROUPS, S5_ROW, C_STATE)
    w_im = jnp.transpose(ab_im[::-1], (1, 0, 3, 2)).reshape(C_GROUPS, S5_ROW, C_STATE)
    tw = jnp.concatenate([toep, w_re, w_im, w_im, w_re], axis=-1).astype(BF16)
    pt_re = jnp.swapaxes(p_re[1:], 0, 1)[:, :, None, :]
    pt_im = jnp.swapaxes(p_im[1:], 0, 1)[:, :, None, :]
    ca_re = c_re[:, None] * pt_re - c_im[:, None] * pt_im
    ca_im = c_re[:, None] * pt_im + c_im[:, None] * pt_re
    v_re = jnp.transpose(ca_re, (0, 3, 1, 2)).reshape(C_GROUPS, C_STATE, S5_ROW)
    v_im = jnp.transpose(-ca_im, (0, 3, 1, 2)).reshape(C_GROUPS, C_STATE, S5_ROW)
    v = jnp.concatenate([v_re, v_im], axis=1).astype(BF16)
    al_re, al_im = p_re[ell], p_im[ell]
    coef = jnp.stack([jnp.concatenate([al_re, al_re], -1), jnp.concatenate([-al_im, al_im], -1),
                      jnp.concatenate([al_im, -al_im], -1)], axis=1)
    d_flat = jnp.tile(d_skip, (1, ell)).reshape(C_GROUPS, 1, S5_ROW)
    return tw, v, coef, d_flat


def _s5_kernel(u_ref, tw_ref, v_ref, coef_ref, d_ref, o_ref, sin_ref, xin_ref, *, bn):
    rows = u_ref.shape[1]
    n_chunks = rows // bn
    u = u_ref[0]
    tw = jnp.dot(u.astype(BF16), tw_ref[0], preferred_element_type=F32)
    o_ref[0] = tw[:, :S5_ROW] + u * d_ref[0]
    sin_ref[...] = tw[:, S5_ROW:]
    a1, a2, a2s = coef_ref[0, 0:1, :], coef_ref[0, 1:2, :], coef_ref[0, 2:3, :]

    def step(c, carry):
        new = []
        for b in range(bn):
            x, xs = carry[2 * b], carry[2 * b + 1]
            row = b * n_chunks + c
            xin_ref[pl.ds(row, 1), :] = x
            s_row = sin_ref[pl.ds(row, 1), :]
            new.append(a1 * x + a2 * xs + s_row[:, :STATE2])
            new.append(a1 * xs + a2s * x + s_row[:, STATE2:])
        return tuple(new)

    zero = jnp.zeros((1, STATE2), F32)
    lax.fori_loop(0, n_chunks, step, (zero,) * (2 * bn))
    y = o_ref[0] + jnp.dot(xin_ref[...].astype(BF16), v_ref[0], preferred_element_type=F32)
    o_ref[0] = jax.nn.gelu(y)


def _glu_kernel(y_ref, w_ref, b_ref, o_ref):
    y = y_ref[...]
    gate = jnp.dot(y.astype(BF16), w_ref[...], preferred_element_type=F32) + b_ref[...]
    o_ref[...] = (y * jax.nn.sigmoid(gate)).astype(o_ref.dtype)


def s5_branch(u, lam_re, lam_im, log_dt, b_re, b_im, c_re, c_im, d_skip, glu_w, glu_b):
    bn, s, _ = u.shape
    n_chunks = s // S5_CHUNK
    rows = bn * n_chunks
    tw, v, coef, d_flat = _s5_tables(lam_re, lam_im, log_dt, b_re, b_im, c_re, c_im, d_skip)
    uf = jnp.transpose(u.reshape(bn, n_chunks, S5_CHUNK, C_GROUPS, C_GROUP), (3, 0, 1, 2, 4)).reshape(C_GROUPS, rows, S5_ROW)
    per_group = lambda shape: pl.BlockSpec((1,) + shape, lambda g: (g, 0, 0))
    y = pl.pallas_call(
        functools.partial(_s5_kernel, bn=bn),
        grid=(C_GROUPS,),
        in_specs=[per_group((rows, S5_ROW)), per_group((S5_ROW, S5_ROW + 2 * STATE2)), per_group((STATE2, S5_ROW)),
                  per_group((3, STATE2)), per_group((1, S5_ROW))],
        out_specs=per_group((rows, S5_ROW)),
        out_shape=jax.ShapeDtypeStruct((C_GROUPS, rows, S5_ROW), F32),
        scratch_shapes=[pltpu.VMEM((rows, 2 * STATE2), F32), pltpu.VMEM((rows, STATE2), F32)],
        compiler_params=pltpu.CompilerParams(dimension_semantics=("arbitrary",), vmem_limit_bytes=VMEM_LIMIT_BYTES),
        name="s5_scan",
    )(uf, tw, v, coef, d_flat)
    y = jnp.transpose(y.reshape(C_GROUPS, bn, n_chunks, S5_CHUNK, C_GROUP), (1, 2, 3, 0, 4)).reshape(bn * s, C_WIDTH)
    t = bn * s
    tm = _pick_tile(t, (1024, 512, 256, 128, 8))
    return pl.pallas_call(
        _glu_kernel,
        grid=(t // tm,),
        in_specs=[pl.BlockSpec((tm, C_WIDTH), lambda i: (i, 0)), pl.BlockSpec((C_WIDTH, C_WIDTH), lambda i: (0, 0)),
                  pl.BlockSpec((1, C_WIDTH), lambda i: (0, 0))],
        out_specs=pl.BlockSpec((tm, C_WIDTH), lambda i: (i, 0)),
        out_shape=jax.ShapeDtypeStruct((t, C_WIDTH), BF16),
        compiler_params=pltpu.CompilerParams(dimension_semantics=("arbitrary",), vmem_limit_bytes=VMEM_LIMIT_BYTES),
        name="s5_glu",
    )(y, glu_w.astype(BF16), glu_b.reshape(1, C_WIDTH))


def hybrid_mixer(x, rel_bias, w_in, a_kv_norm, a_kv_up, b_conv, b_a_log, b_dt_bias, b_out_norm,
                 c_lambda_re, c_lambda_im, c_log_dt, c_b_re, c_b_im, c_c_re, c_c_im, c_d, c_glu_w, c_glu_b,
                 w_br_a, w_br_b, w_br_c, w_o, ln_g, ln_b):
    bn, s, d = x.shape
    t = bn * s
    xt = x.reshape(t, d)
    offsets = [0] + np.cumsum(SPLITS).tolist()
    w_cols = [w_in[:, offsets[i]:offsets[i + 1]] for i in range(len(SPLITS))]
    (w_aq, w_ckv, w_iq, w_ik, w_iw, w_bqkv, w_bz, w_ba, w_bb, w_cu, w_gate) = w_cols
    w_ik_pad = jnp.pad(w_ik, ((0, 0), (0, LANE - IDX_DIM)))
    w_rows = jnp.concatenate([w_gate, w_ckv, w_ik_pad, w_bqkv, w_bz, w_cu], axis=1)
    w_rows = jnp.pad(w_rows, ((0, 0), (0, -w_rows.shape[1] % PROJ_TILE_N)))
    h = matmul(xt, w_rows)
    o = np.cumsum([0, 3 * D_MODEL, A_KV_RANK, LANE, 3 * B_WIDTH, B_WIDTH, C_WIDTH]).tolist()
    idx_k = h[:, o[2]:o[2] + IDX_DIM].astype(BF16).reshape(bn, s, IDX_DIM)
    c_u = h[:, o[5]:o[6]].reshape(bn, s, C_WIDTH)
    qiq_t = matmul_nt(jnp.concatenate([w_aq * A_HEAD_DIM ** -0.5, w_iq], axis=1).T, xt, bn, s)
    scal_t = matmul_nt(jnp.concatenate([w_ba, w_bb, w_iw], axis=1).T, xt, bn, s, out_dtype=F32)
    k, vt = kv_project(h, o[1] // LANE, a_kv_norm, a_kv_up, bn, s)
    ya = dsa_attention(qiq_t, scal_t, idx_k, k, vt, _rel_bias_tiles(rel_bias), bn, s)
    yb = gated_deltanet(h, o[3] // LANE, scal_t, b_conv, b_a_log, b_dt_bias, b_out_norm, bn, s)
    yc = s5_branch(c_u, c_lambda_re, c_lambda_im, c_log_dt, c_b_re, c_b_im, c_c_re, c_c_im, c_d, c_glu_w, c_glu_b)
    return merge_branches(ya.reshape(t, A_WIDTH), yb, yc, h, w_br_a, w_br_b, w_br_c, w_o, xt, ln_g, ln_b)


def _merge_kernel(ya_ref, yb_ref, yc_ref, ga_ref, gb_ref, gc_ref, wa_ref, wb_ref, wc_ref, wo_ref, x_ref, g_ref, b_ref, o_ref):
    branch = lambda y_ref, w_ref, gate_ref: jax.nn.sigmoid(gate_ref[...]) * jnp.dot(y_ref[...], w_ref[...], preferred_element_type=F32)
    merged = branch(ya_ref, wa_ref, ga_ref) + branch(yb_ref, wb_ref, gb_ref) + branch(yc_ref, wc_ref, gc_ref)
    mix = jnp.dot(merged.astype(BF16), wo_ref[...], preferred_element_type=F32)
    z = DEEPNORM_ALPHA * x_ref[...] + mix
    mu = jnp.mean(z, -1, keepdims=True)
    zc = z - mu
    var = jnp.mean(zc * zc, -1, keepdims=True)
    o_ref[...] = zc * lax.rsqrt(var + LN_EPS) * g_ref[...] + b_ref[...]


def merge_branches(ya, yb, yc, h, w_br_a, w_br_b, w_br_c, w_o, x, ln_g, ln_b):
    t, d = x.shape
    tm = _pick_tile(t, (512, 256, 128, 8))
    row = lambda width, col=0: pl.BlockSpec((tm, width), lambda i: (i, col))
    full = lambda a: pl.BlockSpec(a.shape, lambda i: (0, 0))
    ws = [w.astype(BF16) for w in (w_br_a, w_br_b, w_br_c, w_o)]
    return pl.pallas_call(
        _merge_kernel,
        grid=(t // tm,),
        in_specs=[row(A_WIDTH), row(B_WIDTH), row(C_WIDTH), row(d, 0), row(d, 1), row(d, 2)] + [full(w) for w in ws]
                 + [row(d), pl.BlockSpec((1, d), lambda i: (0, 0)), pl.BlockSpec((1, d), lambda i: (0, 0))],
        out_specs=row(d),
        out_shape=jax.ShapeDtypeStruct((t, d), F32),
        compiler_params=pltpu.CompilerParams(dimension_semantics=("arbitrary",), vmem_limit_bytes=VMEM_LIMIT_BYTES),
        name="merge_branches",
    )(ya, yb, yc, h, h, h, *ws, x, ln_g.reshape(1, d), ln_b.reshape(1, d))


ROUTER_TILE = 512


def _router_kernel(x_ref, rwt_ref, bias_ref, tri_ref, idx_ref, gate_ref, pos_ref, cnt_ref, carry_ref):
    @pl.when(pl.program_id(0) == 0)
    def _():
        carry_ref[...] = jnp.zeros_like(carry_ref)

    logits = lax.dot_general(rwt_ref[...], x_ref[...].astype(BF16), (((1,), (1,)), ((), ())), preferred_element_type=F32)
    scores = jax.nn.sigmoid(logits)
    remaining = scores + bias_ref[...]
    expert = lax.broadcasted_iota(jnp.int32, scores.shape, 0)
    picks = []
    for _ in range(TOP_K):
        best = jnp.max(remaining, axis=0, keepdims=True)
        first = jnp.min(jnp.where(remaining == best, expert, N_EXPERTS), axis=0, keepdims=True)
        pick = expert == first
        picks.append((first, pick))
        remaining = jnp.where(pick, -jnp.inf, remaining)
    chosen = picks[0][1]
    for _, pick in picks[1:]:
        chosen = chosen | pick
    total = jnp.sum(jnp.where(chosen, scores, 0.0), axis=0, keepdims=True)
    prefix = jnp.dot(jnp.where(chosen, 1.0, 0.0).astype(BF16), tri_ref[...], preferred_element_type=F32)
    rank = carry_ref[...] + prefix.astype(jnp.int32) - 1
    carry_ref[...] = carry_ref[...] + prefix[:, -1:].astype(jnp.int32)
    cnt_ref[...] = carry_ref[...]
    for k, (first, pick) in enumerate(picks):
        idx_ref[k:k + 1, :] = first
        gate_ref[k:k + 1, :] = jnp.sum(jnp.where(pick, scores, 0.0), axis=0, keepdims=True) / total * ROUTED_SCALE
        pos_ref[k:k + 1, :] = jnp.sum(jnp.where(pick, rank, 0), axis=0, keepdims=True)


def moe_router(xt, router_w, router_bias):
    t, d = xt.shape
    tile = ROUTER_TILE
    tri = (jnp.arange(tile)[:, None] <= jnp.arange(tile)[None, :]).astype(BF16)
    row = pl.BlockSpec((TOP_K, tile), lambda i: (0, i))
    return pl.pallas_call(
        _router_kernel,
        grid=(t // tile,),
        in_specs=[pl.BlockSpec((tile, d), lambda i: (i, 0)), pl.BlockSpec((N_EXPERTS, d), lambda i: (0, 0)),
                  pl.BlockSpec((N_EXPERTS, 1), lambda i: (0, 0)), pl.BlockSpec((tile, tile), lambda i: (0, 0))],
        out_specs=[row, row, row, pl.BlockSpec((N_EXPERTS, 1), lambda i: (0, 0))],
        out_shape=[jax.ShapeDtypeStruct((TOP_K, t), jnp.int32), jax.ShapeDtypeStruct((TOP_K, t), F32),
                   jax.ShapeDtypeStruct((TOP_K, t), jnp.int32), jax.ShapeDtypeStruct((N_EXPERTS, 1), jnp.int32)],
        scratch_shapes=[pltpu.VMEM((N_EXPERTS, 1), jnp.int32)],
        compiler_params=pltpu.CompilerParams(dimension_semantics=("arbitrary",), vmem_limit_bytes=VMEM_LIMIT_BYTES),
        name="moe_router",
    )(xt, router_w.T.astype(BF16), router_bias.reshape(N_EXPERTS, 1), tri)


def _expert_kernel(blk_expert_ref, blk_rows_ref, x_ref, wgu_ref, wdn_ref, o_ref):
    n_valid = blk_rows_ref[pl.program_id(0)]

    @pl.when(n_valid > 0)
    def _():
        row = lax.broadcasted_iota(jnp.int32, x_ref.shape, 0)
        x = jnp.where(row < n_valid, x_ref[...], 0.0).astype(BF16)
        h = jnp.dot(x, wgu_ref[0], preferred_element_type=F32)
        hg, hu = h[:, :D_EXPERT], h[:, D_EXPERT:]
        act = (hg * jax.nn.sigmoid(hg) * hu).astype(BF16)
        o_ref[...] = jnp.dot(act, wdn_ref[0], preferred_element_type=F32)

    @pl.when(n_valid <= 0)
    def _():
        o_ref[...] = jnp.zeros_like(o_ref)


def expert_ffn(xs, blk_expert, blk_rows, w_gu, w_dn):
    n_pad, d = xs.shape
    n_blk = n_pad // MOE_BLOCK
    return pl.pallas_call(
        _expert_kernel,
        grid_spec=pltpu.PrefetchScalarGridSpec(
            num_scalar_prefetch=2,
            grid=(n_blk,),
            in_specs=[pl.BlockSpec((MOE_BLOCK, d), lambda b, be, br: (b, 0)),
                      pl.BlockSpec((1, d, 2 * D_EXPERT), lambda b, be, br: (be[b], 0, 0)),
                      pl.BlockSpec((1, D_EXPERT, d), lambda b, be, br: (be[b], 0, 0))],
            out_specs=pl.BlockSpec((MOE_BLOCK, d), lambda b, be, br: (b, 0))),
        out_shape=jax.ShapeDtypeStruct((n_pad, d), F32),
        compiler_params=pltpu.CompilerParams(dimension_semantics=("arbitrary",), vmem_limit_bytes=VMEM_LIMIT_BYTES),
        name="expert_ffn",
    )(blk_expert, blk_rows, xs, w_gu, w_dn)


DISPATCH_TILE = 256
COMBINE_TILE = 128


def _rows_kernel(pad_start_ref, idx_ref, pos_ref, o_ref):
    idx = idx_ref[...]
    row = pos_ref[...]
    for e in range(N_EXPERTS):
        row = row + jnp.where(idx == e, pad_start_ref[e], 0)
    o_ref[...] = row


def moe_rows(pad_start, idx, pos):
    k, t = idx.shape
    tile = _pick_tile(t, (4096, 2048, 1024, 512, 256, 128))
    blk = pl.BlockSpec((k, tile), lambda i, ps: (0, i))
    return pl.pallas_call(
        _rows_kernel,
        grid_spec=pltpu.PrefetchScalarGridSpec(num_scalar_prefetch=1, grid=(t // tile,), in_specs=[blk, blk], out_specs=blk),
        out_shape=jax.ShapeDtypeStruct((k, t), jnp.int32),
        compiler_params=pltpu.CompilerParams(dimension_semantics=("arbitrary",)),
        name="moe_rows",
    )(pad_start, idx, pos)


def _dispatch_kernel(dest_ref, x_ref, xs_hbm, sem):
    def issue(r, carry):
        for k in range(TOP_K):
            row = dest_ref[0, 0, r * TOP_K + k]
            pltpu.make_async_copy(x_ref.at[pl.ds(r, 1)], xs_hbm.at[pl.ds(row, 1)], sem).start(priority=k % 2)
        return carry

    lax.fori_loop(0, DISPATCH_TILE, issue, 0)
    n_rows = DISPATCH_TILE * TOP_K
    pltpu.make_async_copy(xs_hbm.at[pl.ds(0, n_rows)], xs_hbm.at[pl.ds(0, n_rows)], sem).wait()


def moe_dispatch(xt, dest_tok_major, n_pad):
    t, d = xt.shape
    n_tiles = t // DISPATCH_TILE
    return pl.pallas_call(
        _dispatch_kernel,
        grid=(n_tiles,),
        in_specs=[pl.BlockSpec((1, 1, DISPATCH_TILE * TOP_K), lambda i: (i, 0, 0), memory_space=pltpu.SMEM),
                  pl.BlockSpec((DISPATCH_TILE, d), lambda i: (i, 0))],
        out_specs=pl.BlockSpec(memory_space=pl.ANY),
        scratch_shapes=[pltpu.SemaphoreType.DMA(())],
        out_shape=jax.ShapeDtypeStruct((n_pad, d), xt.dtype),
        compiler_params=pltpu.CompilerParams(dimension_semantics=("arbitrary",), has_side_effects=True),
        name="moe_dispatch",
    )(dest_tok_major.reshape(n_tiles, 1, -1), xt)


def _combine_kernel(dest_ref, dest_next_ref, ys_hbm, gate_ref, x_ref, wgu_ref, wdn_ref, g_ref, b_ref, o_ref, buf_ref, sem):
    i = pl.program_id(0)
    n = pl.num_programs(0)
    slot = lax.rem(i, 2)

    def gather(d_ref, into):
        def issue(r, carry):
            for k in range(TOP_K):
                row = d_ref[0, 0, r * TOP_K + k]
                pltpu.make_async_copy(ys_hbm.at[pl.ds(row, 1)], buf_ref.at[into, k, pl.ds(r, 1)],
                                      sem.at[into]).start(priority=k % 2)
            return carry
        lax.fori_loop(0, COMBINE_TILE, issue, 0)

    @pl.when(i == 0)
    def _():
        gather(dest_ref, 0)

    @pl.when(i + 1 < n)
    def _():
        gather(dest_next_ref, 1 - slot)

    x = x_ref[...]
    h = jnp.dot(x.astype(BF16), wgu_ref[...], preferred_element_type=F32)
    hg, hu = h[:, :D_SHARED], h[:, D_SHARED:]
    acc = DEEPNORM_ALPHA * x + jnp.dot((hg * jax.nn.sigmoid(hg) * hu).astype(BF16), wdn_ref[...], preferred_element_type=F32)
    pltpu.make_async_copy(buf_ref.at[slot], buf_ref.at[slot], sem.at[slot]).wait()
    gate = gate_ref[...]
    for k in range(TOP_K):
        acc = acc + buf_ref[slot, k] * gate[:, k:k + 1]
    mu = jnp.mean(acc, -1, keepdims=True)
    zc = acc - mu
    var = jnp.mean(zc * zc, -1, keepdims=True)
    o_ref[...] = zc * lax.rsqrt(var + LN_EPS) * g_ref[...] + b_ref[...]


def moe_combine(ys, dest_tok_major, gate_tok_major, xt, sh_w_gu, sh_w_down, ln_g, ln_b):
    t, d = xt.shape
    tm = COMBINE_TILE
    n_tiles = t // tm
    dest3 = dest_tok_major.reshape(n_tiles, 1, -1)
    row = pl.BlockSpec((tm, d), lambda i: (i, 0))
    full = lambda shape: pl.BlockSpec(shape, lambda i: (0, 0))
    this = pl.BlockSpec((1, 1, tm * TOP_K), lambda i: (i, 0, 0), memory_space=pltpu.SMEM)
    nxt = pl.BlockSpec((1, 1, tm * TOP_K), lambda i: (jnp.minimum(i + 1, n_tiles - 1), 0, 0), memory_space=pltpu.SMEM)
    return pl.pallas_call(
        _combine_kernel,
        grid=(n_tiles,),
        in_specs=[this, nxt, pl.BlockSpec(memory_space=pl.ANY), pl.BlockSpec((tm, TOP_K), lambda i: (i, 0)), row,
                  full((d, 2 * D_SHARED)), full((D_SHARED, d)), full((1, d)), full((1, d))],
        out_specs=row,
        scratch_shapes=[pltpu.VMEM((2, TOP_K, tm, d), F32), pltpu.SemaphoreType.DMA((2,))],
        out_shape=jax.ShapeDtypeStruct((t, d), F32),
        compiler_params=pltpu.CompilerParams(dimension_semantics=("arbitrary",), vmem_limit_bytes=VMEM_LIMIT_BYTES),
        name="moe_combine",
    )(dest3, dest3, ys, gate_tok_major, xt, sh_w_gu.astype(BF16), sh_w_down.astype(BF16),
      ln_g.reshape(1, d), ln_b.reshape(1, d))


def moe_ffn(xt, router_w, router_bias, exp_w_gu, exp_w_down, sh_w_gu, sh_w_down, ln_g, ln_b):
    n_tok, d = xt.shape
    idx, gate, pos, counts = moe_router(xt, router_w, router_bias)
    counts = counts[:, 0]
    padded = (counts + MOE_BLOCK - 1) // MOE_BLOCK * MOE_BLOCK
    pad_ends = jnp.cumsum(padded)
    pad_start = pad_ends - padded
    n_blk = n_tok * TOP_K // MOE_BLOCK + N_EXPERTS
    n_pad = n_blk * MOE_BLOCK
    blk_lo = jnp.arange(n_blk, dtype=jnp.int32) * MOE_BLOCK
    owner = (pad_start[None, :] <= blk_lo[:, None]) & (blk_lo[:, None] < pad_ends[None, :])
    blk_expert = jnp.minimum(jnp.sum(pad_ends[None, :] <= blk_lo[:, None], axis=1), N_EXPERTS - 1).astype(jnp.int32)
    blk_rows = jnp.clip(jnp.sum(jnp.where(owner, (pad_start + counts)[None, :] - blk_lo[:, None], 0), axis=1),
                        0, MOE_BLOCK).astype(jnp.int32)
    dest = moe_rows(pad_start.astype(jnp.int32), idx, pos).T.reshape(-1)
    xs = moe_dispatch(xt, dest, n_pad)
    ys = expert_ffn(xs, blk_expert, blk_rows, exp_w_gu.astype(BF16), exp_w_down.astype(BF16))
    return moe_combine(ys, dest, gate.T, xt, sh_w_gu, sh_w_down, ln_g, ln_b)


def kernel(x, rel_bias, w_in, a_kv_norm, a_kv_up, b_conv, b_a_log, b_dt_bias, b_out_norm, c_lambda_re, c_lambda_im, c_log_dt, c_b_re, c_b_im, c_c_re, c_c_im, c_d, c_glu_w, c_glu_b, w_br_a, w_br_b, w_br_c, w_o, ln1_g, ln1_b, router_w, router_bias, exp_w_gu, exp_w_down, sh_w_gu, sh_w_down, ln2_g, ln2_b):
    bn, s, d = x.shape
    t = bn * s
    for i in range(w_in.shape[0]):
        xt = hybrid_mixer(x, rel_bias, w_in[i], a_kv_norm[i], a_kv_up[i], b_conv[i], b_a_log[i], b_dt_bias[i],
                          b_out_norm[i], c_lambda_re[i], c_lambda_im[i], c_log_dt[i], c_b_re[i], c_b_im[i],
                          c_c_re[i], c_c_im[i], c_d[i], c_glu_w[i], c_glu_b[i], w_br_a[i], w_br_b[i], w_br_c[i], w_o[i],
                          ln1_g[i], ln1_b[i])
        x = moe_ffn(xt, router_w[i], router_bias[i], exp_w_gu[i], exp_w_down[i], sh_w_gu[i], sh_w_down[i],
                    ln2_g[i], ln2_b[i]).reshape(bn, s, d)
    return x
```

```python
import functools
import math

import jax
import jax.numpy as jnp
import numpy as np
from jax import lax
from jax.experimental import pallas as pl
from jax.experimental.pallas import tpu as pltpu

F32 = jnp.float32
BF16 = jnp.bfloat16

D_MODEL = 1024
DEPTH = 4
A_HEADS = 8
A_HEAD_DIM = 64
A_WIDTH = A_HEADS * A_HEAD_DIM
A_KV_RANK = 128
IDX_HEADS = 8
IDX_DIM = 64
TOPK_MAX = 256
Q_BLOCK = 128
REL_BUCKETS = 32
REL_MAX_DIST = 128
B_HEADS = 4
B_HEAD_DIM = 128
B_WIDTH = B_HEADS * B_HEAD_DIM
CONV_WIDTH = 4
CHUNK = 64
C_WIDTH = 512
C_GROUP = 16
C_GROUPS = C_WIDTH // C_GROUP
C_STATE = 64
N_EXPERTS = 64
TOP_K = 8
D_EXPERT = 256
D_SHARED = 256
ROUTED_SCALE = 2.5
MOE_BLOCK = 512
SPLITS = (A_WIDTH, A_KV_RANK, IDX_HEADS * IDX_DIM, IDX_DIM, IDX_HEADS, 3 * B_WIDTH, B_WIDTH, B_HEADS, B_HEADS, C_WIDTH, 3 * D_MODEL)
DEEPNORM_ALPHA = (2 * DEPTH) ** 0.25
LN_EPS = 1e-5
RMS_EPS = 1e-6
NEG_INF = -1e30

VMEM_LIMIT_BYTES = 48 * 1024 * 1024
LANE = 128
PROJ_TILE_N = 1024


def _pick_tile(n, candidates):
    for c in candidates:
        if n % c == 0:
            return c
    return n


def _mm_kernel(a_ref, b_ref, o_ref):
    o_ref[...] = jnp.dot(a_ref[...].astype(BF16), b_ref[...], preferred_element_type=F32).astype(o_ref.dtype)


def matmul(a, b, out_dtype=F32):
    m, k = a.shape
    _, n = b.shape
    tm = _pick_tile(m, (1024, 512, 256, 128, 8))
    tn = _pick_tile(n, (1024, 768, 512, 384, 256, 128))
    return pl.pallas_call(
        _mm_kernel,
        grid=(m // tm, n // tn),
        in_specs=[pl.BlockSpec((tm, k), lambda i, j: (i, 0)), pl.BlockSpec((k, tn), lambda i, j: (0, j))],
        out_specs=pl.BlockSpec((tm, tn), lambda i, j: (i, j)),
        out_shape=jax.ShapeDtypeStruct((m, n), out_dtype),
        compiler_params=pltpu.CompilerParams(dimension_semantics=("arbitrary", "arbitrary"), vmem_limit_bytes=VMEM_LIMIT_BYTES),
        name="matmul",
    )(a, b.astype(BF16))


def rel_bucket(dist):
    n_exact = REL_BUCKETS // 2
    d = jnp.maximum(dist, 1).astype(F32)
    large = n_exact + (jnp.log(d / n_exact) / math.log(REL_MAX_DIST / n_exact) * (REL_BUCKETS - n_exact)).astype(jnp.int32)
    return jnp.where(dist < n_exact, dist, jnp.minimum(large, REL_BUCKETS - 1))


KEY_BLOCK = 128
DSA_CHUNK = 512
BLOCKS_PER_CHUNK = DSA_CHUNK // KEY_BLOCK
HEAD_PAIRS = A_HEADS // 2
PAIR_WIDTH = 2 * A_HEAD_DIM
INT_MIN = -2 ** 31
IDX_W_SCALE = IDX_HEADS ** -0.5 * IDX_DIM ** -0.5


def _kv_kernel(c_ref, g_ref, wk_ref, wvt_ref, k_ref, vt_ref):
    c = c_ref[...]
    cn = (c * lax.rsqrt(jnp.mean(c * c, -1, keepdims=True) + RMS_EPS) * g_ref[...]).astype(BF16)
    k_ref[...] = jnp.dot(cn, wk_ref[...], preferred_element_type=F32).astype(BF16)
    vt = lax.dot_general(wvt_ref[...], cn, (((1,), (1,)), ((), ())), preferred_element_type=F32).astype(BF16)
    for i in range(vt_ref.shape[1]):
        vt_ref[0, i] = vt[:, i * KEY_BLOCK:(i + 1) * KEY_BLOCK]


def kv_project(c_kv, col, g, w_kv_up, bn, s):
    ts = 512
    w = w_kv_up.astype(BF16)
    return pl.pallas_call(
        _kv_kernel,
        grid=(bn, s // ts),
        in_specs=[pl.BlockSpec((ts, A_KV_RANK), lambda b, i: (b * (s // ts) + i, col)),
                  pl.BlockSpec((1, A_KV_RANK), lambda b, i: (0, 0)),
                  pl.BlockSpec((A_KV_RANK, A_WIDTH), lambda b, i: (0, 0)),
                  pl.BlockSpec((A_WIDTH, A_KV_RANK), lambda b, i: (0, 0))],
        out_specs=[pl.BlockSpec((None, ts, A_WIDTH), lambda b, i: (b, i, 0)),
                   pl.BlockSpec((1, ts // KEY_BLOCK, A_WIDTH, KEY_BLOCK), lambda b, i: (b, i, 0, 0))],
        out_shape=[jax.ShapeDtypeStruct((bn, s, A_WIDTH), BF16),
                   jax.ShapeDtypeStruct((bn, s // KEY_BLOCK, A_WIDTH, KEY_BLOCK), BF16)],
        compiler_params=pltpu.CompilerParams(dimension_semantics=("arbitrary", "arbitrary"), vmem_limit_bytes=VMEM_LIMIT_BYTES),
        name="kv_project",
    )(c_kv, g.reshape(1, A_KV_RANK), w[:, :A_WIDTH], w[:, A_WIDTH:].T)


def _mm_nt_kernel(wt_ref, x_ref, o_ref):
    o_ref[0] = lax.dot_general(wt_ref[...], x_ref[...].astype(BF16), (((1,), (1,)), ((), ())),
                               preferred_element_type=F32).astype(o_ref.dtype)


def matmul_nt(wt, x, bn, s, out_dtype=BF16):
    n, k = wt.shape
    ts = 512
    return pl.pallas_call(
        _mm_nt_kernel,
        grid=(bn, s // ts),
        in_specs=[pl.BlockSpec((n, k), lambda b, i: (0, 0)),
                  pl.BlockSpec((ts, k), lambda b, i: (b * (s // ts) + i, 0))],
        out_specs=pl.BlockSpec((1, n, ts), lambda b, i: (b, 0, i)),
        out_shape=jax.ShapeDtypeStruct((bn, n, s), out_dtype),
        compiler_params=pltpu.CompilerParams(dimension_semantics=("arbitrary", "arbitrary"), vmem_limit_bytes=VMEM_LIMIT_BYTES),
        name="matmul_nt",
    )(wt.astype(BF16), x)


def _sortable_key(score):
    bits = pltpu.bitcast(score, jnp.int32)
    return bits ^ (lax.shift_right_arithmetic(bits, 31) & jnp.int32(0x7FFFFFFF))


def _dsa_kernel(qt_ref, iqt_ref, wt_ref, kidx_ref, k_ref, vt_ref, bias_ref, o_ref,
                skey_ref, widx_ref, wq_ref, acc_ref, m_ref, l_ref, *, k_sel):
    j = pl.program_id(1)
    n_chunks = (j + BLOCKS_PER_CHUNK) // BLOCKS_PER_CHUNK
    q_pos = j * Q_BLOCK + lax.broadcasted_iota(jnp.int32, (1, Q_BLOCK), 1)

    for h in range(IDX_HEADS):
        widx_ref[:, h * Q_BLOCK:(h + 1) * Q_BLOCK] = iqt_ref[0, h * IDX_DIM:(h + 1) * IDX_DIM, :]
    wq_ref[...] = jnp.zeros_like(wq_ref)
    for p in range(HEAD_PAIRS):
        wq_ref[p, 0:A_HEAD_DIM, 0:Q_BLOCK] = qt_ref[0, (2 * p) * A_HEAD_DIM:(2 * p + 1) * A_HEAD_DIM, :]
        wq_ref[p, A_HEAD_DIM:PAIR_WIDTH, Q_BLOCK:2 * Q_BLOCK] = qt_ref[0, (2 * p + 1) * A_HEAD_DIM:(2 * p + 2) * A_HEAD_DIM, :]

    w_rows = [wt_ref[0, h:h + 1, :] * IDX_W_SCALE for h in range(IDX_HEADS)]

    def idx_chunk(c, carry):
        start = pl.multiple_of(c * DSA_CHUNK, DSA_CHUNK)
        kc = kidx_ref[0, pl.ds(start, DSA_CHUNK), :]
        score = jnp.zeros((DSA_CHUNK, Q_BLOCK), F32)
        for p in range(IDX_HEADS // 2):
            z = jnp.dot(kc, widx_ref[:, p * 2 * Q_BLOCK:(p + 1) * 2 * Q_BLOCK], preferred_element_type=F32)
            score = score + jnp.maximum(z[:, :Q_BLOCK], 0.0) * w_rows[2 * p]
            score = score + jnp.maximum(z[:, Q_BLOCK:], 0.0) * w_rows[2 * p + 1]
        pos = start + lax.broadcasted_iota(jnp.int32, (DSA_CHUNK, Q_BLOCK), 0)
        skey_ref[pl.ds(start, DSA_CHUNK), :] = jnp.where(pos <= q_pos, _sortable_key(score), INT_MIN)
        return carry

    lax.fori_loop(0, n_chunks, idx_chunk, 0)

    def count_ge(cand):
        def body(c, cnt):
            start = pl.multiple_of(c * DSA_CHUNK, DSA_CHUNK)
            hit = (skey_ref[pl.ds(start, DSA_CHUNK), :] >= cand).astype(jnp.int32)
            return cnt + jnp.sum(hit.reshape(DSA_CHUNK // 8, 8, Q_BLOCK), axis=0)
        cnt = lax.fori_loop(0, n_chunks, body, jnp.zeros((8, Q_BLOCK), jnp.int32))
        return jnp.sum(cnt, axis=0, keepdims=True)

    def bit_step(i, carry):
        thr_biased, n_ge = carry
        cand_biased = thr_biased | lax.shift_left(jnp.int32(1), 31 - i)
        cnt = count_ge(cand_biased ^ INT_MIN)
        keep = cnt >= k_sel
        return jnp.where(keep, cand_biased, thr_biased), jnp.where(keep, cnt, n_ge)

    thr, n_ge = lax.fori_loop(0, 32, bit_step, (jnp.zeros((1, Q_BLOCK), jnp.int32),
                                                jnp.full((1, Q_BLOCK), k_sel, jnp.int32)))
    thr = thr ^ INT_MIN

    tied = (n_ge > k_sel) & (thr > INT_MIN)

    @pl.when(jnp.max(tied.astype(jnp.int32)) > 0)
    def _():
        need = k_sel - count_ge(thr + 1)

        def count_tied_upto(limit):
            def body(c, cnt):
                start = pl.multiple_of(c * DSA_CHUNK, DSA_CHUNK)
                pos = start + lax.broadcasted_iota(jnp.int32, (DSA_CHUNK, Q_BLOCK), 0)
                hit = jnp.where(skey_ref[pl.ds(start, DSA_CHUNK), :] == thr, (pos <= limit).astype(jnp.int32), 0)
                return cnt + jnp.sum(hit.reshape(DSA_CHUNK // 8, 8, Q_BLOCK), axis=0)
            cnt = lax.fori_loop(0, n_chunks, body, jnp.zeros((8, Q_BLOCK), jnp.int32))
            return jnp.sum(cnt, axis=0, keepdims=True)

        def pos_step(i, lim):
            cand = lim - lax.shift_left(jnp.int32(1), 30 - i)
            return jnp.where(count_tied_upto(cand) >= need, cand, lim)

        limit = lax.fori_loop(0, 31, pos_step, jnp.full((1, Q_BLOCK), 2 ** 31 - 1, jnp.int32))

        def demote(c, carry):
            start = pl.multiple_of(c * DSA_CHUNK, DSA_CHUNK)
            pos = start + lax.broadcasted_iota(jnp.int32, (DSA_CHUNK, Q_BLOCK), 0)
            sk = skey_ref[pl.ds(start, DSA_CHUNK), :]
            skey_ref[pl.ds(start, DSA_CHUNK), :] = jnp.where(tied & (sk == thr) & (pos > limit), thr - 1, sk)
            return carry

        lax.fori_loop(0, n_chunks, demote, 0)

    m_ref[...] = jnp.full_like(m_ref, NEG_INF)
    l_ref[...] = jnp.zeros_like(l_ref)
    acc_ref[...] = jnp.zeros_like(acc_ref)

    def attend(s_list, vt_of, n_blocks):
        for p in range(HEAD_PAIRS):
            s = s_list[p]
            m_old = m_ref[p]
            m_new = jnp.maximum(m_old, jnp.max(s, axis=0, keepdims=True))
            alpha = jnp.exp(m_old - m_new)
            pexp = jnp.exp(s - m_new)
            l_ref[p] = l_ref[p] * alpha + jnp.sum(pexp, axis=0, keepdims=True)
            pb = pexp.astype(BF16)
            pv = jnp.dot(vt_of(p, 0), pb[0:KEY_BLOCK], preferred_element_type=F32)
            for i in range(1, n_blocks):
                pv = pv + jnp.dot(vt_of(p, i), pb[i * KEY_BLOCK:(i + 1) * KEY_BLOCK], preferred_element_type=F32)
            acc_ref[p] = acc_ref[p] * alpha + pv
            m_ref[p] = m_new

    far_end = (j - 1) * KEY_BLOCK
    n_far = (jnp.maximum(j - 1, 0) + BLOCKS_PER_CHUNK - 1) // BLOCKS_PER_CHUNK

    def far_chunk(c, carry):
        start = pl.multiple_of(c * DSA_CHUNK, DSA_CHUNK)
        pos = start + lax.broadcasted_iota(jnp.int32, (DSA_CHUNK, Q_BLOCK), 0)
        sel = (skey_ref[pl.ds(start, DSA_CHUNK), :] >= thr) & (pos < far_end)
        madd = jnp.where(sel, 0.0, NEG_INF)
        madd2 = jnp.concatenate([madd, madd], axis=1)
        kc = k_ref[0, pl.ds(start, DSA_CHUNK), :]
        s_list = [jnp.dot(kc[:, p * PAIR_WIDTH:(p + 1) * PAIR_WIDTH], wq_ref[p], preferred_element_type=F32) + madd2
                  for p in range(HEAD_PAIRS)]
        blk0 = c * BLOCKS_PER_CHUNK
        attend(s_list, lambda p, i: vt_ref[0, blk0 + i, p * PAIR_WIDTH:(p + 1) * PAIR_WIDTH, :], BLOCKS_PER_CHUNK)
        return carry

    lax.fori_loop(0, n_far, far_chunk, 0)

    def near_block(kb, delta):
        start = pl.multiple_of(kb * KEY_BLOCK, KEY_BLOCK)
        pos = start + lax.broadcasted_iota(jnp.int32, (KEY_BLOCK, Q_BLOCK), 0)
        sel = (skey_ref[pl.ds(start, KEY_BLOCK), :] >= thr) & (pos <= q_pos)
        madd = jnp.where(sel, 0.0, NEG_INF)
        madd2 = jnp.concatenate([madd, madd], axis=1)
        kc = k_ref[0, pl.ds(start, KEY_BLOCK), :]
        s_list = [jnp.dot(kc[:, p * PAIR_WIDTH:(p + 1) * PAIR_WIDTH], wq_ref[p], preferred_element_type=F32)
                  + bias_ref[delta, p] + madd2 for p in range(HEAD_PAIRS)]
        attend(s_list, lambda p, i: vt_ref[0, kb, p * PAIR_WIDTH:(p + 1) * PAIR_WIDTH, :], 1)

    @pl.when(j > 0)
    def _():
        near_block(j - 1, 1)

    near_block(j, 0)

    outs = []
    for p in range(HEAD_PAIRS):
        o = acc_ref[p] / l_ref[p]
        outs.append(o[0:A_HEAD_DIM, 0:Q_BLOCK])
        outs.append(o[A_HEAD_DIM:PAIR_WIDTH, Q_BLOCK:2 * Q_BLOCK])
    o_ref[0] = jnp.concatenate(outs, axis=0).T.astype(o_ref.dtype)


def _rel_bias_tiles(rel_bias):
    tab = rel_bias[rel_bucket(jnp.arange(2 * Q_BLOCK, dtype=jnp.int32))] - rel_bias[REL_BUCKETS - 1]
    key = jnp.arange(KEY_BLOCK)[:, None]
    qry = jnp.arange(Q_BLOCK)[None, :]
    tiles = []
    for delta in (0, 1):
        dist = jnp.maximum(delta * Q_BLOCK + qry - key, 0)
        t = jnp.moveaxis(tab[dist], 2, 0)
        t = t.reshape(HEAD_PAIRS, 2, KEY_BLOCK, Q_BLOCK)
        tiles.append(jnp.concatenate([t[:, 0], t[:, 1]], axis=-1))
    return jnp.stack(tiles)


def dsa_attention(qiq_t, wt, k_idx, k, vt, bias_tiles, bn, s):
    assert A_WIDTH == IDX_HEADS * IDX_DIM
    scal_rows = wt.shape[1]
    assert scal_rows % IDX_HEADS == 0
    k_sel = min(TOPK_MAX, s // 4)
    nb = s // Q_BLOCK
    s_pad = s + DSA_CHUNK
    return pl.pallas_call(
        functools.partial(_dsa_kernel, k_sel=k_sel),
        grid=(bn, nb),
        in_specs=[pl.BlockSpec((1, A_WIDTH, Q_BLOCK), lambda b, j: (b, 0, j)),
                  pl.BlockSpec((1, IDX_HEADS * IDX_DIM, Q_BLOCK), lambda b, j: (b, 1, j)),
                  pl.BlockSpec((1, IDX_HEADS, Q_BLOCK), lambda b, j: (b, scal_rows // IDX_HEADS - 1, j)),
                  pl.BlockSpec((1, s, IDX_DIM), lambda b, j: (b, 0, 0)),
                  pl.BlockSpec((1, s, A_WIDTH), lambda b, j: (b, 0, 0)),
                  pl.BlockSpec((1, s // KEY_BLOCK, A_WIDTH, KEY_BLOCK), lambda b, j: (b, 0, 0, 0)),
                  pl.BlockSpec((2, HEAD_PAIRS, KEY_BLOCK, 2 * Q_BLOCK), lambda b, j: (0, 0, 0, 0))],
        out_specs=pl.BlockSpec((1, Q_BLOCK, A_WIDTH), lambda b, j: (b, j, 0)),
        out_shape=jax.ShapeDtypeStruct((bn, s, A_WIDTH), BF16),
        scratch_shapes=[pltpu.VMEM((s_pad, Q_BLOCK), jnp.int32),
                        pltpu.VMEM((IDX_DIM, IDX_HEADS * Q_BLOCK), BF16),
                        pltpu.VMEM((HEAD_PAIRS, PAIR_WIDTH, 2 * Q_BLOCK), BF16),
                        pltpu.VMEM((HEAD_PAIRS, PAIR_WIDTH, 2 * Q_BLOCK), F32),
                        pltpu.VMEM((HEAD_PAIRS, 1, 2 * Q_BLOCK), F32),
                        pltpu.VMEM((HEAD_PAIRS, 1, 2 * Q_BLOCK), F32)],
        compiler_params=pltpu.CompilerParams(dimension_semantics=("arbitrary", "arbitrary"), vmem_limit_bytes=56 * 1024 * 1024),
        name="dsa_attention",
    )(qiq_t, qiq_t, wt, k_idx, k, vt, bias_tiles)


GDN_CHUNK = B_HEAD_DIM
GDN_BLOCK = 512
GDN_HALO = 8
NEUMANN_STEPS = 6


def _split_bf16(x):
    hi = x.astype(BF16)
    return hi, (x - hi.astype(F32)).astype(BF16)


def _dot_f32(a, b):
    a_hi, a_lo = _split_bf16(a)
    b_hi, b_lo = _split_bf16(b)
    d = lambda x, y: jnp.dot(x, y, preferred_element_type=F32)
    return d(a_hi, b_hi) + (d(a_hi, b_lo) + d(a_lo, b_hi))


def _bdot(a, b):
    return jnp.dot(a.astype(BF16), b.astype(BF16), preferred_element_type=F32)


def _bdot_nt(a, b):
    return lax.dot_general(a.astype(BF16), b.astype(BF16), (((1,), (1,)), ((), ())), preferred_element_type=F32)


def _bdot_tn(a, b):
    return lax.dot_general(a.astype(BF16), b.astype(BF16), (((0,), (0,)), ((), ())), preferred_element_type=F32)


def _softplus(x):
    return jnp.maximum(x, 0.0) + jnp.log(1.0 + jnp.exp(-jnp.abs(x)))


def _gdn_kernel(hq_ref, hk_ref, hv_ref, hz_ref, wq_ref, wk_ref, wv_ref, ab_ref, const_ref, gout_ref, tri_ref,
                o_ref, state_ref, xbuf_ref):
    head = pl.program_id(1)

    @pl.when(pl.program_id(2) == 0)
    def _():
        state_ref[...] = jnp.zeros_like(state_ref)
        xbuf_ref[:, 0:GDN_HALO, :] = jnp.zeros((3, GDN_HALO, B_HEAD_DIM), F32)

    def conv_silu(n, x_ref, w_ref):
        x = x_ref[...]
        xbuf_ref[n, GDN_HALO:, :] = x
        acc = xbuf_ref[n, GDN_HALO - CONV_WIDTH + 1:GDN_HALO - CONV_WIDTH + 1 + GDN_BLOCK, :] * w_ref[0:1, :]
        for j in range(1, CONV_WIDTH):
            lo = GDN_HALO - CONV_WIDTH + 1 + j
            acc = acc + xbuf_ref[n, lo:lo + GDN_BLOCK, :] * w_ref[j:j + 1, :]
        xbuf_ref[n, 0:GDN_HALO, :] = x[GDN_BLOCK - GDN_HALO:, :]
        return acc * jax.nn.sigmoid(acc)

    q = conv_silu(0, hq_ref, wq_ref)
    k = conv_silu(1, hk_ref, wk_ref)
    v = conv_silu(2, hv_ref, wv_ref)
    q = q * lax.rsqrt(jnp.sum(q * q, -1, keepdims=True) + RMS_EPS) * B_HEAD_DIM ** -0.5
    k = k * lax.rsqrt(jnp.sum(k * k, -1, keepdims=True) + RMS_EPS)

    a_row = ab_ref[0, pl.ds(head, 1), :]
    b_row = ab_ref[0, pl.ds(B_HEADS + head, 1), :]
    beta_row = jax.nn.sigmoid(b_row)
    g_row = const_ref[0, 0:1, :] * _softplus(a_row + const_ref[0, 1:2, :])
    gc_rows = _dot_f32(jnp.broadcast_to(g_row, (8, GDN_BLOCK)), tri_ref[...])[0:1, :]

    row_i = lax.broadcasted_iota(jnp.int32, (GDN_CHUNK, GDN_CHUNK), 0)
    col_j = lax.broadcasted_iota(jnp.int32, (GDN_CHUNK, GDN_CHUNK), 1)
    eye = jnp.where(row_i == col_j, 1.0, 0.0)
    z_all = hz_ref[...]
    chunks = []
    for c in range(GDN_BLOCK // GDN_CHUNK):
        sl = slice(c * GDN_CHUNK, (c + 1) * GDN_CHUNK)
        qc, kc, vc = q[sl], k[sl], v[sl]
        g_lane = jnp.broadcast_to(gc_rows[:, sl], (GDN_CHUNK, GDN_CHUNK))
        g_sub = g_lane.T
        beta_sub = jnp.broadcast_to(beta_row[:, sl], (GDN_CHUNK, GDN_CHUNK)).T
        g_end = jnp.broadcast_to(g_lane[:, GDN_CHUNK - 1:GDN_CHUNK], (GDN_CHUNK, GDN_CHUNK))
        decay = jnp.exp(jnp.where(row_i >= col_j, g_sub - g_lane, -jnp.inf))
        kb = kc * beta_sub
        x = -jnp.where(row_i > col_j, _bdot_nt(kb, kc) * decay, 0.0)
        chunks.append(dict(sl=sl, qc=qc, kc=kc, vc=vc, g_sub=g_sub, beta_sub=beta_sub, g_end=g_end, decay=decay,
                           kb=kb, x=x, t_inv=eye + x))
    for _ in range(NEUMANN_STEPS):
        for ch in chunks:
            ch['x'] = _dot_f32(ch['x'], ch['x'])
        for ch in chunks:
            ch['t_inv'] = ch['t_inv'] + _bdot(ch['t_inv'], ch['x'])
    for ch in chunks:
        g_exp = jnp.exp(ch['g_sub'])
        ch['u'] = _bdot(ch['t_inv'], ch['vc'] * ch['beta_sub'])
        ch['w'] = _bdot(ch['t_inv'], ch['kb'] * g_exp)
        ch['qk'] = _bdot_nt(ch['qc'], ch['kc']) * ch['decay']
        ch['q_dec'] = ch['qc'] * g_exp
        ch['k_dec'] = ch['kc'] * jnp.exp(ch['g_end'] - ch['g_sub'])
    for ch in chunks:
        state = state_ref[...]
        v_new = ch['u'] - _bdot(ch['w'], state)
        o = _bdot(ch['q_dec'], state) + _bdot(ch['qk'], v_new)
        state_ref[...] = state * jnp.exp(ch['g_end']) + _bdot_tn(ch['k_dec'], v_new)
        o = o * lax.rsqrt(jnp.mean(o * o, -1, keepdims=True) + RMS_EPS) * gout_ref[...]
        zc = z_all[ch['sl']]
        o_ref[ch['sl'], :] = (o * (zc * jax.nn.sigmoid(zc))).astype(o_ref.dtype)


def gated_deltanet(h, col0, ab_t, conv_w, a_log, dt_bias, out_norm_g, bn, s):
    nblk = s // GDN_BLOCK
    tok = jnp.arange(GDN_BLOCK)
    tri = ((tok[:, None] // GDN_CHUNK == tok[None, :] // GDN_CHUNK) & (tok[:, None] <= tok[None, :])).astype(F32)
    const = jnp.stack([jnp.broadcast_to(-jnp.exp(a_log)[:, None], (B_HEADS, GDN_BLOCK)),
                       jnp.broadcast_to(dt_bias[:, None], (B_HEADS, GDN_BLOCK))], axis=1)
    const = jnp.pad(const, ((0, 0), (0, 6), (0, 0)))
    hcol = lambda part: pl.BlockSpec((GDN_BLOCK, B_HEAD_DIM), lambda b, hh, i: (b * nblk + i, col0 + part * B_HEADS + hh))
    wcol = lambda part: pl.BlockSpec((CONV_WIDTH, B_HEAD_DIM), lambda b, hh, i: (0, part * B_HEADS + hh))
    return pl.pallas_call(
        _gdn_kernel,
        grid=(bn, B_HEADS, nblk),
        in_specs=[hcol(0), hcol(1), hcol(2), hcol(3), wcol(0), wcol(1), wcol(2),
                  pl.BlockSpec((1, 16, GDN_BLOCK), lambda b, hh, i: (b, 0, i)),
                  pl.BlockSpec((1, 8, GDN_BLOCK), lambda b, hh, i: (hh, 0, 0)),
                  pl.BlockSpec((1, B_HEAD_DIM), lambda b, hh, i: (0, 0)),
                  pl.BlockSpec((GDN_BLOCK, GDN_BLOCK), lambda b, hh, i: (0, 0))],
        out_specs=pl.BlockSpec((GDN_BLOCK, B_HEAD_DIM), lambda b, hh, i: (b * nblk + i, hh)),
        out_shape=jax.ShapeDtypeStruct((bn * s, B_WIDTH), BF16),
        scratch_shapes=[pltpu.VMEM((B_HEAD_DIM, B_HEAD_DIM), F32),
                        pltpu.VMEM((3, GDN_HALO + GDN_BLOCK, B_HEAD_DIM), F32)],
        compiler_params=pltpu.CompilerParams(dimension_semantics=("arbitrary", "arbitrary", "arbitrary"), vmem_limit_bytes=VMEM_LIMIT_BYTES),
        name="gated_deltanet",
    )(h, h, h, h, conv_w, conv_w, conv_w, ab_t, const, out_norm_g.reshape(1, B_HEAD_DIM), tri)


S5_CHUNK = 32
S5_ROW = S5_CHUNK * C_GROUP
STATE2 = 2 * C_STATE


def _s5_tables(lam_re, lam_im, log_dt, b_re, b_im, c_re, c_im):
    hp = lax.Precision.HIGHEST
    ell = S5_CHUNK
    dt = jnp.exp(log_dt)[:, None]
    lr, li = lam_re, lam_im
    mag = jnp.exp(lr * dt)
    a_re, a_im = mag * jnp.cos(li * dt), mag * jnp.sin(li * dt)
    den = lr * lr + li * li
    f_re = ((a_re - 1.0) * lr + a_im * li) / den
    f_im = (a_im * lr - (a_re - 1.0) * li) / den
    bb_re = f_re[..., None] * b_re - f_im[..., None] * b_im
    bb_im = f_re[..., None] * b_im + f_im[..., None] * b_re
    tau = jnp.arange(ell + 1, dtype=F32)[:, None, None]
    pmag, ang = jnp.exp(tau * (lr * dt)), tau * (li * dt)
    p_re, p_im = pmag * jnp.cos(ang), pmag * jnp.sin(ang)
    ab_re = p_re[:ell, ..., None] * bb_re - p_im[:ell, ..., None] * bb_im
    ab_im = p_re[:ell, ..., None] * bb_im + p_im[:ell, ..., None] * bb_re
    kern = (jnp.einsum('gip,tgpj->tgij', c_re, ab_re, precision=hp)
            - jnp.einsum('gip,tgpj->tgij', c_im, ab_im, precision=hp))
    lag = jnp.arange(ell)[None, :] - jnp.arange(ell)[:, None]
    toep = jnp.where((lag >= 0)[:, :, None, None, None], kern[jnp.maximum(lag, 0)], 0.0)
    toep = jnp.transpose(toep, (2, 0, 4, 1, 3)).reshape(C_GROUPS, S5_ROW, S5_ROW)
    w_re = jnp.transpose(ab_re[::-1], (1, 0, 3, 2)).reshape(C_GROUPS, S5_ROW, C_STATE)
    w_im = jnp.transpose(ab_im[::-1], (1, 0, 3, 2)).reshape(C_GROUPS, S5_ROW, C_STATE)
    tw = jnp.concatenate([toep, w_re, w_im, w_im, w_re], axis=-1).astype(BF16)
    pt_re = jnp.swapaxes(p_re[1:], 0, 1)[:, :, None, :]
    pt_im = jnp.swapaxes(p_im[1:], 0, 1)[:, :, None, :]
    ca_re = c_re[:, None] * pt_re - c_im[:, None] * pt_im
    ca_im = c_re[:, None] * pt_im + c_im[:, None] * pt_re
    v_re = jnp.transpose(ca_re, (0, 3, 1, 2)).reshape(C_GROUPS, C_STATE, S5_ROW)
    v_im = jnp.transpose(-ca_im, (0, 3, 1, 2)).reshape(C_GROUPS, C_STATE, S5_ROW)
    v = jnp.concatenate([v_re, v_im], axis=1).astype(BF16)
    al_re, al_im = p_re[ell], p_im[ell]
    coef = jnp.stack([jnp.concatenate([al_re, al_re], -1), jnp.concatenate([-al_im, al_im], -1),
                      jnp.concatenate([al_im, -al_im], -1)], axis=1)
    return tw, v, coef


def _s5_kernel(u_ref, tw_ref, v_ref, coef_ref, o_ref, sin_ref, xin_ref, *, bn):
    rows = u_ref.shape[1]
    n_chunks = rows // bn
    tw = jnp.dot(u_ref[0], tw_ref[0], preferred_element_type=F32)
    o_ref[0] = tw[:, :S5_ROW]
    sin_ref[...] = tw[:, S5_ROW:]
    a1, a2, a2s = coef_ref[0, 0:1, :], coef_ref[0, 1:2, :], coef_ref[0, 2:3, :]

    def step(c, carry):
        new = []
        for b in range(bn):
            x, xs = carry[2 * b], carry[2 * b + 1]
            row = b * n_chunks + c
            xin_ref[pl.ds(row, 1), :] = x
            s_row = sin_ref[pl.ds(row, 1), :]
            new.append(a1 * x + a2 * xs + s_row[:, :STATE2])
            new.append(a1 * xs + a2s * x + s_row[:, STATE2:])
        return tuple(new)

    zero = jnp.zeros((1, STATE2), F32)
    lax.fori_loop(0, n_chunks, step, (zero,) * (2 * bn))
    o_ref[0] = o_ref[0] + jnp.dot(xin_ref[...].astype(BF16), v_ref[0], preferred_element_type=F32)


def _glu_kernel(cx_ref, u_ref, d_ref, w_ref, b_ref, o_ref):
    y = jax.nn.gelu(cx_ref[...] + d_ref[...] * u_ref[...])
    gate = jnp.dot(y.astype(BF16), w_ref[...], preferred_element_type=F32) + b_ref[...]
    o_ref[...] = (y * jax.nn.sigmoid(gate)).astype(o_ref.dtype)


def s5_branch(h, col, lam_re, lam_im, log_dt, b_re, b_im, c_re, c_im, d_skip, glu_w, glu_b, bn, s):
    n_chunks = s // S5_CHUNK
    rows = bn * n_chunks
    t = bn * s
    tw, v, coef = _s5_tables(lam_re, lam_im, log_dt, b_re, b_im, c_re, c_im)
    u = h[:, col * C_WIDTH:(col + 1) * C_WIDTH].astype(BF16)
    uf = jnp.transpose(u.reshape(bn, n_chunks, S5_CHUNK, C_GROUPS, C_GROUP), (3, 0, 1, 2, 4)).reshape(C_GROUPS, rows, S5_ROW)
    per_group = lambda shape: pl.BlockSpec((1,) + shape, lambda g: (g, 0, 0))
    cx = pl.pallas_call(
        functools.partial(_s5_kernel, bn=bn),
        grid=(C_GROUPS,),
        in_specs=[per_group((rows, S5_ROW)), per_group((S5_ROW, S5_ROW + 2 * STATE2)), per_group((STATE2, S5_ROW)),
                  per_group((3, STATE2))],
        out_specs=per_group((rows, S5_ROW)),
        out_shape=jax.ShapeDtypeStruct((C_GROUPS, rows, S5_ROW), F32),
        scratch_shapes=[pltpu.VMEM((rows, 2 * STATE2), F32), pltpu.VMEM((rows, STATE2), F32)],
        compiler_params=pltpu.CompilerParams(dimension_semantics=("arbitrary",), vmem_limit_bytes=VMEM_LIMIT_BYTES),
        name="s5_scan",
    )(uf, tw, v, coef)
    cx = jnp.transpose(cx.reshape(C_GROUPS, bn, n_chunks, S5_CHUNK, C_GROUP), (1, 2, 3, 0, 4)).reshape(t, C_WIDTH)
    tm = _pick_tile(t, (1024, 512, 256, 128, 8))
    row = lambda c: pl.BlockSpec((tm, C_WIDTH), lambda i: (i, c))
    vec = pl.BlockSpec((1, C_WIDTH), lambda i: (0, 0))
    return pl.pallas_call(
        _glu_kernel,
        grid=(t // tm,),
        in_specs=[row(0), row(col), vec, pl.BlockSpec((C_WIDTH, C_WIDTH), lambda i: (0, 0)), vec],
        out_specs=row(0),
        out_shape=jax.ShapeDtypeStruct((t, C_WIDTH), BF16),
        compiler_params=pltpu.CompilerParams(dimension_semantics=("arbitrary",), vmem_limit_bytes=VMEM_LIMIT_BYTES),
        name="s5_glu",
    )(cx, h, d_skip.reshape(1, C_WIDTH), glu_w.astype(BF16), glu_b.reshape(1, C_WIDTH))


def hybrid_mixer(x, rel_bias, w_in, a_kv_norm, a_kv_up, b_conv, b_a_log, b_dt_bias, b_out_norm,
                 c_lambda_re, c_lambda_im, c_log_dt, c_b_re, c_b_im, c_c_re, c_c_im, c_d, c_glu_w, c_glu_b,
                 w_br_a, w_br_b, w_br_c, w_o, ln_g, ln_b):
    bn, s, d = x.shape
    t = bn * s
    xt = x.reshape(t, d)
    offsets = [0] + np.cumsum(SPLITS).tolist()
    w_cols = [w_in[:, offsets[i]:offsets[i + 1]] for i in range(len(SPLITS))]
    (w_aq, w_ckv, w_iq, w_ik, w_iw, w_bqkv, w_bz, w_ba, w_bb, w_cu, w_gate) = w_cols
    w_ik_pad = jnp.pad(w_ik, ((0, 0), (0, LANE - IDX_DIM)))
    w_rows = jnp.concatenate([w_gate, w_cu, w_bqkv, w_bz, w_ckv, w_ik_pad], axis=1)
    w_rows = jnp.pad(w_rows, ((0, 0), (0, -w_rows.shape[1] % PROJ_TILE_N)))
    h = matmul(xt, w_rows)
    o = np.cumsum([0, 3 * D_MODEL, C_WIDTH, 3 * B_WIDTH, B_WIDTH, A_KV_RANK, LANE]).tolist()
    assert o[1] % C_WIDTH == 0
    idx_k = h[:, o[5]:o[5] + IDX_DIM].astype(BF16).reshape(bn, s, IDX_DIM)
    qiq_t = matmul_nt(jnp.concatenate([w_aq * A_HEAD_DIM ** -0.5, w_iq], axis=1).T, xt, bn, s)
    scal_t = matmul_nt(jnp.concatenate([w_ba, w_bb, w_iw], axis=1).T, xt, bn, s, out_dtype=F32)
    k, vt = kv_project(h, o[4] // LANE, a_kv_norm, a_kv_up, bn, s)
    ya = dsa_attention(qiq_t, scal_t, idx_k, k, vt, _rel_bias_tiles(rel_bias), bn, s)
    yb = gated_deltanet(h, o[2] // LANE, scal_t, b_conv, b_a_log, b_dt_bias, b_out_norm, bn, s)
    yc = s5_branch(h, o[1] // C_WIDTH, c_lambda_re, c_lambda_im, c_log_dt, c_b_re, c_b_im, c_c_re, c_c_im, c_d,
                   c_glu_w, c_glu_b, bn, s)
    return merge_branches(ya.reshape(t, A_WIDTH), yb, yc, h, w_br_a, w_br_b, w_br_c, w_o, xt, ln_g, ln_b)


def _merge_kernel(ya_ref, yb_ref, yc_ref, ga_ref, gb_ref, gc_ref, wa_ref, wb_ref, wc_ref, wo_ref, x_ref, g_ref, b_ref, o_ref):
    branch = lambda y_ref, w_ref, gate_ref: jax.nn.sigmoid(gate_ref[...]) * jnp.dot(y_ref[...], w_ref[...], preferred_element_type=F32)
    merged = branch(ya_ref, wa_ref, ga_ref) + branch(yb_ref, wb_ref, gb_ref) + branch(yc_ref, wc_ref, gc_ref)
    mix = jnp.dot(merged.astype(BF16), wo_ref[...], preferred_element_type=F32)
    z = DEEPNORM_ALPHA * x_ref[...] + mix
    mu = jnp.mean(z, -1, keepdims=True)
    zc = z - mu
    var = jnp.mean(zc * zc, -1, keepdims=True)
    o_ref[...] = zc * lax.rsqrt(var + LN_EPS) * g_ref[...] + b_ref[...]


def merge_branches(ya, yb, yc, h, w_br_a, w_br_b, w_br_c, w_o, x, ln_g, ln_b):
    t, d = x.shape
    tm = _pick_tile(t, (512, 256, 128, 8))
    row = lambda width, col=0: pl.BlockSpec((tm, width), lambda i: (i, col))
    full = lambda a: pl.BlockSpec(a.shape, lambda i: (0, 0))
    ws = [w.astype(BF16) for w in (w_br_a, w_br_b, w_br_c, w_o)]
    return pl.pallas_call(
        _merge_kernel,
        grid=(t // tm,),
        in_specs=[row(A_WIDTH), row(B_WIDTH), row(C_WIDTH), row(d, 0), row(d, 1), row(d, 2)] + [full(w) for w in ws]
                 + [row(d), pl.BlockSpec((1, d), lambda i: (0, 0)), pl.BlockSpec((1, d), lambda i: (0, 0))],
        out_specs=row(d),
        out_shape=jax.ShapeDtypeStruct((t, d), F32),
        compiler_params=pltpu.CompilerParams(dimension_semantics=("arbitrary",), vmem_limit_bytes=VMEM_LIMIT_BYTES),
        name="merge_branches",
    )(ya, yb, yc, h, h, h, *ws, x, ln_g.reshape(1, d), ln_b.reshape(1, d))


ROUTER_TILE = 512


def _router_kernel(x_ref, rwt_ref, bias_ref, tri_ref, idx_ref, gate_ref, pos_ref, cnt_ref, carry_ref):
    @pl.when(pl.program_id(0) == 0)
    def _():
        carry_ref[...] = jnp.zeros_like(carry_ref)

    logits = lax.dot_general(rwt_ref[...], x_ref[...].astype(BF16), (((1,), (1,)), ((), ())), preferred_element_type=F32)
    scores = jax.nn.sigmoid(logits)
    remaining = scores + bias_ref[...]
    expert = lax.broadcasted_iota(jnp.int32, scores.shape, 0)
    picks = []
    for _ in range(TOP_K):
        best = jnp.max(remaining, axis=0, keepdims=True)
        first = jnp.min(jnp.where(remaining == best, expert, N_EXPERTS), axis=0, keepdims=True)
        pick = expert == first
        picks.append((first, pick))
        remaining = jnp.where(pick, -jnp.inf, remaining)
    chosen = picks[0][1]
    for _, pick in picks[1:]:
        chosen = chosen | pick
    total = jnp.sum(jnp.where(chosen, scores, 0.0), axis=0, keepdims=True)
    prefix = jnp.dot(jnp.where(chosen, 1.0, 0.0).astype(BF16), tri_ref[...], preferred_element_type=F32)
    rank = carry_ref[...] + prefix.astype(jnp.int32) - 1
    carry_ref[...] = carry_ref[...] + prefix[:, -1:].astype(jnp.int32)
    cnt_ref[...] = carry_ref[...]
    for k, (first, pick) in enumerate(picks):
        idx_ref[k:k + 1, :] = first
        gate_ref[k:k + 1, :] = jnp.sum(jnp.where(pick, scores, 0.0), axis=0, keepdims=True) / total * ROUTED_SCALE
        pos_ref[k:k + 1, :] = jnp.sum(jnp.where(pick, rank, 0), axis=0, keepdims=True)


def moe_router(xt, router_w, router_bias):
    t, d = xt.shape
    tile = ROUTER_TILE
    tri = (jnp.arange(tile)[:, None] <= jnp.arange(tile)[None, :]).astype(BF16)
    row = pl.BlockSpec((TOP_K, tile), lambda i: (0, i))
    return pl.pallas_call(
        _router_kernel,
        grid=(t // tile,),
        in_specs=[pl.BlockSpec((tile, d), lambda i: (i, 0)), pl.BlockSpec((N_EXPERTS, d), lambda i: (0, 0)),
                  pl.BlockSpec((N_EXPERTS, 1), lambda i: (0, 0)), pl.BlockSpec((tile, tile), lambda i: (0, 0))],
        out_specs=[row, row, row, pl.BlockSpec((N_EXPERTS, 1), lambda i: (0, 0))],
        out_shape=[jax.ShapeDtypeStruct((TOP_K, t), jnp.int32), jax.ShapeDtypeStruct((TOP_K, t), F32),
                   jax.ShapeDtypeStruct((TOP_K, t), jnp.int32), jax.ShapeDtypeStruct((N_EXPERTS, 1), jnp.int32)],
        scratch_shapes=[pltpu.VMEM((N_EXPERTS, 1), jnp.int32)],
        compiler_params=pltpu.CompilerParams(dimension_semantics=("arbitrary",), vmem_limit_bytes=VMEM_LIMIT_BYTES),
        name="moe_router",
    )(xt, router_w.T.astype(BF16), router_bias.reshape(N_EXPERTS, 1), tri)


def _expert_kernel(blk_expert_ref, blk_rows_ref, x_ref, wgu_ref, wdn_ref, o_ref, wgu_bf_ref, wdn_bf_ref):
    b = pl.program_id(0)
    n_valid = blk_rows_ref[b]

    @pl.when((b == 0) | (blk_expert_ref[b] != blk_expert_ref[jnp.maximum(b - 1, 0)]))
    def _():
        wgu_bf_ref[...] = wgu_ref[0].astype(BF16)
        wdn_bf_ref[...] = wdn_ref[0].astype(BF16)

    @pl.when(n_valid > 0)
    def _():
        row = lax.broadcasted_iota(jnp.int32, x_ref.shape, 0)
        x = jnp.where(row < n_valid, x_ref[...], 0.0).astype(BF16)
        h = jnp.dot(x, wgu_bf_ref[...], preferred_element_type=F32)
        hg, hu = h[:, :D_EXPERT], h[:, D_EXPERT:]
        act = (hg * jax.nn.sigmoid(hg) * hu).astype(BF16)
        o_ref[...] = jnp.dot(act, wdn_bf_ref[...], preferred_element_type=F32)

    @pl.when(n_valid <= 0)
    def _():
        o_ref[...] = jnp.zeros_like(o_ref)


def expert_ffn(xs, blk_expert, blk_rows, w_gu, w_dn):
    n_pad, d = xs.shape
    n_blk = n_pad // MOE_BLOCK
    return pl.pallas_call(
        _expert_kernel,
        grid_spec=pltpu.PrefetchScalarGridSpec(
            num_scalar_prefetch=2,
            grid=(n_blk,),
            in_specs=[pl.BlockSpec((MOE_BLOCK, d), lambda b, be, br: (b, 0)),
                      pl.BlockSpec((1, d, 2 * D_EXPERT), lambda b, be, br: (be[b], 0, 0)),
                      pl.BlockSpec((1, D_EXPERT, d), lambda b, be, br: (be[b], 0, 0))],
            out_specs=pl.BlockSpec((MOE_BLOCK, d), lambda b, be, br: (b, 0)),
            scratch_shapes=[pltpu.VMEM((d, 2 * D_EXPERT), BF16), pltpu.VMEM((D_EXPERT, d), BF16)]),
        out_shape=jax.ShapeDtypeStruct((n_pad, d), F32),
        compiler_params=pltpu.CompilerParams(dimension_semantics=("arbitrary",), vmem_limit_bytes=VMEM_LIMIT_BYTES),
        name="expert_ffn",
    )(blk_expert, blk_rows, xs, w_gu, w_dn)


DISPATCH_TILE = 256
COMBINE_TILE = 128


def _rows_kernel(pad_start_ref, idx_ref, pos_ref, o_ref):
    idx = idx_ref[...]
    row = pos_ref[...]
    for e in range(N_EXPERTS):
        row = row + jnp.where(idx == e, pad_start_ref[e], 0)
    o_ref[...] = row


def moe_rows(pad_start, idx, pos):
    k, t = idx.shape
    tile = _pick_tile(t, (4096, 2048, 1024, 512, 256, 128))
    blk = pl.BlockSpec((k, tile), lambda i, ps: (0, i))
    return pl.pallas_call(
        _rows_kernel,
        grid_spec=pltpu.PrefetchScalarGridSpec(num_scalar_prefetch=1, grid=(t // tile,), in_specs=[blk, blk], out_specs=blk),
        out_shape=jax.ShapeDtypeStruct((k, t), jnp.int32),
        compiler_params=pltpu.CompilerParams(dimension_semantics=("arbitrary",)),
        name="moe_rows",
    )(pad_start, idx, pos)


def _dispatch_kernel(dest_ref, x_ref, xs_hbm, sem):
    def issue(r, carry):
        for k in range(TOP_K):
            row = dest_ref[0, 0, r * TOP_K + k]
            pltpu.make_async_copy(x_ref.at[pl.ds(r, 1)], xs_hbm.at[pl.ds(row, 1)], sem).start(priority=k % 2)
        return carry

    lax.fori_loop(0, DISPATCH_TILE, issue, 0)
    n_rows = DISPATCH_TILE * TOP_K
    pltpu.make_async_copy(xs_hbm.at[pl.ds(0, n_rows)], xs_hbm.at[pl.ds(0, n_rows)], sem).wait()


def moe_dispatch(xt, dest_tok_major, n_pad):
    t, d = xt.shape
    n_tiles = t // DISPATCH_TILE
    return pl.pallas_call(
        _dispatch_kernel,
        grid=(n_tiles,),
        in_specs=[pl.BlockSpec((1, 1, DISPATCH_TILE * TOP_K), lambda i: (i, 0, 0), memory_space=pltpu.SMEM),
                  pl.BlockSpec((DISPATCH_TILE, d), lambda i: (i, 0))],
        out_specs=pl.BlockSpec(memory_space=pl.ANY),
        scratch_shapes=[pltpu.SemaphoreType.DMA(())],
        out_shape=jax.ShapeDtypeStruct((n_pad, d), xt.dtype),
        compiler_params=pltpu.CompilerParams(dimension_semantics=("arbitrary",), has_side_effects=True),
        name="moe_dispatch",
    )(dest_tok_major.reshape(n_tiles, 1, -1), xt)


def _combine_kernel(dest_ref, dest_next_ref, ys_hbm, gate_ref, x_ref, wgu_ref, wdn_ref, g_ref, b_ref, o_ref, buf_ref, sem):
    i = pl.program_id(0)
    n = pl.num_programs(0)
    slot = lax.rem(i, 2)

    def gather(d_ref, into):
        def issue(r, carry):
            for k in range(TOP_K):
                row = d_ref[0, 0, r * TOP_K + k]
                pltpu.make_async_copy(ys_hbm.at[pl.ds(row, 1)], buf_ref.at[into, k, pl.ds(r, 1)],
                                      sem.at[into]).start(priority=k % 2)
            return carry
        lax.fori_loop(0, COMBINE_TILE, issue, 0)

    @pl.when(i == 0)
    def _():
        gather(dest_ref, 0)

    @pl.when(i + 1 < n)
    def _():
        gather(dest_next_ref, 1 - slot)

    x = x_ref[...]
    h = jnp.dot(x.astype(BF16), wgu_ref[...], preferred_element_type=F32)
    hg, hu = h[:, :D_SHARED], h[:, D_SHARED:]
    acc = DEEPNORM_ALPHA * x + jnp.dot((hg * jax.nn.sigmoid(hg) * hu).astype(BF16), wdn_ref[...], preferred_element_type=F32)
    pltpu.make_async_copy(buf_ref.at[slot], buf_ref.at[slot], sem.at[slot]).wait()
    gate = gate_ref[...]
    for k in range(TOP_K):
        acc = acc + buf_ref[slot, k] * gate[:, k:k + 1]
    mu = jnp.mean(acc, -1, keepdims=True)
    zc = acc - mu
    var = jnp.mean(zc * zc, -1, keepdims=True)
    o_ref[...] = zc * lax.rsqrt(var + LN_EPS) * g_ref[...] + b_ref[...]


def moe_combine(ys, dest_tok_major, gate_tok_major, xt, sh_w_gu, sh_w_down, ln_g, ln_b):
    t, d = xt.shape
    tm = COMBINE_TILE
    n_tiles = t // tm
    dest3 = dest_tok_major.reshape(n_tiles, 1, -1)
    row = pl.BlockSpec((tm, d), lambda i: (i, 0))
    full = lambda shape: pl.BlockSpec(shape, lambda i: (0, 0))
    this = pl.BlockSpec((1, 1, tm * TOP_K), lambda i: (i, 0, 0), memory_space=pltpu.SMEM)
    nxt = pl.BlockSpec((1, 1, tm * TOP_K), lambda i: (jnp.minimum(i + 1, n_tiles - 1), 0, 0), memory_space=pltpu.SMEM)
    return pl.pallas_call(
        _combine_kernel,
        grid=(n_tiles,),
        in_specs=[this, nxt, pl.BlockSpec(memory_space=pl.ANY), pl.BlockSpec((tm, TOP_K), lambda i: (i, 0)), row,
                  full((d, 2 * D_SHARED)), full((D_SHARED, d)), full((1, d)), full((1, d))],
        out_specs=row,
        scratch_shapes=[pltpu.VMEM((2, TOP_K, tm, d), F32), pltpu.SemaphoreType.DMA((2,))],
        out_shape=jax.ShapeDtypeStruct((t, d), F32),
        compiler_params=pltpu.CompilerParams(dimension_semantics=("arbitrary",), vmem_limit_bytes=VMEM_LIMIT_BYTES),
        name="moe_combine",
    )(dest3, dest3, ys, gate_tok_major, xt, sh_w_gu.astype(BF16), sh_w_down.astype(BF16),
      ln_g.reshape(1, d), ln_b.reshape(1, d))


def moe_ffn(xt, router_w, router_bias, exp_w_gu, exp_w_down, sh_w_gu, sh_w_down, ln_g, ln_b):
    n_tok, d = xt.shape
    idx, gate, pos, counts = moe_router(xt, router_w, router_bias)
    counts = counts[:, 0]
    padded = (counts + MOE_BLOCK - 1) // MOE_BLOCK * MOE_BLOCK
    pad_ends = jnp.cumsum(padded)
    pad_start = pad_ends - padded
    n_blk = n_tok * TOP_K // MOE_BLOCK + N_EXPERTS
    n_pad = n_blk * MOE_BLOCK
    blk_lo = jnp.arange(n_blk, dtype=jnp.int32) * MOE_BLOCK
    owner = (pad_start[None, :] <= blk_lo[:, None]) & (blk_lo[:, None] < pad_ends[None, :])
    blk_expert = jnp.minimum(jnp.sum(pad_ends[None, :] <= blk_lo[:, None], axis=1), N_EXPERTS - 1).astype(jnp.int32)
    blk_rows = jnp.clip(jnp.sum(jnp.where(owner, (pad_start + counts)[None, :] - blk_lo[:, None], 0), axis=1),
                        0, MOE_BLOCK).astype(jnp.int32)
    dest = moe_rows(pad_start.astype(jnp.int32), idx, pos).T.reshape(-1)
    xs = moe_dispatch(xt, dest, n_pad)
    ys = expert_ffn(xs, blk_expert, blk_rows, exp_w_gu, exp_w_down)
    return moe_combine(ys, dest, gate.T, xt, sh_w_gu, sh_w_down, ln_g, ln_b)


def kernel(x, rel_bias, w_in, a_kv_norm, a_kv_up, b_conv, b_a_log, b_dt_bias, b_out_norm, c_lambda_re, c_lambda_im, c_log_dt, c_b_re, c_b_im, c_c_re, c_c_im, c_d, c_glu_w, c_glu_b, w_br_a, w_br_b, w_br_c, w_o, ln1_g, ln1_b, router_w, router_bias, exp_w_gu, exp_w_down, sh_w_gu, sh_w_down, ln2_g, ln2_b):
    bn, s, d = x.shape
    t = bn * s
    for i in range(w_in.shape[0]):
        xt = hybrid_mixer(x, rel_bias, w_in[i], a_kv_norm[i], a_kv_up[i], b_conv[i], b_a_log[i], b_dt_bias[i],
                          b_out_norm[i], c_lambda_re[i], c_lambda_im[i], c_log_dt[i], c_b_re[i], c_b_im[i],
                          c_c_re[i], c_c_im[i], c_d[i], c_glu_w[i], c_glu_b[i], w_br_a[i], w_br_b[i], w_br_c[i], w_o[i],
                          ln1_g[i], ln1_b[i])
        x = moe_ffn(xt, router_w[i], router_bias[i], exp_w_gu[i], exp_w_down[i], sh_w_gu[i], sh_w_down[i],
                    ln2_g[i], ln2_b[i]).reshape(bn, s, d)
    return x
```

```python
import functools
import math

import jax
import jax.numpy as jnp
import numpy as np
from jax import lax
from jax.experimental import pallas as pl
from jax.experimental.pallas import tpu as pltpu

F32 = jnp.float32
BF16 = jnp.bfloat16

D_MODEL = 1024
DEPTH = 4
A_HEADS = 8
A_HEAD_DIM = 64
A_WIDTH = A_HEADS * A_HEAD_DIM
A_KV_RANK = 128
IDX_HEADS = 8
IDX_DIM = 64
TOPK_MAX = 256
Q_BLOCK = 128
REL_BUCKETS = 32
REL_MAX_DIST = 128
B_HEADS = 4
B_HEAD_DIM = 128
B_WIDTH = B_HEADS * B_HEAD_DIM
CONV_WIDTH = 4
CHUNK = 64
C_WIDTH = 512
C_GROUP = 16
C_GROUPS = C_WIDTH // C_GROUP
C_STATE = 64
N_EXPERTS = 64
TOP_K = 8
D_EXPERT = 256
D_SHARED = 256
ROUTED_SCALE = 2.5
MOE_BLOCK = 512
SPLITS = (A_WIDTH, A_KV_RANK, IDX_HEADS * IDX_DIM, IDX_DIM, IDX_HEADS, 3 * B_WIDTH, B_WIDTH, B_HEADS, B_HEADS, C_WIDTH, 3 * D_MODEL)
DEEPNORM_ALPHA = (2 * DEPTH) ** 0.25
LN_EPS = 1e-5
RMS_EPS = 1e-6
NEG_INF = -1e30

VMEM_LIMIT_BYTES = 48 * 1024 * 1024
LANE = 128
PROJ_TILE_N = 1024


def _pick_tile(n, candidates):
    for c in candidates:
        if n % c == 0:
            return c
    return n


def _mm_kernel(a_ref, b_ref, o_ref):
    o_ref[...] = jnp.dot(a_ref[...].astype(BF16), b_ref[...], preferred_element_type=F32).astype(o_ref.dtype)


def matmul(a, b, out_dtype=F32):
    m, k = a.shape
    _, n = b.shape
    tm = _pick_tile(m, (1024, 512, 256, 128, 8))
    tn = _pick_tile(n, (1024, 768, 512, 384, 256, 128))
    return pl.pallas_call(
        _mm_kernel,
        grid=(m // tm, n // tn),
        in_specs=[pl.BlockSpec((tm, k), lambda i, j: (i, 0)), pl.BlockSpec((k, tn), lambda i, j: (0, j))],
        out_specs=pl.BlockSpec((tm, tn), lambda i, j: (i, j)),
        out_shape=jax.ShapeDtypeStruct((m, n), out_dtype),
        compiler_params=pltpu.CompilerParams(dimension_semantics=("arbitrary", "arbitrary"), vmem_limit_bytes=VMEM_LIMIT_BYTES),
        name="matmul",
    )(a, b.astype(BF16))


def rel_bucket(dist):
    n_exact = REL_BUCKETS // 2
    d = jnp.maximum(dist, 1).astype(F32)
    large = n_exact + (jnp.log(d / n_exact) / math.log(REL_MAX_DIST / n_exact) * (REL_BUCKETS - n_exact)).astype(jnp.int32)
    return jnp.where(dist < n_exact, dist, jnp.minimum(large, REL_BUCKETS - 1))


KEY_BLOCK = 128
DSA_CHUNK = 512
COUNT_CHUNK = 2 * DSA_CHUNK
COUNT_LANES = 8
BLOCKS_PER_CHUNK = DSA_CHUNK // KEY_BLOCK
HEAD_PAIRS = A_HEADS // 2
PAIR_WIDTH = 2 * A_HEAD_DIM
INT_MIN = -2 ** 31
IDX_W_SCALE = IDX_HEADS ** -0.5 * IDX_DIM ** -0.5


def _kv_kernel(c_ref, g_ref, wk_ref, wvt_ref, k_ref, vt_ref):
    c = c_ref[...]
    cn = (c * lax.rsqrt(jnp.mean(c * c, -1, keepdims=True) + RMS_EPS) * g_ref[...]).astype(BF16)
    k_ref[...] = jnp.dot(cn, wk_ref[...], preferred_element_type=F32).astype(BF16)
    vt = lax.dot_general(wvt_ref[...], cn, (((1,), (1,)), ((), ())), preferred_element_type=F32).astype(BF16)
    for i in range(vt_ref.shape[1]):
        vt_ref[0, i] = vt[:, i * KEY_BLOCK:(i + 1) * KEY_BLOCK]


def kv_project(c_kv, col, g, w_kv_up, bn, s):
    ts = 512
    w = w_kv_up.astype(BF16)
    return pl.pallas_call(
        _kv_kernel,
        grid=(bn, s // ts),
        in_specs=[pl.BlockSpec((ts, A_KV_RANK), lambda b, i: (b * (s // ts) + i, col)),
                  pl.BlockSpec((1, A_KV_RANK), lambda b, i: (0, 0)),
                  pl.BlockSpec((A_KV_RANK, A_WIDTH), lambda b, i: (0, 0)),
                  pl.BlockSpec((A_WIDTH, A_KV_RANK), lambda b, i: (0, 0))],
        out_specs=[pl.BlockSpec((None, ts, A_WIDTH), lambda b, i: (b, i, 0)),
                   pl.BlockSpec((1, ts // KEY_BLOCK, A_WIDTH, KEY_BLOCK), lambda b, i: (b, i, 0, 0))],
        out_shape=[jax.ShapeDtypeStruct((bn, s, A_WIDTH), BF16),
                   jax.ShapeDtypeStruct((bn, s // KEY_BLOCK, A_WIDTH, KEY_BLOCK), BF16)],
        compiler_params=pltpu.CompilerParams(dimension_semantics=("arbitrary", "arbitrary"), vmem_limit_bytes=VMEM_LIMIT_BYTES),
        name="kv_project",
    )(c_kv, g.reshape(1, A_KV_RANK), w[:, :A_WIDTH], w[:, A_WIDTH:].T)


def _mm_nt_kernel(wt_ref, x_ref, o_ref):
    o_ref[0] = lax.dot_general(wt_ref[...], x_ref[...].astype(BF16), (((1,), (1,)), ((), ())),
                               preferred_element_type=F32).astype(o_ref.dtype)


def matmul_nt(wt, x, bn, s, out_dtype=BF16):
    n, k = wt.shape
    ts = 512
    return pl.pallas_call(
        _mm_nt_kernel,
        grid=(bn, s // ts),
        in_specs=[pl.BlockSpec((n, k), lambda b, i: (0, 0)),
                  pl.BlockSpec((ts, k), lambda b, i: (b * (s // ts) + i, 0))],
        out_specs=pl.BlockSpec((1, n, ts), lambda b, i: (b, 0, i)),
        out_shape=jax.ShapeDtypeStruct((bn, n, s), out_dtype),
        compiler_params=pltpu.CompilerParams(dimension_semantics=("arbitrary", "arbitrary"), vmem_limit_bytes=VMEM_LIMIT_BYTES),
        name="matmul_nt",
    )(wt.astype(BF16), x)


def _sortable_key(score):
    bits = pltpu.bitcast(score, jnp.int32)
    return bits ^ (lax.shift_right_arithmetic(bits, 31) & jnp.int32(0x7FFFFFFF))


def _dsa_kernel(qt_ref, iqt_ref, wt_ref, kidx_ref, k_ref, vt_ref, bias_ref, o_ref,
                skey_ref, widx_ref, wq_ref, acc_ref, m_ref, l_ref, *, k_sel):
    j = pl.program_id(1)
    n_chunks = (j + BLOCKS_PER_CHUNK) // BLOCKS_PER_CHUNK
    q_pos = j * Q_BLOCK + lax.broadcasted_iota(jnp.int32, (1, Q_BLOCK), 1)

    for h in range(IDX_HEADS):
        widx_ref[:, h * Q_BLOCK:(h + 1) * Q_BLOCK] = iqt_ref[0, h * IDX_DIM:(h + 1) * IDX_DIM, :]
    wq_ref[...] = jnp.zeros_like(wq_ref)
    for p in range(HEAD_PAIRS):
        wq_ref[p, 0:A_HEAD_DIM, 0:Q_BLOCK] = qt_ref[0, (2 * p) * A_HEAD_DIM:(2 * p + 1) * A_HEAD_DIM, :]
        wq_ref[p, A_HEAD_DIM:PAIR_WIDTH, Q_BLOCK:2 * Q_BLOCK] = qt_ref[0, (2 * p + 1) * A_HEAD_DIM:(2 * p + 2) * A_HEAD_DIM, :]

    w_rows = [wt_ref[0, h:h + 1, :] * IDX_W_SCALE for h in range(IDX_HEADS)]

    def idx_chunk(c, carry):
        start = pl.multiple_of(c * DSA_CHUNK, DSA_CHUNK)
        kc = kidx_ref[0, pl.ds(start, DSA_CHUNK), :]
        score = jnp.zeros((DSA_CHUNK, Q_BLOCK), F32)
        for p in range(IDX_HEADS // 2):
            z = jnp.dot(kc, widx_ref[:, p * 2 * Q_BLOCK:(p + 1) * 2 * Q_BLOCK], preferred_element_type=F32)
            score = score + jnp.maximum(z[:, :Q_BLOCK], 0.0) * w_rows[2 * p]
            score = score + jnp.maximum(z[:, Q_BLOCK:], 0.0) * w_rows[2 * p + 1]
        pos = start + lax.broadcasted_iota(jnp.int32, (DSA_CHUNK, Q_BLOCK), 0)
        skey_ref[pl.ds(start, DSA_CHUNK), :] = jnp.where(pos <= q_pos, _sortable_key(score), INT_MIN)
        return carry

    lax.fori_loop(0, n_chunks, idx_chunk, 0)

    n_count = (n_chunks + COUNT_CHUNK // DSA_CHUNK - 1) // (COUNT_CHUNK // DSA_CHUNK)

    @pl.when(n_chunks * DSA_CHUNK < n_count * COUNT_CHUNK)
    def _():
        skey_ref[pl.ds(pl.multiple_of(n_chunks * DSA_CHUNK, DSA_CHUNK), DSA_CHUNK), :] = jnp.full(
            (DSA_CHUNK, Q_BLOCK), INT_MIN, jnp.int32)

    def count_ge(cand):
        def body(c, cnt):
            start = pl.multiple_of(c * COUNT_CHUNK, COUNT_CHUNK)
            hit = (skey_ref[pl.ds(start, COUNT_CHUNK), :] >= cand).astype(jnp.int32)
            return cnt + jnp.sum(hit.reshape(COUNT_CHUNK // (8 * COUNT_LANES), COUNT_LANES, 8, Q_BLOCK), axis=0)
        cnt = lax.fori_loop(0, n_count, body, jnp.zeros((COUNT_LANES, 8, Q_BLOCK), jnp.int32))
        return jnp.sum(jnp.sum(cnt, axis=0), axis=0, keepdims=True)

    def bit_step(i, carry):
        thr_biased, n_ge = carry
        cand_biased = thr_biased | lax.shift_left(jnp.int32(1), 31 - i)
        cnt = count_ge(cand_biased ^ INT_MIN)
        keep = cnt >= k_sel
        return jnp.where(keep, cand_biased, thr_biased), jnp.where(keep, cnt, n_ge)

    thr, n_ge = lax.fori_loop(0, 32, bit_step, (jnp.zeros((1, Q_BLOCK), jnp.int32),
                                                jnp.full((1, Q_BLOCK), k_sel, jnp.int32)))
    thr = thr ^ INT_MIN

    tied = (n_ge > k_sel) & (thr > INT_MIN)

    @pl.when(jnp.max(tied.astype(jnp.int32)) > 0)
    def _():
        need = k_sel - count_ge(thr + 1)

        def count_tied_upto(limit):
            def body(c, cnt):
                start = pl.multiple_of(c * DSA_CHUNK, DSA_CHUNK)
                pos = start + lax.broadcasted_iota(jnp.int32, (DSA_CHUNK, Q_BLOCK), 0)
                hit = jnp.where(skey_ref[pl.ds(start, DSA_CHUNK), :] == thr, (pos <= limit).astype(jnp.int32), 0)
                return cnt + jnp.sum(hit.reshape(DSA_CHUNK // 8, 8, Q_BLOCK), axis=0)
            cnt = lax.fori_loop(0, n_chunks, body, jnp.zeros((8, Q_BLOCK), jnp.int32))
            return jnp.sum(cnt, axis=0, keepdims=True)

        def pos_step(i, lim):
            cand = lim - lax.shift_left(jnp.int32(1), 30 - i)
            return jnp.where(count_tied_upto(cand) >= need, cand, lim)

        limit = lax.fori_loop(0, 31, pos_step, jnp.full((1, Q_BLOCK), 2 ** 31 - 1, jnp.int32))

        def demote(c, carry):
            start = pl.multiple_of(c * DSA_CHUNK, DSA_CHUNK)
            pos = start + lax.broadcasted_iota(jnp.int32, (DSA_CHUNK, Q_BLOCK), 0)
            sk = skey_ref[pl.ds(start, DSA_CHUNK), :]
            skey_ref[pl.ds(start, DSA_CHUNK), :] = jnp.where(tied & (sk == thr) & (pos > limit), thr - 1, sk)
            return carry

        lax.fori_loop(0, n_chunks, demote, 0)

    m_ref[...] = jnp.full_like(m_ref, NEG_INF)
    l_ref[...] = jnp.zeros_like(l_ref)
    acc_ref[...] = jnp.zeros_like(acc_ref)

    def attend(s_list, vt_of, n_blocks):
        for p in range(HEAD_PAIRS):
            s = s_list[p]
            m_old = m_ref[p]
            m_new = jnp.maximum(m_old, jnp.max(s, axis=0, keepdims=True))
            alpha = jnp.exp(m_old - m_new)
            pexp = jnp.exp(s - m_new)
            l_ref[p] = l_ref[p] * alpha + jnp.sum(pexp, axis=0, keepdims=True)
            pb = pexp.astype(BF16)
            pv = jnp.dot(vt_of(p, 0), pb[0:KEY_BLOCK], preferred_element_type=F32)
            for i in range(1, n_blocks):
                pv = pv + jnp.dot(vt_of(p, i), pb[i * KEY_BLOCK:(i + 1) * KEY_BLOCK], preferred_element_type=F32)
            acc_ref[p] = acc_ref[p] * alpha + pv
            m_ref[p] = m_new

    far_end = (j - 1) * KEY_BLOCK
    n_far = (jnp.maximum(j - 1, 0) + BLOCKS_PER_CHUNK - 1) // BLOCKS_PER_CHUNK

    def far_chunk(c, carry):
        start = pl.multiple_of(c * DSA_CHUNK, DSA_CHUNK)
        pos = start + lax.broadcasted_iota(jnp.int32, (DSA_CHUNK, Q_BLOCK), 0)
        sel = (skey_ref[pl.ds(start, DSA_CHUNK), :] >= thr) & (pos < far_end)
        madd = jnp.where(sel, 0.0, NEG_INF)
        madd2 = jnp.concatenate([madd, madd], axis=1)
        kc = k_ref[0, pl.ds(start, DSA_CHUNK), :]
        s_list = [jnp.dot(kc[:, p * PAIR_WIDTH:(p + 1) * PAIR_WIDTH], wq_ref[p], preferred_element_type=F32) + madd2
                  for p in range(HEAD_PAIRS)]
        blk0 = c * BLOCKS_PER_CHUNK
        attend(s_list, lambda p, i: vt_ref[0, blk0 + i, p * PAIR_WIDTH:(p + 1) * PAIR_WIDTH, :], BLOCKS_PER_CHUNK)
        return carry

    lax.fori_loop(0, n_far, far_chunk, 0)

    def near_block(kb, delta):
        start = pl.multiple_of(kb * KEY_BLOCK, KEY_BLOCK)
        pos = start + lax.broadcasted_iota(jnp.int32, (KEY_BLOCK, Q_BLOCK), 0)
        sel = (skey_ref[pl.ds(start, KEY_BLOCK), :] >= thr) & (pos <= q_pos)
        madd = jnp.where(sel, 0.0, NEG_INF)
        madd2 = jnp.concatenate([madd, madd], axis=1)
        kc = k_ref[0, pl.ds(start, KEY_BLOCK), :]
        s_list = [jnp.dot(kc[:, p * PAIR_WIDTH:(p + 1) * PAIR_WIDTH], wq_ref[p], preferred_element_type=F32)
                  + bias_ref[delta, p] + madd2 for p in range(HEAD_PAIRS)]
        attend(s_list, lambda p, i: vt_ref[0, kb, p * PAIR_WIDTH:(p + 1) * PAIR_WIDTH, :], 1)

    @pl.when(j > 0)
    def _():
        near_block(j - 1, 1)

    near_block(j, 0)

    outs = []
    for p in range(HEAD_PAIRS):
        o = acc_ref[p] / l_ref[p]
        outs.append(o[0:A_HEAD_DIM, 0:Q_BLOCK])
        outs.append(o[A_HEAD_DIM:PAIR_WIDTH, Q_BLOCK:2 * Q_BLOCK])
    o_ref[0] = jnp.concatenate(outs, axis=0).T.astype(o_ref.dtype)


def _rel_bias_tiles(rel_bias):
    tab = rel_bias[rel_bucket(jnp.arange(2 * Q_BLOCK, dtype=jnp.int32))] - rel_bias[REL_BUCKETS - 1]
    key = jnp.arange(KEY_BLOCK)[:, None]
    qry = jnp.arange(Q_BLOCK)[None, :]
    tiles = []
    for delta in (0, 1):
        dist = jnp.maximum(delta * Q_BLOCK + qry - key, 0)
        t = jnp.moveaxis(tab[dist], 2, 0)
        t = t.reshape(HEAD_PAIRS, 2, KEY_BLOCK, Q_BLOCK)
        tiles.append(jnp.concatenate([t[:, 0], t[:, 1]], axis=-1))
    return jnp.stack(tiles)


def dsa_attention(qiq_t, wt, k_idx, k, vt, bias_tiles, bn, s):
    assert A_WIDTH == IDX_HEADS * IDX_DIM
    scal_rows = wt.shape[1]
    assert scal_rows % IDX_HEADS == 0
    k_sel = min(TOPK_MAX, s // 4)
    nb = s // Q_BLOCK
    s_pad = s + DSA_CHUNK
    return pl.pallas_call(
        functools.partial(_dsa_kernel, k_sel=k_sel),
        grid=(bn, nb),
        in_specs=[pl.BlockSpec((1, A_WIDTH, Q_BLOCK), lambda b, j: (b, 0, j)),
                  pl.BlockSpec((1, IDX_HEADS * IDX_DIM, Q_BLOCK), lambda b, j: (b, 1, j)),
                  pl.BlockSpec((1, IDX_HEADS, Q_BLOCK), lambda b, j: (b, scal_rows // IDX_HEADS - 1, j)),
                  pl.BlockSpec((1, s, IDX_DIM), lambda b, j: (b, 0, 0)),
                  pl.BlockSpec((1, s, A_WIDTH), lambda b, j: (b, 0, 0)),
                  pl.BlockSpec((1, s // KEY_BLOCK, A_WIDTH, KEY_BLOCK), lambda b, j: (b, 0, 0, 0)),
                  pl.BlockSpec((2, HEAD_PAIRS, KEY_BLOCK, 2 * Q_BLOCK), lambda b, j: (0, 0, 0, 0))],
        out_specs=pl.BlockSpec((1, Q_BLOCK, A_WIDTH), lambda b, j: (b, j, 0)),
        out_shape=jax.ShapeDtypeStruct((bn, s, A_WIDTH), BF16),
        scratch_shapes=[pltpu.VMEM((s_pad, Q_BLOCK), jnp.int32),
                        pltpu.VMEM((IDX_DIM, IDX_HEADS * Q_BLOCK), BF16),
                        pltpu.VMEM((HEAD_PAIRS, PAIR_WIDTH, 2 * Q_BLOCK), BF16),
                        pltpu.VMEM((HEAD_PAIRS, PAIR_WIDTH, 2 * Q_BLOCK), F32),
                        pltpu.VMEM((HEAD_PAIRS, 1, 2 * Q_BLOCK), F32),
                        pltpu.VMEM((HEAD_PAIRS, 1, 2 * Q_BLOCK), F32)],
        compiler_params=pltpu.CompilerParams(dimension_semantics=("arbitrary", "arbitrary"), vmem_limit_bytes=56 * 1024 * 1024),
        name="dsa_attention",
    )(qiq_t, qiq_t, wt, k_idx, k, vt, bias_tiles)


GDN_CHUNK = B_HEAD_DIM
GDN_BLOCK = 1024
GDN_HALO = 8
NEUMANN_STEPS = 6


def _split_bf16(x):
    hi = x.astype(BF16)
    return hi, (x - hi.astype(F32)).astype(BF16)


def _dot_f32(a, b):
    a_hi, a_lo = _split_bf16(a)
    b_hi, b_lo = _split_bf16(b)
    d = lambda x, y: jnp.dot(x, y, preferred_element_type=F32)
    return d(a_hi, b_hi) + (d(a_hi, b_lo) + d(a_lo, b_hi))


def _bdot(a, b):
    return jnp.dot(a.astype(BF16), b.astype(BF16), preferred_element_type=F32)


def _bdot_nt(a, b):
    return lax.dot_general(a.astype(BF16), b.astype(BF16), (((1,), (1,)), ((), ())), preferred_element_type=F32)


def _bdot_tn(a, b):
    return lax.dot_general(a.astype(BF16), b.astype(BF16), (((0,), (0,)), ((), ())), preferred_element_type=F32)


def _softplus(x):
    return jnp.maximum(x, 0.0) + jnp.log(1.0 + jnp.exp(-jnp.abs(x)))


def _gdn_kernel(hq_ref, hk_ref, hv_ref, hz_ref, wq_ref, wk_ref, wv_ref, ab_ref, const_ref, gout_ref, tri_ref,
                o_ref, state_ref, xbuf_ref):
    head = pl.program_id(1)

    @pl.when(pl.program_id(2) == 0)
    def _():
        state_ref[...] = jnp.zeros_like(state_ref)
        xbuf_ref[:, 0:GDN_HALO, :] = jnp.zeros((3, GDN_HALO, B_HEAD_DIM), F32)

    def conv_silu(n, x_ref, w_ref):
        x = x_ref[...]
        xbuf_ref[n, GDN_HALO:, :] = x
        acc = xbuf_ref[n, GDN_HALO - CONV_WIDTH + 1:GDN_HALO - CONV_WIDTH + 1 + GDN_BLOCK, :] * w_ref[0:1, :]
        for j in range(1, CONV_WIDTH):
            lo = GDN_HALO - CONV_WIDTH + 1 + j
            acc = acc + xbuf_ref[n, lo:lo + GDN_BLOCK, :] * w_ref[j:j + 1, :]
        xbuf_ref[n, 0:GDN_HALO, :] = x[GDN_BLOCK - GDN_HALO:, :]
        return acc * jax.nn.sigmoid(acc)

    q = conv_silu(0, hq_ref, wq_ref)
    k = conv_silu(1, hk_ref, wk_ref)
    v = conv_silu(2, hv_ref, wv_ref)
    q = q * lax.rsqrt(jnp.sum(q * q, -1, keepdims=True) + RMS_EPS) * B_HEAD_DIM ** -0.5
    k = k * lax.rsqrt(jnp.sum(k * k, -1, keepdims=True) + RMS_EPS)

    a_row = ab_ref[0, pl.ds(head, 1), :]
    b_row = ab_ref[0, pl.ds(B_HEADS + head, 1), :]
    beta_row = jax.nn.sigmoid(b_row)
    g_row = const_ref[0, 0:1, :] * _softplus(a_row + const_ref[0, 1:2, :])
    gc_rows = _dot_f32(jnp.broadcast_to(g_row, (8, GDN_BLOCK)), tri_ref[...])[0:1, :]

    row_i = lax.broadcasted_iota(jnp.int32, (GDN_CHUNK, GDN_CHUNK), 0)
    col_j = lax.broadcasted_iota(jnp.int32, (GDN_CHUNK, GDN_CHUNK), 1)
    eye = jnp.where(row_i == col_j, 1.0, 0.0)
    z_all = hz_ref[...]
    chunks = []
    for c in range(GDN_BLOCK // GDN_CHUNK):
        sl = slice(c * GDN_CHUNK, (c + 1) * GDN_CHUNK)
        qc, kc, vc = q[sl], k[sl], v[sl]
        g_lane = jnp.broadcast_to(gc_rows[:, sl], (GDN_CHUNK, GDN_CHUNK))
        g_sub = g_lane.T
        beta_sub = jnp.broadcast_to(beta_row[:, sl], (GDN_CHUNK, GDN_CHUNK)).T
        g_end = jnp.broadcast_to(g_lane[:, GDN_CHUNK - 1:GDN_CHUNK], (GDN_CHUNK, GDN_CHUNK))
        decay = jnp.exp(jnp.where(row_i >= col_j, g_sub - g_lane, -jnp.inf))
        kb = kc * beta_sub
        x = -jnp.where(row_i > col_j, _bdot_nt(kb, kc) * decay, 0.0)
        chunks.append(dict(sl=sl, qc=qc, kc=kc, vc=vc, g_sub=g_sub, beta_sub=beta_sub, g_end=g_end, decay=decay,
                           kb=kb, x=x, t_inv=eye + x))
    for _ in range(NEUMANN_STEPS):
        for ch in chunks:
            ch['x'] = _dot_f32(ch['x'], ch['x'])
        for ch in chunks:
            ch['t_inv'] = ch['t_inv'] + _bdot(ch['t_inv'], ch['x'])
    for ch in chunks:
        g_exp = jnp.exp(ch['g_sub'])
        ch['u'] = _bdot(ch['t_inv'], ch['vc'] * ch['beta_sub'])
        ch['w'] = _bdot(ch['t_inv'], ch['kb'] * g_exp)
        ch['qk'] = _bdot_nt(ch['qc'], ch['kc']) * ch['decay']
        ch['q_dec'] = ch['qc'] * g_exp
        ch['k_dec'] = ch['kc'] * jnp.exp(ch['g_end'] - ch['g_sub'])
    for ch in chunks:
        state = state_ref[...]
        v_new = ch['u'] - _bdot(ch['w'], state)
        o = _bdot(ch['q_dec'], state) + _bdot(ch['qk'], v_new)
        state_ref[...] = state * jnp.exp(ch['g_end']) + _bdot_tn(ch['k_dec'], v_new)
        o = o * lax.rsqrt(jnp.mean(o * o, -1, keepdims=True) + RMS_EPS) * gout_ref[...]
        zc = z_all[ch['sl']]
        o_ref[ch['sl'], :] = (o * (zc * jax.nn.sigmoid(zc))).astype(o_ref.dtype)


def gated_deltanet(h, col0, ab_t, conv_w, a_log, dt_bias, out_norm_g, bn, s):
    nblk = s // GDN_BLOCK
    tok = jnp.arange(GDN_BLOCK)
    tri = ((tok[:, None] // GDN_CHUNK == tok[None, :] // GDN_CHUNK) & (tok[:, None] <= tok[None, :])).astype(F32)
    const = jnp.stack([jnp.broadcast_to(-jnp.exp(a_log)[:, None], (B_HEADS, GDN_BLOCK)),
                       jnp.broadcast_to(dt_bias[:, None], (B_HEADS, GDN_BLOCK))], axis=1)
    const = jnp.pad(const, ((0, 0), (0, 6), (0, 0)))
    hcol = lambda part: pl.BlockSpec((GDN_BLOCK, B_HEAD_DIM), lambda b, hh, i: (b * nblk + i, col0 + part * B_HEADS + hh))
    wcol = lambda part: pl.BlockSpec((CONV_WIDTH, B_HEAD_DIM), lambda b, hh, i: (0, part * B_HEADS + hh))
    return pl.pallas_call(
        _gdn_kernel,
        grid=(bn, B_HEADS, nblk),
        in_specs=[hcol(0), hcol(1), hcol(2), hcol(3), wcol(0), wcol(1), wcol(2),
                  pl.BlockSpec((1, 16, GDN_BLOCK), lambda b, hh, i: (b, 0, i)),
                  pl.BlockSpec((1, 8, GDN_BLOCK), lambda b, hh, i: (hh, 0, 0)),
                  pl.BlockSpec((1, B_HEAD_DIM), lambda b, hh, i: (0, 0)),
                  pl.BlockSpec((GDN_BLOCK, GDN_BLOCK), lambda b, hh, i: (0, 0))],
        out_specs=pl.BlockSpec((GDN_BLOCK, B_HEAD_DIM), lambda b, hh, i: (b * nblk + i, hh)),
        out_shape=jax.ShapeDtypeStruct((bn * s, B_WIDTH), BF16),
        scratch_shapes=[pltpu.VMEM((B_HEAD_DIM, B_HEAD_DIM), F32),
                        pltpu.VMEM((3, GDN_HALO + GDN_BLOCK, B_HEAD_DIM), F32)],
        compiler_params=pltpu.CompilerParams(dimension_semantics=("arbitrary", "arbitrary", "arbitrary"), vmem_limit_bytes=VMEM_LIMIT_BYTES),
        name="gated_deltanet",
    )(h, h, h, h, conv_w, conv_w, conv_w, ab_t, const, out_norm_g.reshape(1, B_HEAD_DIM), tri)


S5_CHUNK = 32
S5_ROW = S5_CHUNK * C_GROUP
STATE2 = 2 * C_STATE


def _s5_tables(lam_re, lam_im, log_dt, b_re, b_im, c_re, c_im, d_skip):
    hp = lax.Precision.HIGHEST
    ell = S5_CHUNK
    dt = jnp.exp(log_dt)[:, None]
    lr, li = lam_re, lam_im
    mag = jnp.exp(lr * dt)
    a_re, a_im = mag * jnp.cos(li * dt), mag * jnp.sin(li * dt)
    den = lr * lr + li * li
    f_re = ((a_re - 1.0) * lr + a_im * li) / den
    f_im = (a_im * lr - (a_re - 1.0) * li) / den
    bb_re = f_re[..., None] * b_re - f_im[..., None] * b_im
    bb_im = f_re[..., None] * b_im + f_im[..., None] * b_re
    tau = jnp.arange(ell + 1, dtype=F32)[:, None, None]
    pmag, ang = jnp.exp(tau * (lr * dt)), tau * (li * dt)
    p_re, p_im = pmag * jnp.cos(ang), pmag * jnp.sin(ang)
    ab_re = p_re[:ell, ..., None] * bb_re - p_im[:ell, ..., None] * bb_im
    ab_im = p_re[:ell, ..., None] * bb_im + p_im[:ell, ..., None] * bb_re
    kern = (jnp.einsum('gip,tgpj->tgij', c_re, ab_re, precision=hp)
            - jnp.einsum('gip,tgpj->tgij', c_im, ab_im, precision=hp))
    lag = jnp.arange(ell)[None, :] - jnp.arange(ell)[:, None]
    toep = jnp.where((lag >= 0)[:, :, None, None, None], kern[jnp.maximum(lag, 0)], 0.0)
    toep = jnp.transpose(toep, (2, 0, 4, 1, 3)).reshape(C_GROUPS, S5_ROW, S5_ROW)
    w_re = jnp.transpose(ab_re[::-1], (1, 0, 3, 2)).reshape(C_GROUPS, S5_ROW, C_STATE)
    w_im = jnp.transpose(ab_im[::-1], (1, 0, 3, 2)).reshape(C_GROUPS, S5_ROW, C_STATE)
    tw = jnp.concatenate([toep, w_re, w_im, w_im, w_re], axis=-1).astype(BF16)
    pt_re = jnp.swapaxes(p_re[1:], 0, 1)[:, :, None, :]
    pt_im = jnp.swapaxes(p_im[1:], 0, 1)[:, :, None, :]
    ca_re = c_re[:, None] * pt_re - c_im[:, None] * pt_im
    ca_im = c_re[:, None] * pt_im + c_im[:, None] * pt_re
    v_re = jnp.transpose(ca_re, (0, 3, 1, 2)).reshape(C_GROUPS, C_STATE, S5_ROW)
    v_im = jnp.transpose(-ca_im, (0, 3, 1, 2)).reshape(C_GROUPS, C_STATE, S5_ROW)
    v = jnp.concatenate([v_re, v_im], axis=1).astype(BF16)
    al_re, al_im = p_re[ell], p_im[ell]
    coef = jnp.stack([jnp.concatenate([al_re, al_re], -1), jnp.concatenate([-al_im, al_im], -1),
                      jnp.concatenate([al_im, -al_im], -1)], axis=1)
    d_flat = jnp.tile(d_skip, (1, ell)).reshape(C_GROUPS, 1, S5_ROW)
    return tw, v, coef, d_flat


def _s5_kernel(u_ref, tw_ref, v_ref, coef_ref, d_ref, o_ref, sin_ref, xin_ref, *, bn):
    rows = u_ref.shape[1]
    n_chunks = rows // bn
    u = u_ref[0]
    tw = jnp.dot(u.astype(BF16), tw_ref[0], preferred_element_type=F32)
    o_ref[0] = tw[:, :S5_ROW] + u * d_ref[0]
    sin_ref[...] = tw[:, S5_ROW:]
    a1, a2, a2s = coef_ref[0, 0:1, :], coef_ref[0, 1:2, :], coef_ref[0, 2:3, :]

    def step(c, carry):
        new = []
        for b in range(bn):
            x, xs = carry[2 * b], carry[2 * b + 1]
            row = b * n_chunks + c
            xin_ref[pl.ds(row, 1), :] = x
            s_row = sin_ref[pl.ds(row, 1), :]
            new.append(a1 * x + a2 * xs + s_row[:, :STATE2])
            new.append(a1 * xs + a2s * x + s_row[:, STATE2:])
        return tuple(new)

    zero = jnp.zeros((1, STATE2), F32)
    lax.fori_loop(0, n_chunks, step, (zero,) * (2 * bn))
    y = o_ref[0] + jnp.dot(xin_ref[...].astype(BF16), v_ref[0], preferred_element_type=F32)
    o_ref[0] = jax.nn.gelu(y)


def _glu_kernel(y_ref, w_ref, b_ref, o_ref):
    y = y_ref[...]
    gate = jnp.dot(y.astype(BF16), w_ref[...], preferred_element_type=F32) + b_ref[...]
    o_ref[...] = (y * jax.nn.sigmoid(gate)).astype(o_ref.dtype)


def s5_branch(u, lam_re, lam_im, log_dt, b_re, b_im, c_re, c_im, d_skip, glu_w, glu_b):
    bn, s, _ = u.shape
    n_chunks = s // S5_CHUNK
    rows = bn * n_chunks
    tw, v, coef, d_flat = _s5_tables(lam_re, lam_im, log_dt, b_re, b_im, c_re, c_im, d_skip)
    uf = jnp.transpose(u.reshape(bn, n_chunks, S5_CHUNK, C_GROUPS, C_GROUP), (3, 0, 1, 2, 4)).reshape(C_GROUPS, rows, S5_ROW)
    per_group = lambda shape: pl.BlockSpec((1,) + shape, lambda g: (g, 0, 0))
    y = pl.pallas_call(
        functools.partial(_s5_kernel, bn=bn),
        grid=(C_GROUPS,),
        in_specs=[per_group((rows, S5_ROW)), per_group((S5_ROW, S5_ROW + 2 * STATE2)), per_group((STATE2, S5_ROW)),
                  per_group((3, STATE2)), per_group((1, S5_ROW))],
        out_specs=per_group((rows, S5_ROW)),
        out_shape=jax.ShapeDtypeStruct((C_GROUPS, rows, S5_ROW), F32),
        scratch_shapes=[pltpu.VMEM((rows, 2 * STATE2), F32), pltpu.VMEM((rows, STATE2), F32)],
        compiler_params=pltpu.CompilerParams(dimension_semantics=("arbitrary",), vmem_limit_bytes=VMEM_LIMIT_BYTES),
        name="s5_scan",
    )(uf, tw, v, coef, d_flat)
    y = jnp.transpose(y.reshape(C_GROUPS, bn, n_chunks, S5_CHUNK, C_GROUP), (1, 2, 3, 0, 4)).reshape(bn * s, C_WIDTH)
    t = bn * s
    tm = _pick_tile(t, (1024, 512, 256, 128, 8))
    return pl.pallas_call(
        _glu_kernel,
        grid=(t // tm,),
        in_specs=[pl.BlockSpec((tm, C_WIDTH), lambda i: (i, 0)), pl.BlockSpec((C_WIDTH, C_WIDTH), lambda i: (0, 0)),
                  pl.BlockSpec((1, C_WIDTH), lambda i: (0, 0))],
        out_specs=pl.BlockSpec((tm, C_WIDTH), lambda i: (i, 0)),
        out_shape=jax.ShapeDtypeStruct((t, C_WIDTH), BF16),
        compiler_params=pltpu.CompilerParams(dimension_semantics=("arbitrary",), vmem_limit_bytes=VMEM_LIMIT_BYTES),
        name="s5_glu",
    )(y, glu_w.astype(BF16), glu_b.reshape(1, C_WIDTH))


def hybrid_mixer(x, rel_bias, w_in, a_kv_norm, a_kv_up, b_conv, b_a_log, b_dt_bias, b_out_norm,
                 c_lambda_re, c_lambda_im, c_log_dt, c_b_re, c_b_im, c_c_re, c_c_im, c_d, c_glu_w, c_glu_b,
                 w_br_a, w_br_b, w_br_c, w_o, ln_g, ln_b):
    bn, s, d = x.shape
    t = bn * s
    xt = x.reshape(t, d)
    offsets = [0] + np.cumsum(SPLITS).tolist()
    w_cols = [w_in[:, offsets[i]:offsets[i + 1]] for i in range(len(SPLITS))]
    (w_aq, w_ckv, w_iq, w_ik, w_iw, w_bqkv, w_bz, w_ba, w_bb, w_cu, w_gate) = w_cols
    w_ik_pad = jnp.pad(w_ik, ((0, 0), (0, LANE - IDX_DIM)))
    w_rows = jnp.concatenate([w_gate, w_ckv, w_ik_pad, w_bqkv, w_bz, w_cu], axis=1)
    w_rows = jnp.pad(w_rows, ((0, 0), (0, -w_rows.shape[1] % PROJ_TILE_N)))
    h = matmul(xt, w_rows)
    o = np.cumsum([0, 3 * D_MODEL, A_KV_RANK, LANE, 3 * B_WIDTH, B_WIDTH, C_WIDTH]).tolist()
    idx_k = h[:, o[2]:o[2] + IDX_DIM].astype(BF16).reshape(bn, s, IDX_DIM)
    c_u = h[:, o[5]:o[6]].reshape(bn, s, C_WIDTH)
    qiq_t = matmul_nt(jnp.concatenate([w_aq * A_HEAD_DIM ** -0.5, w_iq], axis=1).T, xt, bn, s)
    scal_t = matmul_nt(jnp.concatenate([w_ba, w_bb, w_iw], axis=1).T, xt, bn, s, out_dtype=F32)
    k, vt = kv_project(h, o[1] // LANE, a_kv_norm, a_kv_up, bn, s)
    ya = dsa_attention(qiq_t, scal_t, idx_k, k, vt, _rel_bias_tiles(rel_bias), bn, s)
    yb = gated_deltanet(h, o[3] // LANE, scal_t, b_conv, b_a_log, b_dt_bias, b_out_norm, bn, s)
    yc = s5_branch(c_u, c_lambda_re, c_lambda_im, c_log_dt, c_b_re, c_b_im, c_c_re, c_c_im, c_d, c_glu_w, c_glu_b)
    return merge_branches(ya.reshape(t, A_WIDTH), yb, yc, h, w_br_a, w_br_b, w_br_c, w_o, xt, ln_g, ln_b)


def _merge_kernel(ya_ref, yb_ref, yc_ref, ga_ref, gb_ref, gc_ref, wa_ref, wb_ref, wc_ref, wo_ref, x_ref, g_ref, b_ref, o_ref):
    branch = lambda y_ref, w_ref, gate_ref: jax.nn.sigmoid(gate_ref[...]) * jnp.dot(y_ref[...], w_ref[...], preferred_element_type=F32)
    merged = branch(ya_ref, wa_ref, ga_ref) + branch(yb_ref, wb_ref, gb_ref) + branch(yc_ref, wc_ref, gc_ref)
    mix = jnp.dot(merged.astype(BF16), wo_ref[...], preferred_element_type=F32)
    z = DEEPNORM_ALPHA * x_ref[...] + mix
    mu = jnp.mean(z, -1, keepdims=True)
    zc = z - mu
    var = jnp.mean(zc * zc, -1, keepdims=True)
    o_ref[...] = zc * lax.rsqrt(var + LN_EPS) * g_ref[...] + b_ref[...]


def merge_branches(ya, yb, yc, h, w_br_a, w_br_b, w_br_c, w_o, x, ln_g, ln_b):
    t, d = x.shape
    tm = _pick_tile(t, (512, 256, 128, 8))
    row = lambda width, col=0: pl.BlockSpec((tm, width), lambda i: (i, col))
    full = lambda a: pl.BlockSpec(a.shape, lambda i: (0, 0))
    ws = [w.astype(BF16) for w in (w_br_a, w_br_b, w_br_c, w_o)]
    return pl.pallas_call(
        _merge_kernel,
        grid=(t // tm,),
        in_specs=[row(A_WIDTH), row(B_WIDTH), row(C_WIDTH), row(d, 0), row(d, 1), row(d, 2)] + [full(w) for w in ws]
                 + [row(d), pl.BlockSpec((1, d), lambda i: (0, 0)), pl.BlockSpec((1, d), lambda i: (0, 0))],
        out_specs=row(d),
        out_shape=jax.ShapeDtypeStruct((t, d), F32),
        compiler_params=pltpu.CompilerParams(dimension_semantics=("arbitrary",), vmem_limit_bytes=VMEM_LIMIT_BYTES),
        name="merge_branches",
    )(ya, yb, yc, h, h, h, *ws, x, ln_g.reshape(1, d), ln_b.reshape(1, d))


ROUTER_TILE = 512


def _router_kernel(x_ref, rwt_ref, bias_ref, tri_ref, idx_ref, gate_ref, pos_ref, cnt_ref, carry_ref):
    @pl.when(pl.program_id(0) == 0)
    def _():
        carry_ref[...] = jnp.zeros_like(carry_ref)

    logits = lax.dot_general(rwt_ref[...], x_ref[...].astype(BF16), (((1,), (1,)), ((), ())), preferred_element_type=F32)
    scores = jax.nn.sigmoid(logits)
    remaining = scores + bias_ref[...]
    expert = lax.broadcasted_iota(jnp.int32, scores.shape, 0)
    picks = []
    for _ in range(TOP_K):
        best = jnp.max(remaining, axis=0, keepdims=True)
        first = jnp.min(jnp.where(remaining == best, expert, N_EXPERTS), axis=0, keepdims=True)
        pick = expert == first
        picks.append((first, pick))
        remaining = jnp.where(pick, -jnp.inf, remaining)
    chosen = picks[0][1]
    for _, pick in picks[1:]:
        chosen = chosen | pick
    total = jnp.sum(jnp.where(chosen, scores, 0.0), axis=0, keepdims=True)
    prefix = jnp.dot(jnp.where(chosen, 1.0, 0.0).astype(BF16), tri_ref[...], preferred_element_type=F32)
    rank = carry_ref[...] + prefix.astype(jnp.int32) - 1
    carry_ref[...] = carry_ref[...] + prefix[:, -1:].astype(jnp.int32)
    cnt_ref[...] = carry_ref[...]
    for k, (first, pick) in enumerate(picks):
        idx_ref[k:k + 1, :] = first
        gate_ref[k:k + 1, :] = jnp.sum(jnp.where(pick, scores, 0.0), axis=0, keepdims=True) / total * ROUTED_SCALE
        pos_ref[k:k + 1, :] = jnp.sum(jnp.where(pick, rank, 0), axis=0, keepdims=True)


def moe_router(xt, router_w, router_bias):
    t, d = xt.shape
    tile = ROUTER_TILE
    tri = (jnp.arange(tile)[:, None] <= jnp.arange(tile)[None, :]).astype(BF16)
    row = pl.BlockSpec((TOP_K, tile), lambda i: (0, i))
    return pl.pallas_call(
        _router_kernel,
        grid=(t // tile,),
        in_specs=[pl.BlockSpec((tile, d), lambda i: (i, 0)), pl.BlockSpec((N_EXPERTS, d), lambda i: (0, 0)),
                  pl.BlockSpec((N_EXPERTS, 1), lambda i: (0, 0)), pl.BlockSpec((tile, tile), lambda i: (0, 0))],
        out_specs=[row, row, row, pl.BlockSpec((N_EXPERTS, 1), lambda i: (0, 0))],
        out_shape=[jax.ShapeDtypeStruct((TOP_K, t), jnp.int32), jax.ShapeDtypeStruct((TOP_K, t), F32),
                   jax.ShapeDtypeStruct((TOP_K, t), jnp.int32), jax.ShapeDtypeStruct((N_EXPERTS, 1), jnp.int32)],
        scratch_shapes=[pltpu.VMEM((N_EXPERTS, 1), jnp.int32)],
        compiler_params=pltpu.CompilerParams(dimension_semantics=("arbitrary",), vmem_limit_bytes=VMEM_LIMIT_BYTES),
        name="moe_router",
    )(xt, router_w.T.astype(BF16), router_bias.reshape(N_EXPERTS, 1), tri)


def _expert_kernel(blk_expert_ref, blk_rows_ref, x_ref, wgu_ref, wdn_ref, o_ref):
    n_valid = blk_rows_ref[pl.program_id(0)]

    @pl.when(n_valid > 0)
    def _():
        row = lax.broadcasted_iota(jnp.int32, x_ref.shape, 0)
        x = jnp.where(row < n_valid, x_ref[...], 0.0).astype(BF16)
        h = jnp.dot(x, wgu_ref[0], preferred_element_type=F32)
        hg, hu = h[:, :D_EXPERT], h[:, D_EXPERT:]
        act = (hg * jax.nn.sigmoid(hg) * hu).astype(BF16)
        o_ref[...] = jnp.dot(act, wdn_ref[0], preferred_element_type=F32)

    @pl.when(n_valid <= 0)
    def _():
        o_ref[...] = jnp.zeros_like(o_ref)


def expert_ffn(xs, blk_expert, blk_rows, w_gu, w_dn):
    n_pad, d = xs.shape
    n_blk = n_pad // MOE_BLOCK
    return pl.pallas_call(
        _expert_kernel,
        grid_spec=pltpu.PrefetchScalarGridSpec(
            num_scalar_prefetch=2,
            grid=(n_blk,),
            in_specs=[pl.BlockSpec((MOE_BLOCK, d), lambda b, be, br: (b, 0)),
                      pl.BlockSpec((1, d, 2 * D_EXPERT), lambda b, be, br: (be[b], 0, 0)),
                      pl.BlockSpec((1, D_EXPERT, d), lambda b, be, br: (be[b], 0, 0))],
            out_specs=pl.BlockSpec((MOE_BLOCK, d), lambda b, be, br: (b, 0))),
        out_shape=jax.ShapeDtypeStruct((n_pad, d), F32),
        compiler_params=pltpu.CompilerParams(dimension_semantics=("arbitrary",), vmem_limit_bytes=VMEM_LIMIT_BYTES),
        name="expert_ffn",
    )(blk_expert, blk_rows, xs, w_gu, w_dn)


DISPATCH_TILE = 256
COMBINE_TILE = 128


def _rows_kernel(pad_start_ref, idx_ref, pos_ref, o_ref):
    idx = idx_ref[...]
    row = pos_ref[...]
    for e in range(N_EXPERTS):
        row = row + jnp.where(idx == e, pad_start_ref[e], 0)
    o_ref[...] = row


def moe_rows(pad_start, idx, pos):
    k, t = idx.shape
    tile = _pick_tile(t, (4096, 2048, 1024, 512, 256, 128))
    blk = pl.BlockSpec((k, tile), lambda i, ps: (0, i))
    return pl.pallas_call(
        _rows_kernel,
        grid_spec=pltpu.PrefetchScalarGridSpec(num_scalar_prefetch=1, grid=(t // tile,), in_specs=[blk, blk], out_specs=blk),
        out_shape=jax.ShapeDtypeStruct((k, t), jnp.int32),
        compiler_params=pltpu.CompilerParams(dimension_semantics=("arbitrary",)),
        name="moe_rows",
    )(pad_start, idx, pos)


def _dispatch_kernel(dest_ref, x_ref, xs_hbm, sem):
    def issue(r, carry):
        for k in range(TOP_K):
            row = dest_ref[0, 0, r * TOP_K + k]
            pltpu.make_async_copy(x_ref.at[pl.ds(r, 1)], xs_hbm.at[pl.ds(row, 1)], sem).start(priority=k % 2)
        return carry

    lax.fori_loop(0, DISPATCH_TILE, issue, 0)
    n_rows = DISPATCH_TILE * TOP_K
    pltpu.make_async_copy(xs_hbm.at[pl.ds(0, n_rows)], xs_hbm.at[pl.ds(0, n_rows)], sem).wait()


def moe_dispatch(xt, dest_tok_major, n_pad):
    t, d = xt.shape
    n_tiles = t // DISPATCH_TILE
    return pl.pallas_call(
        _dispatch_kernel,
        grid=(n_tiles,),
        in_specs=[pl.BlockSpec((1, 1, DISPATCH_TILE * TOP_K), lambda i: (i, 0, 0), memory_space=pltpu.SMEM),
                  pl.BlockSpec((DISPATCH_TILE, d), lambda i: (i, 0))],
        out_specs=pl.BlockSpec(memory_space=pl.ANY),
        scratch_shapes=[pltpu.SemaphoreType.DMA(())],
        out_shape=jax.ShapeDtypeStruct((n_pad, d), xt.dtype),
        compiler_params=pltpu.CompilerParams(dimension_semantics=("arbitrary",), has_side_effects=True),
        name="moe_dispatch",
    )(dest_tok_major.reshape(n_tiles, 1, -1), xt)


def _combine_kernel(dest_ref, dest_next_ref, ys_hbm, gate_ref, x_ref, wgu_ref, wdn_ref, g_ref, b_ref, o_ref, buf_ref, sem):
    i = pl.program_id(0)
    n = pl.num_programs(0)
    slot = lax.rem(i, 2)

    def gather(d_ref, into):
        def issue(r, carry):
            for k in range(TOP_K):
                row = d_ref[0, 0, r * TOP_K + k]
                pltpu.make_async_copy(ys_hbm.at[pl.ds(row, 1)], buf_ref.at[into, k, pl.ds(r, 1)],
                                      sem.at[into]).start(priority=k % 2)
            return carry
        lax.fori_loop(0, COMBINE_TILE, issue, 0)

    @pl.when(i == 0)
    def _():
        gather(dest_ref, 0)

    @pl.when(i + 1 < n)
    def _():
        gather(dest_next_ref, 1 - slot)

    x = x_ref[...]
    h = jnp.dot(x.astype(BF16), wgu_ref[...], preferred_element_type=F32)
    hg, hu = h[:, :D_SHARED], h[:, D_SHARED:]
    acc = DEEPNORM_ALPHA * x + jnp.dot((hg * jax.nn.sigmoid(hg) * hu).astype(BF16), wdn_ref[...], preferred_element_type=F32)
    pltpu.make_async_copy(buf_ref.at[slot], buf_ref.at[slot], sem.at[slot]).wait()
    gate = gate_ref[...]
    for k in range(TOP_K):
        acc = acc + buf_ref[slot, k] * gate[:, k:k + 1]
    mu = jnp.mean(acc, -1, keepdims=True)
    zc = acc - mu
    var = jnp.mean(zc * zc, -1, keepdims=True)
    o_ref[...] = zc * lax.rsqrt(var + LN_EPS) * g_ref[...] + b_ref[...]


def moe_combine(ys, dest_tok_major, gate_tok_major, xt, sh_w_gu, sh_w_down, ln_g, ln_b):
    t, d = xt.shape
    tm = COMBINE_TILE
    n_tiles = t // tm
    dest3 = dest_tok_major.reshape(n_tiles, 1, -1)
    row = pl.BlockSpec((tm, d), lambda i: (i, 0))
    full = lambda shape: pl.BlockSpec(shape, lambda i: (0, 0))
    this = pl.BlockSpec((1, 1, tm * TOP_K), lambda i: (i, 0, 0), memory_space=pltpu.SMEM)
    nxt = pl.BlockSpec((1, 1, tm * TOP_K), lambda i: (jnp.minimum(i + 1, n_tiles - 1), 0, 0), memory_space=pltpu.SMEM)
    return pl.pallas_call(
        _combine_kernel,
        grid=(n_tiles,),
        in_specs=[this, nxt, pl.BlockSpec(memory_space=pl.ANY), pl.BlockSpec((tm, TOP_K), lambda i: (i, 0)), row,
                  full((d, 2 * D_SHARED)), full((D_SHARED, d)), full((1, d)), full((1, d))],
        out_specs=row,
        scratch_shapes=[pltpu.VMEM((2, TOP_K, tm, d), F32), pltpu.SemaphoreType.DMA((2,))],
        out_shape=jax.ShapeDtypeStruct((t, d), F32),
        compiler_params=pltpu.CompilerParams(dimension_semantics=("arbitrary",), vmem_limit_bytes=VMEM_LIMIT_BYTES),
        name="moe_combine",
    )(dest3, dest3, ys, gate_tok_major, xt, sh_w_gu.astype(BF16), sh_w_down.astype(BF16),
      ln_g.reshape(1, d), ln_b.reshape(1, d))


def moe_ffn(xt, router_w, router_bias, exp_w_gu, exp_w_down, sh_w_gu, sh_w_down, ln_g, ln_b):
    n_tok, d = xt.shape
    idx, gate, pos, counts = moe_router(xt, router_w, router_bias)
    counts = counts[:, 0]
    padded = (counts + MOE_BLOCK - 1) // MOE_BLOCK * MOE_BLOCK
    pad_ends = jnp.cumsum(padded)
    pad_start = pad_ends - padded
    n_blk = n_tok * TOP_K // MOE_BLOCK + N_EXPERTS
    n_pad = n_blk * MOE_BLOCK
    blk_lo = jnp.arange(n_blk, dtype=jnp.int32) * MOE_BLOCK
    owner = (pad_start[None, :] <= blk_lo[:, None]) & (blk_lo[:, None] < pad_ends[None, :])
    blk_expert = jnp.minimum(jnp.sum(pad_ends[None, :] <= blk_lo[:, None], axis=1), N_EXPERTS - 1).astype(jnp.int32)
    blk_rows = jnp.clip(jnp.sum(jnp.where(owner, (pad_start + counts)[None, :] - blk_lo[:, None], 0), axis=1),
                        0, MOE_BLOCK).astype(jnp.int32)
    dest = moe_rows(pad_start.astype(jnp.int32), idx, pos).T.reshape(-1)
    xs = moe_dispatch(xt, dest, n_pad)
    ys = expert_ffn(xs, blk_expert, blk_rows, exp_w_gu.astype(BF16), exp_w_down.astype(BF16))
    return moe_combine(ys, dest, gate.T, xt, sh_w_gu, sh_w_down, ln_g, ln_b)


def kernel(x, rel_bias, w_in, a_kv_norm, a_kv_up, b_conv, b_a_log, b_dt_bias, b_out_norm, c_lambda_re, c_lambda_im, c_log_dt, c_b_re, c_b_im, c_c_re, c_c_im, c_d, c_glu_w, c_glu_b, w_br_a, w_br_b, w_br_c, w_o, ln1_g, ln1_b, router_w, router_bias, exp_w_gu, exp_w_down, sh_w_gu, sh_w_down, ln2_g, ln2_b):
    bn, s, d = x.shape
    t = bn * s
    for i in range(w_in.shape[0]):
        xt = hybrid_mixer(x, rel_bias, w_in[i], a_kv_norm[i], a_kv_up[i], b_conv[i], b_a_log[i], b_dt_bias[i],
                          b_out_norm[i], c_lambda_re[i], c_lambda_im[i], c_log_dt[i], c_b_re[i], c_b_im[i],
                          c_c_re[i], c_c_im[i], c_d[i], c_glu_w[i], c_glu_b[i], w_br_a[i], w_br_b[i], w_br_c[i], w_o[i],
                          ln1_g[i], ln1_b[i])
        x = moe_ffn(xt, router_w[i], router_bias[i], exp_w_gu[i], exp_w_down[i], sh_w_gu[i], sh_w_down[i],
                    ln2_g[i], ln2_b[i]).reshape(bn, s, d)
    return x
```

```python
import functools
import math

import jax
import jax.numpy as jnp
import numpy as np
from jax import lax
from jax.experimental import pallas as pl
from jax.experimental.pallas import tpu as pltpu

F32 = jnp.float32
BF16 = jnp.bfloat16

D_MODEL = 1024
DEPTH = 4
A_HEADS = 8
A_HEAD_DIM = 64
A_WIDTH = A_HEADS * A_HEAD_DIM
A_KV_RANK = 128
IDX_HEADS = 8
IDX_DIM = 64
TOPK_MAX = 256
Q_BLOCK = 128
REL_BUCKETS = 32
REL_MAX_DIST = 128
B_HEADS = 4
B_HEAD_DIM = 128
B_WIDTH = B_HEADS * B_HEAD_DIM
CONV_WIDTH = 4
CHUNK = 64
C_WIDTH = 512
C_GROUP = 16
C_GROUPS = C_WIDTH // C_GROUP
C_STATE = 64
N_EXPERTS = 64
TOP_K = 8
D_EXPERT = 256
D_SHARED = 256
ROUTED_SCALE = 2.5
MOE_BLOCK = 512
SPLITS = (A_WIDTH, A_KV_RANK, IDX_HEADS * IDX_DIM, IDX_DIM, IDX_HEADS, 3 * B_WIDTH, B_WIDTH, B_HEADS, B_HEADS, C_WIDTH, 3 * D_MODEL)
DEEPNORM_ALPHA = (2 * DEPTH) ** 0.25
LN_EPS = 1e-5
RMS_EPS = 1e-6
NEG_INF = -1e30

VMEM_LIMIT_BYTES = 48 * 1024 * 1024
LANE = 128
PROJ_TILE_N = 1024


def _pick_tile(n, candidates):
    for c in candidates:
        if n % c == 0:
            return c
    return n


def _mm_kernel(a_ref, b_ref, o_ref):
    o_ref[...] = jnp.dot(a_ref[...].astype(BF16), b_ref[...], preferred_element_type=F32).astype(o_ref.dtype)


def matmul(a, b, out_dtype=F32):
    m, k = a.shape
    _, n = b.shape
    tm = _pick_tile(m, (1024, 512, 256, 128, 8))
    tn = _pick_tile(n, (1024, 768, 512, 384, 256, 128))
    return pl.pallas_call(
        _mm_kernel,
        grid=(m // tm, n // tn),
        in_specs=[pl.BlockSpec((tm, k), lambda i, j: (i, 0)), pl.BlockSpec((k, tn), lambda i, j: (0, j))],
        out_specs=pl.BlockSpec((tm, tn), lambda i, j: (i, j)),
        out_shape=jax.ShapeDtypeStruct((m, n), out_dtype),
        compiler_params=pltpu.CompilerParams(dimension_semantics=("arbitrary", "arbitrary"), vmem_limit_bytes=VMEM_LIMIT_BYTES),
        name="matmul",
    )(a, b.astype(BF16))


def rel_bucket(dist):
    n_exact = REL_BUCKETS // 2
    d = jnp.maximum(dist, 1).astype(F32)
    large = n_exact + (jnp.log(d / n_exact) / math.log(REL_MAX_DIST / n_exact) * (REL_BUCKETS - n_exact)).astype(jnp.int32)
    return jnp.where(dist < n_exact, dist, jnp.minimum(large, REL_BUCKETS - 1))


KEY_BLOCK = 128
DSA_CHUNK = 512
COUNT_CHUNK = 2 * DSA_CHUNK
COUNT_LANES = 8
BLOCKS_PER_CHUNK = DSA_CHUNK // KEY_BLOCK
HEAD_PAIRS = A_HEADS // 2
PAIR_WIDTH = 2 * A_HEAD_DIM
INT_MIN = -2 ** 31
IDX_W_SCALE = IDX_HEADS ** -0.5 * IDX_DIM ** -0.5


def _kv_kernel(c_ref, g_ref, wk_ref, wvt_ref, k_ref, vt_ref):
    c = c_ref[...]
    cn = (c * lax.rsqrt(jnp.mean(c * c, -1, keepdims=True) + RMS_EPS) * g_ref[...]).astype(BF16)
    k_ref[...] = jnp.dot(cn, wk_ref[...], preferred_element_type=F32).astype(BF16)
    vt = lax.dot_general(wvt_ref[...], cn, (((1,), (1,)), ((), ())), preferred_element_type=F32).astype(BF16)
    for i in range(vt_ref.shape[1]):
        vt_ref[0, i] = vt[:, i * KEY_BLOCK:(i + 1) * KEY_BLOCK]


def kv_project(c_kv, col, g, w_kv_up, bn, s):
    ts = 512
    w = w_kv_up.astype(BF16)
    return pl.pallas_call(
        _kv_kernel,
        grid=(bn, s // ts),
        in_specs=[pl.BlockSpec((ts, A_KV_RANK), lambda b, i: (b * (s // ts) + i, col)),
                  pl.BlockSpec((1, A_KV_RANK), lambda b, i: (0, 0)),
                  pl.BlockSpec((A_KV_RANK, A_WIDTH), lambda b, i: (0, 0)),
                  pl.BlockSpec((A_WIDTH, A_KV_RANK), lambda b, i: (0, 0))],
        out_specs=[pl.BlockSpec((None, ts, A_WIDTH), lambda b, i: (b, i, 0)),
                   pl.BlockSpec((1, ts // KEY_BLOCK, A_WIDTH, KEY_BLOCK), lambda b, i: (b, i, 0, 0))],
        out_shape=[jax.ShapeDtypeStruct((bn, s, A_WIDTH), BF16),
                   jax.ShapeDtypeStruct((bn, s // KEY_BLOCK, A_WIDTH, KEY_BLOCK), BF16)],
        compiler_params=pltpu.CompilerParams(dimension_semantics=("arbitrary", "arbitrary"), vmem_limit_bytes=VMEM_LIMIT_BYTES),
        name="kv_project",
    )(c_kv, g.reshape(1, A_KV_RANK), w[:, :A_WIDTH], w[:, A_WIDTH:].T)


def _mm_nt_kernel(wt_ref, x_ref, o_ref):
    o_ref[0] = lax.dot_general(wt_ref[...], x_ref[...].astype(BF16), (((1,), (1,)), ((), ())),
                               preferred_element_type=F32).astype(o_ref.dtype)


def matmul_nt(wt, x, bn, s, out_dtype=BF16):
    n, k = wt.shape
    ts = 512
    return pl.pallas_call(
        _mm_nt_kernel,
        grid=(bn, s // ts),
        in_specs=[pl.BlockSpec((n, k), lambda b, i: (0, 0)),
                  pl.BlockSpec((ts, k), lambda b, i: (b * (s // ts) + i, 0))],
        out_specs=pl.BlockSpec((1, n, ts), lambda b, i: (b, 0, i)),
        out_shape=jax.ShapeDtypeStruct((bn, n, s), out_dtype),
        compiler_params=pltpu.CompilerParams(dimension_semantics=("arbitrary", "arbitrary"), vmem_limit_bytes=VMEM_LIMIT_BYTES),
        name="matmul_nt",
    )(wt.astype(BF16), x)


def _sortable_key(score):
    bits = pltpu.bitcast(score, jnp.int32)
    return bits ^ (lax.shift_right_arithmetic(bits, 31) & jnp.int32(0x7FFFFFFF))


def _dsa_kernel(qt_ref, iqt_ref, wt_ref, kidx_ref, k_ref, vt_ref, bias_ref, o_ref,
                skey_ref, widx_ref, wq_ref, acc_ref, m_ref, l_ref, *, k_sel):
    j = pl.program_id(1)
    n_chunks = (j + BLOCKS_PER_CHUNK) // BLOCKS_PER_CHUNK
    q_pos = j * Q_BLOCK + lax.broadcasted_iota(jnp.int32, (1, Q_BLOCK), 1)

    for h in range(IDX_HEADS):
        widx_ref[:, h * Q_BLOCK:(h + 1) * Q_BLOCK] = iqt_ref[0, h * IDX_DIM:(h + 1) * IDX_DIM, :]
    wq_ref[...] = jnp.zeros_like(wq_ref)
    for p in range(HEAD_PAIRS):
        wq_ref[p, 0:A_HEAD_DIM, 0:Q_BLOCK] = qt_ref[0, (2 * p) * A_HEAD_DIM:(2 * p + 1) * A_HEAD_DIM, :]
        wq_ref[p, A_HEAD_DIM:PAIR_WIDTH, Q_BLOCK:2 * Q_BLOCK] = qt_ref[0, (2 * p + 1) * A_HEAD_DIM:(2 * p + 2) * A_HEAD_DIM, :]

    w_rows = [wt_ref[0, h:h + 1, :] * IDX_W_SCALE for h in range(IDX_HEADS)]

    def idx_chunk(c, carry):
        start = pl.multiple_of(c * DSA_CHUNK, DSA_CHUNK)
        kc = kidx_ref[0, pl.ds(start, DSA_CHUNK), :]
        score = jnp.zeros((DSA_CHUNK, Q_BLOCK), F32)
        for p in range(IDX_HEADS // 2):
            z = jnp.dot(kc, widx_ref[:, p * 2 * Q_BLOCK:(p + 1) * 2 * Q_BLOCK], preferred_element_type=F32)
            score = score + jnp.maximum(z[:, :Q_BLOCK], 0.0) * w_rows[2 * p]
            score = score + jnp.maximum(z[:, Q_BLOCK:], 0.0) * w_rows[2 * p + 1]
        pos = start + lax.broadcasted_iota(jnp.int32, (DSA_CHUNK, Q_BLOCK), 0)
        skey_ref[pl.ds(start, DSA_CHUNK), :] = jnp.where(pos <= q_pos, _sortable_key(score), INT_MIN)
        return carry

    lax.fori_loop(0, n_chunks, idx_chunk, 0)

    n_count = (n_chunks + COUNT_CHUNK // DSA_CHUNK - 1) // (COUNT_CHUNK // DSA_CHUNK)

    @pl.when(n_chunks * DSA_CHUNK < n_count * COUNT_CHUNK)
    def _():
        skey_ref[pl.ds(pl.multiple_of(n_chunks * DSA_CHUNK, DSA_CHUNK), DSA_CHUNK), :] = jnp.full(
            (DSA_CHUNK, Q_BLOCK), INT_MIN, jnp.int32)

    def count_ge(cand):
        def body(c, cnt):
            start = pl.multiple_of(c * COUNT_CHUNK, COUNT_CHUNK)
            hit = (skey_ref[pl.ds(start, COUNT_CHUNK), :] >= cand).astype(jnp.int32)
            return cnt + jnp.sum(hit.reshape(COUNT_CHUNK // (8 * COUNT_LANES), COUNT_LANES, 8, Q_BLOCK), axis=0)
        cnt = lax.fori_loop(0, n_count, body, jnp.zeros((COUNT_LANES, 8, Q_BLOCK), jnp.int32))
        return jnp.sum(jnp.sum(cnt, axis=0), axis=0, keepdims=True)

    def bit_step(i, carry):
        thr_biased, n_ge = carry
        cand_biased = thr_biased | lax.shift_left(jnp.int32(1), 31 - i)
        cnt = count_ge(cand_biased ^ INT_MIN)
        keep = cnt >= k_sel
        return jnp.where(keep, cand_biased, thr_biased), jnp.where(keep, cnt, n_ge)

    thr, n_ge = lax.fori_loop(0, 32, bit_step, (jnp.zeros((1, Q_BLOCK), jnp.int32),
                                                jnp.full((1, Q_BLOCK), k_sel, jnp.int32)))
    thr = thr ^ INT_MIN

    tied = (n_ge > k_sel) & (thr > INT_MIN)

    @pl.when(jnp.max(tied.astype(jnp.int32)) > 0)
    def _():
        need = k_sel - count_ge(thr + 1)

        def count_tied_upto(limit):
            def body(c, cnt):
                start = pl.multiple_of(c * DSA_CHUNK, DSA_CHUNK)
                pos = start + lax.broadcasted_iota(jnp.int32, (DSA_CHUNK, Q_BLOCK), 0)
                hit = jnp.where(skey_ref[pl.ds(start, DSA_CHUNK), :] == thr, (pos <= limit).astype(jnp.int32), 0)
                return cnt + jnp.sum(hit.reshape(DSA_CHUNK // 8, 8, Q_BLOCK), axis=0)
            cnt = lax.fori_loop(0, n_chunks, body, jnp.zeros((8, Q_BLOCK), jnp.int32))
            return jnp.sum(cnt, axis=0, keepdims=True)

        def pos_step(i, lim):
            cand = lim - lax.shift_left(jnp.int32(1), 30 - i)
            return jnp.where(count_tied_upto(cand) >= need, cand, lim)

        limit = lax.fori_loop(0, 31, pos_step, jnp.full((1, Q_BLOCK), 2 ** 31 - 1, jnp.int32))

        def demote(c, carry):
            start = pl.multiple_of(c * DSA_CHUNK, DSA_CHUNK)
            pos = start + lax.broadcasted_iota(jnp.int32, (DSA_CHUNK, Q_BLOCK), 0)
            sk = skey_ref[pl.ds(start, DSA_CHUNK), :]
            skey_ref[pl.ds(start, DSA_CHUNK), :] = jnp.where(tied & (sk == thr) & (pos > limit), thr - 1, sk)
            return carry

        lax.fori_loop(0, n_chunks, demote, 0)

    m_ref[...] = jnp.full_like(m_ref, NEG_INF)
    l_ref[...] = jnp.zeros_like(l_ref)
    acc_ref[...] = jnp.zeros_like(acc_ref)

    def attend(s_list, vt_of, n_blocks):
        for p in range(HEAD_PAIRS):
            s = s_list[p]
            m_old = m_ref[p]
            m_new = jnp.maximum(m_old, jnp.max(s, axis=0, keepdims=True))
            alpha = jnp.exp(m_old - m_new)
            pexp = jnp.exp(s - m_new)
            l_ref[p] = l_ref[p] * alpha + jnp.sum(pexp, axis=0, keepdims=True)
            pb = pexp.astype(BF16)
            pv = jnp.dot(vt_of(p, 0), pb[0:KEY_BLOCK], preferred_element_type=F32)
            for i in range(1, n_blocks):
                pv = pv + jnp.dot(vt_of(p, i), pb[i * KEY_BLOCK:(i + 1) * KEY_BLOCK], preferred_element_type=F32)
            acc_ref[p] = acc_ref[p] * alpha + pv
            m_ref[p] = m_new

    far_end = (j - 1) * KEY_BLOCK
    n_far = (jnp.maximum(j - 1, 0) + BLOCKS_PER_CHUNK - 1) // BLOCKS_PER_CHUNK

    def far_chunk(c, carry):
        start = pl.multiple_of(c * DSA_CHUNK, DSA_CHUNK)
        pos = start + lax.broadcasted_iota(jnp.int32, (DSA_CHUNK, Q_BLOCK), 0)
        sel = (skey_ref[pl.ds(start, DSA_CHUNK), :] >= thr) & (pos < far_end)
        madd = jnp.where(sel, 0.0, NEG_INF)
        madd2 = jnp.concatenate([madd, madd], axis=1)
        kc = k_ref[0, pl.ds(start, DSA_CHUNK), :]
        s_list = [jnp.dot(kc[:, p * PAIR_WIDTH:(p + 1) * PAIR_WIDTH], wq_ref[p], preferred_element_type=F32) + madd2
                  for p in range(HEAD_PAIRS)]
        blk0 = c * BLOCKS_PER_CHUNK
        attend(s_list, lambda p, i: vt_ref[0, blk0 + i, p * PAIR_WIDTH:(p + 1) * PAIR_WIDTH, :], BLOCKS_PER_CHUNK)
        return carry

    lax.fori_loop(0, n_far, far_chunk, 0)

    def near_block(kb, delta):
        start = pl.multiple_of(kb * KEY_BLOCK, KEY_BLOCK)
        pos = start + lax.broadcasted_iota(jnp.int32, (KEY_BLOCK, Q_BLOCK), 0)
        sel = (skey_ref[pl.ds(start, KEY_BLOCK), :] >= thr) & (pos <= q_pos)
        madd = jnp.where(sel, 0.0, NEG_INF)
        madd2 = jnp.concatenate([madd, madd], axis=1)
        kc = k_ref[0, pl.ds(start, KEY_BLOCK), :]
        s_list = [jnp.dot(kc[:, p * PAIR_WIDTH:(p + 1) * PAIR_WIDTH], wq_ref[p], preferred_element_type=F32)
                  + bias_ref[delta, p] + madd2 for p in range(HEAD_PAIRS)]
        attend(s_list, lambda p, i: vt_ref[0, kb, p * PAIR_WIDTH:(p + 1) * PAIR_WIDTH, :], 1)

    @pl.when(j > 0)
    def _():
        near_block(j - 1, 1)

    near_block(j, 0)

    outs = []
    for p in range(HEAD_PAIRS):
        o = acc_ref[p] / l_ref[p]
        outs.append(o[0:A_HEAD_DIM, 0:Q_BLOCK])
        outs.append(o[A_HEAD_DIM:PAIR_WIDTH, Q_BLOCK:2 * Q_BLOCK])
    o_ref[0] = jnp.concatenate(outs, axis=0).T.astype(o_ref.dtype)


def _rel_bias_tiles(rel_bias):
    tab = rel_bias[rel_bucket(jnp.arange(2 * Q_BLOCK, dtype=jnp.int32))] - rel_bias[REL_BUCKETS - 1]
    key = jnp.arange(KEY_BLOCK)[:, None]
    qry = jnp.arange(Q_BLOCK)[None, :]
    tiles = []
    for delta in (0, 1):
        dist = jnp.maximum(delta * Q_BLOCK + qry - key, 0)
        t = jnp.moveaxis(tab[dist], 2, 0)
        t = t.reshape(HEAD_PAIRS, 2, KEY_BLOCK, Q_BLOCK)
        tiles.append(jnp.concatenate([t[:, 0], t[:, 1]], axis=-1))
    return jnp.stack(tiles)


def dsa_attention(qiq_t, wt, k_idx, k, vt, bias_tiles, bn, s):
    assert A_WIDTH == IDX_HEADS * IDX_DIM
    scal_rows = wt.shape[1]
    assert scal_rows % IDX_HEADS == 0
    k_sel = min(TOPK_MAX, s // 4)
    nb = s // Q_BLOCK
    s_pad = s + DSA_CHUNK
    return pl.pallas_call(
        functools.partial(_dsa_kernel, k_sel=k_sel),
        grid=(bn, nb),
        in_specs=[pl.BlockSpec((1, A_WIDTH, Q_BLOCK), lambda b, j: (b, 0, j)),
                  pl.BlockSpec((1, IDX_HEADS * IDX_DIM, Q_BLOCK), lambda b, j: (b, 1, j)),
                  pl.BlockSpec((1, IDX_HEADS, Q_BLOCK), lambda b, j: (b, scal_rows // IDX_HEADS - 1, j)),
                  pl.BlockSpec((1, s, IDX_DIM), lambda b, j: (b, 0, 0)),
                  pl.BlockSpec((1, s, A_WIDTH), lambda b, j: (b, 0, 0)),
                  pl.BlockSpec((1, s // KEY_BLOCK, A_WIDTH, KEY_BLOCK), lambda b, j: (b, 0, 0, 0)),
                  pl.BlockSpec((2, HEAD_PAIRS, KEY_BLOCK, 2 * Q_BLOCK), lambda b, j: (0, 0, 0, 0))],
        out_specs=pl.BlockSpec((1, Q_BLOCK, A_WIDTH), lambda b, j: (b, j, 0)),
        out_shape=jax.ShapeDtypeStruct((bn, s, A_WIDTH), BF16),
        scratch_shapes=[pltpu.VMEM((s_pad, Q_BLOCK), jnp.int32),
                        pltpu.VMEM((IDX_DIM, IDX_HEADS * Q_BLOCK), BF16),
                        pltpu.VMEM((HEAD_PAIRS, PAIR_WIDTH, 2 * Q_BLOCK), BF16),
                        pltpu.VMEM((HEAD_PAIRS, PAIR_WIDTH, 2 * Q_BLOCK), F32),
                        pltpu.VMEM((HEAD_PAIRS, 1, 2 * Q_BLOCK), F32),
                        pltpu.VMEM((HEAD_PAIRS, 1, 2 * Q_BLOCK), F32)],
        compiler_params=pltpu.CompilerParams(dimension_semantics=("arbitrary", "arbitrary"), vmem_limit_bytes=56 * 1024 * 1024),
        name="dsa_attention",
    )(qiq_t, qiq_t, wt, k_idx, k, vt, bias_tiles)


GDN_CHUNK = B_HEAD_DIM
GDN_BLOCK = 1024
GDN_HALO = 8
NEUMANN_STEPS = 6


def _split_bf16(x):
    hi = x.astype(BF16)
    return hi, (x - hi.astype(F32)).astype(BF16)


def _dot_f32(a, b):
    a_hi, a_lo = _split_bf16(a)
    b_hi, b_lo = _split_bf16(b)
    d = lambda x, y: jnp.dot(x, y, preferred_element_type=F32)
    return d(a_hi, b_hi) + (d(a_hi, b_lo) + d(a_lo, b_hi))


def _bdot(a, b):
    return jnp.dot(a.astype(BF16), b.astype(BF16), preferred_element_type=F32)


def _bdot_nt(a, b):
    return lax.dot_general(a.astype(BF16), b.astype(BF16), (((1,), (1,)), ((), ())), preferred_element_type=F32)


def _bdot_tn(a, b):
    return lax.dot_general(a.astype(BF16), b.astype(BF16), (((0,), (0,)), ((), ())), preferred_element_type=F32)


def _softplus(x):
    return jnp.maximum(x, 0.0) + jnp.log(1.0 + jnp.exp(-jnp.abs(x)))


def _gdn_kernel(hq_ref, hk_ref, hv_ref, hz_ref, wq_ref, wk_ref, wv_ref, ab_ref, const_ref, gout_ref, tri_ref,
                o_ref, state_ref, xbuf_ref):
    head = pl.program_id(1)

    @pl.when(pl.program_id(2) == 0)
    def _():
        state_ref[...] = jnp.zeros_like(state_ref)
        xbuf_ref[:, 0:GDN_HALO, :] = jnp.zeros((3, GDN_HALO, B_HEAD_DIM), F32)

    def conv_silu(n, x_ref, w_ref):
        x = x_ref[...]
        xbuf_ref[n, GDN_HALO:, :] = x
        acc = xbuf_ref[n, GDN_HALO - CONV_WIDTH + 1:GDN_HALO - CONV_WIDTH + 1 + GDN_BLOCK, :] * w_ref[0:1, :]
        for j in range(1, CONV_WIDTH):
            lo = GDN_HALO - CONV_WIDTH + 1 + j
            acc = acc + xbuf_ref[n, lo:lo + GDN_BLOCK, :] * w_ref[j:j + 1, :]
        xbuf_ref[n, 0:GDN_HALO, :] = x[GDN_BLOCK - GDN_HALO:, :]
        return acc * jax.nn.sigmoid(acc)

    q = conv_silu(0, hq_ref, wq_ref)
    k = conv_silu(1, hk_ref, wk_ref)
    v = conv_silu(2, hv_ref, wv_ref)
    q = q * lax.rsqrt(jnp.sum(q * q, -1, keepdims=True) + RMS_EPS) * B_HEAD_DIM ** -0.5
    k = k * lax.rsqrt(jnp.sum(k * k, -1, keepdims=True) + RMS_EPS)

    a_row = ab_ref[0, pl.ds(head, 1), :]
    b_row = ab_ref[0, pl.ds(B_HEADS + head, 1), :]
    beta_row = jax.nn.sigmoid(b_row)
    g_row = const_ref[0, 0:1, :] * _softplus(a_row + const_ref[0, 1:2, :])
    gc_rows = _dot_f32(jnp.broadcast_to(g_row, (8, GDN_BLOCK)), tri_ref[...])[0:1, :]

    row_i = lax.broadcasted_iota(jnp.int32, (GDN_CHUNK, GDN_CHUNK), 0)
    col_j = lax.broadcasted_iota(jnp.int32, (GDN_CHUNK, GDN_CHUNK), 1)
    eye = jnp.where(row_i == col_j, 1.0, 0.0)
    z_all = hz_ref[...]
    chunks = []
    for c in range(GDN_BLOCK // GDN_CHUNK):
        sl = slice(c * GDN_CHUNK, (c + 1) * GDN_CHUNK)
        qc, kc, vc = q[sl], k[sl], v[sl]
        g_lane = jnp.broadcast_to(gc_rows[:, sl], (GDN_CHUNK, GDN_CHUNK))
        g_sub = g_lane.T
        beta_sub = jnp.broadcast_to(beta_row[:, sl], (GDN_CHUNK, GDN_CHUNK)).T
        g_end = jnp.broadcast_to(g_lane[:, GDN_CHUNK - 1:GDN_CHUNK], (GDN_CHUNK, GDN_CHUNK))
        decay = jnp.exp(jnp.where(row_i >= col_j, g_sub - g_lane, -jnp.inf))
        kb = kc * beta_sub
        x = -jnp.where(row_i > col_j, _bdot_nt(kb, kc) * decay, 0.0)
        chunks.append(dict(sl=sl, qc=qc, kc=kc, vc=vc, g_sub=g_sub, beta_sub=beta_sub, g_end=g_end, decay=decay,
                           kb=kb, x=x, t_inv=eye + x))
    for _ in range(NEUMANN_STEPS):
        for ch in chunks:
            ch['x'] = _dot_f32(ch['x'], ch['x'])
        for ch in chunks:
            ch['t_inv'] = ch['t_inv'] + _bdot(ch['t_inv'], ch['x'])
    for ch in chunks:
        g_exp = jnp.exp(ch['g_sub'])
        ch['u'] = _bdot(ch['t_inv'], ch['vc'] * ch['beta_sub'])
        ch['w'] = _bdot(ch['t_inv'], ch['kb'] * g_exp)
        ch['qk'] = _bdot_nt(ch['qc'], ch['kc']) * ch['decay']
        ch['q_dec'] = ch['qc'] * g_exp
        ch['k_dec'] = ch['kc'] * jnp.exp(ch['g_end'] - ch['g_sub'])
    for ch in chunks:
        state = state_ref[...]
        v_new = ch['u'] - _bdot(ch['w'], state)
        o = _bdot(ch['q_dec'], state) + _bdot(ch['qk'], v_new)
        state_ref[...] = state * jnp.exp(ch['g_end']) + _bdot_tn(ch['k_dec'], v_new)
        o = o * lax.rsqrt(jnp.mean(o * o, -1, keepdims=True) + RMS_EPS) * gout_ref[...]
        zc = z_all[ch['sl']]
        o_ref[ch['sl'], :] = (o * (zc * jax.nn.sigmoid(zc))).astype(o_ref.dtype)


def gated_deltanet(h, col0, ab_t, conv_w, a_log, dt_bias, out_norm_g, bn, s):
    nblk = s // GDN_BLOCK
    tok = jnp.arange(GDN_BLOCK)
    tri = ((tok[:, None] // GDN_CHUNK == tok[None, :] // GDN_CHUNK) & (tok[:, None] <= tok[None, :])).astype(F32)
    const = jnp.stack([jnp.broadcast_to(-jnp.exp(a_log)[:, None], (B_HEADS, GDN_BLOCK)),
                       jnp.broadcast_to(dt_bias[:, None], (B_HEADS, GDN_BLOCK))], axis=1)
    const = jnp.pad(const, ((0, 0), (0, 6), (0, 0)))
    hcol = lambda part: pl.BlockSpec((GDN_BLOCK, B_HEAD_DIM), lambda b, hh, i: (b * nblk + i, col0 + part * B_HEADS + hh))
    wcol = lambda part: pl.BlockSpec((CONV_WIDTH, B_HEAD_DIM), lambda b, hh, i: (0, part * B_HEADS + hh))
    return pl.pallas_call(
        _gdn_kernel,
        grid=(bn, B_HEADS, nblk),
        in_specs=[hcol(0), hcol(1), hcol(2), hcol(3), wcol(0), wcol(1), wcol(2),
                  pl.BlockSpec((1, 16, GDN_BLOCK), lambda b, hh, i: (b, 0, i)),
                  pl.BlockSpec((1, 8, GDN_BLOCK), lambda b, hh, i: (hh, 0, 0)),
                  pl.BlockSpec((1, B_HEAD_DIM), lambda b, hh, i: (0, 0)),
                  pl.BlockSpec((GDN_BLOCK, GDN_BLOCK), lambda b, hh, i: (0, 0))],
        out_specs=pl.BlockSpec((GDN_BLOCK, B_HEAD_DIM), lambda b, hh, i: (b * nblk + i, hh)),
        out_shape=jax.ShapeDtypeStruct((bn * s, B_WIDTH), BF16),
        scratch_shapes=[pltpu.VMEM((B_HEAD_DIM, B_HEAD_DIM), F32),
                        pltpu.VMEM((3, GDN_HALO + GDN_BLOCK, B_HEAD_DIM), F32)],
        compiler_params=pltpu.CompilerParams(dimension_semantics=("arbitrary", "arbitrary", "arbitrary"), vmem_limit_bytes=VMEM_LIMIT_BYTES),
        name="gated_deltanet",
    )(h, h, h, h, conv_w, conv_w, conv_w, ab_t, const, out_norm_g.reshape(1, B_HEAD_DIM), tri)


S5_CHUNK = 32
S5_ROW = S5_CHUNK * C_GROUP
STATE2 = 2 * C_STATE


def _s5_tables(lam_re, lam_im, log_dt, b_re, b_im, c_re, c_im, d_skip):
    hp = lax.Precision.HIGHEST
    ell = S5_CHUNK
    dt = jnp.exp(log_dt)[:, None]
    lr, li = lam_re, lam_im
    mag = jnp.exp(lr * dt)
    a_re, a_im = mag * jnp.cos(li * dt), mag * jnp.sin(li * dt)
    den = lr * lr + li * li
    f_re = ((a_re - 1.0) * lr + a_im * li) / den
    f_im = (a_im * lr - (a_re - 1.0) * li) / den
    bb_re = f_re[..., None] * b_re - f_im[..., None] * b_im
    bb_im = f_re[..., None] * b_im + f_im[..., None] * b_re
    tau = jnp.arange(ell + 1, dtype=F32)[:, None, None]
    pmag, ang = jnp.exp(tau * (lr * dt)), tau * (li * dt)
    p_re, p_im = pmag * jnp.cos(ang), pmag * jnp.sin(ang)
    ab_re = p_re[:ell, ..., None] * bb_re - p_im[:ell, ..., None] * bb_im
    ab_im = p_re[:ell, ..., None] * bb_im + p_im[:ell, ..., None] * bb_re
    kern = (jnp.einsum('gip,tgpj->tgij', c_re, ab_re, precision=hp)
            - jnp.einsum('gip,tgpj->tgij', c_im, ab_im, precision=hp))
    lag = jnp.arange(ell)[None, :] - jnp.arange(ell)[:, None]
    toep = jnp.where((lag >= 0)[:, :, None, None, None], kern[jnp.maximum(lag, 0)], 0.0)
    toep = jnp.transpose(toep, (2, 0, 4, 1, 3)).reshape(C_GROUPS, S5_ROW, S5_ROW)
    w_re = jnp.transpose(ab_re[::-1], (1, 0, 3, 2)).reshape(C_GROUPS, S5_ROW, C_STATE)
    w_im = jnp.transpose(ab_im[::-1], (1, 0, 3, 2)).reshape(C_GROUPS, S5_ROW, C_STATE)
    tw = jnp.concatenate([toep, w_re, w_im, w_im, w_re], axis=-1).astype(BF16)
    pt_re = jnp.swapaxes(p_re[1:], 0, 1)[:, :, None, :]
    pt_im = jnp.swapaxes(p_im[1:], 0, 1)[:, :, None, :]
    ca_re = c_re[:, None] * pt_re - c_im[:, None] * pt_im
    ca_im = c_re[:, None] * pt_im + c_im[:, None] * pt_re
    v_re = jnp.transpose(ca_re, (0, 3, 1, 2)).reshape(C_GROUPS, C_STATE, S5_ROW)
    v_im = jnp.transpose(-ca_im, (0, 3, 1, 2)).reshape(C_GROUPS, C_STATE, S5_ROW)
    v = jnp.concatenate([v_re, v_im], axis=1).astype(BF16)
    al_re, al_im = p_re[ell], p_im[ell]
    coef = jnp.stack([jnp.concatenate([al_re, al_re], -1), jnp.concatenate([-al_im, al_im], -1),
                      jnp.concatenate([al_im, -al_im], -1)], axis=1)
    d_flat = jnp.tile(d_skip, (1, ell)).reshape(C_GROUPS, 1, S5_ROW)
    return tw, v, coef, d_flat


def _s5_kernel(u_ref, tw_ref, v_ref, coef_ref, d_ref, o_ref, sin_ref, xin_ref, *, bn):
    rows = u_ref.shape[1]
    n_chunks = rows // bn
    u = u_ref[0]
    tw = jnp.dot(u.astype(BF16), tw_ref[0], preferred_element_type=F32)
    o_ref[0] = tw[:, :S5_ROW] + u * d_ref[0]
    sin_ref[...] = tw[:, S5_ROW:]
    a1, a2, a2s = coef_ref[0, 0:1, :], coef_ref[0, 1:2, :], coef_ref[0, 2:3, :]

    def step(c, carry):
        new = []
        for b in range(bn):
            x, xs = carry[2 * b], carry[2 * b + 1]
            row = b * n_chunks + c
            xin_ref[pl.ds(row, 1), :] = x
            s_row = sin_ref[pl.ds(row, 1), :]
            new.append(a1 * x + a2 * xs + s_row[:, :STATE2])
            new.append(a1 * xs + a2s * x + s_row[:, STATE2:])
        return tuple(new)

    zero = jnp.zeros((1, STATE2), F32)
    lax.fori_loop(0, n_chunks, step, (zero,) * (2 * bn))
    y = o_ref[0] + jnp.dot(xin_ref[...].astype(BF16), v_ref[0], preferred_element_type=F32)
    o_ref[0] = jax.nn.gelu(y)


def _glu_kernel(y_ref, w_ref, b_ref, o_ref):
    y = y_ref[...]
    gate = jnp.dot(y.astype(BF16), w_ref[...], preferred_element_type=F32) + b_ref[...]
    o_ref[...] = (y * jax.nn.sigmoid(gate)).astype(o_ref.dtype)


def s5_branch(u, lam_re, lam_im, log_dt, b_re, b_im, c_re, c_im, d_skip, glu_w, glu_b):
    bn, s, _ = u.shape
    n_chunks = s // S5_CHUNK
    rows = bn * n_chunks
    tw, v, coef, d_flat = _s5_tables(lam_re, lam_im, log_dt, b_re, b_im, c_re, c_im, d_skip)
    uf = jnp.transpose(u.reshape(bn, n_chunks, S5_CHUNK, C_GROUPS, C_GROUP), (3, 0, 1, 2, 4)).reshape(C_GROUPS, rows, S5_ROW)
    per_group = lambda shape: pl.BlockSpec((1,) + shape, lambda g: (g, 0, 0))
    y = pl.pallas_call(
        functools.partial(_s5_kernel, bn=bn),
        grid=(C_GROUPS,),
        in_specs=[per_group((rows, S5_ROW)), per_group((S5_ROW, S5_ROW + 2 * STATE2)), per_group((STATE2, S5_ROW)),
                  per_group((3, STATE2)), per_group((1, S5_ROW))],
        out_specs=per_group((rows, S5_ROW)),
        out_shape=jax.ShapeDtypeStruct((C_GROUPS, rows, S5_ROW), F32),
        scratch_shapes=[pltpu.VMEM((rows, 2 * STATE2), F32), pltpu.VMEM((rows, STATE2), F32)],
        compiler_params=pltpu.CompilerParams(dimension_semantics=("arbitrary",), vmem_limit_bytes=VMEM_LIMIT_BYTES),
        name="s5_scan",
    )(uf, tw, v, coef, d_flat)
    y = jnp.transpose(y.reshape(C_GROUPS, bn, n_chunks, S5_CHUNK, C_GROUP), (1, 2, 3, 0, 4)).reshape(bn * s, C_WIDTH)
    t = bn * s
    tm = _pick_tile(t, (1024, 512, 256, 128, 8))
    return pl.pallas_call(
        _glu_kernel,
        grid=(t // tm,),
        in_specs=[pl.BlockSpec((tm, C_WIDTH), lambda i: (i, 0)), pl.BlockSpec((C_WIDTH, C_WIDTH), lambda i: (0, 0)),
                  pl.BlockSpec((1, C_WIDTH), lambda i: (0, 0))],
        out_specs=pl.BlockSpec((tm, C_WIDTH), lambda i: (i, 0)),
        out_shape=jax.ShapeDtypeStruct((t, C_WIDTH), BF16),
        compiler_params=pltpu.CompilerParams(dimension_semantics=("arbitrary",), vmem_limit_bytes=VMEM_LIMIT_BYTES),
        name="s5_glu",
    )(y, glu_w.astype(BF16), glu_b.reshape(1, C_WIDTH))


def hybrid_mixer(x, rel_bias, w_in, a_kv_norm, a_kv_up, b_conv, b_a_log, b_dt_bias, b_out_norm,
                 c_lambda_re, c_lambda_im, c_log_dt, c_b_re, c_b_im, c_c_re, c_c_im, c_d, c_glu_w, c_glu_b,
                 w_br_a, w_br_b, w_br_c, w_o, ln_g, ln_b):
    bn, s, d = x.shape
    t = bn * s
    xt = x.reshape(t, d)
    offsets = [0] + np.cumsum(SPLITS).tolist()
    w_cols = [w_in[:, offsets[i]:offsets[i + 1]] for i in range(len(SPLITS))]
    (w_aq, w_ckv, w_iq, w_ik, w_iw, w_bqkv, w_bz, w_ba, w_bb, w_cu, w_gate) = w_cols
    w_ik_pad = jnp.pad(w_ik, ((0, 0), (0, LANE - IDX_DIM)))
    w_rows = jnp.concatenate([w_gate, w_ckv, w_ik_pad, w_bqkv, w_bz, w_cu], axis=1)
    w_rows = jnp.pad(w_rows, ((0, 0), (0, -w_rows.shape[1] % PROJ_TILE_N)))
    h = matmul(xt, w_rows)
    o = np.cumsum([0, 3 * D_MODEL, A_KV_RANK, LANE, 3 * B_WIDTH, B_WIDTH, C_WIDTH]).tolist()
    idx_k = h[:, o[2]:o[2] + IDX_DIM].astype(BF16).reshape(bn, s, IDX_DIM)
    c_u = h[:, o[5]:o[6]].reshape(bn, s, C_WIDTH)
    qiq_t = matmul_nt(jnp.concatenate([w_aq * A_HEAD_DIM ** -0.5, w_iq], axis=1).T, xt, bn, s)
    scal_t = matmul_nt(jnp.concatenate([w_ba, w_bb, w_iw], axis=1).T, xt, bn, s, out_dtype=F32)
    k, vt = kv_project(h, o[1] // LANE, a_kv_norm, a_kv_up, bn, s)
    ya = dsa_attention(qiq_t, scal_t, idx_k, k, vt, _rel_bias_tiles(rel_bias), bn, s)
    yb = gated_deltanet(h, o[3] // LANE, scal_t, b_conv, b_a_log, b_dt_bias, b_out_norm, bn, s)
    yc = s5_branch(c_u, c_lambda_re, c_lambda_im, c_log_dt, c_b_re, c_b_im, c_c_re, c_c_im, c_d, c_glu_w, c_glu_b)
    return merge_branches(ya.reshape(t, A_WIDTH), yb, yc, h, w_br_a, w_br_b, w_br_c, w_o, xt, ln_g, ln_b)


def _merge_kernel(ya_ref, yb_ref, yc_ref, ga_ref, gb_ref, gc_ref, wa_ref, wb_ref, wc_ref, wo_ref, x_ref, g_ref, b_ref, o_ref):
    branch = lambda y_ref, w_ref, gate_ref: jax.nn.sigmoid(gate_ref[...]) * jnp.dot(y_ref[...], w_ref[...], preferred_element_type=F32)
    merged = branch(ya_ref, wa_ref, ga_ref) + branch(yb_ref, wb_ref, gb_ref) + branch(yc_ref, wc_ref, gc_ref)
    mix = jnp.dot(merged.astype(BF16), wo_ref[...], preferred_element_type=F32)
    z = DEEPNORM_ALPHA * x_ref[...] + mix
    mu = jnp.mean(z, -1, keepdims=True)
    zc = z - mu
    var = jnp.mean(zc * zc, -1, keepdims=True)
    o_ref[...] = zc * lax.rsqrt(var + LN_EPS) * g_ref[...] + b_ref[...]


def merge_branches(ya, yb, yc, h, w_br_a, w_br_b, w_br_c, w_o, x, ln_g, ln_b):
    t, d = x.shape
    tm = _pick_tile(t, (512, 256, 128, 8))
    row = lambda width, col=0: pl.BlockSpec((tm, width), lambda i: (i, col))
    full = lambda a: pl.BlockSpec(a.shape, lambda i: (0, 0))
    ws = [w.astype(BF16) for w in (w_br_a, w_br_b, w_br_c, w_o)]
    return pl.pallas_call(
        _merge_kernel,
        grid=(t // tm,),
        in_specs=[row(A_WIDTH), row(B_WIDTH), row(C_WIDTH), row(d, 0), row(d, 1), row(d, 2)] + [full(w) for w in ws]
                 + [row(d), pl.BlockSpec((1, d), lambda i: (0, 0)), pl.BlockSpec((1, d), lambda i: (0, 0))],
        out_specs=row(d),
        out_shape=jax.ShapeDtypeStruct((t, d), F32),
        compiler_params=pltpu.CompilerParams(dimension_semantics=("arbitrary",), vmem_limit_bytes=VMEM_LIMIT_BYTES),
        name="merge_branches",
    )(ya, yb, yc, h, h, h, *ws, x, ln_g.reshape(1, d), ln_b.reshape(1, d))


ROUTER_TILE = 512


def _router_kernel(x_ref, rwt_ref, bias_ref, tri_ref, idx_ref, gate_ref, pos_ref, cnt_ref, carry_ref):
    @pl.when(pl.program_id(0) == 0)
    def _():
        carry_ref[...] = jnp.zeros_like(carry_ref)

    logits = lax.dot_general(rwt_ref[...], x_ref[...].astype(BF16), (((1,), (1,)), ((), ())), preferred_element_type=F32)
    scores = jax.nn.sigmoid(logits)
    remaining = scores + bias_ref[...]
    expert = lax.broadcasted_iota(jnp.int32, scores.shape, 0)
    picks = []
    for _ in range(TOP_K):
        best = jnp.max(remaining, axis=0, keepdims=True)
        first = jnp.min(jnp.where(remaining == best, expert, N_EXPERTS), axis=0, keepdims=True)
        pick = expert == first
        picks.append((first, pick))
        remaining = jnp.where(pick, -jnp.inf, remaining)
    chosen = picks[0][1]
    for _, pick in picks[1:]:
        chosen = chosen | pick
    total = jnp.sum(jnp.where(chosen, scores, 0.0), axis=0, keepdims=True)
    prefix = jnp.dot(jnp.where(chosen, 1.0, 0.0).astype(BF16), tri_ref[...], preferred_element_type=F32)
    rank = carry_ref[...] + prefix.astype(jnp.int32) - 1
    carry_ref[...] = carry_ref[...] + prefix[:, -1:].astype(jnp.int32)
    cnt_ref[...] = carry_ref[...]
    for k, (first, pick) in enumerate(picks):
        idx_ref[k:k + 1, :] = first
        gate_ref[k:k + 1, :] = jnp.sum(jnp.where(pick, scores, 0.0), axis=0, keepdims=True) / total * ROUTED_SCALE
        pos_ref[k:k + 1, :] = jnp.sum(jnp.where(pick, rank, 0), axis=0, keepdims=True)


def moe_router(xt, router_w, router_bias):
    t, d = xt.shape
    tile = ROUTER_TILE
    tri = (jnp.arange(tile)[:, None] <= jnp.arange(tile)[None, :]).astype(BF16)
    row = pl.BlockSpec((TOP_K, tile), lambda i: (0, i))
    return pl.pallas_call(
        _router_kernel,
        grid=(t // tile,),
        in_specs=[pl.BlockSpec((tile, d), lambda i: (i, 0)), pl.BlockSpec((N_EXPERTS, d), lambda i: (0, 0)),
                  pl.BlockSpec((N_EXPERTS, 1), lambda i: (0, 0)), pl.BlockSpec((tile, tile), lambda i: (0, 0))],
        out_specs=[row, row, row, pl.BlockSpec((N_EXPERTS, 1), lambda i: (0, 0))],
        out_shape=[jax.ShapeDtypeStruct((TOP_K, t), jnp.int32), jax.ShapeDtypeStruct((TOP_K, t), F32),
                   jax.ShapeDtypeStruct((TOP_K, t), jnp.int32), jax.ShapeDtypeStruct((N_EXPERTS, 1), jnp.int32)],
        scratch_shapes=[pltpu.VMEM((N_EXPERTS, 1), jnp.int32)],
        compiler_params=pltpu.CompilerParams(dimension_semantics=("arbitrary",), vmem_limit_bytes=VMEM_LIMIT_BYTES),
        name="moe_router",
    )(xt, router_w.T.astype(BF16), router_bias.reshape(N_EXPERTS, 1), tri)


def _expert_kernel(blk_expert_ref, blk_rows_ref, x_ref, wgu_ref, wdn_ref, o_ref):
    n_valid = blk_rows_ref[pl.program_id(0)]

    @pl.when(n_valid > 0)
    def _():
        row = lax.broadcasted_iota(jnp.int32, x_ref.shape, 0)
        x = jnp.where(row < n_valid, x_ref[...], 0.0).astype(BF16)
        h = jnp.dot(x, wgu_ref[0], preferred_element_type=F32)
        hg, hu = h[:, :D_EXPERT], h[:, D_EXPERT:]
        act = (hg * jax.nn.sigmoid(hg) * hu).astype(BF16)
        o_ref[...] = jnp.dot(act, wdn_ref[0], preferred_element_type=F32)

    @pl.when(n_valid <= 0)
    def _():
        o_ref[...] = jnp.zeros_like(o_ref)


def expert_ffn(xs, blk_expert, blk_rows, w_gu, w_dn):
    n_pad, d = xs.shape
    n_blk = n_pad // MOE_BLOCK
    return pl.pallas_call(
        _expert_kernel,
        grid_spec=pltpu.PrefetchScalarGridSpec(
            num_scalar_prefetch=2,
            grid=(n_blk,),
            in_specs=[pl.BlockSpec((MOE_BLOCK, d), lambda b, be, br: (b, 0)),
                      pl.BlockSpec((1, d, 2 * D_EXPERT), lambda b, be, br: (be[b], 0, 0)),
                      pl.BlockSpec((1, D_EXPERT, d), lambda b, be, br: (be[b], 0, 0))],
            out_specs=pl.BlockSpec((MOE_BLOCK, d), lambda b, be, br: (b, 0))),
        out_shape=jax.ShapeDtypeStruct((n_pad, d), F32),
        compiler_params=pltpu.CompilerParams(dimension_semantics=("arbitrary",), vmem_limit_bytes=VMEM_LIMIT_BYTES),
        name="expert_ffn",
    )(blk_expert, blk_rows, xs, w_gu, w_dn)


DISPATCH_TILE = 256
COMBINE_TILE = 128
SUBLANES = 8


def _rows_kernel(pad_start_ref, idx_ref, pos_ref, o_ref):
    idx = idx_ref[...]
    row = pos_ref[...]
    for e in range(N_EXPERTS):
        row = row + jnp.where(idx == e, pad_start_ref[e], 0)
    o_ref[...] = row


def moe_rows(pad_start, idx, pos):
    k, t = idx.shape
    tile = _pick_tile(t, (4096, 2048, 1024, 512, 256, 128))
    blk = pl.BlockSpec((k, tile), lambda i, ps: (0, i))
    return pl.pallas_call(
        _rows_kernel,
        grid_spec=pltpu.PrefetchScalarGridSpec(num_scalar_prefetch=1, grid=(t // tile,), in_specs=[blk, blk], out_specs=blk),
        out_shape=jax.ShapeDtypeStruct((k, t), jnp.int32),
        compiler_params=pltpu.CompilerParams(dimension_semantics=("arbitrary",)),
        name="moe_rows",
    )(pad_start, idx, pos)


def _dispatch_kernel(dest_ref, x_ref, xs_hbm, sem):
    def issue(g, carry):
        base = pl.multiple_of(g * SUBLANES, SUBLANES)
        for sub in range(SUBLANES):
            for k in range(TOP_K):
                row = dest_ref[0, 0, (base + sub) * TOP_K + k]
                pltpu.make_async_copy(x_ref.at[pl.ds(base + sub, 1)], xs_hbm.at[pl.ds(row, 1)], sem).start(priority=k % 2)
        return carry

    lax.fori_loop(0, DISPATCH_TILE // SUBLANES, issue, 0)
    n_rows = DISPATCH_TILE * TOP_K
    pltpu.make_async_copy(xs_hbm.at[pl.ds(0, n_rows)], xs_hbm.at[pl.ds(0, n_rows)], sem).wait()


def moe_dispatch(xt, dest_tok_major, n_pad):
    t, d = xt.shape
    n_tiles = t // DISPATCH_TILE
    return pl.pallas_call(
        _dispatch_kernel,
        grid=(n_tiles,),
        in_specs=[pl.BlockSpec((1, 1, DISPATCH_TILE * TOP_K), lambda i: (i, 0, 0), memory_space=pltpu.SMEM),
                  pl.BlockSpec((DISPATCH_TILE, d), lambda i: (i, 0))],
        out_specs=pl.BlockSpec(memory_space=pl.ANY),
        scratch_shapes=[pltpu.SemaphoreType.DMA(())],
        out_shape=jax.ShapeDtypeStruct((n_pad, d), xt.dtype),
        compiler_params=pltpu.CompilerParams(dimension_semantics=("arbitrary",), has_side_effects=True),
        name="moe_dispatch",
    )(dest_tok_major.reshape(n_tiles, 1, -1), xt)


def _combine_kernel(dest_ref, dest_next_ref, ys_hbm, gate_ref, x_ref, wgu_ref, wdn_ref, g_ref, b_ref, o_ref, buf_ref, sem):
    i = pl.program_id(0)
    n = pl.num_programs(0)
    slot = lax.rem(i, 2)

    def gather(d_ref, into):
        def issue(g, carry):
            base = pl.multiple_of(g * SUBLANES, SUBLANES)
            for sub in range(SUBLANES):
                for k in range(TOP_K):
                    row = d_ref[0, 0, (base + sub) * TOP_K + k]
                    pltpu.make_async_copy(ys_hbm.at[pl.ds(row, 1)], buf_ref.at[into, k, pl.ds(base + sub, 1)],
                                          sem.at[into]).start(priority=k % 2)
            return carry
        lax.fori_loop(0, COMBINE_TILE // SUBLANES, issue, 0)

    @pl.when(i == 0)
    def _():
        gather(dest_ref, 0)

    @pl.when(i + 1 < n)
    def _():
        gather(dest_next_ref, 1 - slot)

    x = x_ref[...]
    h = jnp.dot(x.astype(BF16), wgu_ref[...], preferred_element_type=F32)
    hg, hu = h[:, :D_SHARED], h[:, D_SHARED:]
    acc = DEEPNORM_ALPHA * x + jnp.dot((hg * jax.nn.sigmoid(hg) * hu).astype(BF16), wdn_ref[...], preferred_element_type=F32)
    pltpu.make_async_copy(buf_ref.at[slot], buf_ref.at[slot], sem.at[slot]).wait()
    gate = gate_ref[...]
    for k in range(TOP_K):
        acc = acc + buf_ref[slot, k] * gate[:, k:k + 1]
    mu = jnp.mean(acc, -1, keepdims=True)
    zc = acc - mu
    var = jnp.mean(zc * zc, -1, keepdims=True)
    o_ref[...] = zc * lax.rsqrt(var + LN_EPS) * g_ref[...] + b_ref[...]


def moe_combine(ys, dest_tok_major, gate_tok_major, xt, sh_w_gu, sh_w_down, ln_g, ln_b):
    t, d = xt.shape
    tm = COMBINE_TILE
    n_tiles = t // tm
    dest3 = dest_tok_major.reshape(n_tiles, 1, -1)
    row = pl.BlockSpec((tm, d), lambda i: (i, 0))
    full = lambda shape: pl.BlockSpec(shape, lambda i: (0, 0))
    this = pl.BlockSpec((1, 1, tm * TOP_K), lambda i: (i, 0, 0), memory_space=pltpu.SMEM)
    nxt = pl.BlockSpec((1, 1, tm * TOP_K), lambda i: (jnp.minimum(i + 1, n_tiles - 1), 0, 0), memory_space=pltpu.SMEM)
    return pl.pallas_call(
        _combine_kernel,
        grid=(n_tiles,),
        in_specs=[this, nxt, pl.BlockSpec(memory_space=pl.ANY), pl.BlockSpec((tm, TOP_K), lambda i: (i, 0)), row,
                  full((d, 2 * D_SHARED)), full((D_SHARED, d)), full((1, d)), full((1, d))],
        out_specs=row,
        scratch_shapes=[pltpu.VMEM((2, TOP_K, tm, d), F32), pltpu.SemaphoreType.DMA((2,))],
        out_shape=jax.ShapeDtypeStruct((t, d), F32),
        compiler_params=pltpu.CompilerParams(dimension_semantics=("arbitrary",), vmem_limit_bytes=VMEM_LIMIT_BYTES),
        name="moe_combine",
    )(dest3, dest3, ys, gate_tok_major, xt, sh_w_gu.astype(BF16), sh_w_down.astype(BF16),
      ln_g.reshape(1, d), ln_b.reshape(1, d))


def moe_ffn(xt, router_w, router_bias, exp_w_gu, exp_w_down, sh_w_gu, sh_w_down, ln_g, ln_b):
    n_tok, d = xt.shape
    idx, gate, pos, counts = moe_router(xt, router_w, router_bias)
    counts = counts[:, 0]
    padded = (counts + MOE_BLOCK - 1) // MOE_BLOCK * MOE_BLOCK
    pad_ends = jnp.cumsum(padded)
    pad_start = pad_ends - padded
    n_blk = n_tok * TOP_K // MOE_BLOCK + N_EXPERTS
    n_pad = n_blk * MOE_BLOCK
    blk_lo = jnp.arange(n_blk, dtype=jnp.int32) * MOE_BLOCK
    owner = (pad_start[None, :] <= blk_lo[:, None]) & (blk_lo[:, None] < pad_ends[None, :])
    blk_expert = jnp.minimum(jnp.sum(pad_ends[None, :] <= blk_lo[:, None], axis=1), N_EXPERTS - 1).astype(jnp.int32)
    blk_rows = jnp.clip(jnp.sum(jnp.where(owner, (pad_start + counts)[None, :] - blk_lo[:, None], 0), axis=1),
                        0, MOE_BLOCK).astype(jnp.int32)
    dest = moe_rows(pad_start.astype(jnp.int32), idx, pos).T.reshape(-1)
    xs = moe_dispatch(xt, dest, n_pad)
    ys = expert_ffn(xs, blk_expert, blk_rows, exp_w_gu.astype(BF16), exp_w_down.astype(BF16))
    return moe_combine(ys, dest, gate.T, xt, sh_w_gu, sh_w_down, ln_g, ln_b)


def kernel(x, rel_bias, w_in, a_kv_norm, a_kv_up, b_conv, b_a_log, b_dt_bias, b_out_norm, c_lambda_re, c_lambda_im, c_log_dt, c_b_re, c_b_im, c_c_re, c_c_im, c_d, c_glu_w, c_glu_b, w_br_a, w_br_b, w_br_c, w_o, ln1_g, ln1_b, router_w, router_bias, exp_w_gu, exp_w_down, sh_w_gu, sh_w_down, ln2_g, ln2_b):
    bn, s, d = x.shape
    t = bn * s
    for i in range(w_in.shape[0]):
        xt = hybrid_mixer(x, rel_bias, w_in[i], a_kv_norm[i], a_kv_up[i], b_conv[i], b_a_log[i], b_dt_bias[i],
                          b_out_norm[i], c_lambda_re[i], c_lambda_im[i], c_log_dt[i], c_b_re[i], c_b_im[i],
                          c_c_re[i], c_c_im[i], c_d[i], c_glu_w[i], c_glu_b[i], w_br_a[i], w_br_b[i], w_br_c[i], w_o[i],
                          ln1_g[i], ln1_b[i])
        x = moe_ffn(xt, router_w[i], router_bias[i], exp_w_gu[i], exp_w_down[i], sh_w_gu[i], sh_w_down[i],
                    ln2_g[i], ln2_b[i]).reshape(bn, s, d)
    return x
```

```python
import functools
import math

import jax
import jax.numpy as jnp
import numpy as np
from jax import lax
from jax.experimental import pallas as pl
from jax.experimental.pallas import tpu as pltpu

F32 = jnp.float32
BF16 = jnp.bfloat16

D_MODEL = 1024
DEPTH = 4
A_HEADS = 8
A_HEAD_DIM = 64
A_WIDTH = A_HEADS * A_HEAD_DIM
A_KV_RANK = 128
IDX_HEADS = 8
IDX_DIM = 64
TOPK_MAX = 256
Q_BLOCK = 128
REL_BUCKETS = 32
REL_MAX_DIST = 128
B_HEADS = 4
B_HEAD_DIM = 128
B_WIDTH = B_HEADS * B_HEAD_DIM
CONV_WIDTH = 4
C_WIDTH = 512
C_GROUP = 16
C_GROUPS = C_WIDTH // C_GROUP
C_STATE = 64
N_EXPERTS = 64
TOP_K = 8
D_EXPERT = 256
D_SHARED = 256
ROUTED_SCALE = 2.5
MOE_BLOCK = 512
SPLITS = (A_WIDTH, A_KV_RANK, IDX_HEADS * IDX_DIM, IDX_DIM, IDX_HEADS, 3 * B_WIDTH, B_WIDTH, B_HEADS, B_HEADS, C_WIDTH, 3 * D_MODEL)
DEEPNORM_ALPHA = (2 * DEPTH) ** 0.25
LN_EPS = 1e-5
RMS_EPS = 1e-6
NEG_INF = -1e30

VMEM_LIMIT_BYTES = 48 * 1024 * 1024
LANE = 128
PROJ_TILE_N = 1024


def _pick_tile(n, candidates):
    for c in candidates:
        if n % c == 0:
            return c
    return n


def _mm_kernel(a_ref, b_ref, o_ref):
    o_ref[...] = jnp.dot(a_ref[...].astype(BF16), b_ref[...], preferred_element_type=F32).astype(o_ref.dtype)


def matmul(a, b, out_dtype=F32):
    m, k = a.shape
    _, n = b.shape
    tm = _pick_tile(m, (1024, 512, 256, 128, 8))
    tn = _pick_tile(n, (1024, 768, 512, 384, 256, 128))
    return pl.pallas_call(
        _mm_kernel,
        grid=(m // tm, n // tn),
        in_specs=[pl.BlockSpec((tm, k), lambda i, j: (i, 0)), pl.BlockSpec((k, tn), lambda i, j: (0, j))],
        out_specs=pl.BlockSpec((tm, tn), lambda i, j: (i, j)),
        out_shape=jax.ShapeDtypeStruct((m, n), out_dtype),
        compiler_params=pltpu.CompilerParams(dimension_semantics=("arbitrary", "arbitrary"), vmem_limit_bytes=VMEM_LIMIT_BYTES),
        name="matmul",
    )(a, b.astype(BF16))


def rel_bucket(dist):
    n_exact = REL_BUCKETS // 2
    d = jnp.maximum(dist, 1).astype(F32)
    large = n_exact + (jnp.log(d / n_exact) / math.log(REL_MAX_DIST / n_exact) * (REL_BUCKETS - n_exact)).astype(jnp.int32)
    return jnp.where(dist < n_exact, dist, jnp.minimum(large, REL_BUCKETS - 1))


KEY_BLOCK = 128
DSA_CHUNK = 512
COUNT_CHUNK = 2 * DSA_CHUNK
COUNT_LANES = 8
ATTN_CHUNK = DSA_CHUNK
BLOCKS_PER_CHUNK = DSA_CHUNK // KEY_BLOCK
HEAD_PAIRS = A_HEADS // 2
PAIR_WIDTH = 2 * A_HEAD_DIM
INT_MIN = -2 ** 31
IDX_W_SCALE = IDX_HEADS ** -0.5 * IDX_DIM ** -0.5


def _kv_kernel(c_ref, g_ref, wk_ref, wvt_ref, k_ref, vt_ref):
    c = c_ref[...]
    cn = (c * lax.rsqrt(jnp.mean(c * c, -1, keepdims=True) + RMS_EPS) * g_ref[...]).astype(BF16)
    k_ref[...] = jnp.dot(cn, wk_ref[...], preferred_element_type=F32).astype(BF16)
    vt = lax.dot_general(wvt_ref[...], cn, (((1,), (1,)), ((), ())), preferred_element_type=F32).astype(BF16)
    for i in range(vt_ref.shape[1]):
        vt_ref[0, i] = vt[:, i * KEY_BLOCK:(i + 1) * KEY_BLOCK]


def kv_project(c_kv, col, g, w_kv_up, bn, s):
    ts = 512
    w = w_kv_up.astype(BF16)
    return pl.pallas_call(
        _kv_kernel,
        grid=(bn, s // ts),
        in_specs=[pl.BlockSpec((ts, A_KV_RANK), lambda b, i: (b * (s // ts) + i, col)),
                  pl.BlockSpec((1, A_KV_RANK), lambda b, i: (0, 0)),
                  pl.BlockSpec((A_KV_RANK, A_WIDTH), lambda b, i: (0, 0)),
                  pl.BlockSpec((A_WIDTH, A_KV_RANK), lambda b, i: (0, 0))],
        out_specs=[pl.BlockSpec((None, ts, A_WIDTH), lambda b, i: (b, i, 0)),
                   pl.BlockSpec((1, ts // KEY_BLOCK, A_WIDTH, KEY_BLOCK), lambda b, i: (b, i, 0, 0))],
        out_shape=[jax.ShapeDtypeStruct((bn, s, A_WIDTH), BF16),
                   jax.ShapeDtypeStruct((bn, s // KEY_BLOCK, A_WIDTH, KEY_BLOCK), BF16)],
        compiler_params=pltpu.CompilerParams(dimension_semantics=("arbitrary", "arbitrary"), vmem_limit_bytes=VMEM_LIMIT_BYTES),
        name="kv_project",
    )(c_kv, g.reshape(1, A_KV_RANK), w[:, :A_WIDTH], w[:, A_WIDTH:].T)


def _mm_nt_kernel(wt_ref, x_ref, o_ref):
    o_ref[0] = lax.dot_general(wt_ref[...], x_ref[...].astype(BF16), (((1,), (1,)), ((), ())),
                               preferred_element_type=F32).astype(o_ref.dtype)


def matmul_nt(wt, x, bn, s, out_dtype=BF16):
    n, k = wt.shape
    ts = 512
    return pl.pallas_call(
        _mm_nt_kernel,
        grid=(bn, s // ts),
        in_specs=[pl.BlockSpec((n, k), lambda b, i: (0, 0)),
                  pl.BlockSpec((ts, k), lambda b, i: (b * (s // ts) + i, 0))],
        out_specs=pl.BlockSpec((1, n, ts), lambda b, i: (b, 0, i)),
        out_shape=jax.ShapeDtypeStruct((bn, n, s), out_dtype),
        compiler_params=pltpu.CompilerParams(dimension_semantics=("arbitrary", "arbitrary"), vmem_limit_bytes=VMEM_LIMIT_BYTES),
        name="matmul_nt",
    )(wt.astype(BF16), x)


def _sortable_key(score):
    bits = pltpu.bitcast(score, jnp.int32)
    return bits ^ (lax.shift_right_arithmetic(bits, 31) & jnp.int32(0x7FFFFFFF))


def _dsa_kernel(qt_ref, iqt_ref, wt_ref, kidx_ref, k_ref, vt_ref, bias_ref, o_ref,
                skey_ref, widx_ref, wq_ref, acc_ref, m_ref, l_ref, *, k_sel):
    j = pl.program_id(1)
    n_chunks = (j + BLOCKS_PER_CHUNK) // BLOCKS_PER_CHUNK
    q_pos = j * Q_BLOCK + lax.broadcasted_iota(jnp.int32, (1, Q_BLOCK), 1)

    for h in range(IDX_HEADS):
        widx_ref[:, h * Q_BLOCK:(h + 1) * Q_BLOCK] = iqt_ref[0, h * IDX_DIM:(h + 1) * IDX_DIM, :]
    wq_ref[...] = jnp.zeros_like(wq_ref)
    for p in range(HEAD_PAIRS):
        wq_ref[p, 0:A_HEAD_DIM, 0:Q_BLOCK] = qt_ref[0, (2 * p) * A_HEAD_DIM:(2 * p + 1) * A_HEAD_DIM, :]
        wq_ref[p, A_HEAD_DIM:PAIR_WIDTH, Q_BLOCK:2 * Q_BLOCK] = qt_ref[0, (2 * p + 1) * A_HEAD_DIM:(2 * p + 2) * A_HEAD_DIM, :]

    w_rows = [wt_ref[0, h:h + 1, :] * IDX_W_SCALE for h in range(IDX_HEADS)]

    def idx_chunk(c, carry):
        start = pl.multiple_of(c * DSA_CHUNK, DSA_CHUNK)
        kc = kidx_ref[0, pl.ds(start, DSA_CHUNK), :]
        score = jnp.zeros((DSA_CHUNK, Q_BLOCK), F32)
        for p in range(IDX_HEADS // 2):
            z = jnp.dot(kc, widx_ref[:, p * 2 * Q_BLOCK:(p + 1) * 2 * Q_BLOCK], preferred_element_type=F32)
            score = score + jnp.maximum(z[:, :Q_BLOCK], 0.0) * w_rows[2 * p]
            score = score + jnp.maximum(z[:, Q_BLOCK:], 0.0) * w_rows[2 * p + 1]
        pos = start + lax.broadcasted_iota(jnp.int32, (DSA_CHUNK, Q_BLOCK), 0)
        skey_ref[pl.ds(start, DSA_CHUNK), :] = jnp.where(pos <= q_pos, _sortable_key(score), INT_MIN)
        return carry

    lax.fori_loop(0, n_chunks, idx_chunk, 0)

    n_count = (n_chunks + COUNT_CHUNK // DSA_CHUNK - 1) // (COUNT_CHUNK // DSA_CHUNK)

    @pl.when(n_chunks * DSA_CHUNK < n_count * COUNT_CHUNK)
    def _():
        skey_ref[pl.ds(pl.multiple_of(n_chunks * DSA_CHUNK, DSA_CHUNK), DSA_CHUNK), :] = jnp.full(
            (DSA_CHUNK, Q_BLOCK), INT_MIN, jnp.int32)

    def count_ge(cand):
        def body(c, cnt):
            start = pl.multiple_of(c * COUNT_CHUNK, COUNT_CHUNK)
            hit = (skey_ref[pl.ds(start, COUNT_CHUNK), :] >= cand).astype(jnp.int32)
            return cnt + jnp.sum(hit.reshape(COUNT_CHUNK // (8 * COUNT_LANES), COUNT_LANES, 8, Q_BLOCK), axis=0)
        cnt = lax.fori_loop(0, n_count, body, jnp.zeros((COUNT_LANES, 8, Q_BLOCK), jnp.int32))
        return jnp.sum(jnp.sum(cnt, axis=0), axis=0, keepdims=True)

    def bit_step(i, carry):
        thr_biased, n_ge = carry
        cand_biased = thr_biased | lax.shift_left(jnp.int32(1), 31 - i)
        cnt = count_ge(cand_biased ^ INT_MIN)
        keep = cnt >= k_sel
        return jnp.where(keep, cand_biased, thr_biased), jnp.where(keep, cnt, n_ge)

    thr, n_ge = lax.fori_loop(0, 32, bit_step, (jnp.zeros((1, Q_BLOCK), jnp.int32),
                                                jnp.full((1, Q_BLOCK), k_sel, jnp.int32)))
    thr = thr ^ INT_MIN

    tied = (n_ge > k_sel) & (thr > INT_MIN)

    @pl.when(jnp.max(tied.astype(jnp.int32)) > 0)
    def _():
        need = k_sel - count_ge(thr + 1)

        def count_tied_upto(limit):
            def body(c, cnt):
                start = pl.multiple_of(c * DSA_CHUNK, DSA_CHUNK)
                pos = start + lax.broadcasted_iota(jnp.int32, (DSA_CHUNK, Q_BLOCK), 0)
                hit = jnp.where(skey_ref[pl.ds(start, DSA_CHUNK), :] == thr, (pos <= limit).astype(jnp.int32), 0)
                return cnt + jnp.sum(hit.reshape(DSA_CHUNK // 8, 8, Q_BLOCK), axis=0)
            cnt = lax.fori_loop(0, n_chunks, body, jnp.zeros((8, Q_BLOCK), jnp.int32))
            return jnp.sum(cnt, axis=0, keepdims=True)

        def pos_step(i, lim):
            cand = lim - lax.shift_left(jnp.int32(1), 30 - i)
            return jnp.where(count_tied_upto(cand) >= need, cand, lim)

        limit = lax.fori_loop(0, 31, pos_step, jnp.full((1, Q_BLOCK), 2 ** 31 - 1, jnp.int32))

        def demote(c, carry):
            start = pl.multiple_of(c * DSA_CHUNK, DSA_CHUNK)
            pos = start + lax.broadcasted_iota(jnp.int32, (DSA_CHUNK, Q_BLOCK), 0)
            sk = skey_ref[pl.ds(start, DSA_CHUNK), :]
            skey_ref[pl.ds(start, DSA_CHUNK), :] = jnp.where(tied & (sk == thr) & (pos > limit), thr - 1, sk)
            return carry

        lax.fori_loop(0, n_chunks, demote, 0)

    m_ref[...] = jnp.full_like(m_ref, NEG_INF)
    l_ref[...] = jnp.zeros_like(l_ref)
    acc_ref[...] = jnp.zeros_like(acc_ref)

    def attend(s_list, vt_of, n_blocks):
        for p in range(HEAD_PAIRS):
            s = s_list[p]
            m_old = m_ref[p]
            m_new = jnp.maximum(m_old, jnp.max(s, axis=0, keepdims=True))
            alpha = jnp.exp(m_old - m_new)
            pexp = jnp.exp(s - m_new)
            l_ref[p] = l_ref[p] * alpha + jnp.sum(pexp, axis=0, keepdims=True)
            pb = pexp.astype(BF16)
            pv = jnp.dot(vt_of(p, 0), pb[0:KEY_BLOCK], preferred_element_type=F32)
            for i in range(1, n_blocks):
                pv = pv + jnp.dot(vt_of(p, i), pb[i * KEY_BLOCK:(i + 1) * KEY_BLOCK], preferred_element_type=F32)
            acc_ref[p] = acc_ref[p] * alpha + pv
            m_ref[p] = m_new

    far_end = (j - 1) * KEY_BLOCK
    attn_blocks = ATTN_CHUNK // KEY_BLOCK
    n_far = (jnp.maximum(j - 1, 0) + attn_blocks - 1) // attn_blocks

    def far_chunk(c, carry):
        start = pl.multiple_of(c * ATTN_CHUNK, ATTN_CHUNK)
        pos = start + lax.broadcasted_iota(jnp.int32, (ATTN_CHUNK, Q_BLOCK), 0)
        sel = (skey_ref[pl.ds(start, ATTN_CHUNK), :] >= thr) & (pos < far_end)
        madd = jnp.where(sel, 0.0, NEG_INF)
        madd2 = jnp.concatenate([madd, madd], axis=1)
        kc = k_ref[0, pl.ds(start, ATTN_CHUNK), :]
        s_list = [jnp.dot(kc[:, p * PAIR_WIDTH:(p + 1) * PAIR_WIDTH], wq_ref[p], preferred_element_type=F32) + madd2
                  for p in range(HEAD_PAIRS)]
        blk0 = c * attn_blocks
        attend(s_list, lambda p, i: vt_ref[0, blk0 + i, p * PAIR_WIDTH:(p + 1) * PAIR_WIDTH, :], attn_blocks)
        return carry

    lax.fori_loop(0, n_far, far_chunk, 0)

    def near_block(kb, delta):
        start = pl.multiple_of(kb * KEY_BLOCK, KEY_BLOCK)
        pos = start + lax.broadcasted_iota(jnp.int32, (KEY_BLOCK, Q_BLOCK), 0)
        sel = (skey_ref[pl.ds(start, KEY_BLOCK), :] >= thr) & (pos <= q_pos)
        madd = jnp.where(sel, 0.0, NEG_INF)
        madd2 = jnp.concatenate([madd, madd], axis=1)
        kc = k_ref[0, pl.ds(start, KEY_BLOCK), :]
        s_list = [jnp.dot(kc[:, p * PAIR_WIDTH:(p + 1) * PAIR_WIDTH], wq_ref[p], preferred_element_type=F32)
                  + bias_ref[delta, p] + madd2 for p in range(HEAD_PAIRS)]
        attend(s_list, lambda p, i: vt_ref[0, kb, p * PAIR_WIDTH:(p + 1) * PAIR_WIDTH, :], 1)

    @pl.when(j > 0)
    def _():
        near_block(j - 1, 1)

    near_block(j, 0)

    outs = []
    for p in range(HEAD_PAIRS):
        o = acc_ref[p] / l_ref[p]
        outs.append(o[0:A_HEAD_DIM, 0:Q_BLOCK])
        outs.append(o[A_HEAD_DIM:PAIR_WIDTH, Q_BLOCK:2 * Q_BLOCK])
    o_ref[0] = jnp.concatenate(outs, axis=0).T.astype(o_ref.dtype)


def _rel_bias_tiles(rel_bias):
    tab = rel_bias[rel_bucket(jnp.arange(2 * Q_BLOCK, dtype=jnp.int32))] - rel_bias[REL_BUCKETS - 1]
    key = jnp.arange(KEY_BLOCK)[:, None]
    qry = jnp.arange(Q_BLOCK)[None, :]
    tiles = []
    for delta in (0, 1):
        dist = jnp.maximum(delta * Q_BLOCK + qry - key, 0)
        t = jnp.moveaxis(tab[dist], 2, 0)
        t = t.reshape(HEAD_PAIRS, 2, KEY_BLOCK, Q_BLOCK)
        tiles.append(jnp.concatenate([t[:, 0], t[:, 1]], axis=-1))
    return jnp.stack(tiles)


def dsa_attention(qiq_t, wt, k_idx, k, vt, bias_tiles, bn, s):
    assert A_WIDTH == IDX_HEADS * IDX_DIM
    scal_rows = wt.shape[1]
    assert scal_rows % IDX_HEADS == 0
    assert s % ATTN_CHUNK == 0 and s % DSA_CHUNK == 0
    k_sel = min(TOPK_MAX, s // 4)
    nb = s // Q_BLOCK
    s_pad = s + DSA_CHUNK
    return pl.pallas_call(
        functools.partial(_dsa_kernel, k_sel=k_sel),
        grid=(bn, nb),
        in_specs=[pl.BlockSpec((1, A_WIDTH, Q_BLOCK), lambda b, j: (b, 0, j)),
                  pl.BlockSpec((1, IDX_HEADS * IDX_DIM, Q_BLOCK), lambda b, j: (b, 1, j)),
                  pl.BlockSpec((1, IDX_HEADS, Q_BLOCK), lambda b, j: (b, scal_rows // IDX_HEADS - 1, j)),
                  pl.BlockSpec((1, s, IDX_DIM), lambda b, j: (b, 0, 0)),
                  pl.BlockSpec((1, s, A_WIDTH), lambda b, j: (b, 0, 0)),
                  pl.BlockSpec((1, s // KEY_BLOCK, A_WIDTH, KEY_BLOCK), lambda b, j: (b, 0, 0, 0)),
                  pl.BlockSpec((2, HEAD_PAIRS, KEY_BLOCK, 2 * Q_BLOCK), lambda b, j: (0, 0, 0, 0))],
        out_specs=pl.BlockSpec((1, Q_BLOCK, A_WIDTH), lambda b, j: (b, j, 0)),
        out_shape=jax.ShapeDtypeStruct((bn, s, A_WIDTH), BF16),
        scratch_shapes=[pltpu.VMEM((s_pad, Q_BLOCK), jnp.int32),
                        pltpu.VMEM((IDX_DIM, IDX_HEADS * Q_BLOCK), BF16),
                        pltpu.VMEM((HEAD_PAIRS, PAIR_WIDTH, 2 * Q_BLOCK), BF16),
                        pltpu.VMEM((HEAD_PAIRS, PAIR_WIDTH, 2 * Q_BLOCK), F32),
                        pltpu.VMEM((HEAD_PAIRS, 1, 2 * Q_BLOCK), F32),
                        pltpu.VMEM((HEAD_PAIRS, 1, 2 * Q_BLOCK), F32)],
        compiler_params=pltpu.CompilerParams(dimension_semantics=("arbitrary", "arbitrary"), vmem_limit_bytes=56 * 1024 * 1024),
        name="dsa_attention",
    )(qiq_t, qiq_t, wt, k_idx, k, vt, bias_tiles)


GDN_CHUNK = B_HEAD_DIM
GDN_BLOCK = 1024
GDN_HALO = 8
NEUMANN_STEPS = 6


def _split_bf16(x):
    hi = x.astype(BF16)
    return hi, (x - hi.astype(F32)).astype(BF16)


def _dot_f32(a, b):
    a_hi, a_lo = _split_bf16(a)
    b_hi, b_lo = _split_bf16(b)
    d = lambda x, y: jnp.dot(x, y, preferred_element_type=F32)
    return d(a_hi, b_hi) + (d(a_hi, b_lo) + d(a_lo, b_hi))


def _bdot(a, b):
    return jnp.dot(a.astype(BF16), b.astype(BF16), preferred_element_type=F32)


def _bdot_nt(a, b):
    return lax.dot_general(a.astype(BF16), b.astype(BF16), (((1,), (1,)), ((), ())), preferred_element_type=F32)


def _bdot_tn(a, b):
    return lax.dot_general(a.astype(BF16), b.astype(BF16), (((0,), (0,)), ((), ())), preferred_element_type=F32)


def _softplus(x):
    return jnp.maximum(x, 0.0) + jnp.log(1.0 + jnp.exp(-jnp.abs(x)))


def _gdn_kernel(hq_ref, hk_ref, hv_ref, hz_ref, wq_ref, wk_ref, wv_ref, ab_ref, const_ref, gout_ref, tri_ref,
                o_ref, state_ref, xbuf_ref):
    head = pl.program_id(1)

    @pl.when(pl.program_id(2) == 0)
    def _():
        state_ref[...] = jnp.zeros_like(state_ref)
        xbuf_ref[:, 0:GDN_HALO, :] = jnp.zeros((3, GDN_HALO, B_HEAD_DIM), F32)

    def conv_silu(n, x_ref, w_ref):
        x = x_ref[...]
        xbuf_ref[n, GDN_HALO:, :] = x
        acc = xbuf_ref[n, GDN_HALO - CONV_WIDTH + 1:GDN_HALO - CONV_WIDTH + 1 + GDN_BLOCK, :] * w_ref[0:1, :]
        for j in range(1, CONV_WIDTH):
            lo = GDN_HALO - CONV_WIDTH + 1 + j
            acc = acc + xbuf_ref[n, lo:lo + GDN_BLOCK, :] * w_ref[j:j + 1, :]
        xbuf_ref[n, 0:GDN_HALO, :] = x[GDN_BLOCK - GDN_HALO:, :]
        return acc * jax.nn.sigmoid(acc)

    q = conv_silu(0, hq_ref, wq_ref)
    k = conv_silu(1, hk_ref, wk_ref)
    v = conv_silu(2, hv_ref, wv_ref)
    q = q * lax.rsqrt(jnp.sum(q * q, -1, keepdims=True) + RMS_EPS) * B_HEAD_DIM ** -0.5
    k = k * lax.rsqrt(jnp.sum(k * k, -1, keepdims=True) + RMS_EPS)

    a_row = ab_ref[0, pl.ds(head, 1), :]
    b_row = ab_ref[0, pl.ds(B_HEADS + head, 1), :]
    beta_row = jax.nn.sigmoid(b_row)
    g_row = const_ref[0, 0:1, :] * _softplus(a_row + const_ref[0, 1:2, :])
    gc_rows = _dot_f32(jnp.broadcast_to(g_row, (8, GDN_BLOCK)), tri_ref[...])[0:1, :]

    row_i = lax.broadcasted_iota(jnp.int32, (GDN_CHUNK, GDN_CHUNK), 0)
    col_j = lax.broadcasted_iota(jnp.int32, (GDN_CHUNK, GDN_CHUNK), 1)
    eye = jnp.where(row_i == col_j, 1.0, 0.0)
    z_all = hz_ref[...]
    chunks = []
    for c in range(GDN_BLOCK // GDN_CHUNK):
        sl = slice(c * GDN_CHUNK, (c + 1) * GDN_CHUNK)
        qc, kc, vc = q[sl], k[sl], v[sl]
        g_lane = jnp.broadcast_to(gc_rows[:, sl], (GDN_CHUNK, GDN_CHUNK))
        g_sub = g_lane.T
        beta_sub = jnp.broadcast_to(beta_row[:, sl], (GDN_CHUNK, GDN_CHUNK)).T
        g_end = jnp.broadcast_to(g_lane[:, GDN_CHUNK - 1:GDN_CHUNK], (GDN_CHUNK, GDN_CHUNK))
        decay = jnp.exp(jnp.where(row_i >= col_j, g_sub - g_lane, -jnp.inf))
        kb = kc * beta_sub
        x = -jnp.where(row_i > col_j, _bdot_nt(kb, kc) * decay, 0.0)
        chunks.append(dict(sl=sl, qc=qc, kc=kc, vc=vc, g_sub=g_sub, beta_sub=beta_sub, g_end=g_end, decay=decay,
                           kb=kb, x=x, t_inv=eye + x))
    for _ in range(NEUMANN_STEPS):
        for ch in chunks:
            ch['x'] = _dot_f32(ch['x'], ch['x'])
        for ch in chunks:
            ch['t_inv'] = ch['t_inv'] + _bdot(ch['t_inv'], ch['x'])
    for ch in chunks:
        g_exp = jnp.exp(ch['g_sub'])
        ch['u'] = _bdot(ch['t_inv'], ch['vc'] * ch['beta_sub'])
        ch['w'] = _bdot(ch['t_inv'], ch['kb'] * g_exp)
        ch['qk'] = _bdot_nt(ch['qc'], ch['kc']) * ch['decay']
        ch['q_dec'] = ch['qc'] * g_exp
        ch['k_dec'] = ch['kc'] * jnp.exp(ch['g_end'] - ch['g_sub'])
    for ch in chunks:
        state = state_ref[...]
        v_new = ch['u'] - _bdot(ch['w'], state)
        o = _bdot(ch['q_dec'], state) + _bdot(ch['qk'], v_new)
        state_ref[...] = state * jnp.exp(ch['g_end']) + _bdot_tn(ch['k_dec'], v_new)
        o = o * lax.rsqrt(jnp.mean(o * o, -1, keepdims=True) + RMS_EPS) * gout_ref[...]
        zc = z_all[ch['sl']]
        o_ref[ch['sl'], :] = (o * (zc * jax.nn.sigmoid(zc))).astype(o_ref.dtype)


def gated_deltanet(h, col0, ab_t, conv_w, a_log, dt_bias, out_norm_g, bn, s):
    nblk = s // GDN_BLOCK
    tok = jnp.arange(GDN_BLOCK)
    tri = ((tok[:, None] // GDN_CHUNK == tok[None, :] // GDN_CHUNK) & (tok[:, None] <= tok[None, :])).astype(F32)
    const = jnp.stack([jnp.broadcast_to(-jnp.exp(a_log)[:, None], (B_HEADS, GDN_BLOCK)),
                       jnp.broadcast_to(dt_bias[:, None], (B_HEADS, GDN_BLOCK))], axis=1)
    const = jnp.pad(const, ((0, 0), (0, 6), (0, 0)))
    hcol = lambda part: pl.BlockSpec((GDN_BLOCK, B_HEAD_DIM), lambda b, hh, i: (b * nblk + i, col0 + part * B_HEADS + hh))
    wcol = lambda part: pl.BlockSpec((CONV_WIDTH, B_HEAD_DIM), lambda b, hh, i: (0, part * B_HEADS + hh))
    return pl.pallas_call(
        _gdn_kernel,
        grid=(bn, B_HEADS, nblk),
        in_specs=[hcol(0), hcol(1), hcol(2), hcol(3), wcol(0), wcol(1), wcol(2),
                  pl.BlockSpec((1, 16, GDN_BLOCK), lambda b, hh, i: (b, 0, i)),
                  pl.BlockSpec((1, 8, GDN_BLOCK), lambda b, hh, i: (hh, 0, 0)),
                  pl.BlockSpec((1, B_HEAD_DIM), lambda b, hh, i: (0, 0)),
                  pl.BlockSpec((GDN_BLOCK, GDN_BLOCK), lambda b, hh, i: (0, 0))],
        out_specs=pl.BlockSpec((GDN_BLOCK, B_HEAD_DIM), lambda b, hh, i: (b * nblk + i, hh)),
        out_shape=jax.ShapeDtypeStruct((bn * s, B_WIDTH), BF16),
        scratch_shapes=[pltpu.VMEM((B_HEAD_DIM, B_HEAD_DIM), F32),
                        pltpu.VMEM((3, GDN_HALO + GDN_BLOCK, B_HEAD_DIM), F32)],
        compiler_params=pltpu.CompilerParams(dimension_semantics=("arbitrary", "arbitrary", "arbitrary"), vmem_limit_bytes=VMEM_LIMIT_BYTES),
        name="gated_deltanet",
    )(h, h, h, h, conv_w, conv_w, conv_w, ab_t, const, out_norm_g.reshape(1, B_HEAD_DIM), tri)


S5_CHUNK = 32
S5_ROW = S5_CHUNK * C_GROUP
STATE2 = 2 * C_STATE


def _s5_tables(lam_re, lam_im, log_dt, b_re, b_im, c_re, c_im, d_skip):
    hp = lax.Precision.HIGHEST
    ell = S5_CHUNK
    dt = jnp.exp(log_dt)[:, None]
    lr, li = lam_re, lam_im
    mag = jnp.exp(lr * dt)
    a_re, a_im = mag * jnp.cos(li * dt), mag * jnp.sin(li * dt)
    den = lr * lr + li * li
    f_re = ((a_re - 1.0) * lr + a_im * li) / den
    f_im = (a_im * lr - (a_re - 1.0) * li) / den
    bb_re = f_re[..., None] * b_re - f_im[..., None] * b_im
    bb_im = f_re[..., None] * b_im + f_im[..., None] * b_re
    tau = jnp.arange(ell + 1, dtype=F32)[:, None, None]
    pmag, ang = jnp.exp(tau * (lr * dt)), tau * (li * dt)
    p_re, p_im = pmag * jnp.cos(ang), pmag * jnp.sin(ang)
    ab_re = p_re[:ell, ..., None] * bb_re - p_im[:ell, ..., None] * bb_im
    ab_im = p_re[:ell, ..., None] * bb_im + p_im[:ell, ..., None] * bb_re
    kern = (jnp.einsum('gip,tgpj->tgij', c_re, ab_re, precision=hp)
            - jnp.einsum('gip,tgpj->tgij', c_im, ab_im, precision=hp))
    lag = jnp.arange(ell)[None, :] - jnp.arange(ell)[:, None]
    toep = jnp.where((lag >= 0)[:, :, None, None, None], kern[jnp.maximum(lag, 0)], 0.0)
    toep = jnp.transpose(toep, (2, 0, 4, 1, 3)).reshape(C_GROUPS, S5_ROW, S5_ROW)
    w_re = jnp.transpose(ab_re[::-1], (1, 0, 3, 2)).reshape(C_GROUPS, S5_ROW, C_STATE)
    w_im = jnp.transpose(ab_im[::-1], (1, 0, 3, 2)).reshape(C_GROUPS, S5_ROW, C_STATE)
    tw = jnp.concatenate([toep, w_re, w_im, w_im, w_re], axis=-1).astype(BF16)
    pt_re = jnp.swapaxes(p_re[1:], 0, 1)[:, :, None, :]
    pt_im = jnp.swapaxes(p_im[1:], 0, 1)[:, :, None, :]
    ca_re = c_re[:, None] * pt_re - c_im[:, None] * pt_im
    ca_im = c_re[:, None] * pt_im + c_im[:, None] * pt_re
    v_re = jnp.transpose(ca_re, (0, 3, 1, 2)).reshape(C_GROUPS, C_STATE, S5_ROW)
    v_im = jnp.transpose(-ca_im, (0, 3, 1, 2)).reshape(C_GROUPS, C_STATE, S5_ROW)
    v = jnp.concatenate([v_re, v_im], axis=1).astype(BF16)
    al_re, al_im = p_re[ell], p_im[ell]
    coef = jnp.stack([jnp.concatenate([al_re, al_re], -1), jnp.concatenate([-al_im, al_im], -1),
                      jnp.concatenate([al_im, -al_im], -1)], axis=1)
    d_flat = jnp.tile(d_skip, (1, ell)).reshape(C_GROUPS, 1, S5_ROW)
    return tw, v, coef, d_flat


def _s5_kernel(u_ref, tw_ref, v_ref, coef_ref, d_ref, o_ref, sin_ref, xin_ref, *, bn):
    rows = u_ref.shape[1]
    n_chunks = rows // bn
    u = u_ref[0]
    tw = jnp.dot(u.astype(BF16), tw_ref[0], preferred_element_type=F32)
    o_ref[0] = tw[:, :S5_ROW] + u * d_ref[0]
    sin_ref[...] = tw[:, S5_ROW:]
    a1, a2, a2s = coef_ref[0, 0:1, :], coef_ref[0, 1:2, :], coef_ref[0, 2:3, :]

    def step(c, carry):
        new = []
        for b in range(bn):
            x, xs = carry[2 * b], carry[2 * b + 1]
            row = b * n_chunks + c
            xin_ref[pl.ds(row, 1), :] = x
            s_row = sin_ref[pl.ds(row, 1), :]
            new.append(a1 * x + a2 * xs + s_row[:, :STATE2])
            new.append(a1 * xs + a2s * x + s_row[:, STATE2:])
        return tuple(new)

    zero = jnp.zeros((1, STATE2), F32)
    lax.fori_loop(0, n_chunks, step, (zero,) * (2 * bn))
    y = o_ref[0] + jnp.dot(xin_ref[...].astype(BF16), v_ref[0], preferred_element_type=F32)
    o_ref[0] = jax.nn.gelu(y)


def _glu_kernel(y_ref, w_ref, b_ref, o_ref):
    y = y_ref[...]
    gate = jnp.dot(y.astype(BF16), w_ref[...], preferred_element_type=F32) + b_ref[...]
    o_ref[...] = (y * jax.nn.sigmoid(gate)).astype(o_ref.dtype)


def s5_branch(u, lam_re, lam_im, log_dt, b_re, b_im, c_re, c_im, d_skip, glu_w, glu_b):
    bn, s, _ = u.shape
    n_chunks = s // S5_CHUNK
    rows = bn * n_chunks
    tw, v, coef, d_flat = _s5_tables(lam_re, lam_im, log_dt, b_re, b_im, c_re, c_im, d_skip)
    uf = jnp.transpose(u.reshape(bn, n_chunks, S5_CHUNK, C_GROUPS, C_GROUP), (3, 0, 1, 2, 4)).reshape(C_GROUPS, rows, S5_ROW)
    per_group = lambda shape: pl.BlockSpec((1,) + shape, lambda g: (g, 0, 0))
    y = pl.pallas_call(
        functools.partial(_s5_kernel, bn=bn),
        grid=(C_GROUPS,),
        in_specs=[per_group((rows, S5_ROW)), per_group((S5_ROW, S5_ROW + 2 * STATE2)), per_group((STATE2, S5_ROW)),
                  per_group((3, STATE2)), per_group((1, S5_ROW))],
        out_specs=per_group((rows, S5_ROW)),
        out_shape=jax.ShapeDtypeStruct((C_GROUPS, rows, S5_ROW), F32),
        scratch_shapes=[pltpu.VMEM((rows, 2 * STATE2), F32), pltpu.VMEM((rows, STATE2), F32)],
        compiler_params=pltpu.CompilerParams(dimension_semantics=("arbitrary",), vmem_limit_bytes=VMEM_LIMIT_BYTES),
        name="s5_scan",
    )(uf, tw, v, coef, d_flat)
    y = jnp.transpose(y.reshape(C_GROUPS, bn, n_chunks, S5_CHUNK, C_GROUP), (1, 2, 3, 0, 4)).reshape(bn * s, C_WIDTH)
    t = bn * s
    tm = _pick_tile(t, (1024, 512, 256, 128, 8))
    return pl.pallas_call(
        _glu_kernel,
        grid=(t // tm,),
        in_specs=[pl.BlockSpec((tm, C_WIDTH), lambda i: (i, 0)), pl.BlockSpec((C_WIDTH, C_WIDTH), lambda i: (0, 0)),
                  pl.BlockSpec((1, C_WIDTH), lambda i: (0, 0))],
        out_specs=pl.BlockSpec((tm, C_WIDTH), lambda i: (i, 0)),
        out_shape=jax.ShapeDtypeStruct((t, C_WIDTH), BF16),
        compiler_params=pltpu.CompilerParams(dimension_semantics=("arbitrary",), vmem_limit_bytes=VMEM_LIMIT_BYTES),
        name="s5_glu",
    )(y, glu_w.astype(BF16), glu_b.reshape(1, C_WIDTH))


def hybrid_mixer(x, rel_bias, w_in, a_kv_norm, a_kv_up, b_conv, b_a_log, b_dt_bias, b_out_norm,
                 c_lambda_re, c_lambda_im, c_log_dt, c_b_re, c_b_im, c_c_re, c_c_im, c_d, c_glu_w, c_glu_b,
                 w_br_a, w_br_b, w_br_c, w_o, ln_g, ln_b):
    bn, s, d = x.shape
    t = bn * s
    xt = x.reshape(t, d)
    offsets = [0] + np.cumsum(SPLITS).tolist()
    w_cols = [w_in[:, offsets[i]:offsets[i + 1]] for i in range(len(SPLITS))]
    (w_aq, w_ckv, w_iq, w_ik, w_iw, w_bqkv, w_bz, w_ba, w_bb, w_cu, w_gate) = w_cols
    w_ik_pad = jnp.pad(w_ik, ((0, 0), (0, LANE - IDX_DIM)))
    w_rows = jnp.concatenate([w_gate, w_ckv, w_ik_pad, w_bqkv, w_bz, w_cu], axis=1)
    w_rows = jnp.pad(w_rows, ((0, 0), (0, -w_rows.shape[1] % PROJ_TILE_N)))
    h = matmul(xt, w_rows)
    o = np.cumsum([0, 3 * D_MODEL, A_KV_RANK, LANE, 3 * B_WIDTH, B_WIDTH, C_WIDTH]).tolist()
    idx_k = h[:, o[2]:o[2] + IDX_DIM].astype(BF16).reshape(bn, s, IDX_DIM)
    c_u = h[:, o[5]:o[6]].reshape(bn, s, C_WIDTH)
    qiq_t = matmul_nt(jnp.concatenate([w_aq * A_HEAD_DIM ** -0.5, w_iq], axis=1).T, xt, bn, s)
    scal_t = matmul_nt(jnp.concatenate([w_ba, w_bb, w_iw], axis=1).T, xt, bn, s, out_dtype=F32)
    k, vt = kv_project(h, o[1] // LANE, a_kv_norm, a_kv_up, bn, s)
    ya = dsa_attention(qiq_t, scal_t, idx_k, k, vt, _rel_bias_tiles(rel_bias), bn, s)
    yb = gated_deltanet(h, o[3] // LANE, scal_t, b_conv, b_a_log, b_dt_bias, b_out_norm, bn, s)
    yc = s5_branch(c_u, c_lambda_re, c_lambda_im, c_log_dt, c_b_re, c_b_im, c_c_re, c_c_im, c_d, c_glu_w, c_glu_b)
    return merge_branches(ya.reshape(t, A_WIDTH), yb, yc, h, w_br_a, w_br_b, w_br_c, w_o, xt, ln_g, ln_b)


def _merge_kernel(ya_ref, yb_ref, yc_ref, ga_ref, gb_ref, gc_ref, wa_ref, wb_ref, wc_ref, wo_ref, x_ref, g_ref, b_ref, o_ref):
    branch = lambda y_ref, w_ref, gate_ref: jax.nn.sigmoid(gate_ref[...]) * jnp.dot(y_ref[...], w_ref[...], preferred_element_type=F32)
    merged = branch(ya_ref, wa_ref, ga_ref) + branch(yb_ref, wb_ref, gb_ref) + branch(yc_ref, wc_ref, gc_ref)
    mix = jnp.dot(merged.astype(BF16), wo_ref[...], preferred_element_type=F32)
    z = DEEPNORM_ALPHA * x_ref[...] + mix
    mu = jnp.mean(z, -1, keepdims=True)
    zc = z - mu
    var = jnp.mean(zc * zc, -1, keepdims=True)
    o_ref[...] = zc * lax.rsqrt(var + LN_EPS) * g_ref[...] + b_ref[...]


def merge_branches(ya, yb, yc, h, w_br_a, w_br_b, w_br_c, w_o, x, ln_g, ln_b):
    t, d = x.shape
    tm = _pick_tile(t, (512, 256, 128, 8))
    row = lambda width, col=0: pl.BlockSpec((tm, width), lambda i: (i, col))
    full = lambda a: pl.BlockSpec(a.shape, lambda i: (0, 0))
    ws = [w.astype(BF16) for w in (w_br_a, w_br_b, w_br_c, w_o)]
    return pl.pallas_call(
        _merge_kernel,
        grid=(t // tm,),
        in_specs=[row(A_WIDTH), row(B_WIDTH), row(C_WIDTH), row(d, 0), row(d, 1), row(d, 2)] + [full(w) for w in ws]
                 + [row(d), pl.BlockSpec((1, d), lambda i: (0, 0)), pl.BlockSpec((1, d), lambda i: (0, 0))],
        out_specs=row(d),
        out_shape=jax.ShapeDtypeStruct((t, d), F32),
        compiler_params=pltpu.CompilerParams(dimension_semantics=("arbitrary",), vmem_limit_bytes=VMEM_LIMIT_BYTES),
        name="merge_branches",
    )(ya, yb, yc, h, h, h, *ws, x, ln_g.reshape(1, d), ln_b.reshape(1, d))


ROUTER_TILE = 512


def _router_kernel(x_ref, rwt_ref, bias_ref, tri_ref, idx_ref, gate_ref, pos_ref, cnt_ref, carry_ref):
    @pl.when(pl.program_id(0) == 0)
    def _():
        carry_ref[...] = jnp.zeros_like(carry_ref)

    logits = lax.dot_general(rwt_ref[...], x_ref[...].astype(BF16), (((1,), (1,)), ((), ())), preferred_element_type=F32)
    scores = jax.nn.sigmoid(logits)
    remaining = scores + bias_ref[...]
    expert = lax.broadcasted_iota(jnp.int32, scores.shape, 0)
    picks = []
    for _ in range(TOP_K):
        best = jnp.max(remaining, axis=0, keepdims=True)
        first = jnp.min(jnp.where(remaining == best, expert, N_EXPERTS), axis=0, keepdims=True)
        pick = expert == first
        picks.append((first, pick))
        remaining = jnp.where(pick, -jnp.inf, remaining)
    chosen = picks[0][1]
    for _, pick in picks[1:]:
        chosen = chosen | pick
    total = jnp.sum(jnp.where(chosen, scores, 0.0), axis=0, keepdims=True)
    prefix = jnp.dot(jnp.where(chosen, 1.0, 0.0).astype(BF16), tri_ref[...], preferred_element_type=F32)
    rank = carry_ref[...] + prefix.astype(jnp.int32) - 1
    carry_ref[...] = carry_ref[...] + prefix[:, -1:].astype(jnp.int32)
    cnt_ref[...] = carry_ref[...]
    for k, (first, pick) in enumerate(picks):
        idx_ref[k:k + 1, :] = first
        gate_ref[k:k + 1, :] = jnp.sum(jnp.where(pick, scores, 0.0), axis=0, keepdims=True) / total * ROUTED_SCALE
        pos_ref[k:k + 1, :] = jnp.sum(jnp.where(pick, rank, 0), axis=0, keepdims=True)


def moe_router(xt, router_w, router_bias):
    t, d = xt.shape
    tile = ROUTER_TILE
    tri = (jnp.arange(tile)[:, None] <= jnp.arange(tile)[None, :]).astype(BF16)
    row = pl.BlockSpec((TOP_K, tile), lambda i: (0, i))
    return pl.pallas_call(
        _router_kernel,
        grid=(t // tile,),
        in_specs=[pl.BlockSpec((tile, d), lambda i: (i, 0)), pl.BlockSpec((N_EXPERTS, d), lambda i: (0, 0)),
                  pl.BlockSpec((N_EXPERTS, 1), lambda i: (0, 0)), pl.BlockSpec((tile, tile), lambda i: (0, 0))],
        out_specs=[row, row, row, pl.BlockSpec((N_EXPERTS, 1), lambda i: (0, 0))],
        out_shape=[jax.ShapeDtypeStruct((TOP_K, t), jnp.int32), jax.ShapeDtypeStruct((TOP_K, t), F32),
                   jax.ShapeDtypeStruct((TOP_K, t), jnp.int32), jax.ShapeDtypeStruct((N_EXPERTS, 1), jnp.int32)],
        scratch_shapes=[pltpu.VMEM((N_EXPERTS, 1), jnp.int32)],
        compiler_params=pltpu.CompilerParams(dimension_semantics=("arbitrary",), vmem_limit_bytes=VMEM_LIMIT_BYTES),
        name="moe_router",
    )(xt, router_w.T.astype(BF16), router_bias.reshape(N_EXPERTS, 1), tri)


def _expert_kernel(blk_expert_ref, blk_rows_ref, x_ref, wgu_ref, wdn_ref, o_ref):
    n_valid = blk_rows_ref[pl.program_id(0)]

    @pl.when(n_valid > 0)
    def _():
        row = lax.broadcasted_iota(jnp.int32, x_ref.shape, 0)
        x = jnp.where(row < n_valid, x_ref[...], 0.0).astype(BF16)
        h = jnp.dot(x, wgu_ref[0], preferred_element_type=F32)
        hg, hu = h[:, :D_EXPERT], h[:, D_EXPERT:]
        act = (hg * jax.nn.sigmoid(hg) * hu).astype(BF16)
        o_ref[...] = jnp.dot(act, wdn_ref[0], preferred_element_type=F32)

    @pl.when(n_valid <= 0)
    def _():
        o_ref[...] = jnp.zeros_like(o_ref)


def expert_ffn(xs, blk_expert, blk_rows, w_gu, w_dn):
    n_pad, d = xs.shape
    n_blk = n_pad // MOE_BLOCK
    return pl.pallas_call(
        _expert_kernel,
        grid_spec=pltpu.PrefetchScalarGridSpec(
            num_scalar_prefetch=2,
            grid=(n_blk,),
            in_specs=[pl.BlockSpec((MOE_BLOCK, d), lambda b, be, br: (b, 0)),
                      pl.BlockSpec((1, d, 2 * D_EXPERT), lambda b, be, br: (be[b], 0, 0)),
                      pl.BlockSpec((1, D_EXPERT, d), lambda b, be, br: (be[b], 0, 0))],
            out_specs=pl.BlockSpec((MOE_BLOCK, d), lambda b, be, br: (b, 0))),
        out_shape=jax.ShapeDtypeStruct((n_pad, d), F32),
        compiler_params=pltpu.CompilerParams(dimension_semantics=("arbitrary",), vmem_limit_bytes=VMEM_LIMIT_BYTES),
        name="expert_ffn",
    )(blk_expert, blk_rows, xs, w_gu, w_dn)


DISPATCH_TILE = 256
COMBINE_TILE = 128
SUBLANES = 8


def _rows_kernel(pad_start_ref, idx_ref, pos_ref, o_ref):
    idx = idx_ref[...]
    row = pos_ref[...]
    for e in range(N_EXPERTS):
        row = row + jnp.where(idx == e, pad_start_ref[e], 0)
    o_ref[...] = row


def moe_rows(pad_start, idx, pos):
    k, t = idx.shape
    tile = _pick_tile(t, (4096, 2048, 1024, 512, 256, 128))
    blk = pl.BlockSpec((k, tile), lambda i, ps: (0, i))
    return pl.pallas_call(
        _rows_kernel,
        grid_spec=pltpu.PrefetchScalarGridSpec(num_scalar_prefetch=1, grid=(t // tile,), in_specs=[blk, blk], out_specs=blk),
        out_shape=jax.ShapeDtypeStruct((k, t), jnp.int32),
        compiler_params=pltpu.CompilerParams(dimension_semantics=("arbitrary",)),
        name="moe_rows",
    )(pad_start, idx, pos)


def _dispatch_kernel(dest_ref, x_ref, xs_hbm, sem):
    def issue(g, carry):
        base = pl.multiple_of(g * SUBLANES, SUBLANES)
        for sub in range(SUBLANES):
            for k in range(TOP_K):
                row = dest_ref[0, 0, (base + sub) * TOP_K + k]
                pltpu.make_async_copy(x_ref.at[pl.ds(base + sub, 1)], xs_hbm.at[pl.ds(row, 1)], sem).start(priority=k % 2)
        return carry

    lax.fori_loop(0, DISPATCH_TILE // SUBLANES, issue, 0)
    n_rows = DISPATCH_TILE * TOP_K
    pltpu.make_async_copy(xs_hbm.at[pl.ds(0, n_rows)], xs_hbm.at[pl.ds(0, n_rows)], sem).wait()


def moe_dispatch(xt, dest_tok_major, n_pad):
    t, d = xt.shape
    n_tiles = t // DISPATCH_TILE
    return pl.pallas_call(
        _dispatch_kernel,
        grid=(n_tiles,),
        in_specs=[pl.BlockSpec((1, 1, DISPATCH_TILE * TOP_K), lambda i: (i, 0, 0), memory_space=pltpu.SMEM),
                  pl.BlockSpec((DISPATCH_TILE, d), lambda i: (i, 0))],
        out_specs=pl.BlockSpec(memory_space=pl.ANY),
        scratch_shapes=[pltpu.SemaphoreType.DMA(())],
        out_shape=jax.ShapeDtypeStruct((n_pad, d), xt.dtype),
        compiler_params=pltpu.CompilerParams(dimension_semantics=("arbitrary",), has_side_effects=True),
        name="moe_dispatch",
    )(dest_tok_major.reshape(n_tiles, 1, -1), xt)


def _combine_kernel(dest_ref, dest_next_ref, ys_hbm, gate_ref, x_ref, wgu_ref, wdn_ref, g_ref, b_ref, o_ref, buf_ref, sem):
    i = pl.program_id(0)
    n = pl.num_programs(0)
    slot = lax.rem(i, 2)

    def gather(d_ref, into):
        def issue(g, carry):
            base = pl.multiple_of(g * SUBLANES, SUBLANES)
            for sub in range(SUBLANES):
                for k in range(TOP_K):
                    row = d_ref[0, 0, (base + sub) * TOP_K + k]
                    pltpu.make_async_copy(ys_hbm.at[pl.ds(row, 1)], buf_ref.at[into, k, pl.ds(base + sub, 1)],
                                          sem.at[into]).start(priority=k % 2)
            return carry
        lax.fori_loop(0, COMBINE_TILE // SUBLANES, issue, 0)

    @pl.when(i == 0)
    def _():
        gather(dest_ref, 0)

    @pl.when(i + 1 < n)
    def _():
        gather(dest_next_ref, 1 - slot)

    x = x_ref[...]
    h = jnp.dot(x.astype(BF16), wgu_ref[...], preferred_element_type=F32)
    hg, hu = h[:, :D_SHARED], h[:, D_SHARED:]
    acc = DEEPNORM_ALPHA * x + jnp.dot((hg * jax.nn.sigmoid(hg) * hu).astype(BF16), wdn_ref[...], preferred_element_type=F32)
    pltpu.make_async_copy(buf_ref.at[slot], buf_ref.at[slot], sem.at[slot]).wait()
    gate = gate_ref[...]
    for k in range(TOP_K):
        acc = acc + buf_ref[slot, k] * gate[:, k:k + 1]
    mu = jnp.mean(acc, -1, keepdims=True)
    zc = acc - mu
    var = jnp.mean(zc * zc, -1, keepdims=True)
    o_ref[...] = zc * lax.rsqrt(var + LN_EPS) * g_ref[...] + b_ref[...]


def moe_combine(ys, dest_tok_major, gate_tok_major, xt, sh_w_gu, sh_w_down, ln_g, ln_b):
    t, d = xt.shape
    tm = COMBINE_TILE
    n_tiles = t // tm
    dest3 = dest_tok_major.reshape(n_tiles, 1, -1)
    row = pl.BlockSpec((tm, d), lambda i: (i, 0))
    full = lambda shape: pl.BlockSpec(shape, lambda i: (0, 0))
    this = pl.BlockSpec((1, 1, tm * TOP_K), lambda i: (i, 0, 0), memory_space=pltpu.SMEM)
    nxt = pl.BlockSpec((1, 1, tm * TOP_K), lambda i: (jnp.minimum(i + 1, n_tiles - 1), 0, 0), memory_space=pltpu.SMEM)
    return pl.pallas_call(
        _combine_kernel,
        grid=(n_tiles,),
        in_specs=[this, nxt, pl.BlockSpec(memory_space=pl.ANY), pl.BlockSpec((tm, TOP_K), lambda i: (i, 0)), row,
                  full((d, 2 * D_SHARED)), full((D_SHARED, d)), full((1, d)), full((1, d))],
        out_specs=row,
        scratch_shapes=[pltpu.VMEM((2, TOP_K, tm, d), F32), pltpu.SemaphoreType.DMA((2,))],
        out_shape=jax.ShapeDtypeStruct((t, d), F32),
        compiler_params=pltpu.CompilerParams(dimension_semantics=("arbitrary",), vmem_limit_bytes=VMEM_LIMIT_BYTES),
        name="moe_combine",
    )(dest3, dest3, ys, gate_tok_major, xt, sh_w_gu.astype(BF16), sh_w_down.astype(BF16),
      ln_g.reshape(1, d), ln_b.reshape(1, d))


def moe_ffn(xt, router_w, router_bias, exp_w_gu, exp_w_down, sh_w_gu, sh_w_down, ln_g, ln_b):
    n_tok, d = xt.shape
    idx, gate, pos, counts = moe_router(xt, router_w, router_bias)
    counts = counts[:, 0]
    padded = (counts + MOE_BLOCK - 1) // MOE_BLOCK * MOE_BLOCK
    pad_ends = jnp.cumsum(padded)
    pad_start = pad_ends - padded
    n_blk = n_tok * TOP_K // MOE_BLOCK + N_EXPERTS
    n_pad = n_blk * MOE_BLOCK
    blk_lo = jnp.arange(n_blk, dtype=jnp.int32) * MOE_BLOCK
    owner = (pad_start[None, :] <= blk_lo[:, None]) & (blk_lo[:, None] < pad_ends[None, :])
    blk_expert = jnp.minimum(jnp.sum(pad_ends[None, :] <= blk_lo[:, None], axis=1), N_EXPERTS - 1).astype(jnp.int32)
    blk_rows = jnp.clip(jnp.sum(jnp.where(owner, (pad_start + counts)[None, :] - blk_lo[:, None], 0), axis=1),
                        0, MOE_BLOCK).astype(jnp.int32)
    dest = moe_rows(pad_start.astype(jnp.int32), idx, pos).T.reshape(-1)
    xs = moe_dispatch(xt, dest, n_pad)
    ys = expert_ffn(xs, blk_expert, blk_rows, exp_w_gu.astype(BF16), exp_w_down.astype(BF16))
    return moe_combine(ys, dest, gate.T, xt, sh_w_gu, sh_w_down, ln_g, ln_b)


def kernel(x, rel_bias, w_in, a_kv_norm, a_kv_up, b_conv, b_a_log, b_dt_bias, b_out_norm, c_lambda_re, c_lambda_im, c_log_dt, c_b_re, c_b_im, c_c_re, c_c_im, c_d, c_glu_w, c_glu_b, w_br_a, w_br_b, w_br_c, w_o, ln1_g, ln1_b, router_w, router_bias, exp_w_gu, exp_w_down, sh_w_gu, sh_w_down, ln2_g, ln2_b):
    bn, s, d = x.shape
    t = bn * s
    for i in range(w_in.shape[0]):
        xt = hybrid_mixer(x, rel_bias, w_in[i], a_kv_norm[i], a_kv_up[i], b_conv[i], b_a_log[i], b_dt_bias[i],
                          b_out_norm[i], c_lambda_re[i], c_lambda_im[i], c_log_dt[i], c_b_re[i], c_b_im[i],
                          c_c_re[i], c_c_im[i], c_d[i], c_glu_w[i], c_glu_b[i], w_br_a[i], w_br_b[i], w_br_c[i], w_o[i],
                          ln1_g[i], ln1_b[i])
        x = moe_ffn(xt, router_w[i], router_bias[i], exp_w_gu[i], exp_w_down[i], sh_w_gu[i], sh_w_down[i],
                    ln2_g[i], ln2_b[i]).reshape(bn, s, d)
    return x
```
